```python
import jax, jax.numpy as jnp
from jax import lax
import numpy as np

D_MODEL = 2048
BATCH = 1
SEQ = 8192
DEPTH = 2

GRID_W = 64
CTX_LEN = 256
MIX_WIDTH = D_MODEL
HEAD_DIM = 128
A_Q_HEADS = (MIX_WIDTH // 2) // HEAD_DIM
A_KV_HEADS = 2
A_GROUP = A_Q_HEADS // A_KV_HEADS
A_Q_DIM = A_Q_HEADS * HEAD_DIM
A_KV_DIM = A_KV_HEADS * HEAD_DIM
WINDOW = 128
BLOCK = 128
ROPE_THETA = 10000.0
AXIS_DIM = HEAD_DIM // 2
ATTN_SCALE = HEAD_DIM ** -0.5
NEG_INF = -1e30
B_GROUPS = 4
B_WIDTH = MIX_WIDTH // 2
B_GROUP_DIM = B_WIDTH // B_GROUPS
POOL_WINDOWS = (2, 4, 8, 16)
AB_IN = A_Q_DIM + 2 * A_KV_DIM + B_WIDTH
C_WIDTH = MIX_WIDTH // 2
C_GROUPS = 4
C_GROUP_DIM = C_WIDTH // C_GROUPS
CHUNK = 128
D_WIDTH = MIX_WIDTH - C_WIDTH
D_GROUPS = 8
D_GROUP_DIM = D_WIDTH // D_GROUPS
CD_IN = 2 * C_WIDTH + D_WIDTH
D_FF = ((8 * D_MODEL // 3 + 255) // 256) * 256
N_EVEN = (DEPTH + 1) // 2
N_ODD = DEPTH // 2
EPS = 1e-6

kernel_name = "hybrid_diffusion_window_pool_gmlp_fourier"


def _rms(x, g):
    xf = x.astype(jnp.float32)
    y = xf * lax.rsqrt(jnp.mean(xf * xf, axis=-1, keepdims=True) + EPS)
    return (y * g.astype(jnp.float32)).astype(x.dtype)


def _modulate(h, shift, scale):
    return h * (1 + scale) + shift


def _axial_angles(n):
    rows = n // GRID_W
    row = jnp.repeat(jnp.arange(rows, dtype=jnp.float32), GRID_W)
    col = jnp.tile(jnp.arange(GRID_W, dtype=jnp.float32), rows)
    inv = ROPE_THETA ** (-jnp.arange(0, AXIS_DIM, 2, dtype=jnp.float32) / AXIS_DIM)
    return row[:, None] * inv[None, :], col[:, None] * inv[None, :]


def _rope_half(x, ang):
    x1, x2 = jnp.split(x, 2, axis=-1)
    cos = jnp.cos(ang)[:, None, :]
    sin = jnp.sin(ang)[:, None, :]
    return jnp.concatenate([x1 * cos - x2 * sin, x2 * cos + x1 * sin], axis=-1)


def _rope_2d(x, ang_r, ang_c):
    xf = x.astype(jnp.float32)
    xr, xc = jnp.split(xf, 2, axis=-1)
    return jnp.concatenate([_rope_half(xr, ang_r), _rope_half(xc, ang_c)], axis=-1).astype(x.dtype)


def _window_attention(q, k, v, kc, vc, sink):
    B, N = q.shape[0], q.shape[1]
    nb = N // BLOCK
    L = kc.shape[1]
    qb = q.reshape(B, nb, BLOCK, A_KV_HEADS, A_GROUP, HEAD_DIM)

    def band(t):
        tp = jnp.pad(t, ((0, 0), (BLOCK, BLOCK), (0, 0), (0, 0))).reshape(B, nb + 2, BLOCK, A_KV_HEADS, HEAD_DIM)
        return jnp.concatenate([tp[:, :-2], tp[:, 1:-1], tp[:, 2:]], axis=2)

    kw, vw = band(k), band(v)
    qpos = jnp.arange(N).reshape(nb, BLOCK)
    kpos = (jnp.arange(nb)[:, None] - 1) * BLOCK + jnp.arange(3 * BLOCK)[None, :]
    valid = ((jnp.abs(qpos[:, :, None] - kpos[:, None, :]) <= WINDOW)
             & (kpos[:, None, :] >= 0) & (kpos[:, None, :] < N))
    s_band = jnp.einsum('bnqhgd,bnkhd->bnhgqk', qb, kw).astype(jnp.float32) * ATTN_SCALE
    s_band = jnp.where(valid[None, :, None, None], s_band, NEG_INF)
    s_ctx = jnp.einsum('bnqhgd,blhd->bnhgql', qb, kc).astype(jnp.float32) * ATTN_SCALE
    s_sink = jnp.broadcast_to(sink.astype(jnp.float32).reshape(1, 1, A_KV_HEADS, A_GROUP, 1, 1),
                              (B, nb, A_KV_HEADS, A_GROUP, BLOCK, 1))
    pr = jax.nn.softmax(jnp.concatenate([s_band, s_ctx, s_sink], axis=-1), axis=-1).astype(v.dtype)
    o = (jnp.einsum('bnhgqk,bnkhd->bnqhgd', pr[..., :3 * BLOCK], vw)
         + jnp.einsum('bnhgql,blhd->bnqhgd', pr[..., 3 * BLOCK:3 * BLOCK + L], vc))
    return o.reshape(B, N, A_Q_DIM)


def _ctx_attention(q, k, v, sink):
    B, L = q.shape[0], q.shape[1]
    qg = q.reshape(B, L, A_KV_HEADS, A_GROUP, HEAD_DIM)
    s = jnp.einsum('blhgd,bmhd->bhglm', qg, k).astype(jnp.float32) * ATTN_SCALE
    s_sink = jnp.broadcast_to(sink.astype(jnp.float32).reshape(1, A_KV_HEADS, A_GROUP, 1, 1),
                              (B, A_KV_HEADS, A_GROUP, L, 1))
    pr = jax.nn.softmax(jnp.concatenate([s, s_sink], axis=-1), axis=-1)[..., :L].astype(v.dtype)
    o = jnp.einsum('bhglm,bmhd->blhgd', pr, v)
    return o.reshape(B, L, A_Q_DIM)


def _pool_mix(z, w_pool, pool_scale):
    B, N = z.shape[0], z.shape[1]
    zf = z.astype(jnp.float32)
    cs = jnp.concatenate([jnp.zeros((B, 1, B_WIDTH), jnp.float32), jnp.cumsum(zf, axis=1)], axis=1)
    cs = cs.reshape(B, N + 1, B_GROUPS, B_GROUP_DIM)
    t = jnp.arange(N)
    half = jnp.array(POOL_WINDOWS, dtype=jnp.int32) // 2
    lo = jnp.clip(t[:, None] - half[None, :], 0, N)
    hi = jnp.clip(t[:, None] + half[None, :], 0, N)
    gidx = jnp.arange(B_GROUPS)[None, :]
    mean = (cs[:, hi, gidx] - cs[:, lo, gidx]) / (hi - lo).astype(jnp.float32)[None, :, :, None]
    d = (mean - zf.reshape(B, N, B_GROUPS, B_GROUP_DIM)).astype(z.dtype)
    y = jnp.einsum('bngc,gcd->bngd', d, w_pool).reshape(B, N, B_WIDTH)
    return y * pool_scale


def _even_mix(h, hc, ang_r, ang_c, w_in, qn_g, kn_g, sink, w_pool, pool_scale, w_out, need_ctx):
    B, N = h.shape[0], h.shape[1]
    L = hc.shape[1]
    p = h @ w_in
    q = _rms(p[..., :A_Q_DIM].reshape(B, N, A_Q_HEADS, HEAD_DIM), qn_g)
    k = _rms(p[..., A_Q_DIM:A_Q_DIM + A_KV_DIM].reshape(B, N, A_KV_HEADS, HEAD_DIM), kn_g)
    v = p[..., A_Q_DIM + A_KV_DIM:A_Q_DIM + 2 * A_KV_DIM].reshape(B, N, A_KV_HEADS, HEAD_DIM)
    q = _rope_2d(q, ang_r, ang_c)
    k = _rope_2d(k, ang_r, ang_c)
    pkv = hc @ w_in[:, A_Q_DIM:A_Q_DIM + 2 * A_KV_DIM]
    kc = _rms(pkv[..., :A_KV_DIM].reshape(B, L, A_KV_HEADS, HEAD_DIM), kn_g)
    vc = pkv[..., A_KV_DIM:].reshape(B, L, A_KV_HEADS, HEAD_DIM)
    y = jnp.concatenate([_window_attention(q, k, v, kc, vc, sink),
                         _pool_mix(p[..., A_Q_DIM + 2 * A_KV_DIM:], w_pool, pool_scale)], axis=-1) @ w_out
    if not need_ctx:
        return y, None
    qc = _rms((hc @ w_in[:, :A_Q_DIM]).reshape(B, L, A_Q_HEADS, HEAD_DIM), qn_g)
    yc = jnp.concatenate([_ctx_attention(qc, kc, vc, sink),
                          _pool_mix(hc @ w_in[:, A_Q_DIM + 2 * A_KV_DIM:], w_pool, pool_scale)], axis=-1) @ w_out
    return y, yc


def _fourier_mix(f, w_fourier):
    B, N = f.shape[0], f.shape[1]
    fg = f.astype(jnp.float32).reshape(B, N, D_GROUPS, D_GROUP_DIM)
    z = jnp.fft.fftn(fg, axes=(1, 3), norm='ortho').real.astype(f.dtype)
    return z.reshape(B, N, D_WIDTH) @ w_fourier


def _odd_mix(h, w_in, v_norm_g, w_spatial, b_spatial, w_fourier, w_out):
    B, N = h.shape[0], h.shape[1]
    nc = N // CHUNK
    p = h @ w_in
    u = jax.nn.gelu(p[..., :C_WIDTH], approximate=False)
    v = _rms(jax.nn.gelu(p[..., C_WIDTH:2 * C_WIDTH], approximate=False), v_norm_g)
    vc = v.reshape(B, nc, CHUNK, C_GROUPS, C_GROUP_DIM)
    s = jnp.einsum('gpq,bkqgc->bkpgc', w_spatial, vc) + b_spatial.T[:, :, None]
    c_out = u * s.reshape(B, N, C_WIDTH)
    d_out = _fourier_mix(p[..., 2 * C_WIDTH:], w_fourier)
    return jnp.concatenate([c_out, d_out], axis=-1) @ w_out


def _dwconv3(u, w, b):
    up = jnp.pad(u, ((0, 0), (1, 1), (0, 0)))
    return up[:, :-2] * w[0] + up[:, 1:-1] * w[1] + up[:, 2:] * w[2] + b


def _conv_ffn(h, w_up, conv_w, conv_b, w_down):
    u = _dwconv3(h @ w_up, conv_w, conv_b)
    g, val = jnp.split(u, 2, axis=-1)
    return (jax.nn.silu(g) * val) @ w_down


def setup_inputs(seed: int = 0) -> dict:
    key = jax.random.key(seed)
    ks = iter(jax.random.split(key, 32))

    def nrm(shape, scale):
        return jax.random.normal(next(ks), shape, jnp.float32) * scale

    def gain(shape):
        return 1.0 + nrm(shape, 0.02)

    return {
        'x': nrm((BATCH, SEQ, D_MODEL), 1.0),
        'c': nrm((BATCH, D_MODEL), 1.0),
        'ctx': nrm((BATCH, CTX_LEN, D_MODEL), 1.0),
        'c_ctx': nrm((D_MODEL,), 1.0),
        'w_mod': nrm((DEPTH, D_MODEL, 6 * D_MODEL), D_MODEL ** -0.5),
        'b_mod': nrm((DEPTH, 6 * D_MODEL), 0.02),
        'norm1_g': gain((DEPTH, D_MODEL)),
        'norm2_g': gain((DEPTH, D_MODEL)),
        'ab_w_in': nrm((N_EVEN, D_MODEL, AB_IN), D_MODEL ** -0.5),
        'a_q_norm_g': gain((N_EVEN, HEAD_DIM)),
        'a_k_norm_g': gain((N_EVEN, HEAD_DIM)),
        'a_sink': nrm((N_EVEN, A_Q_HEADS), 0.5),
        'b_w_pool': nrm((N_EVEN, B_GROUPS, B_GROUP_DIM, B_GROUP_DIM), B_GROUP_DIM ** -0.5),
        'b_pool_scale': gain((N_EVEN, B_WIDTH)),
        'ab_w_out': nrm((N_EVEN, MIX_WIDTH, D_MODEL), MIX_WIDTH ** -0.5),
        'cd_w_in': nrm((N_ODD, D_MODEL, CD_IN), D_MODEL ** -0.5),
        'c_v_norm_g': gain((N_ODD, C_WIDTH)),
        'c_w_spatial': nrm((N_ODD, C_GROUPS, CHUNK, CHUNK), CHUNK ** -0.5),
        'c_b_spatial': gain((N_ODD, C_GROUPS, CHUNK)),
        'd_w_fourier': nrm((N_ODD, D_WIDTH, D_WIDTH), D_WIDTH ** -0.5),
        'cd_w_out': nrm((N_ODD, MIX_WIDTH, D_MODEL), MIX_WIDTH ** -0.5),
        'f_w_up': nrm((DEPTH, D_MODEL, 2 * D_FF), D_MODEL ** -0.5),
        'f_conv_w': nrm((DEPTH, 3, 2 * D_FF), 3 ** -0.5),
        'f_conv_b': nrm((DEPTH, 2 * D_FF), 0.02),
        'f_w_down': nrm((DEPTH, D_FF, D_MODEL), D_FF ** -0.5),
    }


def reference(x, c, ctx, c_ctx, w_mod, b_mod, norm1_g, norm2_g, ab_w_in, a_q_norm_g, a_k_norm_g, a_sink,
              b_w_pool, b_pool_scale, ab_w_out, cd_w_in, c_v_norm_g, c_w_spatial, c_b_spatial, d_w_fourier,
              cd_w_out, f_w_up, f_conv_w, f_conv_b, f_w_down):
    ang_r, ang_c = _axial_angles(x.shape[1])
    for layer in range(DEPTH):
        need_ctx = layer < DEPTH - 1
        is_even = layer % 2 == 0
        i = layer // 2
        mod = jax.nn.silu(c) @ w_mod[layer] + b_mod[layer]
        ml = jnp.split(mod[:, None, :], 6, axis=-1)
        mod_c = jax.nn.silu(c_ctx) @ w_mod[layer] + b_mod[layer]
        mc = jnp.split(mod_c[None, None, :], 6, axis=-1)
        h = _modulate(_rms(x, norm1_g[layer]), ml[0], ml[1])
        if is_even:
            hc = _modulate(_rms(ctx, norm1_g[layer]), mc[0], mc[1])
            y, yc = _even_mix(h, hc, ang_r, ang_c, ab_w_in[i], a_q_norm_g[i], a_k_norm_g[i], a_sink[i],
                              b_w_pool[i], b_pool_scale[i], ab_w_out[i], need_ctx)
        else:
            y = _odd_mix(h, cd_w_in[i], c_v_norm_g[i], c_w_spatial[i], c_b_spatial[i], d_w_fourier[i], cd_w_out[i])
            if need_ctx:
                hc = _modulate(_rms(ctx, norm1_g[layer]), mc[0], mc[1])
                yc = _odd_mix(hc, cd_w_in[i], c_v_norm_g[i], c_w_spatial[i], c_b_spatial[i], d_w_fourier[i], cd_w_out[i])
        x = x + ml[2] * y
        x = x + ml[5] * _conv_ffn(_modulate(_rms(x, norm2_g[layer]), ml[3], ml[4]),
                                  f_w_up[layer], f_conv_w[layer], f_conv_b[layer], f_w_down[layer])
        if need_ctx:
            ctx = ctx + mc[2] * yc
            ctx = ctx + mc[5] * _conv_ffn(_modulate(_rms(ctx, norm2_g[layer]), mc[3], mc[4]),
                                          f_w_up[layer], f_conv_w[layer], f_conv_b[layer], f_w_down[layer])
    return x
```

```python
import functools

import numpy as np
import jax
import jax.numpy as jnp
from jax import lax
from jax.experimental import pallas as pl
from jax.experimental.pallas import tpu as pltpu

F32 = jnp.float32
BF16 = jnp.bfloat16

D_MODEL = 2048
DEPTH = 2
GRID_W = 64
HEAD_DIM = 128
A_Q_HEADS = 8
A_KV_HEADS = 2
A_GROUP = A_Q_HEADS // A_KV_HEADS
A_Q_DIM = A_Q_HEADS * HEAD_DIM
A_KV_DIM = A_KV_HEADS * HEAD_DIM
A_QKV_DIM = A_Q_DIM + 2 * A_KV_DIM
WINDOW = 128
BLOCK = 128
ROPE_THETA = 10000.0
AXIS_DIM = HEAD_DIM // 2
ATTN_SCALE = HEAD_DIM ** -0.5
NEG_INF = -1e30
B_GROUPS = 4
B_WIDTH = 1024
B_GROUP_DIM = B_WIDTH // B_GROUPS
POOL_WINDOWS = (2, 4, 8, 16)
AB_IN = A_QKV_DIM + B_WIDTH
C_WIDTH = 1024
C_GROUPS = 4
C_GROUP_DIM = C_WIDTH // C_GROUPS
CHUNK = 128
D_WIDTH = 1024
D_GROUPS = 8
D_GROUP_DIM = D_WIDTH // D_GROUPS
CD_IN = 2 * C_WIDTH + D_WIDTH
D_FF = 5632
EPS = 1e-6

V7X_SUBLANES_F32 = 8
V7X_SUBLANES_BF16 = 16
V7X_LANES = 128
V7X_VMEM_BYTES = 64 * 1024 * 1024
VMEM_LIMIT = 56 * 1024 * 1024

FFT_A = 64
FFT_B = 128


def _params(sem):
    return pltpu.CompilerParams(dimension_semantics=sem, vmem_limit_bytes=VMEM_LIMIT)


def _dot(a, b):
    return jnp.dot(a, b, preferred_element_type=F32)


def _norm_mod(x, g, shift, scale):
    ms = jnp.mean(x * x, axis=-1, keepdims=True)
    y = x * lax.rsqrt(ms + EPS) * g
    return y * (1.0 + scale) + shift


def _mod_kernel(cv_ref, w_ref, b_ref, o_ref):
    a = cv_ref[...]
    a = a * jax.nn.sigmoid(a)
    w = w_ref[0]
    a_hi = a.astype(BF16)
    a_lo = (a - a_hi.astype(F32)).astype(BF16)
    w_hi = w.astype(BF16)
    w_lo = (w - w_hi.astype(F32)).astype(BF16)
    acc = _dot(a_hi, w_hi) + _dot(a_lo, w_hi) + _dot(a_hi, w_lo)
    o_ref[0] = acc + b_ref[0]


def _mod_vectors(c, c_ctx, w_mod, b_mod):
    tn = 1024
    cv = jnp.concatenate(
        [c.reshape(1, D_MODEL), c_ctx.reshape(1, D_MODEL),
         jnp.zeros((V7X_SUBLANES_F32 - 2, D_MODEL), F32)], axis=0)
    b3 = b_mod.reshape(DEPTH, 1, 6 * D_MODEL)
    return pl.pallas_call(
        _mod_kernel,
        grid=(DEPTH, 6 * D_MODEL // tn),
        in_specs=[
            pl.BlockSpec((V7X_SUBLANES_F32, D_MODEL), lambda l, j: (0, 0)),
            pl.BlockSpec((1, D_MODEL, tn), lambda l, j: (l, 0, j)),
            pl.BlockSpec((1, 1, tn), lambda l, j: (l, 0, j)),
        ],
        out_specs=pl.BlockSpec((1, V7X_SUBLANES_F32, tn), lambda l, j: (l, 0, j)),
        out_shape=jax.ShapeDtypeStruct((DEPTH, V7X_SUBLANES_F32, 6 * D_MODEL), F32),
        compiler_params=_params(("arbitrary", "arbitrary")),
        name="mod_vectors",
    )(cv, w_mod, b3)


AB_TN = 512


def _rope(t, cos, sin_signed):
    lane = lax.broadcasted_iota(jnp.int32, t.shape, 1)
    first = (lane % AXIS_DIM) < (AXIS_DIM // 2)
    partner = jnp.where(first,
                        pltpu.roll(t, HEAD_DIM - AXIS_DIM // 2, 1),
                        pltpu.roll(t, AXIS_DIM // 2, 1))
    return t * cos + partner * sin_signed


def _head_norm_rope(t, g, cos, sin_signed):
    ms = jnp.mean(t * t, axis=-1, keepdims=True)
    return _rope(t * lax.rsqrt(ms + EPS) * g, cos, sin_signed)


def _ab_in_kernel(x_ref, g_ref, sh_ref, sc_ref, w_ref, cos_ref, sin_ref, qn_ref, kn_ref,
                  qkv_ref, z_ref, h_scr):
    j = pl.program_id(1)

    @pl.when(j == 0)
    def _():
        h_scr[...] = _norm_mod(x_ref[...], g_ref[...], sh_ref[...], sc_ref[...]).astype(BF16)

    p = _dot(h_scr[...], w_ref[...])
    cos = cos_ref[...]
    sin = sin_ref[...]
    heads_per_tile = AB_TN // HEAD_DIM

    @pl.when(j < A_Q_DIM // AB_TN)
    def _():
        for hh in range(heads_per_tile):
            sl = slice(hh * HEAD_DIM, (hh + 1) * HEAD_DIM)
            qkv_ref[:, sl] = _head_norm_rope(p[:, sl], qn_ref[...], cos, sin).astype(BF16)

    @pl.when(j == A_Q_DIM // AB_TN)
    def _():
        for hh in range(A_KV_HEADS):
            sl = slice(hh * HEAD_DIM, (hh + 1) * HEAD_DIM)
            qkv_ref[:, sl] = _head_norm_rope(p[:, sl], kn_ref[...], cos, sin).astype(BF16)
        qkv_ref[:, A_KV_DIM:] = p[:, A_KV_DIM:].astype(BF16)

    @pl.when(j > A_Q_DIM // AB_TN)
    def _():
        z_ref[...] = p


def _ab_in_proj(x2, g, shift, scale, w_in_bf, cos, sin, qn, kn, tm):
    n = x2.shape[0]
    assert A_Q_DIM % AB_TN == 0 and 2 * A_KV_DIM == AB_TN and B_WIDTH % AB_TN == 0
    n_qkv_tiles = A_QKV_DIM // AB_TN
    row = lambda i, j: (0, 0)
    return pl.pallas_call(
        _ab_in_kernel,
        grid=(n // tm, AB_IN // AB_TN),
        in_specs=[
            pl.BlockSpec((tm, D_MODEL), lambda i, j: (i, 0)),
            pl.BlockSpec((1, D_MODEL), row),
            pl.BlockSpec((1, D_MODEL), row),
            pl.BlockSpec((1, D_MODEL), row),
            pl.BlockSpec((D_MODEL, AB_TN), lambda i, j: (0, j)),
            pl.BlockSpec((tm, HEAD_DIM), lambda i, j: (i, 0)),
            pl.BlockSpec((tm, HEAD_DIM), lambda i, j: (i, 0)),
            pl.BlockSpec((1, HEAD_DIM), row),
            pl.BlockSpec((1, HEAD_DIM), row),
        ],
        out_specs=[
            pl.BlockSpec((tm, AB_TN), lambda i, j: (i, jnp.minimum(j, n_qkv_tiles - 1))),
            pl.BlockSpec((tm, AB_TN), lambda i, j: (i, jnp.maximum(j - n_qkv_tiles, 0))),
        ],
        out_shape=[jax.ShapeDtypeStruct((n, A_QKV_DIM), BF16),
                   jax.ShapeDtypeStruct((n, B_WIDTH), F32)],
        scratch_shapes=[pltpu.VMEM((tm, D_MODEL), BF16)],
        compiler_params=_params(("arbitrary", "arbitrary")),
        name="ab_in_proj",
    )(x2, g, shift, scale, w_in_bf, cos, sin, qn, kn)


def _attn_kernel(sink_ref, q_ref, kp_ref, kc_ref, kn_ref, vp_ref, vc_ref, vn_ref,
                 kx_ref, vx_ref, o_ref, *, n_blocks, ctx_len):
    nb = pl.program_id(0)
    hk = pl.program_id(1)
    rows = A_GROUP * BLOCK
    cols = 3 * BLOCK + ctx_len
    kcat = jnp.concatenate([kp_ref[...], kc_ref[...], kn_ref[...], kx_ref[...]], axis=0)
    vcat = jnp.concatenate([vp_ref[...], vc_ref[...], vn_ref[...], vx_ref[...]], axis=0)
    q = q_ref[...]
    q4 = jnp.concatenate([q[:, g * HEAD_DIM:(g + 1) * HEAD_DIM] for g in range(A_GROUP)], axis=0)
    s = lax.dot_general(q4, kcat, (((1,), (1,)), ((), ())), preferred_element_type=F32) * ATTN_SCALE

    r = lax.broadcasted_iota(jnp.int32, (rows, cols), 0)
    c = lax.broadcasted_iota(jnp.int32, (rows, cols), 1)
    rel = (BLOCK + r % BLOCK) - c
    kpos = (nb - 1) * BLOCK + c
    in_band = (jnp.abs(rel) <= WINDOW) & (kpos >= 0) & (kpos < n_blocks * BLOCK)
    valid = (c >= 3 * BLOCK) | in_band
    s = jnp.where(valid, s, NEG_INF)

    r1 = lax.broadcasted_iota(jnp.int32, (rows, 1), 0) // BLOCK
    sink = jnp.zeros((rows, 1), F32)
    for g in range(A_GROUP):
        sink = jnp.where(r1 == g, sink_ref[hk * A_GROUP + g], sink)

    m = jnp.maximum(jnp.max(s, axis=-1, keepdims=True), sink)
    e = jnp.exp(s - m)
    den = jnp.sum(e, axis=-1, keepdims=True) + jnp.exp(sink - m)
    pr = (e / den).astype(BF16)
    o4 = _dot(pr, vcat)
    for g in range(A_GROUP):
        o_ref[:, g * HEAD_DIM:(g + 1) * HEAD_DIM] = o4[g * BLOCK:(g + 1) * BLOCK].astype(BF16)


def _window_attention(qkv, qkv_ctx, sink):
    n = qkv.shape[0]
    ctx_len = qkv_ctx.shape[0]
    n_blocks = n // BLOCK
    k_col = A_Q_DIM // HEAD_DIM
    v_col = (A_Q_DIM + A_KV_DIM) // HEAD_DIM
    prev = lambda i: jnp.maximum(i - 1, 0)
    nxt = lambda i: jnp.minimum(i + 1, n_blocks - 1)
    blk = (BLOCK, HEAD_DIM)
    return pl.pallas_call(
        functools.partial(_attn_kernel, n_blocks=n_blocks, ctx_len=ctx_len),
        grid=(n_blocks, A_KV_HEADS),
        in_specs=[
            pl.BlockSpec(memory_space=pltpu.SMEM),
            pl.BlockSpec((BLOCK, A_GROUP * HEAD_DIM), lambda i, h: (i, h)),
            pl.BlockSpec(blk, lambda i, h: (prev(i), k_col + h)),
            pl.BlockSpec(blk, lambda i, h: (i, k_col + h)),
            pl.BlockSpec(blk, lambda i, h: (nxt(i), k_col + h)),
            pl.BlockSpec(blk, lambda i, h: (prev(i), v_col + h)),
            pl.BlockSpec(blk, lambda i, h: (i, v_col + h)),
            pl.BlockSpec(blk, lambda i, h: (nxt(i), v_col + h)),
            pl.BlockSpec((ctx_len, HEAD_DIM), lambda i, h: (0, k_col + h)),
            pl.BlockSpec((ctx_len, HEAD_DIM), lambda i, h: (0, v_col + h)),
        ],
        out_specs=pl.BlockSpec((BLOCK, A_GROUP * HEAD_DIM), lambda i, h: (i, h)),
        out_shape=jax.ShapeDtypeStruct((n, A_Q_DIM), BF16),
        compiler_params=_params(("arbitrary", "arbitrary")),
        name="window_attention",
    )(sink, qkv, qkv, qkv, qkv, qkv, qkv, qkv, qkv_ctx, qkv_ctx)


POOL_HALO = 8
assert max(POOL_WINDOWS) // 2 <= POOL_HALO


def _pool_kernel(zm_ref, zp_ref, zn_ref, w_ref, ps_ref, o_ref, z_scr, *, n_rows, tm):
    i = pl.program_id(0)
    last = pl.num_programs(0) - 1
    z_scr[0:POOL_HALO, :] = jnp.where(i > 0, zp_ref[...], 0.0)
    z_scr[POOL_HALO:POOL_HALO + tm, :] = zm_ref[...]
    z_scr[POOL_HALO + tm:, :] = jnp.where(i < last, zn_ref[...], 0.0)
    t = i * tm + lax.broadcasted_iota(jnp.int32, (tm, B_GROUP_DIM), 0)
    for g in range(B_GROUPS):
        half = POOL_WINDOWS[g] // 2
        cs = slice(g * B_GROUP_DIM, (g + 1) * B_GROUP_DIM)
        acc = z_scr[pl.ds(POOL_HALO - half, tm), cs]
        for off in range(-half + 1, half):
            acc = acc + z_scr[pl.ds(POOL_HALO + off, tm), cs]
        cnt = (jnp.minimum(t + half, n_rows) - jnp.maximum(t - half, 0)).astype(F32)
        d = (acc / cnt - zm_ref[:, cs]).astype(BF16)
        y = _dot(d, w_ref[g]) * ps_ref[:, cs]
        o_ref[:, cs] = y.astype(BF16)


def _pool_mix(z, w_pool_bf, pool_scale, tm):
    n = z.shape[0]
    hb = tm // POOL_HALO
    n_halo_blocks = n // POOL_HALO
    return pl.pallas_call(
        functools.partial(_pool_kernel, n_rows=n, tm=tm),
        grid=(n // tm,),
        in_specs=[
            pl.BlockSpec((tm, B_WIDTH), lambda i: (i, 0)),
            pl.BlockSpec((POOL_HALO, B_WIDTH), lambda i: (jnp.maximum(i * hb - 1, 0), 0)),
            pl.BlockSpec((POOL_HALO, B_WIDTH), lambda i: (jnp.minimum((i + 1) * hb, n_halo_blocks - 1), 0)),
            pl.BlockSpec((B_GROUPS, B_GROUP_DIM, B_GROUP_DIM), lambda i: (0, 0, 0)),
            pl.BlockSpec((1, B_WIDTH), lambda i: (0, 0)),
        ],
        out_specs=pl.BlockSpec((tm, B_WIDTH), lambda i: (i, 0)),
        out_shape=jax.ShapeDtypeStruct((n, B_WIDTH), BF16),
        scratch_shapes=[pltpu.VMEM((tm + 2 * POOL_HALO, B_WIDTH), F32)],
        compiler_params=_params(("arbitrary",)),
        name="pool_mix",
    )(z, z, z, w_pool_bf, pool_scale)


def _out_proj_kernel(a1_ref, a2_ref, w1_ref, w2_ref, x_ref, gate_ref, o_ref):
    y = _dot(a1_ref[...].astype(BF16), w1_ref[...]) + _dot(a2_ref[...].astype(BF16), w2_ref[...])
    o_ref[...] = x_ref[...] + gate_ref[...] * y


def _out_proj(a1, a2, w_out_bf, x2, gate, tm, tn=512):
    n = x2.shape[0]
    k1 = a1.shape[1]
    k2 = a2.shape[1]
    assert k1 == k2 and k1 + k2 == w_out_bf.shape[0]
    return pl.pallas_call(
        _out_proj_kernel,
        grid=(n // tm, D_MODEL // tn),
        in_specs=[
            pl.BlockSpec((tm, k1), lambda i, j: (i, 0)),
            pl.BlockSpec((tm, k2), lambda i, j: (i, 0)),
            pl.BlockSpec((k1, tn), lambda i, j: (0, j)),
            pl.BlockSpec((k2, tn), lambda i, j: (1, j)),
            pl.BlockSpec((tm, tn), lambda i, j: (i, j)),
            pl.BlockSpec((1, tn), lambda i, j: (0, j)),
        ],
        out_specs=pl.BlockSpec((tm, tn), lambda i, j: (i, j)),
        out_shape=jax.ShapeDtypeStruct((n, D_MODEL), F32),
        compiler_params=_params(("arbitrary", "arbitrary")),
        name="out_proj",
    )(a1, a2, w_out_bf, w_out_bf, x2, gate)


FFN_HALO = V7X_SUBLANES_BF16
FFN_TF = 512


def _ffn_kernel(xm_ref, xp_ref, xn_ref, g_ref, sh_ref, sc_ref, wg_ref, wv_ref,
                cwg_ref, cwv_ref, cbg_ref, cbv_ref, wd_ref, gate_ref, o_ref, h_scr, *, tm):
    i = pl.program_id(0)
    j = pl.program_id(1)
    last_i = pl.num_programs(0) - 1
    last_j = pl.num_programs(1) - 1
    hl = FFN_HALO

    @pl.when(j == 0)
    def _():
        nm = lambda v: _norm_mod(v, g_ref[...], sh_ref[...], sc_ref[...])
        h_scr[0:hl, :] = jnp.where(i > 0, nm(xp_ref[...]), 0.0).astype(BF16)
        h_scr[hl:hl + tm, :] = nm(xm_ref[...]).astype(BF16)
        h_scr[hl + tm:, :] = jnp.where(i < last_i, nm(xn_ref[...]), 0.0).astype(BF16)

    hext = h_scr[...]

    def conv(u, cw_ref, cb_ref):
        return (u[hl - 1:hl - 1 + tm] * cw_ref[0:1, :] + u[hl:hl + tm] * cw_ref[1:2, :]
                + u[hl + 1:hl + 1 + tm] * cw_ref[2:3, :] + cb_ref[...])

    gg = conv(_dot(hext, wg_ref[...]), cwg_ref, cbg_ref)
    vv = conv(_dot(hext, wv_ref[...]), cwv_ref, cbv_ref)
    act = (gg * jax.nn.sigmoid(gg) * vv).astype(BF16)
    contrib = _dot(act, wd_ref[...])

    @pl.when(j == 0)
    def _():
        o_ref[...] = contrib

    @pl.when(j > 0)
    def _():
        o_ref[...] += contrib

    @pl.when(j == last_j)
    def _():
        o_ref[...] = xm_ref[...] + gate_ref[...] * o_ref[...]


def _conv_ffn(x2, g, shift, scale, w_up_bf, conv_w, conv_b, w_down_bf, gate, tm):
    n = x2.shape[0]
    tf = FFN_TF
    nj = D_FF // tf
    hb = tm // FFN_HALO
    n_halo_blocks = n // FFN_HALO
    row = lambda i, j: (0, 0)
    conv_b2 = conv_b.reshape(1, 2 * D_FF)
    return pl.pallas_call(
        functools.partial(_ffn_kernel, tm=tm),
        grid=(n // tm, nj),
        in_specs=[
            pl.BlockSpec((tm, D_MODEL), lambda i, j: (i, 0)),
            pl.BlockSpec((FFN_HALO, D_MODEL), lambda i, j: (jnp.maximum(i * hb - 1, 0), 0)),
            pl.BlockSpec((FFN_HALO, D_MODEL), lambda i, j: (jnp.minimum((i + 1) * hb, n_halo_blocks - 1), 0)),
            pl.BlockSpec((1, D_MODEL), row),
            pl.BlockSpec((1, D_MODEL), row),
            pl.BlockSpec((1, D_MODEL), row),
            pl.BlockSpec((D_MODEL, tf), lambda i, j: (0, j)),
            pl.BlockSpec((D_MODEL, tf), lambda i, j: (0, j + nj)),
            pl.BlockSpec((3, tf), lambda i, j: (0, j)),
            pl.BlockSpec((3, tf), lambda i, j: (0, j + nj)),
            pl.BlockSpec((1, tf), lambda i, j: (0, j)),
            pl.BlockSpec((1, tf), lambda i, j: (0, j + nj)),
            pl.BlockSpec((tf, D_MODEL), lambda i, j: (j, 0)),
            pl.BlockSpec((1, D_MODEL), row),
        ],
        out_specs=pl.BlockSpec((tm, D_MODEL), lambda i, j: (i, 0)),
        out_shape=jax.ShapeDtypeStruct((n, D_MODEL), F32),
        scratch_shapes=[pltpu.VMEM((tm + 2 * FFN_HALO, D_MODEL), BF16)],
        compiler_params=_params(("arbitrary", "arbitrary")),
        name="conv_ffn",
    )(x2, x2, x2, g, shift, scale, w_up_bf, w_up_bf, conv_w, conv_w, conv_b2, conv_b2,
      w_down_bf, gate)


def _gelu(x):
    return 0.5 * x * (1.0 + lax.erf(x * (2.0 ** -0.5)))


def _cd_in_kernel(x_ref, g_ref, sh_ref, sc_ref, w_ref, vg_ref, uv_ref, f_ref, h_scr):
    j = pl.program_id(1)

    @pl.when(j == 0)
    def _():
        h_scr[...] = _norm_mod(x_ref[...], g_ref[...], sh_ref[...], sc_ref[...]).astype(BF16)

    p = _dot(h_scr[...], w_ref[...])

    @pl.when(j == 0)
    def _():
        uv_ref[...] = _gelu(p).astype(BF16)

    @pl.when(j == 1)
    def _():
        v = _gelu(p)
        ms = jnp.mean(v * v, axis=-1, keepdims=True)
        uv_ref[...] = (v * lax.rsqrt(ms + EPS) * vg_ref[...]).astype(BF16)

    @pl.when(j == 2)
    def _():
        f_ref[...] = p


def _cd_in_proj(x2, g, shift, scale, w_in_bf, v_norm_g, tm):
    n = x2.shape[0]
    assert C_WIDTH == D_WIDTH
    row = lambda i, j: (0, 0)
    return pl.pallas_call(
        _cd_in_kernel,
        grid=(n // tm, CD_IN // C_WIDTH),
        in_specs=[
            pl.BlockSpec((tm, D_MODEL), lambda i, j: (i, 0)),
            pl.BlockSpec((1, D_MODEL), row),
            pl.BlockSpec((1, D_MODEL), row),
            pl.BlockSpec((1, D_MODEL), row),
            pl.BlockSpec((D_MODEL, C_WIDTH), lambda i, j: (0, j)),
            pl.BlockSpec((1, C_WIDTH), row),
        ],
        out_specs=[
            pl.BlockSpec((tm, C_WIDTH), lambda i, j: (i, jnp.minimum(j, 1))),
            pl.BlockSpec((tm, D_WIDTH), lambda i, j: (i, 0)),
        ],
        out_shape=[jax.ShapeDtypeStruct((n, 2 * C_WIDTH), BF16),
                   jax.ShapeDtypeStruct((n, D_WIDTH), F32)],
        scratch_shapes=[pltpu.VMEM((tm, D_MODEL), BF16)],
        compiler_params=_params(("arbitrary", "arbitrary")),
        name="cd_in_proj",
    )(x2, g, shift, scale, w_in_bf, v_norm_g)


def _spatial_kernel(u_ref, v_ref, ws_ref, bias_ref, o_ref, *, chunks):
    for k in range(chunks):
        rs = slice(k * CHUNK, (k + 1) * CHUNK)
        for g in range(C_GROUPS):
            cs = slice(g * C_GROUP_DIM, (g + 1) * C_GROUP_DIM)
            s = _dot(ws_ref[g], v_ref[rs, cs]) + bias_ref[:, cs]
            o_ref[rs, cs] = (u_ref[rs, cs].astype(F32) * s).astype(BF16)


def _spatial_gate(uv, w_spatial_bf, bias_full, chunks=4):
    n = uv.shape[0]
    rows = chunks * CHUNK
    return pl.pallas_call(
        functools.partial(_spatial_kernel, chunks=chunks),
        grid=(n // rows,),
        in_specs=[
            pl.BlockSpec((rows, C_WIDTH), lambda i: (i, 0)),
            pl.BlockSpec((rows, C_WIDTH), lambda i: (i, 1)),
            pl.BlockSpec((C_GROUPS, CHUNK, CHUNK), lambda i: (0, 0, 0)),
            pl.BlockSpec((CHUNK, C_WIDTH), lambda i: (0, 0)),
        ],
        out_specs=pl.BlockSpec((rows, C_WIDTH), lambda i: (i, 0)),
        out_shape=jax.ShapeDtypeStruct((n, C_WIDTH), BF16),
        compiler_params=_params(("arbitrary",)),
        name="spatial_gate",
    )(uv, uv, w_spatial_bf, bias_full)


FFT_SUB = V7X_SUBLANES_F32
FFT1_CT = 512
FFT2_CT = 512


def _fourier_tables(n):
    a_len, b_len, sub = FFT_A, FFT_B, FFT_SUB
    assert a_len * b_len == n
    ch = np.arange(D_GROUP_DIM)
    ang_c = 2.0 * np.pi * ((ch[:, None] * ch[None, :]) % D_GROUP_DIM) / D_GROUP_DIM
    a = np.arange(a_len)
    f_a = np.exp(-2j * np.pi * ((a[:, None] * a[None, :]) % a_len) / a_len)
    m1 = np.kron(f_a, np.eye(sub))
    b = np.arange(b_len)
    tw = np.exp(-2j * np.pi * ((a[:, None] * b[None, :]) % n) / n)
    f_b = np.exp(-2j * np.pi * ((b[:, None] * b[None, :]) % b_len) / b_len)
    m2 = np.einsum('db,pq->dpqb', f_b, np.eye(sub)).reshape(b_len * sub, sub * b_len)
    norm = 1.0 / np.sqrt(float(n) * D_GROUP_DIM)
    m2 = m2 * norm
    tw3 = np.broadcast_to(tw[:, :, None], (a_len, b_len, V7X_LANES))
    f32 = lambda v: jnp.asarray(np.ascontiguousarray(v), dtype=F32)
    return dict(cos_c=f32(np.cos(ang_c)), sin_c=f32(np.sin(ang_c)),
                m1r=f32(m1.real), m1i=f32(m1.imag), m2r=f32(m2.real), m2i=f32(m2.imag),
                twr=f32(tw3.real), twi=f32(tw3.imag))


def _fft1_kernel(f_ref, cc_ref, sc_ref, m1r_ref, m1i_ref, twr_ref, twi_ref, tr_ref, ti_ref):
    rows = FFT_A * FFT_SUB
    ct = FFT1_CT
    fb = f_ref[...].reshape(rows, ct).astype(BF16)
    xr_parts = []
    xi_parts = []
    for q in range(ct // D_GROUP_DIM):
        blk = fb[:, q * D_GROUP_DIM:(q + 1) * D_GROUP_DIM]
        xr_parts.append(_dot(blk, cc_ref[...]))
        xi_parts.append(-_dot(blk, sc_ref[...]))
    xr = jnp.concatenate(xr_parts, axis=1).astype(BF16)
    xi = jnp.concatenate(xi_parts, axis=1).astype(BF16)
    m1r = m1r_ref[...]
    m1i = m1i_ref[...]
    tr = _dot(m1r, xr) - _dot(m1i, xi)
    ti = _dot(m1r, xi) + _dot(m1i, xr)
    reps = ct // V7X_LANES
    twr = jnp.tile(twr_ref[...].reshape(rows, V7X_LANES), (1, reps))
    twi = jnp.tile(twi_ref[...].reshape(rows, V7X_LANES), (1, reps))
    tr_ref[...] = (tr * twr - ti * twi).reshape(FFT_A, FFT_SUB, ct)
    ti_ref[...] = (tr * twi + ti * twr).reshape(FFT_A, FFT_SUB, ct)


def _fft2_kernel(tr_ref, ti_ref, m2r_ref, m2i_ref, wf_ref, o_ref):
    j = pl.program_id(1)
    tr = tr_ref[...].astype(BF16)
    ti = ti_ref[...].astype(BF16)
    z = _dot(m2r_ref[...], tr) - _dot(m2i_ref[...], ti)
    contrib = _dot(z.astype(BF16), wf_ref[...]).reshape(FFT_B, FFT_SUB, D_WIDTH)

    @pl.when(j == 0)
    def _():
        o_ref[...] = contrib

    @pl.when(j > 0)
    def _():
        o_ref[...] += contrib


def _fourier_mix(f, tabs, w_fourier_bf):
    n = f.shape[0]
    a_len, b_len, sub = FFT_A, FFT_B, FFT_SUB
    f3 = f.reshape(a_len, b_len, D_WIDTH)
    rows1 = a_len * sub
    const2 = lambda i, j: (0, 0)
    tr, ti = pl.pallas_call(
        _fft1_kernel,
        grid=(b_len // sub, D_WIDTH // FFT1_CT),
        in_specs=[
            pl.BlockSpec((a_len, sub, FFT1_CT), lambda i, j: (0, i, j)),
            pl.BlockSpec((D_GROUP_DIM, D_GROUP_DIM), const2),
            pl.BlockSpec((D_GROUP_DIM, D_GROUP_DIM), const2),
            pl.BlockSpec((rows1, rows1), const2),
            pl.BlockSpec((rows1, rows1), const2),
            pl.BlockSpec((a_len, sub, V7X_LANES), lambda i, j: (0, i, 0)),
            pl.BlockSpec((a_len, sub, V7X_LANES), lambda i, j: (0, i, 0)),
        ],
        out_specs=[pl.BlockSpec((a_len, sub, FFT1_CT), lambda i, j: (0, i, j))] * 2,
        out_shape=[jax.ShapeDtypeStruct((a_len, b_len, D_WIDTH), F32)] * 2,
        compiler_params=_params(("arbitrary", "arbitrary")),
        name="fourier_stage1",
    )(f3, tabs['cos_c'].astype(BF16), tabs['sin_c'].astype(BF16),
      tabs['m1r'].astype(BF16), tabs['m1i'].astype(BF16), tabs['twr'], tabs['twi'])

    rows2 = sub * b_len
    tr2 = tr.reshape(n, D_WIDTH)
    ti2 = ti.reshape(n, D_WIDTH)
    out = pl.pallas_call(
        _fft2_kernel,
        grid=(a_len // sub, D_WIDTH // FFT2_CT),
        in_specs=[
            pl.BlockSpec((rows2, FFT2_CT), lambda i, j: (i, j)),
            pl.BlockSpec((rows2, FFT2_CT), lambda i, j: (i, j)),
            pl.BlockSpec((rows2, rows2), const2),
            pl.BlockSpec((rows2, rows2), const2),
            pl.BlockSpec((FFT2_CT, D_WIDTH), lambda i, j: (j, 0)),
        ],
        out_specs=pl.BlockSpec((b_len, sub, D_WIDTH), lambda i, j: (0, i, 0)),
        out_shape=jax.ShapeDtypeStruct((b_len, a_len, D_WIDTH), F32),
        compiler_params=_params(("arbitrary", "arbitrary")),
        name="fourier_stage2",
    )(tr2, ti2, tabs['m2r'].astype(BF16), tabs['m2i'].astype(BF16), w_fourier_bf)
    return out.reshape(n, D_WIDTH)


def _rope_tables(n):
    rows = n // GRID_W
    row = jnp.repeat(jnp.arange(rows, dtype=F32), GRID_W)
    col = jnp.tile(jnp.arange(GRID_W, dtype=F32), rows)
    inv = ROPE_THETA ** (-jnp.arange(0, AXIS_DIM, 2, dtype=F32) / AXIS_DIM)
    ang_r = row[:, None] * inv[None, :]
    ang_c = col[:, None] * inv[None, :]
    cos = jnp.concatenate([jnp.cos(ang_r)] * 2 + [jnp.cos(ang_c)] * 2, axis=-1)
    sin = jnp.concatenate([-jnp.sin(ang_r), jnp.sin(ang_r), -jnp.sin(ang_c), jnp.sin(ang_c)], axis=-1)
    return cos, sin


def kernel(x, c, ctx, c_ctx, w_mod, b_mod, norm1_g, norm2_g, ab_w_in, a_q_norm_g, a_k_norm_g, a_sink,
           b_w_pool, b_pool_scale, ab_w_out, cd_w_in, c_v_norm_g, c_w_spatial, c_b_spatial, d_w_fourier,
           cd_w_out, f_w_up, f_conv_w, f_conv_b, f_w_down):
    batch, n, _ = x.shape
    ctx_len = ctx.shape[1]
    assert batch == 1 and DEPTH == 2
    x2 = x.reshape(n, D_MODEL)
    ctx2 = ctx.reshape(ctx_len, D_MODEL)

    mod = _mod_vectors(c, c_ctx, w_mod, b_mod)

    def split6(v):
        return [v[:, k * D_MODEL:(k + 1) * D_MODEL] for k in range(6)]

    row1 = lambda v: v.reshape(1, -1)

    ml = split6(mod[0, 0:1])
    mc = split6(mod[0, 1:2])
    g1 = row1(norm1_g[0])
    w_in = ab_w_in[0].astype(BF16)
    qn = row1(a_q_norm_g[0])
    kn = row1(a_k_norm_g[0])
    cos, sin = _rope_tables(n)
    qkv, z = _ab_in_proj(x2, g1, ml[0], ml[1], w_in, cos, sin, qn, kn, tm=1024)
    ones = jnp.ones((ctx_len, HEAD_DIM), F32)
    zeros = jnp.zeros((ctx_len, HEAD_DIM), F32)
    qkv_ctx, _ = _ab_in_proj(ctx2, g1, mc[0], mc[1], w_in, ones, zeros, qn, kn, tm=ctx_len)
    attn = _window_attention(qkv, qkv_ctx, a_sink[0])
    pooled = _pool_mix(z, b_w_pool[0].astype(BF16), row1(b_pool_scale[0]), tm=256)
    x2 = _out_proj(attn, pooled, ab_w_out[0].astype(BF16), x2, ml[2], tm=1024)
    x2 = _conv_ffn(x2, row1(norm2_g[0]), ml[3], ml[4], f_w_up[0].astype(BF16), f_conv_w[0],
                   f_conv_b[0], f_w_down[0].astype(BF16), ml[5], tm=512)

    ml = split6(mod[1, 0:1])
    uv, f = _cd_in_proj(x2, row1(norm1_g[1]), ml[0], ml[1], cd_w_in[0].astype(BF16),
                        row1(c_v_norm_g[0]), tm=512)
    bias_full = jnp.repeat(c_b_spatial[0].T, C_GROUP_DIM, axis=1)
    c_out = _spatial_gate(uv, c_w_spatial[0].astype(BF16), bias_full)
    d_out = _fourier_mix(f, _fourier_tables(n), d_w_fourier[0].astype(BF16))
    x2 = _out_proj(c_out, d_out, cd_w_out[0].astype(BF16), x2, ml[2], tm=1024)
    x2 = _conv_ffn(x2, row1(norm2_g[1]), ml[3], ml[4], f_w_up[1].astype(BF16), f_conv_w[1],
                   f_conv_b[1], f_w_down[1].astype(BF16), ml[5], tm=512)
    return x2.reshape(batch, n, D_MODEL)
```

```python
import functools

import numpy as np
import jax
import jax.numpy as jnp
from jax import lax
from jax.experimental import pallas as pl
from jax.experimental.pallas import tpu as pltpu

F32 = jnp.float32
BF16 = jnp.bfloat16

D_MODEL = 2048
DEPTH = 2
GRID_W = 64
HEAD_DIM = 128
A_Q_HEADS = 8
A_KV_HEADS = 2
A_GROUP = A_Q_HEADS // A_KV_HEADS
A_Q_DIM = A_Q_HEADS * HEAD_DIM
A_KV_DIM = A_KV_HEADS * HEAD_DIM
A_QKV_DIM = A_Q_DIM + 2 * A_KV_DIM
WINDOW = 128
BLOCK = 128
ROPE_THETA = 10000.0
AXIS_DIM = HEAD_DIM // 2
ATTN_SCALE = HEAD_DIM ** -0.5
NEG_INF = -1e30
B_GROUPS = 4
B_WIDTH = 1024
B_GROUP_DIM = B_WIDTH // B_GROUPS
POOL_WINDOWS = (2, 4, 8, 16)
AB_IN = A_QKV_DIM + B_WIDTH
C_WIDTH = 1024
C_GROUPS = 4
C_GROUP_DIM = C_WIDTH // C_GROUPS
CHUNK = 128
D_WIDTH = 1024
D_GROUPS = 8
D_GROUP_DIM = D_WIDTH // D_GROUPS
CD_IN = 2 * C_WIDTH + D_WIDTH
D_FF = 5632
EPS = 1e-6

V7X_SUBLANES_F32 = 8
V7X_SUBLANES_BF16 = 16
V7X_LANES = 128
V7X_VMEM_BYTES = 64 * 1024 * 1024
VMEM_LIMIT = 56 * 1024 * 1024

PROJ_TM = 512
POOL_TM = 256
FFN_TM = 512

FFT_A = 64
FFT_B = 128


def _params(sem):
    return pltpu.CompilerParams(dimension_semantics=sem, vmem_limit_bytes=VMEM_LIMIT)


def _dot(a, b):
    return jnp.dot(a, b, preferred_element_type=F32)


def _norm_mod(x, g, shift, scale):
    ms = jnp.mean(x * x, axis=-1, keepdims=True)
    y = x * lax.rsqrt(ms + EPS) * g
    return y * (1.0 + scale) + shift


def _mod_kernel(cv_ref, w_ref, b_ref, o_ref):
    a = cv_ref[...]
    a = a * jax.nn.sigmoid(a)
    w = w_ref[0]
    a_hi = a.astype(BF16)
    a_lo = (a - a_hi.astype(F32)).astype(BF16)
    w_hi = w.astype(BF16)
    w_lo = (w - w_hi.astype(F32)).astype(BF16)
    acc = _dot(a_hi, w_hi) + _dot(a_lo, w_hi) + _dot(a_hi, w_lo)
    o_ref[0] = acc + b_ref[0]


def _mod_vectors(c, c_ctx, w_mod, b_mod):
    tn = 1024
    cv = jnp.concatenate(
        [c.reshape(1, D_MODEL), c_ctx.reshape(1, D_MODEL),
         jnp.zeros((V7X_SUBLANES_F32 - 2, D_MODEL), F32)], axis=0)
    b3 = b_mod.reshape(DEPTH, 1, 6 * D_MODEL)
    return pl.pallas_call(
        _mod_kernel,
        grid=(DEPTH, 6 * D_MODEL // tn),
        in_specs=[
            pl.BlockSpec((V7X_SUBLANES_F32, D_MODEL), lambda l, j: (0, 0)),
            pl.BlockSpec((1, D_MODEL, tn), lambda l, j: (l, 0, j)),
            pl.BlockSpec((1, 1, tn), lambda l, j: (l, 0, j)),
        ],
        out_specs=pl.BlockSpec((1, V7X_SUBLANES_F32, tn), lambda l, j: (l, 0, j)),
        out_shape=jax.ShapeDtypeStruct((DEPTH, V7X_SUBLANES_F32, 6 * D_MODEL), F32),
        compiler_params=_params(("arbitrary", "arbitrary")),
        name="mod_vectors",
    )(cv, w_mod, b3)


AB_TN = 512
ROW_SPLIT = 2


def _rope(t, cos, sin_signed):
    lane = lax.broadcasted_iota(jnp.int32, t.shape, 1)
    first = (lane % AXIS_DIM) < (AXIS_DIM // 2)
    partner = jnp.where(first,
                        pltpu.roll(t, HEAD_DIM - AXIS_DIM // 2, 1),
                        pltpu.roll(t, AXIS_DIM // 2, 1))
    return t * cos + partner * sin_signed


def _head_norm_rope(t, g, cos, sin_signed):
    ms = jnp.mean(t * t, axis=-1, keepdims=True)
    return _rope(t * lax.rsqrt(ms + EPS) * g, cos, sin_signed)


def _ab_in_kernel(x_ref, g_ref, sh_ref, sc_ref, w_ref, cos_ref, sin_ref, qn_ref, kn_ref,
                  qkv_ref, z_ref, *, tm):
    rows_per = tm // ROW_SPLIT
    n_q_tiles = A_Q_DIM // AB_TN
    for part in range(ROW_SPLIT):
        rs = slice(part * rows_per, (part + 1) * rows_per)
        h = _norm_mod(x_ref[rs, :], g_ref[...], sh_ref[...], sc_ref[...]).astype(BF16)
        cos = cos_ref[rs, :]
        sin = sin_ref[rs, :]
        for t in range(AB_IN // AB_TN):
            p = _dot(h, w_ref[:, t * AB_TN:(t + 1) * AB_TN])
            if t < n_q_tiles:
                for hh in range(AB_TN // HEAD_DIM):
                    c0 = t * AB_TN + hh * HEAD_DIM
                    qkv_ref[rs, c0:c0 + HEAD_DIM] = _head_norm_rope(
                        p[:, hh * HEAD_DIM:(hh + 1) * HEAD_DIM], qn_ref[...], cos, sin).astype(BF16)
            elif t == n_q_tiles:
                for hh in range(A_KV_HEADS):
                    c0 = A_Q_DIM + hh * HEAD_DIM
                    qkv_ref[rs, c0:c0 + HEAD_DIM] = _head_norm_rope(
                        p[:, hh * HEAD_DIM:(hh + 1) * HEAD_DIM], kn_ref[...], cos, sin).astype(BF16)
                qkv_ref[rs, A_Q_DIM + A_KV_DIM:] = p[:, A_KV_DIM:].astype(BF16)
            else:
                c0 = (t - n_q_tiles - 1) * AB_TN
                z_ref[rs, c0:c0 + AB_TN] = p


def _ab_in_proj(x2, g, shift, scale, w_in_bf, cos, sin, qn, kn, tm):
    n = x2.shape[0]
    assert A_Q_DIM % AB_TN == 0 and 2 * A_KV_DIM == AB_TN and B_WIDTH % AB_TN == 0
    row = lambda i: (0, 0)
    return pl.pallas_call(
        functools.partial(_ab_in_kernel, tm=tm),
        grid=(n // tm,),
        in_specs=[
            pl.BlockSpec((tm, D_MODEL), lambda i: (i, 0)),
            pl.BlockSpec((1, D_MODEL), row),
            pl.BlockSpec((1, D_MODEL), row),
            pl.BlockSpec((1, D_MODEL), row),
            pl.BlockSpec((D_MODEL, AB_IN), row, pipeline_mode=pl.Buffered(1)),
            pl.BlockSpec((tm, HEAD_DIM), lambda i: (i, 0)),
            pl.BlockSpec((tm, HEAD_DIM), lambda i: (i, 0)),
            pl.BlockSpec((1, HEAD_DIM), row),
            pl.BlockSpec((1, HEAD_DIM), row),
        ],
        out_specs=[
            pl.BlockSpec((tm, A_QKV_DIM), lambda i: (i, 0)),
            pl.BlockSpec((tm, B_WIDTH), lambda i: (i, 0)),
        ],
        out_shape=[jax.ShapeDtypeStruct((n, A_QKV_DIM), BF16),
                   jax.ShapeDtypeStruct((n, B_WIDTH), F32)],
        compiler_params=_params(("arbitrary",)),
        name="ab_in_proj",
    )(x2, g, shift, scale, w_in_bf, cos, sin, qn, kn)


def _attn_kernel(sink_ref, bias_ref, q_ref, kp_ref, kc_ref, kn_ref, vp_ref, vc_ref, vn_ref,
                 kx_ref, vx_ref, o_ref):
    rows = A_GROUP * BLOCK
    bias = bias_ref[...]
    r1 = lax.broadcasted_iota(jnp.int32, (rows, 1), 0) // BLOCK
    for hk in range(A_KV_HEADS):
        hs = slice(hk * HEAD_DIM, (hk + 1) * HEAD_DIM)
        kcat = jnp.concatenate([kp_ref[:, hs], kc_ref[:, hs], kn_ref[:, hs], kx_ref[:, hs]], axis=0)
        vcat = jnp.concatenate([vp_ref[:, hs], vc_ref[:, hs], vn_ref[:, hs], vx_ref[:, hs]], axis=0)
        q0 = hk * A_GROUP * HEAD_DIM
        q4 = jnp.concatenate(
            [q_ref[:, q0 + g * HEAD_DIM:q0 + (g + 1) * HEAD_DIM] for g in range(A_GROUP)], axis=0)
        s = lax.dot_general(q4, kcat, (((1,), (1,)), ((), ())), preferred_element_type=F32)
        s = s * ATTN_SCALE + bias
        sink = jnp.zeros((rows, 1), F32)
        for g in range(A_GROUP):
            sink = jnp.where(r1 == g, sink_ref[hk * A_GROUP + g], sink)
        m = jnp.maximum(jnp.max(s, axis=-1, keepdims=True), sink)
        e = jnp.exp(s - m)
        den = jnp.sum(e, axis=-1, keepdims=True) + jnp.exp(sink - m)
        pr = (e / den).astype(BF16)
        o4 = _dot(pr, vcat)
        for g in range(A_GROUP):
            o_ref[:, q0 + g * HEAD_DIM:q0 + (g + 1) * HEAD_DIM] = (
                o4[g * BLOCK:(g + 1) * BLOCK].astype(BF16))


def _attn_bias(ctx_len):
    rows = A_GROUP * BLOCK
    cols = 3 * BLOCK + ctx_len
    r = np.arange(rows)[:, None] % BLOCK
    c = np.arange(cols)[None, :]
    band = (np.abs(BLOCK + r - c) <= WINDOW) | (c >= 3 * BLOCK)
    first = band & (c >= BLOCK)
    last = band & ((c < 2 * BLOCK) | (c >= 3 * BLOCK))
    masks = np.stack([first, band, last])
    return jnp.asarray(np.where(masks, 0.0, NEG_INF), dtype=F32)


def _window_attention(qkv, qkv_ctx, sink):
    n = qkv.shape[0]
    ctx_len = qkv_ctx.shape[0]
    n_blocks = n // BLOCK
    assert n_blocks >= 2
    k_col = A_Q_DIM // A_KV_DIM
    v_col = k_col + 1
    prev = lambda i: jnp.maximum(i - 1, 0)
    nxt = lambda i: jnp.minimum(i + 1, n_blocks - 1)
    which = lambda i: jnp.where(i == 0, 0, jnp.where(i == n_blocks - 1, 2, 1))
    blk = (BLOCK, A_KV_DIM)
    rows = A_GROUP * BLOCK
    cols = 3 * BLOCK + ctx_len
    return pl.pallas_call(
        _attn_kernel,
        grid=(n_blocks,),
        in_specs=[
            pl.BlockSpec(memory_space=pltpu.SMEM),
            pl.BlockSpec((None, rows, cols), lambda i: (which(i), 0, 0)),
            pl.BlockSpec((BLOCK, A_Q_DIM), lambda i: (i, 0)),
            pl.BlockSpec(blk, lambda i: (prev(i), k_col)),
            pl.BlockSpec(blk, lambda i: (i, k_col)),
            pl.BlockSpec(blk, lambda i: (nxt(i), k_col)),
            pl.BlockSpec(blk, lambda i: (prev(i), v_col)),
            pl.BlockSpec(blk, lambda i: (i, v_col)),
            pl.BlockSpec(blk, lambda i: (nxt(i), v_col)),
            pl.BlockSpec((ctx_len, A_KV_DIM), lambda i: (0, k_col)),
            pl.BlockSpec((ctx_len, A_KV_DIM), lambda i: (0, v_col)),
        ],
        out_specs=pl.BlockSpec((BLOCK, A_Q_DIM), lambda i: (i, 0)),
        out_shape=jax.ShapeDtypeStruct((n, A_Q_DIM), BF16),
        compiler_params=_params(("arbitrary",)),
        name="window_attention",
    )(sink, _attn_bias(ctx_len), qkv, qkv, qkv, qkv, qkv, qkv, qkv, qkv_ctx, qkv_ctx)


POOL_HALO = 8
assert max(POOL_WINDOWS) // 2 <= POOL_HALO


def _pool_kernel(zm_ref, zp_ref, zn_ref, w_ref, ps_ref, o_ref, z_scr, *, n_rows, tm):
    i = pl.program_id(0)
    last = pl.num_programs(0) - 1
    z_scr[0:POOL_HALO, :] = jnp.where(i > 0, zp_ref[...], 0.0)
    z_scr[POOL_HALO:POOL_HALO + tm, :] = zm_ref[...]
    z_scr[POOL_HALO + tm:, :] = jnp.where(i < last, zn_ref[...], 0.0)
    t = i * tm + lax.broadcasted_iota(jnp.int32, (tm, B_GROUP_DIM), 0)
    for g in range(B_GROUPS):
        half = POOL_WINDOWS[g] // 2
        cs = slice(g * B_GROUP_DIM, (g + 1) * B_GROUP_DIM)
        acc = z_scr[pl.ds(POOL_HALO - half, tm), cs]
        for off in range(-half + 1, half):
            acc = acc + z_scr[pl.ds(POOL_HALO + off, tm), cs]
        cnt = (jnp.minimum(t + half, n_rows) - jnp.maximum(t - half, 0)).astype(F32)
        d = (acc / cnt - zm_ref[:, cs]).astype(BF16)
        y = _dot(d, w_ref[g]) * ps_ref[:, cs]
        o_ref[:, cs] = y.astype(BF16)


def _pool_mix(z, w_pool_bf, pool_scale, tm):
    n = z.shape[0]
    hb = tm // POOL_HALO
    n_halo_blocks = n // POOL_HALO
    return pl.pallas_call(
        functools.partial(_pool_kernel, n_rows=n, tm=tm),
        grid=(n // tm,),
        in_specs=[
            pl.BlockSpec((tm, B_WIDTH), lambda i: (i, 0)),
            pl.BlockSpec((POOL_HALO, B_WIDTH), lambda i: (jnp.maximum(i * hb - 1, 0), 0)),
            pl.BlockSpec((POOL_HALO, B_WIDTH), lambda i: (jnp.minimum((i + 1) * hb, n_halo_blocks - 1), 0)),
            pl.BlockSpec((B_GROUPS, B_GROUP_DIM, B_GROUP_DIM), lambda i: (0, 0, 0)),
            pl.BlockSpec((1, B_WIDTH), lambda i: (0, 0)),
        ],
        out_specs=pl.BlockSpec((tm, B_WIDTH), lambda i: (i, 0)),
        out_shape=jax.ShapeDtypeStruct((n, B_WIDTH), BF16),
        scratch_shapes=[pltpu.VMEM((tm + 2 * POOL_HALO, B_WIDTH), F32)],
        compiler_params=_params(("arbitrary",)),
        name="pool_mix",
    )(z, z, z, w_pool_bf, pool_scale)


OUT_TN = 512


def _out_proj_kernel(a1_ref, a2_ref, w_ref, x_ref, gate_ref, o_ref, *, tm):
    rows_per = tm // ROW_SPLIT
    for part in range(ROW_SPLIT):
        rs = slice(part * rows_per, (part + 1) * rows_per)
        a = jnp.concatenate([a1_ref[rs, :].astype(BF16), a2_ref[rs, :].astype(BF16)], axis=1)
        for t in range(D_MODEL // OUT_TN):
            cs = slice(t * OUT_TN, (t + 1) * OUT_TN)
            o_ref[rs, cs] = x_ref[rs, cs] + gate_ref[:, cs] * _dot(a, w_ref[:, cs])


def _out_proj(a1, a2, w_out_bf, x2, gate, tm):
    n = x2.shape[0]
    k1 = a1.shape[1]
    k2 = a2.shape[1]
    assert k1 + k2 == w_out_bf.shape[0] and k1 % V7X_LANES == 0
    return pl.pallas_call(
        functools.partial(_out_proj_kernel, tm=tm),
        grid=(n // tm,),
        in_specs=[
            pl.BlockSpec((tm, k1), lambda i: (i, 0)),
            pl.BlockSpec((tm, k2), lambda i: (i, 0)),
            pl.BlockSpec((k1 + k2, D_MODEL), lambda i: (0, 0), pipeline_mode=pl.Buffered(1)),
            pl.BlockSpec((tm, D_MODEL), lambda i: (i, 0)),
            pl.BlockSpec((1, D_MODEL), lambda i: (0, 0)),
        ],
        out_specs=pl.BlockSpec((tm, D_MODEL), lambda i: (i, 0)),
        out_shape=jax.ShapeDtypeStruct((n, D_MODEL), F32),
        compiler_params=_params(("arbitrary",)),
        name="out_proj",
    )(a1, a2, w_out_bf, x2, gate)


FFN_HALO = V7X_SUBLANES_BF16
FFN_CW = 256
FFN_TF = 2 * FFN_CW
FFN_TN = 512
FFN_CHUNKS = D_FF // FFN_CW
FFN_UP_STEPS = D_FF // FFN_TF
FFN_DOWN_STEPS = D_MODEL // FFN_TN


def _ffn_kernel(xm_ref, xp_ref, xn_ref, g_ref, sh_ref, sc_ref, wg_ref, wv_ref, cw_ref, cb_ref,
                wd_ref, gate_ref, o_ref, h_scr, carry_scr, act_scr, *, tm):
    i = pl.program_id(0)
    j = pl.program_id(1)
    last_i = pl.num_programs(0) - 1
    hl = FFN_HALO
    nc = FFN_CHUNKS

    def activation(ug, uv, c):
        def conv(u, k):
            w = cw_ref[k]
            return (u[hl - 1:hl - 1 + tm] * w[0:1] + u[hl:hl + tm] * w[1:2]
                    + u[hl + 1:hl + 1 + tm] * w[2:3] + cb_ref[k])
        gg = conv(ug, c)
        vv = conv(uv, c + nc)
        return (gg * jax.nn.sigmoid(gg) * vv).astype(BF16)

    @pl.when(j == 0)
    def _():
        nm = lambda v: _norm_mod(v, g_ref[...], sh_ref[...], sc_ref[...])
        h_scr[0:hl, :] = jnp.where(i > 0, nm(xp_ref[...]), 0.0).astype(BF16)
        h_scr[hl:hl + tm, :] = nm(xm_ref[...]).astype(BF16)
        h_scr[hl + tm:, :] = jnp.where(i < last_i, nm(xn_ref[...]), 0.0).astype(BF16)
        carry_scr[...] = jnp.zeros_like(carry_scr)

    @pl.when(j < FFN_UP_STEPS)
    def _():
        hext = h_scr[...]
        deferred = jnp.maximum(2 * j - 1, 0)
        act_deferred = activation(carry_scr[0], carry_scr[1], deferred)
        ug = _dot(hext, wg_ref[:, 0:FFN_CW])
        uv = _dot(hext, wv_ref[:, 0:FFN_CW])
        act_first = activation(ug, uv, 2 * j)
        carry_scr[0] = _dot(hext, wg_ref[:, FFN_CW:])
        carry_scr[1] = _dot(hext, wv_ref[:, FFN_CW:])
        act_scr[jnp.where(j == 0, nc, 2 * j - 1)] = act_deferred
        act_scr[2 * j] = act_first

    @pl.when(j == FFN_UP_STEPS)
    def _():
        act_scr[nc - 1] = activation(carry_scr[0], carry_scr[1], nc - 1)

    @pl.when(j >= FFN_UP_STEPS)
    def _():
        a = jnp.concatenate([act_scr[c] for c in range(nc)], axis=1)
        col = pl.multiple_of((j - FFN_UP_STEPS) * FFN_TN, FFN_TN)
        o_ref[...] = xm_ref[:, pl.ds(col, FFN_TN)] + gate_ref[...] * _dot(a, wd_ref[...])


def _conv_ffn(x2, g, shift, scale, w_up_bf, conv_w, conv_b, w_down_bf, gate, layer, tm):
    n = x2.shape[0]
    nj = FFN_UP_STEPS
    hb = tm // FFN_HALO
    n_halo_blocks = n // FFN_HALO
    row = lambda i, j: (0, 0)
    down = lambda j: jnp.maximum(j - nj, 0)
    cw3 = conv_w.reshape(3, 2 * FFN_CHUNKS, FFN_CW).transpose(1, 0, 2)
    cb3 = conv_b.reshape(2 * FFN_CHUNKS, 1, FFN_CW)
    return pl.pallas_call(
        functools.partial(_ffn_kernel, tm=tm),
        grid=(n // tm, nj + FFN_DOWN_STEPS),
        in_specs=[
            pl.BlockSpec((tm, D_MODEL), lambda i, j: (i, 0)),
            pl.BlockSpec((FFN_HALO, D_MODEL), lambda i, j: (jnp.maximum(i * hb - 1, 0), 0)),
            pl.BlockSpec((FFN_HALO, D_MODEL), lambda i, j: (jnp.minimum((i + 1) * hb, n_halo_blocks - 1), 0)),
            pl.BlockSpec((1, D_MODEL), row),
            pl.BlockSpec((1, D_MODEL), row),
            pl.BlockSpec((1, D_MODEL), row),
            pl.BlockSpec((None, D_MODEL, FFN_TF), lambda i, j: (layer, 0, jnp.minimum(j, nj - 1))),
            pl.BlockSpec((None, D_MODEL, FFN_TF), lambda i, j: (layer, 0, jnp.minimum(j, nj - 1) + nj)),
            pl.BlockSpec((2 * FFN_CHUNKS, 3, FFN_CW), lambda i, j: (0, 0, 0)),
            pl.BlockSpec((2 * FFN_CHUNKS, 1, FFN_CW), lambda i, j: (0, 0, 0)),
            pl.BlockSpec((None, D_FF, FFN_TN), lambda i, j: (layer, 0, down(j))),
            pl.BlockSpec((1, FFN_TN), lambda i, j: (0, down(j))),
        ],
        out_specs=pl.BlockSpec((tm, FFN_TN), lambda i, j: (i, down(j))),
        out_shape=jax.ShapeDtypeStruct((n, D_MODEL), F32),
        scratch_shapes=[pltpu.VMEM((tm + 2 * FFN_HALO, D_MODEL), BF16),
                        pltpu.VMEM((2, tm + 2 * FFN_HALO, FFN_CW), F32),
                        pltpu.VMEM((FFN_CHUNKS + 1, tm, FFN_CW), BF16)],
        compiler_params=_params(("arbitrary", "arbitrary")),
        name="conv_ffn",
    )(x2, x2, x2, g, shift, scale, w_up_bf, w_up_bf, cw3, cb3, w_down_bf, gate)


def _gelu(x):
    return 0.5 * x * (1.0 + lax.erf(x * (2.0 ** -0.5)))


def _cd_in_kernel(x_ref, g_ref, sh_ref, sc_ref, w_ref, vg_ref, uv_ref, f_ref, *, tm):
    rows_per = tm // ROW_SPLIT
    for part in range(ROW_SPLIT):
        rs = slice(part * rows_per, (part + 1) * rows_per)
        h = _norm_mod(x_ref[rs, :], g_ref[...], sh_ref[...], sc_ref[...]).astype(BF16)
        uv_ref[rs, 0:C_WIDTH] = _gelu(_dot(h, w_ref[:, 0:C_WIDTH])).astype(BF16)
        v = _gelu(_dot(h, w_ref[:, C_WIDTH:2 * C_WIDTH]))
        ms = jnp.mean(v * v, axis=-1, keepdims=True)
        uv_ref[rs, C_WIDTH:] = (v * lax.rsqrt(ms + EPS) * vg_ref[...]).astype(BF16)
        f_ref[rs, :] = _dot(h, w_ref[:, 2 * C_WIDTH:])


def _cd_in_proj(x2, g, shift, scale, w_in_bf, v_norm_g, tm):
    n = x2.shape[0]
    row = lambda i: (0, 0)
    return pl.pallas_call(
        functools.partial(_cd_in_kernel, tm=tm),
        grid=(n // tm,),
        in_specs=[
            pl.BlockSpec((tm, D_MODEL), lambda i: (i, 0)),
            pl.BlockSpec((1, D_MODEL), row),
            pl.BlockSpec((1, D_MODEL), row),
            pl.BlockSpec((1, D_MODEL), row),
            pl.BlockSpec((D_MODEL, CD_IN), row, pipeline_mode=pl.Buffered(1)),
            pl.BlockSpec((1, C_WIDTH), row),
        ],
        out_specs=[
            pl.BlockSpec((tm, 2 * C_WIDTH), lambda i: (i, 0)),
            pl.BlockSpec((tm, D_WIDTH), lambda i: (i, 0)),
        ],
        out_shape=[jax.ShapeDtypeStruct((n, 2 * C_WIDTH), BF16),
                   jax.ShapeDtypeStruct((n, D_WIDTH), F32)],
        compiler_params=_params(("arbitrary",)),
        name="cd_in_proj",
    )(x2, g, shift, scale, w_in_bf, v_norm_g)


def _spatial_kernel(u_ref, v_ref, ws_ref, bias_ref, o_ref, *, chunks):
    for k in range(chunks):
        rs = slice(k * CHUNK, (k + 1) * CHUNK)
        for g in range(C_GROUPS):
            cs = slice(g * C_GROUP_DIM, (g + 1) * C_GROUP_DIM)
            s = _dot(ws_ref[g], v_ref[rs, cs]) + bias_ref[:, cs]
            o_ref[rs, cs] = (u_ref[rs, cs].astype(F32) * s).astype(BF16)


def _spatial_gate(uv, w_spatial_bf, bias_full, chunks=4):
    n = uv.shape[0]
    rows = chunks * CHUNK
    return pl.pallas_call(
        functools.partial(_spatial_kernel, chunks=chunks),
        grid=(n // rows,),
        in_specs=[
            pl.BlockSpec((rows, C_WIDTH), lambda i: (i, 0)),
            pl.BlockSpec((rows, C_WIDTH), lambda i: (i, 1)),
            pl.BlockSpec((C_GROUPS, CHUNK, CHUNK), lambda i: (0, 0, 0)),
            pl.BlockSpec((CHUNK, C_WIDTH), lambda i: (0, 0)),
        ],
        out_specs=pl.BlockSpec((rows, C_WIDTH), lambda i: (i, 0)),
        out_shape=jax.ShapeDtypeStruct((n, C_WIDTH), BF16),
        compiler_params=_params(("arbitrary",)),
        name="spatial_gate",
    )(uv, uv, w_spatial_bf, bias_full)


FFT_SUB = V7X_SUBLANES_F32
FFT1_CT = 512
FFT2_CT = 512


def _fourier_tables(n):
    a_len, b_len, sub = FFT_A, FFT_B, FFT_SUB
    assert a_len * b_len == n
    ch = np.arange(D_GROUP_DIM)
    ang_c = 2.0 * np.pi * ((ch[:, None] * ch[None, :]) % D_GROUP_DIM) / D_GROUP_DIM
    a = np.arange(a_len)
    f_a = np.exp(-2j * np.pi * ((a[:, None] * a[None, :]) % a_len) / a_len)
    m1 = np.kron(f_a, np.eye(sub))
    b = np.arange(b_len)
    tw = np.exp(-2j * np.pi * ((a[:, None] * b[None, :]) % n) / n)
    f_b = np.exp(-2j * np.pi * ((b[:, None] * b[None, :]) % b_len) / b_len)
    m2 = np.einsum('db,pq->dpqb', f_b, np.eye(sub)).reshape(b_len * sub, sub * b_len)
    norm = 1.0 / np.sqrt(float(n) * D_GROUP_DIM)
    m2 = m2 * norm
    tw3 = np.broadcast_to(tw[:, :, None], (a_len, b_len, V7X_LANES))
    f32 = lambda v: jnp.asarray(np.ascontiguousarray(v), dtype=F32)
    return dict(cos_c=f32(np.cos(ang_c)), sin_c=f32(np.sin(ang_c)),
                m1r=f32(m1.real), m1i=f32(m1.imag), m2r=f32(m2.real), m2i=f32(m2.imag),
                twr=f32(tw3.real), twi=f32(tw3.imag))


def _fft1_kernel(f_ref, cc_ref, sc_ref, m1r_ref, m1i_ref, twr_ref, twi_ref, tr_ref, ti_ref):
    rows = FFT_A * FFT_SUB
    ct = FFT1_CT
    fb = f_ref[...].reshape(rows, ct).astype(BF16)
    xr_parts = []
    xi_parts = []
    for q in range(ct // D_GROUP_DIM):
        blk = fb[:, q * D_GROUP_DIM:(q + 1) * D_GROUP_DIM]
        xr_parts.append(_dot(blk, cc_ref[...]))
        xi_parts.append(-_dot(blk, sc_ref[...]))
    xr = jnp.concatenate(xr_parts, axis=1).astype(BF16)
    xi = jnp.concatenate(xi_parts, axis=1).astype(BF16)
    m1r = m1r_ref[...]
    m1i = m1i_ref[...]
    tr = _dot(m1r, xr) - _dot(m1i, xi)
    ti = _dot(m1r, xi) + _dot(m1i, xr)
    reps = ct // V7X_LANES
    twr = jnp.tile(twr_ref[...].reshape(rows, V7X_LANES), (1, reps))
    twi = jnp.tile(twi_ref[...].reshape(rows, V7X_LANES), (1, reps))
    tr_ref[...] = (tr * twr - ti * twi).reshape(FFT_A, FFT_SUB, ct)
    ti_ref[...] = (tr * twi + ti * twr).reshape(FFT_A, FFT_SUB, ct)


def _fft2_kernel(tr_ref, ti_ref, m2r_ref, m2i_ref, wf_ref, o_ref):
    j = pl.program_id(1)
    tr = tr_ref[...].astype(BF16)
    ti = ti_ref[...].astype(BF16)
    z = _dot(m2r_ref[...], tr) - _dot(m2i_ref[...], ti)
    contrib = _dot(z.astype(BF16), wf_ref[...]).reshape(FFT_B, FFT_SUB, D_WIDTH)

    @pl.when(j == 0)
    def _():
        o_ref[...] = contrib

    @pl.when(j > 0)
    def _():
        o_ref[...] += contrib


def _fourier_mix(f, tabs, w_fourier_bf):
    n = f.shape[0]
    a_len, b_len, sub = FFT_A, FFT_B, FFT_SUB
    f3 = f.reshape(a_len, b_len, D_WIDTH)
    rows1 = a_len * sub
    const2 = lambda i, j: (0, 0)
    tr, ti = pl.pallas_call(
        _fft1_kernel,
        grid=(b_len // sub, D_WIDTH // FFT1_CT),
        in_specs=[
            pl.BlockSpec((a_len, sub, FFT1_CT), lambda i, j: (0, i, j)),
            pl.BlockSpec((D_GROUP_DIM, D_GROUP_DIM), const2),
            pl.BlockSpec((D_GROUP_DIM, D_GROUP_DIM), const2),
            pl.BlockSpec((rows1, rows1), const2),
            pl.BlockSpec((rows1, rows1), const2),
            pl.BlockSpec((a_len, sub, V7X_LANES), lambda i, j: (0, i, 0)),
            pl.BlockSpec((a_len, sub, V7X_LANES), lambda i, j: (0, i, 0)),
        ],
        out_specs=[pl.BlockSpec((a_len, sub, FFT1_CT), lambda i, j: (0, i, j))] * 2,
        out_shape=[jax.ShapeDtypeStruct((a_len, b_len, D_WIDTH), F32)] * 2,
        compiler_params=_params(("arbitrary", "arbitrary")),
        name="fourier_stage1",
    )(f3, tabs['cos_c'].astype(BF16), tabs['sin_c'].astype(BF16),
      tabs['m1r'].astype(BF16), tabs['m1i'].astype(BF16), tabs['twr'], tabs['twi'])

    rows2 = sub * b_len
    tr2 = tr.reshape(n, D_WIDTH)
    ti2 = ti.reshape(n, D_WIDTH)
    out = pl.pallas_call(
        _fft2_kernel,
        grid=(a_len // sub, D_WIDTH // FFT2_CT),
        in_specs=[
            pl.BlockSpec((rows2, FFT2_CT), lambda i, j: (i, j)),
            pl.BlockSpec((rows2, FFT2_CT), lambda i, j: (i, j)),
            pl.BlockSpec((rows2, rows2), const2),
            pl.BlockSpec((rows2, rows2), const2),
            pl.BlockSpec((FFT2_CT, D_WIDTH), lambda i, j: (j, 0)),
        ],
        out_specs=pl.BlockSpec((b_len, sub, D_WIDTH), lambda i, j: (0, i, 0)),
        out_shape=jax.ShapeDtypeStruct((b_len, a_len, D_WIDTH), F32),
        compiler_params=_params(("arbitrary", "arbitrary")),
        name="fourier_stage2",
    )(tr2, ti2, tabs['m2r'].astype(BF16), tabs['m2i'].astype(BF16), w_fourier_bf)
    return out.reshape(n, D_WIDTH)


def _rope_tables(n):
    rows = n // GRID_W
    row = jnp.repeat(jnp.arange(rows, dtype=F32), GRID_W)
    col = jnp.tile(jnp.arange(GRID_W, dtype=F32), rows)
    inv = ROPE_THETA ** (-jnp.arange(0, AXIS_DIM, 2, dtype=F32) / AXIS_DIM)
    ang_r = row[:, None] * inv[None, :]
    ang_c = col[:, None] * inv[None, :]
    cos = jnp.concatenate([jnp.cos(ang_r)] * 2 + [jnp.cos(ang_c)] * 2, axis=-1)
    sin = jnp.concatenate([-jnp.sin(ang_r), jnp.sin(ang_r), -jnp.sin(ang_c), jnp.sin(ang_c)], axis=-1)
    return cos, sin


def kernel(x, c, ctx, c_ctx, w_mod, b_mod, norm1_g, norm2_g, ab_w_in, a_q_norm_g, a_k_norm_g, a_sink,
           b_w_pool, b_pool_scale, ab_w_out, cd_w_in, c_v_norm_g, c_w_spatial, c_b_spatial, d_w_fourier,
           cd_w_out, f_w_up, f_conv_w, f_conv_b, f_w_down):
    batch, n, _ = x.shape
    ctx_len = ctx.shape[1]
    assert batch == 1 and DEPTH == 2
    x2 = x.reshape(n, D_MODEL)
    ctx2 = ctx.reshape(ctx_len, D_MODEL)

    mod = _mod_vectors(c, c_ctx, w_mod, b_mod)

    def split6(v):
        return [v[:, k * D_MODEL:(k + 1) * D_MODEL] for k in range(6)]

    row1 = lambda v: v.reshape(1, -1)
    w_up_bf = f_w_up.astype(BF16)
    w_down_bf = f_w_down.astype(BF16)

    ml = split6(mod[0, 0:1])
    mc = split6(mod[0, 1:2])
    g1 = row1(norm1_g[0])
    w_in = ab_w_in[0].astype(BF16)
    qn = row1(a_q_norm_g[0])
    kn = row1(a_k_norm_g[0])
    cos, sin = _rope_tables(n)
    qkv, z = _ab_in_proj(x2, g1, ml[0], ml[1], w_in, cos, sin, qn, kn, tm=PROJ_TM)
    ones = jnp.ones((ctx_len, HEAD_DIM), F32)
    zeros = jnp.zeros((ctx_len, HEAD_DIM), F32)
    qkv_ctx, _ = _ab_in_proj(ctx2, g1, mc[0], mc[1], w_in, ones, zeros, qn, kn, tm=ctx_len)
    attn = _window_attention(qkv, qkv_ctx, a_sink[0])
    pooled = _pool_mix(z, b_w_pool[0].astype(BF16), row1(b_pool_scale[0]), tm=POOL_TM)
    x2 = _out_proj(attn, pooled, ab_w_out[0].astype(BF16), x2, ml[2], tm=PROJ_TM)
    x2 = _conv_ffn(x2, row1(norm2_g[0]), ml[3], ml[4], w_up_bf, f_conv_w[0], f_conv_b[0],
                   w_down_bf, ml[5], layer=0, tm=FFN_TM)

    ml = split6(mod[1, 0:1])
    uv, f = _cd_in_proj(x2, row1(norm1_g[1]), ml[0], ml[1], cd_w_in[0].astype(BF16),
                        row1(c_v_norm_g[0]), tm=PROJ_TM)
    bias_full = jnp.repeat(c_b_spatial[0].T, C_GROUP_DIM, axis=1)
    c_out = _spatial_gate(uv, c_w_spatial[0].astype(BF16), bias_full)
    d_out = _fourier_mix(f, _fourier_tables(n), d_w_fourier[0].astype(BF16))
    x2 = _out_proj(c_out, d_out, cd_w_out[0].astype(BF16), x2, ml[2], tm=PROJ_TM)
    x2 = _conv_ffn(x2, row1(norm2_g[1]), ml[3], ml[4], w_up_bf, f_conv_w[1], f_conv_b[1],
                   w_down_bf, ml[5], layer=1, tm=FFN_TM)
    return x2.reshape(batch, n, D_MODEL)
```

```python
import functools

import numpy as np
import jax
import jax.numpy as jnp
from jax import lax
from jax.experimental import pallas as pl
from jax.experimental.pallas import tpu as pltpu

F32 = jnp.float32
BF16 = jnp.bfloat16

D_MODEL = 2048
DEPTH = 2
GRID_W = 64
HEAD_DIM = 128
A_Q_HEADS = 8
A_KV_HEADS = 2
A_GROUP = A_Q_HEADS // A_KV_HEADS
A_Q_DIM = A_Q_HEADS * HEAD_DIM
A_KV_DIM = A_KV_HEADS * HEAD_DIM
A_QKV_DIM = A_Q_DIM + 2 * A_KV_DIM
WINDOW = 128
BLOCK = 128
ROPE_THETA = 10000.0
AXIS_DIM = HEAD_DIM // 2
ATTN_SCALE = HEAD_DIM ** -0.5
NEG_INF = -1e30
B_GROUPS = 4
B_WIDTH = 1024
B_GROUP_DIM = B_WIDTH // B_GROUPS
POOL_WINDOWS = (2, 4, 8, 16)
AB_IN = A_QKV_DIM + B_WIDTH
C_WIDTH = 1024
C_GROUPS = 4
C_GROUP_DIM = C_WIDTH // C_GROUPS
CHUNK = 128
D_WIDTH = 1024
D_GROUPS = 8
D_GROUP_DIM = D_WIDTH // D_GROUPS
CD_IN = 2 * C_WIDTH + D_WIDTH
D_FF = 5632
EPS = 1e-6

V7X_SUBLANES_F32 = 8
V7X_SUBLANES_BF16 = 16
V7X_LANES = 128
V7X_VMEM_BYTES = 64 * 1024 * 1024
VMEM_LIMIT = 56 * 1024 * 1024

PROJ_TM = 512
POOL_TM = 256
FFN_TM = 1024

FFT_A = 64
FFT_B = 128


def _params(sem):
    return pltpu.CompilerParams(dimension_semantics=sem, vmem_limit_bytes=VMEM_LIMIT)


def _dot(a, b):
    return jnp.dot(a, b, preferred_element_type=F32)


def _norm_mod(x, g, shift, scale):
    ms = jnp.mean(x * x, axis=-1, keepdims=True)
    y = x * lax.rsqrt(ms + EPS) * g
    return y * (1.0 + scale) + shift


def _mod_kernel(cv_ref, w_ref, b_ref, o_ref):
    a = cv_ref[...]
    a = a * jax.nn.sigmoid(a)
    w = w_ref[0]
    a_hi = a.astype(BF16)
    a_lo = (a - a_hi.astype(F32)).astype(BF16)
    w_hi = w.astype(BF16)
    w_lo = (w - w_hi.astype(F32)).astype(BF16)
    acc = _dot(a_hi, w_hi) + _dot(a_lo, w_hi) + _dot(a_hi, w_lo)
    o_ref[0] = acc + b_ref[0]


def _mod_vectors(c, c_ctx, w_mod, b_mod):
    tn = 1024
    cv = jnp.concatenate(
        [c.reshape(1, D_MODEL), c_ctx.reshape(1, D_MODEL),
         jnp.zeros((V7X_SUBLANES_F32 - 2, D_MODEL), F32)], axis=0)
    b3 = b_mod.reshape(DEPTH, 1, 6 * D_MODEL)
    return pl.pallas_call(
        _mod_kernel,
        grid=(DEPTH, 6 * D_MODEL // tn),
        in_specs=[
            pl.BlockSpec((V7X_SUBLANES_F32, D_MODEL), lambda l, j: (0, 0)),
            pl.BlockSpec((1, D_MODEL, tn), lambda l, j: (l, 0, j)),
            pl.BlockSpec((1, 1, tn), lambda l, j: (l, 0, j)),
        ],
        out_specs=pl.BlockSpec((1, V7X_SUBLANES_F32, tn), lambda l, j: (l, 0, j)),
        out_shape=jax.ShapeDtypeStruct((DEPTH, V7X_SUBLANES_F32, 6 * D_MODEL), F32),
        compiler_params=_params(("arbitrary", "arbitrary")),
        name="mod_vectors",
    )(cv, w_mod, b3)


AB_TN = 512
ROW_SPLIT = 2


def _rope(t, cos, sin_signed):
    lane = lax.broadcasted_iota(jnp.int32, t.shape, 1)
    first = (lane % AXIS_DIM) < (AXIS_DIM // 2)
    partner = jnp.where(first,
                        pltpu.roll(t, HEAD_DIM - AXIS_DIM // 2, 1),
                        pltpu.roll(t, AXIS_DIM // 2, 1))
    return t * cos + partner * sin_signed


def _head_norm_rope(t, g, cos, sin_signed):
    ms = jnp.mean(t * t, axis=-1, keepdims=True)
    return _rope(t * lax.rsqrt(ms + EPS) * g, cos, sin_signed)


def _ab_in_kernel(x_ref, g_ref, sh_ref, sc_ref, w_ref, cos_ref, sin_ref, qn_ref, kn_ref,
                  qkv_ref, z_ref, *, tm):
    rows_per = tm // ROW_SPLIT
    n_q_tiles = A_Q_DIM // AB_TN
    for part in range(ROW_SPLIT):
        rs = slice(part * rows_per, (part + 1) * rows_per)
        h = _norm_mod(x_ref[rs, :], g_ref[...], sh_ref[...], sc_ref[...]).astype(BF16)
        cos = cos_ref[rs, :]
        sin = sin_ref[rs, :]
        for t in range(AB_IN // AB_TN):
            p = _dot(h, w_ref[:, t * AB_TN:(t + 1) * AB_TN])
            if t < n_q_tiles:
                for hh in range(AB_TN // HEAD_DIM):
                    c0 = t * AB_TN + hh * HEAD_DIM
                    qkv_ref[rs, c0:c0 + HEAD_DIM] = _head_norm_rope(
                        p[:, hh * HEAD_DIM:(hh + 1) * HEAD_DIM], qn_ref[...], cos, sin).astype(BF16)
            elif t == n_q_tiles:
                for hh in range(A_KV_HEADS):
                    c0 = A_Q_DIM + hh * HEAD_DIM
                    qkv_ref[rs, c0:c0 + HEAD_DIM] = _head_norm_rope(
                        p[:, hh * HEAD_DIM:(hh + 1) * HEAD_DIM], kn_ref[...], cos, sin).astype(BF16)
                qkv_ref[rs, A_Q_DIM + A_KV_DIM:] = p[:, A_KV_DIM:].astype(BF16)
            else:
                c0 = (t - n_q_tiles - 1) * AB_TN
                z_ref[rs, c0:c0 + AB_TN] = p


def _ab_in_proj(x2, g, shift, scale, w_in_bf, cos, sin, qn, kn, tm):
    n = x2.shape[0]
    assert A_Q_DIM % AB_TN == 0 and 2 * A_KV_DIM == AB_TN and B_WIDTH % AB_TN == 0
    row = lambda i: (0, 0)
    return pl.pallas_call(
        functools.partial(_ab_in_kernel, tm=tm),
        grid=(n // tm,),
        in_specs=[
            pl.BlockSpec((tm, D_MODEL), lambda i: (i, 0)),
            pl.BlockSpec((1, D_MODEL), row),
            pl.BlockSpec((1, D_MODEL), row),
            pl.BlockSpec((1, D_MODEL), row),
            pl.BlockSpec((D_MODEL, AB_IN), row, pipeline_mode=pl.Buffered(1)),
            pl.BlockSpec((tm, HEAD_DIM), lambda i: (i, 0)),
            pl.BlockSpec((tm, HEAD_DIM), lambda i: (i, 0)),
            pl.BlockSpec((1, HEAD_DIM), row),
            pl.BlockSpec((1, HEAD_DIM), row),
        ],
        out_specs=[
            pl.BlockSpec((tm, A_QKV_DIM), lambda i: (i, 0)),
            pl.BlockSpec((tm, B_WIDTH), lambda i: (i, 0)),
        ],
        out_shape=[jax.ShapeDtypeStruct((n, A_QKV_DIM), BF16),
                   jax.ShapeDtypeStruct((n, B_WIDTH), F32)],
        compiler_params=_params(("arbitrary",)),
        name="ab_in_proj",
    )(x2, g, shift, scale, w_in_bf, cos, sin, qn, kn)


def _attn_kernel(sink_ref, bias_ref, q_ref, kp_ref, kc_ref, kn_ref, vp_ref, vc_ref, vn_ref,
                 kx_ref, vx_ref, o_ref):
    rows = A_GROUP * BLOCK
    bias = bias_ref[...]
    r1 = lax.broadcasted_iota(jnp.int32, (rows, 1), 0) // BLOCK
    for hk in range(A_KV_HEADS):
        hs = slice(hk * HEAD_DIM, (hk + 1) * HEAD_DIM)
        kcat = jnp.concatenate([kp_ref[:, hs], kc_ref[:, hs], kn_ref[:, hs], kx_ref[:, hs]], axis=0)
        vcat = jnp.concatenate([vp_ref[:, hs], vc_ref[:, hs], vn_ref[:, hs], vx_ref[:, hs]], axis=0)
        q0 = hk * A_GROUP * HEAD_DIM
        q4 = jnp.concatenate(
            [q_ref[:, q0 + g * HEAD_DIM:q0 + (g + 1) * HEAD_DIM] for g in range(A_GROUP)], axis=0)
        s = lax.dot_general(q4, kcat, (((1,), (1,)), ((), ())), preferred_element_type=F32)
        s = s * ATTN_SCALE + bias
        sink = jnp.zeros((rows, 1), F32)
        for g in range(A_GROUP):
            sink = jnp.where(r1 == g, sink_ref[hk * A_GROUP + g], sink)
        m = jnp.maximum(jnp.max(s, axis=-1, keepdims=True), sink)
        e = jnp.exp(s - m)
        den = jnp.sum(e, axis=-1, keepdims=True) + jnp.exp(sink - m)
        pr = (e / den).astype(BF16)
        o4 = _dot(pr, vcat)
        for g in range(A_GROUP):
            o_ref[:, q0 + g * HEAD_DIM:q0 + (g + 1) * HEAD_DIM] = (
                o4[g * BLOCK:(g + 1) * BLOCK].astype(BF16))


def _attn_bias(ctx_len):
    rows = A_GROUP * BLOCK
    cols = 3 * BLOCK + ctx_len
    r = np.arange(rows)[:, None] % BLOCK
    c = np.arange(cols)[None, :]
    band = (np.abs(BLOCK + r - c) <= WINDOW) | (c >= 3 * BLOCK)
    first = band & (c >= BLOCK)
    last = band & ((c < 2 * BLOCK) | (c >= 3 * BLOCK))
    masks = np.stack([first, band, last])
    return jnp.asarray(np.where(masks, 0.0, NEG_INF), dtype=F32)


def _window_attention(qkv, qkv_ctx, sink):
    n = qkv.shape[0]
    ctx_len = qkv_ctx.shape[0]
    n_blocks = n // BLOCK
    assert n_blocks >= 2
    k_col = A_Q_DIM // A_KV_DIM
    v_col = k_col + 1
    prev = lambda i: jnp.maximum(i - 1, 0)
    nxt = lambda i: jnp.minimum(i + 1, n_blocks - 1)
    which = lambda i: jnp.where(i == 0, 0, jnp.where(i == n_blocks - 1, 2, 1))
    blk = (BLOCK, A_KV_DIM)
    rows = A_GROUP * BLOCK
    cols = 3 * BLOCK + ctx_len
    return pl.pallas_call(
        _attn_kernel,
        grid=(n_blocks,),
        in_specs=[
            pl.BlockSpec(memory_space=pltpu.SMEM),
            pl.BlockSpec((None, rows, cols), lambda i: (which(i), 0, 0)),
            pl.BlockSpec((BLOCK, A_Q_DIM), lambda i: (i, 0)),
            pl.BlockSpec(blk, lambda i: (prev(i), k_col)),
            pl.BlockSpec(blk, lambda i: (i, k_col)),
            pl.BlockSpec(blk, lambda i: (nxt(i), k_col)),
            pl.BlockSpec(blk, lambda i: (prev(i), v_col)),
            pl.BlockSpec(blk, lambda i: (i, v_col)),
            pl.BlockSpec(blk, lambda i: (nxt(i), v_col)),
            pl.BlockSpec((ctx_len, A_KV_DIM), lambda i: (0, k_col)),
            pl.BlockSpec((ctx_len, A_KV_DIM), lambda i: (0, v_col)),
        ],
        out_specs=pl.BlockSpec((BLOCK, A_Q_DIM), lambda i: (i, 0)),
        out_shape=jax.ShapeDtypeStruct((n, A_Q_DIM), BF16),
        compiler_params=_params(("arbitrary",)),
        name="window_attention",
    )(sink, _attn_bias(ctx_len), qkv, qkv, qkv, qkv, qkv, qkv, qkv, qkv_ctx, qkv_ctx)


POOL_HALO = 8
assert max(POOL_WINDOWS) // 2 <= POOL_HALO


def _pool_kernel(zm_ref, zp_ref, zn_ref, w_ref, ps_ref, o_ref, z_scr, *, n_rows, tm):
    i = pl.program_id(0)
    last = pl.num_programs(0) - 1
    z_scr[0:POOL_HALO, :] = jnp.where(i > 0, zp_ref[...], 0.0)
    z_scr[POOL_HALO:POOL_HALO + tm, :] = zm_ref[...]
    z_scr[POOL_HALO + tm:, :] = jnp.where(i < last, zn_ref[...], 0.0)
    t = i * tm + lax.broadcasted_iota(jnp.int32, (tm, B_GROUP_DIM), 0)
    for g in range(B_GROUPS):
        half = POOL_WINDOWS[g] // 2
        cs = slice(g * B_GROUP_DIM, (g + 1) * B_GROUP_DIM)
        acc = z_scr[pl.ds(POOL_HALO - half, tm), cs]
        for off in range(-half + 1, half):
            acc = acc + z_scr[pl.ds(POOL_HALO + off, tm), cs]
        cnt = (jnp.minimum(t + half, n_rows) - jnp.maximum(t - half, 0)).astype(F32)
        d = (acc / cnt - zm_ref[:, cs]).astype(BF16)
        y = _dot(d, w_ref[g]) * ps_ref[:, cs]
        o_ref[:, cs] = y.astype(BF16)


def _pool_mix(z, w_pool_bf, pool_scale, tm):
    n = z.shape[0]
    hb = tm // POOL_HALO
    n_halo_blocks = n // POOL_HALO
    return pl.pallas_call(
        functools.partial(_pool_kernel, n_rows=n, tm=tm),
        grid=(n // tm,),
        in_specs=[
            pl.BlockSpec((tm, B_WIDTH), lambda i: (i, 0)),
            pl.BlockSpec((POOL_HALO, B_WIDTH), lambda i: (jnp.maximum(i * hb - 1, 0), 0)),
            pl.BlockSpec((POOL_HALO, B_WIDTH), lambda i: (jnp.minimum((i + 1) * hb, n_halo_blocks - 1), 0)),
            pl.BlockSpec((B_GROUPS, B_GROUP_DIM, B_GROUP_DIM), lambda i: (0, 0, 0)),
            pl.BlockSpec((1, B_WIDTH), lambda i: (0, 0)),
        ],
        out_specs=pl.BlockSpec((tm, B_WIDTH), lambda i: (i, 0)),
        out_shape=jax.ShapeDtypeStruct((n, B_WIDTH), BF16),
        scratch_shapes=[pltpu.VMEM((tm + 2 * POOL_HALO, B_WIDTH), F32)],
        compiler_params=_params(("arbitrary",)),
        name="pool_mix",
    )(z, z, z, w_pool_bf, pool_scale)


OUT_TN = 512


def _out_proj_kernel(a1_ref, a2_ref, w_ref, x_ref, gate_ref, g_ref, sh_ref, sc_ref, o_ref, h_ref, *, tm):
    rows_per = tm // ROW_SPLIT
    for part in range(ROW_SPLIT):
        rs = slice(part * rows_per, (part + 1) * rows_per)
        a = jnp.concatenate([a1_ref[rs, :].astype(BF16), a2_ref[rs, :].astype(BF16)], axis=1)
        for t in range(D_MODEL // OUT_TN):
            cs = slice(t * OUT_TN, (t + 1) * OUT_TN)
            o_ref[rs, cs] = x_ref[rs, cs] + gate_ref[:, cs] * _dot(a, w_ref[:, cs])
        h_ref[rs, :] = _norm_mod(o_ref[rs, :], g_ref[...], sh_ref[...], sc_ref[...]).astype(BF16)


def _out_proj(a1, a2, w_out_bf, x2, gate, g2, shift2, scale2, tm):
    n = x2.shape[0]
    k1 = a1.shape[1]
    k2 = a2.shape[1]
    assert k1 + k2 == w_out_bf.shape[0] and k1 % V7X_LANES == 0
    row = lambda i: (0, 0)
    return pl.pallas_call(
        functools.partial(_out_proj_kernel, tm=tm),
        grid=(n // tm,),
        in_specs=[
            pl.BlockSpec((tm, k1), lambda i: (i, 0)),
            pl.BlockSpec((tm, k2), lambda i: (i, 0)),
            pl.BlockSpec((k1 + k2, D_MODEL), row, pipeline_mode=pl.Buffered(1)),
            pl.BlockSpec((tm, D_MODEL), lambda i: (i, 0)),
            pl.BlockSpec((1, D_MODEL), row),
            pl.BlockSpec((1, D_MODEL), row),
            pl.BlockSpec((1, D_MODEL), row),
            pl.BlockSpec((1, D_MODEL), row),
        ],
        out_specs=[pl.BlockSpec((tm, D_MODEL), lambda i: (i, 0)),
                   pl.BlockSpec((tm, D_MODEL), lambda i: (i, 0))],
        out_shape=[jax.ShapeDtypeStruct((n, D_MODEL), F32),
                   jax.ShapeDtypeStruct((n, D_MODEL), BF16)],
        compiler_params=_params(("arbitrary",)),
        name="out_proj",
    )(a1, a2, w_out_bf, x2, gate, g2, shift2, scale2)


FFN_HALO = V7X_SUBLANES_BF16
FFN_CW = 256
FFN_TF = 2 * FFN_CW
FFN_TN = 512
FFN_CHUNKS = D_FF // FFN_CW
FFN_UP_STEPS = D_FF // FFN_TF
FFN_DOWN_STEPS = D_MODEL // FFN_TN


def _ffn_up_kernel(hm_ref, hp_ref, hn_ref, wg_ref, wv_ref, cw_ref, cb_ref, o_ref,
                   h_scr, carry_a, carry_b, *, tm):
    i = pl.program_id(0)
    j = pl.program_id(1)
    last_i = pl.num_programs(0) - 1
    hl = FFN_HALO
    nc = FFN_CHUNKS
    pieces = 2
    rows = tm // pieces

    def finish(carry, k, piece, step):
        c = jnp.maximum(2 * step + k, 0)
        r0 = piece * rows

        def conv(idx, kk):
            w = cw_ref[kk]
            return (carry[idx, pl.ds(hl - 1 + r0, rows), :] * w[0:1]
                    + carry[idx, pl.ds(hl + r0, rows), :] * w[1:2]
                    + carry[idx, pl.ds(hl + 1 + r0, rows), :] * w[2:3] + cb_ref[kk])
        gg = conv(k, c)
        vv = conv(2 + k, c + nc)
        o_ref[r0:r0 + rows, k * FFN_CW:(k + 1) * FFN_CW] = (gg * jax.nn.sigmoid(gg) * vv).astype(BF16)

    @pl.when(j == 0)
    def _():
        h_scr[0:hl, :] = jnp.where(i > 0, hp_ref[...], jnp.zeros_like(hp_ref))
        h_scr[hl:hl + tm, :] = hm_ref[...]
        h_scr[hl + tm:, :] = jnp.where(i < last_i, hn_ref[...], jnp.zeros_like(hn_ref))
        carry_b[...] = jnp.zeros_like(carry_b)

    def up_step(carry_w, carry_r):
        hext = h_scr[...]
        for k in range(2):
            cs = slice(k * FFN_CW, (k + 1) * FFN_CW)
            finish(carry_r, k, 0, j - 1)
            carry_w[k] = _dot(hext, wg_ref[:, cs])
            finish(carry_r, k, 1, j - 1)
            carry_w[2 + k] = _dot(hext, wv_ref[:, cs])

    @pl.when((j < FFN_UP_STEPS) & (j % 2 == 0))
    def _():
        up_step(carry_a, carry_b)

    @pl.when((j < FFN_UP_STEPS) & (j % 2 == 1))
    def _():
        up_step(carry_b, carry_a)

    @pl.when(j == FFN_UP_STEPS)
    def _():
        last = carry_a if (FFN_UP_STEPS - 1) % 2 == 0 else carry_b
        for k in range(2):
            for piece in range(pieces):
                finish(last, k, piece, FFN_UP_STEPS - 1)


def _ffn_down_kernel(a_ref, wd_ref, x_ref, gate_ref, o_ref):
    a = a_ref[:, FFN_TF:]
    o_ref[...] = x_ref[...] + gate_ref[...] * _dot(a, wd_ref[...])


def _conv_ffn(x2, h2, w_up_bf, conv_w, conv_b, w_down_bf, gate, layer, tm):
    n = x2.shape[0]
    nj = FFN_UP_STEPS
    hb = tm // FFN_HALO
    n_halo_blocks = n // FFN_HALO
    act_cols = D_FF + FFN_TF
    cw3 = conv_w.reshape(3, 2 * FFN_CHUNKS, FFN_CW).transpose(1, 0, 2)
    cb3 = conv_b.reshape(2 * FFN_CHUNKS, 1, FFN_CW)
    act = pl.pallas_call(
        functools.partial(_ffn_up_kernel, tm=tm),
        grid=(n // tm, nj + 1),
        in_specs=[
            pl.BlockSpec((tm, D_MODEL), lambda i, j: (i, 0)),
            pl.BlockSpec((FFN_HALO, D_MODEL), lambda i, j: (jnp.maximum(i * hb - 1, 0), 0)),
            pl.BlockSpec((FFN_HALO, D_MODEL), lambda i, j: (jnp.minimum((i + 1) * hb, n_halo_blocks - 1), 0)),
            pl.BlockSpec((None, D_MODEL, FFN_TF), lambda i, j: (layer, 0, jnp.minimum(j, nj - 1))),
            pl.BlockSpec((None, D_MODEL, FFN_TF), lambda i, j: (layer, 0, jnp.minimum(j, nj - 1) + nj)),
            pl.BlockSpec((2 * FFN_CHUNKS, 3, FFN_CW), lambda i, j: (0, 0, 0)),
            pl.BlockSpec((2 * FFN_CHUNKS, 1, FFN_CW), lambda i, j: (0, 0, 0)),
        ],
        out_specs=pl.BlockSpec((tm, FFN_TF), lambda i, j: (i, j)),
        out_shape=jax.ShapeDtypeStruct((n, act_cols), BF16),
        scratch_shapes=[pltpu.VMEM((tm + 2 * FFN_HALO, D_MODEL), BF16),
                        pltpu.VMEM((4, tm + 2 * FFN_HALO, FFN_CW), F32),
                        pltpu.VMEM((4, tm + 2 * FFN_HALO, FFN_CW), F32)],
        compiler_params=_params(("arbitrary", "arbitrary")),
        name="ffn_up",
    )(h2, h2, h2, w_up_bf, w_up_bf, cw3, cb3)
    return pl.pallas_call(
        _ffn_down_kernel,
        grid=(n // tm, FFN_DOWN_STEPS),
        in_specs=[
            pl.BlockSpec((tm, act_cols), lambda i, j: (i, 0)),
            pl.BlockSpec((None, D_FF, FFN_TN), lambda i, j: (layer, 0, j)),
            pl.BlockSpec((tm, FFN_TN), lambda i, j: (i, j)),
            pl.BlockSpec((1, FFN_TN), lambda i, j: (0, j)),
        ],
        out_specs=pl.BlockSpec((tm, FFN_TN), lambda i, j: (i, j)),
        out_shape=jax.ShapeDtypeStruct((n, D_MODEL), F32),
        compiler_params=_params(("arbitrary", "arbitrary")),
        name="ffn_down",
    )(act, w_down_bf, x2, gate)


def _gelu(x):
    return 0.5 * x * (1.0 + lax.erf(x * (2.0 ** -0.5)))


def _cd_in_kernel(x_ref, g_ref, sh_ref, sc_ref, w_ref, vg_ref, uv_ref, f_ref, *, tm):
    rows_per = tm // ROW_SPLIT
    for part in range(ROW_SPLIT):
        rs = slice(part * rows_per, (part + 1) * rows_per)
        h = _norm_mod(x_ref[rs, :], g_ref[...], sh_ref[...], sc_ref[...]).astype(BF16)
        uv_ref[rs, 0:C_WIDTH] = _gelu(_dot(h, w_ref[:, 0:C_WIDTH])).astype(BF16)
        v = _gelu(_dot(h, w_ref[:, C_WIDTH:2 * C_WIDTH]))
        ms = jnp.mean(v * v, axis=-1, keepdims=True)
        uv_ref[rs, C_WIDTH:] = (v * lax.rsqrt(ms + EPS) * vg_ref[...]).astype(BF16)
        f_ref[rs, :] = _dot(h, w_ref[:, 2 * C_WIDTH:])


def _cd_in_proj(x2, g, shift, scale, w_in_bf, v_norm_g, tm):
    n = x2.shape[0]
    row = lambda i: (0, 0)
    return pl.pallas_call(
        functools.partial(_cd_in_kernel, tm=tm),
        grid=(n // tm,),
        in_specs=[
            pl.BlockSpec((tm, D_MODEL), lambda i: (i, 0)),
            pl.BlockSpec((1, D_MODEL), row),
            pl.BlockSpec((1, D_MODEL), row),
            pl.BlockSpec((1, D_MODEL), row),
            pl.BlockSpec((D_MODEL, CD_IN), row, pipeline_mode=pl.Buffered(1)),
            pl.BlockSpec((1, C_WIDTH), row),
        ],
        out_specs=[
            pl.BlockSpec((tm, 2 * C_WIDTH), lambda i: (i, 0)),
            pl.BlockSpec((tm, D_WIDTH), lambda i: (i, 0)),
        ],
        out_shape=[jax.ShapeDtypeStruct((n, 2 * C_WIDTH), BF16),
                   jax.ShapeDtypeStruct((n, D_WIDTH), F32)],
        compiler_params=_params(("arbitrary",)),
        name="cd_in_proj",
    )(x2, g, shift, scale, w_in_bf, v_norm_g)


def _spatial_kernel(u_ref, v_ref, ws_ref, bias_ref, o_ref, *, chunks):
    for k in range(chunks):
        rs = slice(k * CHUNK, (k + 1) * CHUNK)
        for g in range(C_GROUPS):
            cs = slice(g * C_GROUP_DIM, (g + 1) * C_GROUP_DIM)
            s = _dot(ws_ref[g], v_ref[rs, cs]) + bias_ref[:, cs]
            o_ref[rs, cs] = (u_ref[rs, cs].astype(F32) * s).astype(BF16)


def _spatial_gate(uv, w_spatial_bf, bias_full, chunks=4):
    n = uv.shape[0]
    rows = chunks * CHUNK
    return pl.pallas_call(
        functools.partial(_spatial_kernel, chunks=chunks),
        grid=(n // rows,),
        in_specs=[
            pl.BlockSpec((rows, C_WIDTH), lambda i: (i, 0)),
            pl.BlockSpec((rows, C_WIDTH), lambda i: (i, 1)),
            pl.BlockSpec((C_GROUPS, CHUNK, CHUNK), lambda i: (0, 0, 0)),
            pl.BlockSpec((CHUNK, C_WIDTH), lambda i: (0, 0)),
        ],
        out_specs=pl.BlockSpec((rows, C_WIDTH), lambda i: (i, 0)),
        out_shape=jax.ShapeDtypeStruct((n, C_WIDTH), BF16),
        compiler_params=_params(("arbitrary",)),
        name="spatial_gate",
    )(uv, uv, w_spatial_bf, bias_full)


FFT_SUB = V7X_SUBLANES_F32
FFT1_CT = 512
FFT2_CT = 512


def _fourier_tables(n):
    a_len, b_len, sub = FFT_A, FFT_B, FFT_SUB
    assert a_len * b_len == n
    ch = np.arange(D_GROUP_DIM)
    ang_c = 2.0 * np.pi * ((ch[:, None] * ch[None, :]) % D_GROUP_DIM) / D_GROUP_DIM
    a = np.arange(a_len)
    f_a = np.exp(-2j * np.pi * ((a[:, None] * a[None, :]) % a_len) / a_len)
    m1 = np.kron(f_a, np.eye(sub))
    b = np.arange(b_len)
    tw = np.exp(-2j * np.pi * ((a[:, None] * b[None, :]) % n) / n)
    f_b = np.exp(-2j * np.pi * ((b[:, None] * b[None, :]) % b_len) / b_len)
    m2 = np.einsum('db,pq->dpqb', f_b, np.eye(sub)).reshape(b_len * sub, sub * b_len)
    norm = 1.0 / np.sqrt(float(n) * D_GROUP_DIM)
    m2 = m2 * norm
    tw3 = np.broadcast_to(tw[:, :, None], (a_len, b_len, V7X_LANES))
    f32 = lambda v: jnp.asarray(np.ascontiguousarray(v), dtype=F32)
    return dict(cos_c=f32(np.cos(ang_c)), sin_c=f32(np.sin(ang_c)),
                m1r=f32(m1.real), m1i=f32(m1.imag), m2r=f32(m2.real), m2i=f32(m2.imag),
                twr=f32(tw3.real), twi=f32(tw3.imag))


def _fft1_kernel(f_ref, cc_ref, sc_ref, m1r_ref, m1i_ref, twr_ref, twi_ref, tr_ref, ti_ref):
    rows = FFT_A * FFT_SUB
    ct = FFT1_CT
    fb = f_ref[...].reshape(rows, ct).astype(BF16)
    xr_parts = []
    xi_parts = []
    for q in range(ct // D_GROUP_DIM):
        blk = fb[:, q * D_GROUP_DIM:(q + 1) * D_GROUP_DIM]
        xr_parts.append(_dot(blk, cc_ref[...]))
        xi_parts.append(-_dot(blk, sc_ref[...]))
    xr = jnp.concatenate(xr_parts, axis=1).astype(BF16)
    xi = jnp.concatenate(xi_parts, axis=1).astype(BF16)
    m1r = m1r_ref[...]
    m1i = m1i_ref[...]
    tr = _dot(m1r, xr) - _dot(m1i, xi)
    ti = _dot(m1r, xi) + _dot(m1i, xr)
    reps = ct // V7X_LANES
    twr = jnp.tile(twr_ref[...].reshape(rows, V7X_LANES), (1, reps))
    twi = jnp.tile(twi_ref[...].reshape(rows, V7X_LANES), (1, reps))
    tr_ref[...] = (tr * twr - ti * twi).reshape(FFT_A, FFT_SUB, ct)
    ti_ref[...] = (tr * twi + ti * twr).reshape(FFT_A, FFT_SUB, ct)


def _fft2_kernel(tr_ref, ti_ref, m2r_ref, m2i_ref, wf_ref, o_ref):
    j = pl.program_id(1)
    tr = tr_ref[...].astype(BF16)
    ti = ti_ref[...].astype(BF16)
    z = _dot(m2r_ref[...], tr) - _dot(m2i_ref[...], ti)
    contrib = _dot(z.astype(BF16), wf_ref[...]).reshape(FFT_B, FFT_SUB, D_WIDTH)

    @pl.when(j == 0)
    def _():
        o_ref[...] = contrib

    @pl.when(j > 0)
    def _():
        o_ref[...] += contrib


def _fourier_mix(f, tabs, w_fourier_bf):
    n = f.shape[0]
    a_len, b_len, sub = FFT_A, FFT_B, FFT_SUB
    f3 = f.reshape(a_len, b_len, D_WIDTH)
    rows1 = a_len * sub
    const2 = lambda i, j: (0, 0)
    tr, ti = pl.pallas_call(
        _fft1_kernel,
        grid=(b_len // sub, D_WIDTH // FFT1_CT),
        in_specs=[
            pl.BlockSpec((a_len, sub, FFT1_CT), lambda i, j: (0, i, j)),
            pl.BlockSpec((D_GROUP_DIM, D_GROUP_DIM), const2),
            pl.BlockSpec((D_GROUP_DIM, D_GROUP_DIM), const2),
            pl.BlockSpec((rows1, rows1), const2),
            pl.BlockSpec((rows1, rows1), const2),
            pl.BlockSpec((a_len, sub, V7X_LANES), lambda i, j: (0, i, 0)),
            pl.BlockSpec((a_len, sub, V7X_LANES), lambda i, j: (0, i, 0)),
        ],
        out_specs=[pl.BlockSpec((a_len, sub, FFT1_CT), lambda i, j: (0, i, j))] * 2,
        out_shape=[jax.ShapeDtypeStruct((a_len, b_len, D_WIDTH), F32)] * 2,
        compiler_params=_params(("arbitrary", "arbitrary")),
        name="fourier_stage1",
    )(f3, tabs['cos_c'].astype(BF16), tabs['sin_c'].astype(BF16),
      tabs['m1r'].astype(BF16), tabs['m1i'].astype(BF16), tabs['twr'], tabs['twi'])

    rows2 = sub * b_len
    tr2 = tr.reshape(n, D_WIDTH)
    ti2 = ti.reshape(n, D_WIDTH)
    out = pl.pallas_call(
        _fft2_kernel,
        grid=(a_len // sub, D_WIDTH // FFT2_CT),
        in_specs=[
            pl.BlockSpec((rows2, FFT2_CT), lambda i, j: (i, j)),
            pl.BlockSpec((rows2, FFT2_CT), lambda i, j: (i, j)),
            pl.BlockSpec((rows2, rows2), const2),
            pl.BlockSpec((rows2, rows2), const2),
            pl.BlockSpec((FFT2_CT, D_WIDTH), lambda i, j: (j, 0)),
        ],
        out_specs=pl.BlockSpec((b_len, sub, D_WIDTH), lambda i, j: (0, i, 0)),
        out_shape=jax.ShapeDtypeStruct((b_len, a_len, D_WIDTH), F32),
        compiler_params=_params(("arbitrary", "arbitrary")),
        name="fourier_stage2",
    )(tr2, ti2, tabs['m2r'].astype(BF16), tabs['m2i'].astype(BF16), w_fourier_bf)
    return out.reshape(n, D_WIDTH)


def _rope_tables(n):
    rows = n // GRID_W
    row = np.repeat(np.arange(rows, dtype=np.float64), GRID_W)
    col = np.tile(np.arange(GRID_W, dtype=np.float64), rows)
    inv = ROPE_THETA ** (-np.arange(0, AXIS_DIM, 2, dtype=np.float64) / AXIS_DIM)
    ang_r = row[:, None] * inv[None, :]
    ang_c = col[:, None] * inv[None, :]
    cos = np.concatenate([np.cos(ang_r)] * 2 + [np.cos(ang_c)] * 2, axis=-1)
    sin = np.concatenate([-np.sin(ang_r), np.sin(ang_r), -np.sin(ang_c), np.sin(ang_c)], axis=-1)
    return jnp.asarray(cos, dtype=F32), jnp.asarray(sin, dtype=F32)


def kernel(x, c, ctx, c_ctx, w_mod, b_mod, norm1_g, norm2_g, ab_w_in, a_q_norm_g, a_k_norm_g, a_sink,
           b_w_pool, b_pool_scale, ab_w_out, cd_w_in, c_v_norm_g, c_w_spatial, c_b_spatial, d_w_fourier,
           cd_w_out, f_w_up, f_conv_w, f_conv_b, f_w_down):
    batch, n, _ = x.shape
    ctx_len = ctx.shape[1]
    assert batch == 1 and DEPTH == 2
    x2 = x.reshape(n, D_MODEL)
    ctx2 = ctx.reshape(ctx_len, D_MODEL)

    mod = _mod_vectors(c, c_ctx, w_mod, b_mod)

    def split6(v):
        return [v[:, k * D_MODEL:(k + 1) * D_MODEL] for k in range(6)]

    row1 = lambda v: v.reshape(1, -1)
    w_up_bf = f_w_up.astype(BF16)
    w_down_bf = f_w_down.astype(BF16)

    ml = split6(mod[0, 0:1])
    mc = split6(mod[0, 1:2])
    g1 = row1(norm1_g[0])
    w_in = ab_w_in[0].astype(BF16)
    qn = row1(a_q_norm_g[0])
    kn = row1(a_k_norm_g[0])
    cos, sin = _rope_tables(n)
    qkv, z = _ab_in_proj(x2, g1, ml[0], ml[1], w_in, cos, sin, qn, kn, tm=PROJ_TM)
    ones = jnp.ones((ctx_len, HEAD_DIM), F32)
    zeros = jnp.zeros((ctx_len, HEAD_DIM), F32)
    qkv_ctx, _ = _ab_in_proj(ctx2, g1, mc[0], mc[1], w_in, ones, zeros, qn, kn, tm=ctx_len)
    attn = _window_attention(qkv, qkv_ctx, a_sink[0])
    pooled = _pool_mix(z, b_w_pool[0].astype(BF16), row1(b_pool_scale[0]), tm=POOL_TM)
    x2, h2 = _out_proj(attn, pooled, ab_w_out[0].astype(BF16), x2, ml[2],
                       row1(norm2_g[0]), ml[3], ml[4], tm=PROJ_TM)
    x2 = _conv_ffn(x2, h2, w_up_bf, f_conv_w[0], f_conv_b[0], w_down_bf, ml[5], layer=0, tm=FFN_TM)

    ml = split6(mod[1, 0:1])
    uv, f = _cd_in_proj(x2, row1(norm1_g[1]), ml[0], ml[1], cd_w_in[0].astype(BF16),
                        row1(c_v_norm_g[0]), tm=PROJ_TM)
    bias_full = jnp.repeat(c_b_spatial[0].T, C_GROUP_DIM, axis=1)
    c_out = _spatial_gate(uv, c_w_spatial[0].astype(BF16), bias_full)
    d_out = _fourier_mix(f, _fourier_tables(n), d_w_fourier[0].astype(BF16))
    x2, h2 = _out_proj(c_out, d_out, cd_w_out[0].astype(BF16), x2, ml[2],
                       row1(norm2_g[1]), ml[3], ml[4], tm=PROJ_TM)
    x2 = _conv_ffn(x2, h2, w_up_bf, f_conv_w[1], f_conv_b[1], w_down_bf, ml[5], layer=1, tm=FFN_TM)
    return x2.reshape(batch, n, D_MODEL)
```

```python
import functools

import numpy as np
import jax
import jax.numpy as jnp
from jax import lax
from jax.experimental import pallas as pl
from jax.experimental.pallas import tpu as pltpu

F32 = jnp.float32
BF16 = jnp.bfloat16

D_MODEL = 2048
DEPTH = 2
GRID_W = 64
HEAD_DIM = 128
A_Q_HEADS = 8
A_KV_HEADS = 2
A_GROUP = A_Q_HEADS // A_KV_HEADS
A_Q_DIM = A_Q_HEADS * HEAD_DIM
A_KV_DIM = A_KV_HEADS * HEAD_DIM
A_QKV_DIM = A_Q_DIM + 2 * A_KV_DIM
WINDOW = 128
BLOCK = 128
ROPE_THETA = 10000.0
AXIS_DIM = HEAD_DIM // 2
ATTN_SCALE = HEAD_DIM ** -0.5
NEG_INF = -1e30
B_GROUPS = 4
B_WIDTH = 1024
B_GROUP_DIM = B_WIDTH // B_GROUPS
POOL_WINDOWS = (2, 4, 8, 16)
AB_IN = A_QKV_DIM + B_WIDTH
C_WIDTH = 1024
C_GROUPS = 4
C_GROUP_DIM = C_WIDTH // C_GROUPS
CHUNK = 128
D_WIDTH = 1024
D_GROUPS = 8
D_GROUP_DIM = D_WIDTH // D_GROUPS
CD_IN = 2 * C_WIDTH + D_WIDTH
D_FF = 5632
EPS = 1e-6

V7X_SUBLANES_F32 = 8
V7X_SUBLANES_BF16 = 16
V7X_LANES = 128
V7X_VMEM_BYTES = 64 * 1024 * 1024
VMEM_LIMIT = 56 * 1024 * 1024

PROJ_TM = 512
POOL_TM = 256
FFN_TM = 1024

FFT_A = 64
FFT_B = 128


def _params(sem):
    return pltpu.CompilerParams(dimension_semantics=sem, vmem_limit_bytes=VMEM_LIMIT)


def _dot(a, b):
    return jnp.dot(a, b, preferred_element_type=F32)


def _norm_mod(x, g, shift, scale):
    ms = jnp.mean(x * x, axis=-1, keepdims=True)
    y = x * lax.rsqrt(ms + EPS) * g
    return y * (1.0 + scale) + shift


def _mod_kernel(cv_ref, w_ref, b_ref, o_ref):
    a = cv_ref[...]
    a = a * jax.nn.sigmoid(a)
    w = w_ref[0]
    a_hi = a.astype(BF16)
    a_lo = (a - a_hi.astype(F32)).astype(BF16)
    w_hi = w.astype(BF16)
    w_lo = (w - w_hi.astype(F32)).astype(BF16)
    acc = _dot(a_hi, w_hi) + _dot(a_lo, w_hi) + _dot(a_hi, w_lo)
    o_ref[0] = acc + b_ref[0]


def _mod_vectors(c, c_ctx, w_mod, b_mod):
    tn = 1024
    cv = jnp.concatenate(
        [c.reshape(1, D_MODEL), c_ctx.reshape(1, D_MODEL),
         jnp.zeros((V7X_SUBLANES_F32 - 2, D_MODEL), F32)], axis=0)
    b3 = b_mod.reshape(DEPTH, 1, 6 * D_MODEL)
    return pl.pallas_call(
        _mod_kernel,
        grid=(DEPTH, 6 * D_MODEL // tn),
        in_specs=[
            pl.BlockSpec((V7X_SUBLANES_F32, D_MODEL), lambda l, j: (0, 0)),
            pl.BlockSpec((1, D_MODEL, tn), lambda l, j: (l, 0, j)),
            pl.BlockSpec((1, 1, tn), lambda l, j: (l, 0, j)),
        ],
        out_specs=pl.BlockSpec((1, V7X_SUBLANES_F32, tn), lambda l, j: (l, 0, j)),
        out_shape=jax.ShapeDtypeStruct((DEPTH, V7X_SUBLANES_F32, 6 * D_MODEL), F32),
        compiler_params=_params(("arbitrary", "arbitrary")),
        name="mod_vectors",
    )(cv, w_mod, b3)


AB_TN = 512
ROW_SPLIT = 2


def _rope(t, cos, sin_signed):
    lane = lax.broadcasted_iota(jnp.int32, t.shape, 1)
    first = (lane % AXIS_DIM) < (AXIS_DIM // 2)
    partner = jnp.where(first,
                        pltpu.roll(t, HEAD_DIM - AXIS_DIM // 2, 1),
                        pltpu.roll(t, AXIS_DIM // 2, 1))
    return t * cos + partner * sin_signed


def _head_norm_rope(t, g, cos, sin_signed):
    ms = jnp.mean(t * t, axis=-1, keepdims=True)
    return _rope(t * lax.rsqrt(ms + EPS) * g, cos, sin_signed)


def _ab_in_kernel(x_ref, g_ref, sh_ref, sc_ref, w_ref, cos_ref, sin_ref, qn_ref, kn_ref,
                  qkv_ref, z_ref, *, tm):
    rows_per = tm // ROW_SPLIT
    n_q_tiles = A_Q_DIM // AB_TN
    for part in range(ROW_SPLIT):
        rs = slice(part * rows_per, (part + 1) * rows_per)
        h = _norm_mod(x_ref[rs, :], g_ref[...], sh_ref[...], sc_ref[...]).astype(BF16)
        cos = cos_ref[rs, :]
        sin = sin_ref[rs, :]
        for t in range(AB_IN // AB_TN):
            p = _dot(h, w_ref[:, t * AB_TN:(t + 1) * AB_TN])
            if t < n_q_tiles:
                for hh in range(AB_TN // HEAD_DIM):
                    c0 = t * AB_TN + hh * HEAD_DIM
                    qkv_ref[rs, c0:c0 + HEAD_DIM] = _head_norm_rope(
                        p[:, hh * HEAD_DIM:(hh + 1) * HEAD_DIM], qn_ref[...], cos, sin).astype(BF16)
            elif t == n_q_tiles:
                for hh in range(A_KV_HEADS):
                    c0 = A_Q_DIM + hh * HEAD_DIM
                    qkv_ref[rs, c0:c0 + HEAD_DIM] = _head_norm_rope(
                        p[:, hh * HEAD_DIM:(hh + 1) * HEAD_DIM], kn_ref[...], cos, sin).astype(BF16)
                qkv_ref[rs, A_Q_DIM + A_KV_DIM:] = p[:, A_KV_DIM:].astype(BF16)
            else:
                c0 = (t - n_q_tiles - 1) * AB_TN
                z_ref[rs, c0:c0 + AB_TN] = p


def _ab_in_proj(x2, g, shift, scale, w_in_bf, cos, sin, qn, kn, tm):
    n = x2.shape[0]
    assert A_Q_DIM % AB_TN == 0 and 2 * A_KV_DIM == AB_TN and B_WIDTH % AB_TN == 0
    row = lambda i: (0, 0)
    return pl.pallas_call(
        functools.partial(_ab_in_kernel, tm=tm),
        grid=(n // tm,),
        in_specs=[
            pl.BlockSpec((tm, D_MODEL), lambda i: (i, 0)),
            pl.BlockSpec((1, D_MODEL), row),
            pl.BlockSpec((1, D_MODEL), row),
            pl.BlockSpec((1, D_MODEL), row),
            pl.BlockSpec((D_MODEL, AB_IN), row, pipeline_mode=pl.Buffered(1)),
            pl.BlockSpec((tm, HEAD_DIM), lambda i: (i, 0)),
            pl.BlockSpec((tm, HEAD_DIM), lambda i: (i, 0)),
            pl.BlockSpec((1, HEAD_DIM), row),
            pl.BlockSpec((1, HEAD_DIM), row),
        ],
        out_specs=[
            pl.BlockSpec((tm, A_QKV_DIM), lambda i: (i, 0)),
            pl.BlockSpec((tm, B_WIDTH), lambda i: (i, 0)),
        ],
        out_shape=[jax.ShapeDtypeStruct((n, A_QKV_DIM), BF16),
                   jax.ShapeDtypeStruct((n, B_WIDTH), F32)],
        compiler_params=_params(("arbitrary",)),
        name="ab_in_proj",
    )(x2, g, shift, scale, w_in_bf, cos, sin, qn, kn)


def _attn_kernel(sink_ref, bias_ref, q_ref, kp_ref, kc_ref, kn_ref, vp_ref, vc_ref, vn_ref,
                 kx_ref, vx_ref, o_ref):
    rows = A_GROUP * BLOCK
    bias = bias_ref[...]
    r1 = lax.broadcasted_iota(jnp.int32, (rows, 1), 0) // BLOCK
    for hk in range(A_KV_HEADS):
        hs = slice(hk * HEAD_DIM, (hk + 1) * HEAD_DIM)
        kcat = jnp.concatenate([kp_ref[:, hs], kc_ref[:, hs], kn_ref[:, hs], kx_ref[:, hs]], axis=0)
        vcat = jnp.concatenate([vp_ref[:, hs], vc_ref[:, hs], vn_ref[:, hs], vx_ref[:, hs]], axis=0)
        q0 = hk * A_GROUP * HEAD_DIM
        q4 = jnp.concatenate(
            [q_ref[:, q0 + g * HEAD_DIM:q0 + (g + 1) * HEAD_DIM] for g in range(A_GROUP)], axis=0)
        s = lax.dot_general(q4, kcat, (((1,), (1,)), ((), ())), preferred_element_type=F32)
        s = s * ATTN_SCALE + bias
        sink = jnp.zeros((rows, 1), F32)
        for g in range(A_GROUP):
            sink = jnp.where(r1 == g, sink_ref[hk * A_GROUP + g], sink)
        m = jnp.maximum(jnp.max(s, axis=-1, keepdims=True), sink)
        e = jnp.exp(s - m)
        den = jnp.sum(e, axis=-1, keepdims=True) + jnp.exp(sink - m)
        pr = (e / den).astype(BF16)
        o4 = _dot(pr, vcat)
        for g in range(A_GROUP):
            o_ref[:, q0 + g * HEAD_DIM:q0 + (g + 1) * HEAD_DIM] = (
                o4[g * BLOCK:(g + 1) * BLOCK].astype(BF16))


def _attn_bias(ctx_len):
    rows = A_GROUP * BLOCK
    cols = 3 * BLOCK + ctx_len
    r = np.arange(rows)[:, None] % BLOCK
    c = np.arange(cols)[None, :]
    band = (np.abs(BLOCK + r - c) <= WINDOW) | (c >= 3 * BLOCK)
    first = band & (c >= BLOCK)
    last = band & ((c < 2 * BLOCK) | (c >= 3 * BLOCK))
    masks = np.stack([first, band, last])
    return jnp.asarray(np.where(masks, 0.0, NEG_INF), dtype=F32)


def _window_attention(qkv, qkv_ctx, sink):
    n = qkv.shape[0]
    ctx_len = qkv_ctx.shape[0]
    n_blocks = n // BLOCK
    assert n_blocks >= 2
    k_col = A_Q_DIM // A_KV_DIM
    v_col = k_col + 1
    prev = lambda i: jnp.maximum(i - 1, 0)
    nxt = lambda i: jnp.minimum(i + 1, n_blocks - 1)
    which = lambda i: jnp.where(i == 0, 0, jnp.where(i == n_blocks - 1, 2, 1))
    blk = (BLOCK, A_KV_DIM)
    rows = A_GROUP * BLOCK
    cols = 3 * BLOCK + ctx_len
    return pl.pallas_call(
        _attn_kernel,
        grid=(n_blocks,),
        in_specs=[
            pl.BlockSpec(memory_space=pltpu.SMEM),
            pl.BlockSpec((None, rows, cols), lambda i: (which(i), 0, 0)),
            pl.BlockSpec((BLOCK, A_Q_DIM), lambda i: (i, 0)),
            pl.BlockSpec(blk, lambda i: (prev(i), k_col)),
            pl.BlockSpec(blk, lambda i: (i, k_col)),
            pl.BlockSpec(blk, lambda i: (nxt(i), k_col)),
            pl.BlockSpec(blk, lambda i: (prev(i), v_col)),
            pl.BlockSpec(blk, lambda i: (i, v_col)),
            pl.BlockSpec(blk, lambda i: (nxt(i), v_col)),
            pl.BlockSpec((ctx_len, A_KV_DIM), lambda i: (0, k_col)),
            pl.BlockSpec((ctx_len, A_KV_DIM), lambda i: (0, v_col)),
        ],
        out_specs=pl.BlockSpec((BLOCK, A_Q_DIM), lambda i: (i, 0)),
        out_shape=jax.ShapeDtypeStruct((n, A_Q_DIM), BF16),
        compiler_params=_params(("arbitrary",)),
        name="window_attention",
    )(sink, _attn_bias(ctx_len), qkv, qkv, qkv, qkv, qkv, qkv, qkv, qkv_ctx, qkv_ctx)


POOL_HALO = 8
assert max(POOL_WINDOWS) // 2 <= POOL_HALO


def _pool_kernel(zm_ref, zp_ref, zn_ref, w_ref, ps_ref, o_ref, z_scr, *, n_rows, tm):
    i = pl.program_id(0)
    last = pl.num_programs(0) - 1
    z_scr[0:POOL_HALO, :] = jnp.where(i > 0, zp_ref[...], 0.0)
    z_scr[POOL_HALO:POOL_HALO + tm, :] = zm_ref[...]
    z_scr[POOL_HALO + tm:, :] = jnp.where(i < last, zn_ref[...], 0.0)
    t = i * tm + lax.broadcasted_iota(jnp.int32, (tm, B_GROUP_DIM), 0)
    for g in range(B_GROUPS):
        half = POOL_WINDOWS[g] // 2
        cs = slice(g * B_GROUP_DIM, (g + 1) * B_GROUP_DIM)
        acc = z_scr[pl.ds(POOL_HALO - half, tm), cs]
        for off in range(-half + 1, half):
            acc = acc + z_scr[pl.ds(POOL_HALO + off, tm), cs]
        cnt = (jnp.minimum(t + half, n_rows) - jnp.maximum(t - half, 0)).astype(F32)
        d = (acc / cnt - zm_ref[:, cs]).astype(BF16)
        y = _dot(d, w_ref[g]) * ps_ref[:, cs]
        o_ref[:, cs] = y.astype(BF16)


def _pool_mix(z, w_pool_bf, pool_scale, tm):
    n = z.shape[0]
    hb = tm // POOL_HALO
    n_halo_blocks = n // POOL_HALO
    return pl.pallas_call(
        functools.partial(_pool_kernel, n_rows=n, tm=tm),
        grid=(n // tm,),
        in_specs=[
            pl.BlockSpec((tm, B_WIDTH), lambda i: (i, 0)),
            pl.BlockSpec((POOL_HALO, B_WIDTH), lambda i: (jnp.maximum(i * hb - 1, 0), 0)),
            pl.BlockSpec((POOL_HALO, B_WIDTH), lambda i: (jnp.minimum((i + 1) * hb, n_halo_blocks - 1), 0)),
            pl.BlockSpec((B_GROUPS, B_GROUP_DIM, B_GROUP_DIM), lambda i: (0, 0, 0)),
            pl.BlockSpec((1, B_WIDTH), lambda i: (0, 0)),
        ],
        out_specs=pl.BlockSpec((tm, B_WIDTH), lambda i: (i, 0)),
        out_shape=jax.ShapeDtypeStruct((n, B_WIDTH), BF16),
        scratch_shapes=[pltpu.VMEM((tm + 2 * POOL_HALO, B_WIDTH), F32)],
        compiler_params=_params(("arbitrary",)),
        name="pool_mix",
    )(z, z, z, w_pool_bf, pool_scale)


OUT_TN = 512


def _out_proj_kernel(a1_ref, a2_ref, w_ref, x_ref, gate_ref, g_ref, sh_ref, sc_ref, o_ref, h_ref, *, tm):
    rows_per = tm // ROW_SPLIT
    for part in range(ROW_SPLIT):
        rs = slice(part * rows_per, (part + 1) * rows_per)
        a = jnp.concatenate([a1_ref[rs, :].astype(BF16), a2_ref[rs, :].astype(BF16)], axis=1)
        for t in range(D_MODEL // OUT_TN):
            cs = slice(t * OUT_TN, (t + 1) * OUT_TN)
            o_ref[rs, cs] = x_ref[rs, cs] + gate_ref[:, cs] * _dot(a, w_ref[:, cs])
        h_ref[rs, :] = _norm_mod(o_ref[rs, :], g_ref[...], sh_ref[...], sc_ref[...])


def _out_proj(a1, a2, w_out_bf, x2, gate, g2, shift2, scale2, tm):
    n = x2.shape[0]
    k1 = a1.shape[1]
    k2 = a2.shape[1]
    assert k1 + k2 == w_out_bf.shape[0] and k1 % V7X_LANES == 0
    row = lambda i: (0, 0)
    return pl.pallas_call(
        functools.partial(_out_proj_kernel, tm=tm),
        grid=(n // tm,),
        in_specs=[
            pl.BlockSpec((tm, k1), lambda i: (i, 0)),
            pl.BlockSpec((tm, k2), lambda i: (i, 0)),
            pl.BlockSpec((k1 + k2, D_MODEL), row, pipeline_mode=pl.Buffered(1)),
            pl.BlockSpec((tm, D_MODEL), lambda i: (i, 0)),
            pl.BlockSpec((1, D_MODEL), row),
            pl.BlockSpec((1, D_MODEL), row),
            pl.BlockSpec((1, D_MODEL), row),
            pl.BlockSpec((1, D_MODEL), row),
        ],
        out_specs=[pl.BlockSpec((tm, D_MODEL), lambda i: (i, 0)),
                   pl.BlockSpec((tm, D_MODEL), lambda i: (i, 0))],
        out_shape=[jax.ShapeDtypeStruct((n, D_MODEL), F32),
                   jax.ShapeDtypeStruct((n, D_MODEL), F32)],
        compiler_params=_params(("arbitrary",)),
        name="out_proj",
    )(a1, a2, w_out_bf, x2, gate, g2, shift2, scale2)


FFN_SLABS = V7X_SUBLANES_F32
FFN_EDGE_ROWS = V7X_SUBLANES_BF16
FFN_PIECE_ROWS = 16
FFN_CW = 256
FFN_TF = 2 * FFN_CW
FFN_TN = 512
FFN_CHUNKS = D_FF // FFN_CW
FFN_UP_STEPS = D_FF // FFN_TF
FFN_DOWN_STEPS = D_MODEL // FFN_TN


def _ffn_up_kernel(hm_ref, hp_ref, hn_ref, wg_ref, wv_ref, cw_ref, cb_ref, o_ref,
                   h_scr, carry_a, carry_b, *, tm):
    i = pl.program_id(0)
    j = pl.program_id(1)
    last_i = pl.num_programs(0) - 1
    nc = FFN_CHUNKS
    ns = FFN_SLABS
    sr = tm // ns

    def finish(carry, k, r, step):
        c = jnp.maximum(2 * step + k, 0)

        def rows_of(idx, s, q0, n):
            if q0 < 0:
                return jnp.concatenate([carry[idx, tm:tm + 1, :],
                                        carry[idx, pl.ds(s * sr, n - 1), :]], axis=0)
            if q0 + n > sr:
                return jnp.concatenate([carry[idx, pl.ds(s * sr + q0, n - 1), :],
                                        carry[idx, tm + 1:tm + 2, :]], axis=0)
            return carry[idx, pl.ds(s * sr + q0, n), :]

        def conv(idx, kk, q0):
            n = FFN_PIECE_ROWS
            mid = rows_of(idx, r, q0, n)
            prev = rows_of(idx, r - 1, q0, n) if r > 0 else rows_of(idx, ns - 1, q0 - 1, n)
            nxt = rows_of(idx, r + 1, q0, n) if r < ns - 1 else rows_of(idx, 0, q0 + 1, n)
            return prev * cw_ref[kk, 0] + mid * cw_ref[kk, 1] + nxt * cw_ref[kk, 2] + cb_ref[kk]

        for q0 in range(0, sr, FFN_PIECE_ROWS):
            gg = conv(k, c, q0)
            vv = conv(2 + k, c + nc, q0)
            o_ref[pl.ds(r * sr + q0, FFN_PIECE_ROWS), k * FFN_CW:(k + 1) * FFN_CW] = (
                gg * jax.nn.sigmoid(gg) * vv).astype(BF16)

    @pl.when(j == 0)
    def _():
        hm = hm_ref[...].reshape(sr, ns, D_MODEL)
        h_scr[0:tm, :] = jnp.swapaxes(hm, 0, 1).reshape(tm, D_MODEL).astype(BF16)
        before = jnp.where(i > 0, hp_ref[ns - 1:ns, :], 0.0)
        after = jnp.where(i < last_i, hn_ref[0:1, :], 0.0)
        pad = jnp.zeros((FFN_EDGE_ROWS - 2, D_MODEL), F32)
        h_scr[tm:, :] = jnp.concatenate([before, after, pad], axis=0).astype(BF16)
        carry_b[...] = jnp.zeros_like(carry_b)

    def up_step(carry_w, carry_r):
        for k in range(2):
            for r in range(ns):
                finish(carry_r, k, r, j - 1)
        for k in range(2):
            cs = slice(k * FFN_CW, (k + 1) * FFN_CW)
            carry_w[k] = _dot(h_scr[...], wg_ref[:, cs])
            carry_w[2 + k] = _dot(h_scr[...], wv_ref[:, cs])

    @pl.when((j < FFN_UP_STEPS) & (j % 2 == 0))
    def _():
        up_step(carry_a, carry_b)

    @pl.when((j < FFN_UP_STEPS) & (j % 2 == 1))
    def _():
        up_step(carry_b, carry_a)

    @pl.when(j == FFN_UP_STEPS)
    def _():
        last = carry_a if (FFN_UP_STEPS - 1) % 2 == 0 else carry_b
        for k in range(2):
            for r in range(ns):
                finish(last, k, r, FFN_UP_STEPS - 1)


def _ffn_down_kernel(a_ref, wd_ref, x_ref, gate_ref, o_ref, *, tm):
    ns = FFN_SLABS
    sr = tm // ns
    y = _dot(a_ref[:, FFN_TF:], wd_ref[...])
    y = jnp.swapaxes(y.reshape(ns, sr, FFN_TN), 0, 1).reshape(tm, FFN_TN)
    o_ref[...] = x_ref[...] + gate_ref[...] * y


def _conv_ffn(x2, h2, w_up_bf, conv_w, conv_b, w_down_bf, gate, layer, tm):
    n = x2.shape[0]
    nj = FFN_UP_STEPS
    hb = tm // FFN_SLABS
    n_halo_blocks = n // FFN_SLABS
    act_cols = D_FF + FFN_TF
    cw3 = conv_w.reshape(3, 2 * FFN_CHUNKS, FFN_CW).transpose(1, 0, 2)
    cw3 = jnp.broadcast_to(cw3[:, :, None, :], (2 * FFN_CHUNKS, 3, FFN_PIECE_ROWS, FFN_CW))
    cb3 = jnp.broadcast_to(conv_b.reshape(2 * FFN_CHUNKS, 1, FFN_CW),
                           (2 * FFN_CHUNKS, FFN_PIECE_ROWS, FFN_CW))
    act = pl.pallas_call(
        functools.partial(_ffn_up_kernel, tm=tm),
        grid=(n // tm, nj + 1),
        in_specs=[
            pl.BlockSpec((tm, D_MODEL), lambda i, j: (i, 0)),
            pl.BlockSpec((FFN_SLABS, D_MODEL), lambda i, j: (jnp.maximum(i * hb - 1, 0), 0)),
            pl.BlockSpec((FFN_SLABS, D_MODEL), lambda i, j: (jnp.minimum((i + 1) * hb, n_halo_blocks - 1), 0)),
            pl.BlockSpec((None, D_MODEL, FFN_TF), lambda i, j: (layer, 0, jnp.minimum(j, nj - 1))),
            pl.BlockSpec((None, D_MODEL, FFN_TF), lambda i, j: (layer, 0, jnp.minimum(j, nj - 1) + nj)),
            pl.BlockSpec((2 * FFN_CHUNKS, 3, FFN_PIECE_ROWS, FFN_CW), lambda i, j: (0, 0, 0, 0)),
            pl.BlockSpec((2 * FFN_CHUNKS, FFN_PIECE_ROWS, FFN_CW), lambda i, j: (0, 0, 0)),
        ],
        out_specs=pl.BlockSpec((tm, FFN_TF), lambda i, j: (i, j)),
        out_shape=jax.ShapeDtypeStruct((n, act_cols), BF16),
        scratch_shapes=[pltpu.VMEM((tm + FFN_EDGE_ROWS, D_MODEL), BF16),
                        pltpu.VMEM((4, tm + FFN_EDGE_ROWS, FFN_CW), F32),
                        pltpu.VMEM((4, tm + FFN_EDGE_ROWS, FFN_CW), F32)],
        compiler_params=_params(("arbitrary", "arbitrary")),
        name="ffn_up",
    )(h2, h2, h2, w_up_bf, w_up_bf, cw3, cb3)
    return pl.pallas_call(
        functools.partial(_ffn_down_kernel, tm=tm),
        grid=(n // tm, FFN_DOWN_STEPS),
        in_specs=[
            pl.BlockSpec((tm, act_cols), lambda i, j: (i, 0)),
            pl.BlockSpec((None, D_FF, FFN_TN), lambda i, j: (layer, 0, j)),
            pl.BlockSpec((tm, FFN_TN), lambda i, j: (i, j)),
            pl.BlockSpec((1, FFN_TN), lambda i, j: (0, j)),
        ],
        out_specs=pl.BlockSpec((tm, FFN_TN), lambda i, j: (i, j)),
        out_shape=jax.ShapeDtypeStruct((n, D_MODEL), F32),
        compiler_params=_params(("arbitrary", "arbitrary")),
        name="ffn_down",
    )(act, w_down_bf, x2, gate)


def _gelu(x):
    return 0.5 * x * (1.0 + lax.erf(x * (2.0 ** -0.5)))


def _cd_in_kernel(x_ref, g_ref, sh_ref, sc_ref, w_ref, vg_ref, uv_ref, f_ref, *, tm):
    rows_per = tm // ROW_SPLIT
    for part in range(ROW_SPLIT):
        rs = slice(part * rows_per, (part + 1) * rows_per)
        h = _norm_mod(x_ref[rs, :], g_ref[...], sh_ref[...], sc_ref[...]).astype(BF16)
        uv_ref[rs, 0:C_WIDTH] = _gelu(_dot(h, w_ref[:, 0:C_WIDTH])).astype(BF16)
        v = _gelu(_dot(h, w_ref[:, C_WIDTH:2 * C_WIDTH]))
        ms = jnp.mean(v * v, axis=-1, keepdims=True)
        uv_ref[rs, C_WIDTH:] = (v * lax.rsqrt(ms + EPS) * vg_ref[...]).astype(BF16)
        f_ref[rs, :] = _dot(h, w_ref[:, 2 * C_WIDTH:])


def _cd_in_proj(x2, g, shift, scale, w_in_bf, v_norm_g, tm):
    n = x2.shape[0]
    row = lambda i: (0, 0)
    return pl.pallas_call(
        functools.partial(_cd_in_kernel, tm=tm),
        grid=(n // tm,),
        in_specs=[
            pl.BlockSpec((tm, D_MODEL), lambda i: (i, 0)),
            pl.BlockSpec((1, D_MODEL), row),
            pl.BlockSpec((1, D_MODEL), row),
            pl.BlockSpec((1, D_MODEL), row),
            pl.BlockSpec((D_MODEL, CD_IN), row, pipeline_mode=pl.Buffered(1)),
            pl.BlockSpec((1, C_WIDTH), row),
        ],
        out_specs=[
            pl.BlockSpec((tm, 2 * C_WIDTH), lambda i: (i, 0)),
            pl.BlockSpec((tm, D_WIDTH), lambda i: (i, 0)),
        ],
        out_shape=[jax.ShapeDtypeStruct((n, 2 * C_WIDTH), BF16),
                   jax.ShapeDtypeStruct((n, D_WIDTH), F32)],
        compiler_params=_params(("arbitrary",)),
        name="cd_in_proj",
    )(x2, g, shift, scale, w_in_bf, v_norm_g)


def _spatial_kernel(u_ref, v_ref, ws_ref, bias_ref, o_ref, *, chunks):
    for k in range(chunks):
        rs = slice(k * CHUNK, (k + 1) * CHUNK)
        for g in range(C_GROUPS):
            cs = slice(g * C_GROUP_DIM, (g + 1) * C_GROUP_DIM)
            s = _dot(ws_ref[g], v_ref[rs, cs]) + bias_ref[:, cs]
            o_ref[rs, cs] = (u_ref[rs, cs].astype(F32) * s).astype(BF16)


def _spatial_gate(uv, w_spatial_bf, bias_full, chunks=4):
    n = uv.shape[0]
    rows = chunks * CHUNK
    return pl.pallas_call(
        functools.partial(_spatial_kernel, chunks=chunks),
        grid=(n // rows,),
        in_specs=[
            pl.BlockSpec((rows, C_WIDTH), lambda i: (i, 0)),
            pl.BlockSpec((rows, C_WIDTH), lambda i: (i, 1)),
            pl.BlockSpec((C_GROUPS, CHUNK, CHUNK), lambda i: (0, 0, 0)),
            pl.BlockSpec((CHUNK, C_WIDTH), lambda i: (0, 0)),
        ],
        out_specs=pl.BlockSpec((rows, C_WIDTH), lambda i: (i, 0)),
        out_shape=jax.ShapeDtypeStruct((n, C_WIDTH), BF16),
        compiler_params=_params(("arbitrary",)),
        name="spatial_gate",
    )(uv, uv, w_spatial_bf, bias_full)


FFT_SUB = V7X_SUBLANES_F32
FFT1_CT = 512
FFT2_CT = 512


def _fourier_tables(n):
    a_len, b_len, sub = FFT_A, FFT_B, FFT_SUB
    assert a_len * b_len == n
    ch = np.arange(D_GROUP_DIM)
    ang_c = 2.0 * np.pi * ((ch[:, None] * ch[None, :]) % D_GROUP_DIM) / D_GROUP_DIM
    a = np.arange(a_len)
    f_a = np.exp(-2j * np.pi * ((a[:, None] * a[None, :]) % a_len) / a_len)
    m1 = np.kron(f_a, np.eye(sub))
    b = np.arange(b_len)
    tw = np.exp(-2j * np.pi * ((a[:, None] * b[None, :]) % n) / n)
    f_b = np.exp(-2j * np.pi * ((b[:, None] * b[None, :]) % b_len) / b_len)
    m2 = np.einsum('db,pq->dpqb', f_b, np.eye(sub)).reshape(b_len * sub, sub * b_len)
    norm = 1.0 / np.sqrt(float(n) * D_GROUP_DIM)
    m2 = m2 * norm
    tw3 = np.broadcast_to(tw[:, :, None], (a_len, b_len, V7X_LANES))
    f32 = lambda v: jnp.asarray(np.ascontiguousarray(v), dtype=F32)
    return dict(cos_c=f32(np.cos(ang_c)), sin_c=f32(np.sin(ang_c)),
                m1r=f32(m1.real), m1i=f32(m1.imag), m2r=f32(m2.real), m2i=f32(m2.imag),
                twr=f32(tw3.real), twi=f32(tw3.imag))


def _fft1_kernel(f_ref, cc_ref, sc_ref, m1r_ref, m1i_ref, twr_ref, twi_ref, tr_ref, ti_ref):
    rows = FFT_A * FFT_SUB
    ct = FFT1_CT
    fb = f_ref[...].reshape(rows, ct).astype(BF16)
    xr_parts = []
    xi_parts = []
    for q in range(ct // D_GROUP_DIM):
        blk = fb[:, q * D_GROUP_DIM:(q + 1) * D_GROUP_DIM]
        xr_parts.append(_dot(blk, cc_ref[...]))
        xi_parts.append(-_dot(blk, sc_ref[...]))
    xr = jnp.concatenate(xr_parts, axis=1).astype(BF16)
    xi = jnp.concatenate(xi_parts, axis=1).astype(BF16)
    m1r = m1r_ref[...]
    m1i = m1i_ref[...]
    tr = _dot(m1r, xr) - _dot(m1i, xi)
    ti = _dot(m1r, xi) + _dot(m1i, xr)
    reps = ct // V7X_LANES
    twr = jnp.tile(twr_ref[...].reshape(rows, V7X_LANES), (1, reps))
    twi = jnp.tile(twi_ref[...].reshape(rows, V7X_LANES), (1, reps))
    tr_ref[...] = (tr * twr - ti * twi).reshape(FFT_A, FFT_SUB, ct)
    ti_ref[...] = (tr * twi + ti * twr).reshape(FFT_A, FFT_SUB, ct)


def _fft2_kernel(tr_ref, ti_ref, m2r_ref, m2i_ref, wf_ref, o_ref):
    j = pl.program_id(1)
    tr = tr_ref[...].astype(BF16)
    ti = ti_ref[...].astype(BF16)
    z = _dot(m2r_ref[...], tr) - _dot(m2i_ref[...], ti)
    contrib = _dot(z.astype(BF16), wf_ref[...]).reshape(FFT_B, FFT_SUB, D_WIDTH)

    @pl.when(j == 0)
    def _():
        o_ref[...] = contrib

    @pl.when(j > 0)
    def _():
        o_ref[...] += contrib


def _fourier_mix(f, tabs, w_fourier_bf):
    n = f.shape[0]
    a_len, b_len, sub = FFT_A, FFT_B, FFT_SUB
    f3 = f.reshape(a_len, b_len, D_WIDTH)
    rows1 = a_len * sub
    const2 = lambda i, j: (0, 0)
    tr, ti = pl.pallas_call(
        _fft1_kernel,
        grid=(b_len // sub, D_WIDTH // FFT1_CT),
        in_specs=[
            pl.BlockSpec((a_len, sub, FFT1_CT), lambda i, j: (0, i, j)),
            pl.BlockSpec((D_GROUP_DIM, D_GROUP_DIM), const2),
            pl.BlockSpec((D_GROUP_DIM, D_GROUP_DIM), const2),
            pl.BlockSpec((rows1, rows1), const2),
            pl.BlockSpec((rows1, rows1), const2),
            pl.BlockSpec((a_len, sub, V7X_LANES), lambda i, j: (0, i, 0)),
            pl.BlockSpec((a_len, sub, V7X_LANES), lambda i, j: (0, i, 0)),
        ],
        out_specs=[pl.BlockSpec((a_len, sub, FFT1_CT), lambda i, j: (0, i, j))] * 2,
        out_shape=[jax.ShapeDtypeStruct((a_len, b_len, D_WIDTH), F32)] * 2,
        compiler_params=_params(("arbitrary", "arbitrary")),
        name="fourier_stage1",
    )(f3, tabs['cos_c'].astype(BF16), tabs['sin_c'].astype(BF16),
      tabs['m1r'].astype(BF16), tabs['m1i'].astype(BF16), tabs['twr'], tabs['twi'])

    rows2 = sub * b_len
    tr2 = tr.reshape(n, D_WIDTH)
    ti2 = ti.reshape(n, D_WIDTH)
    out = pl.pallas_call(
        _fft2_kernel,
        grid=(a_len // sub, D_WIDTH // FFT2_CT),
        in_specs=[
            pl.BlockSpec((rows2, FFT2_CT), lambda i, j: (i, j)),
            pl.BlockSpec((rows2, FFT2_CT), lambda i, j: (i, j)),
            pl.BlockSpec((rows2, rows2), const2),
            pl.BlockSpec((rows2, rows2), const2),
            pl.BlockSpec((FFT2_CT, D_WIDTH), lambda i, j: (j, 0)),
        ],
        out_specs=pl.BlockSpec((b_len, sub, D_WIDTH), lambda i, j: (0, i, 0)),
        out_shape=jax.ShapeDtypeStruct((b_len, a_len, D_WIDTH), F32),
        compiler_params=_params(("arbitrary", "arbitrary")),
        name="fourier_stage2",
    )(tr2, ti2, tabs['m2r'].astype(BF16), tabs['m2i'].astype(BF16), w_fourier_bf)
    return out.reshape(n, D_WIDTH)


def _rope_tables(n):
    rows = n // GRID_W
    row = np.repeat(np.arange(rows, dtype=np.float64), GRID_W)
    col = np.tile(np.arange(GRID_W, dtype=np.float64), rows)
    inv = ROPE_THETA ** (-np.arange(0, AXIS_DIM, 2, dtype=np.float64) / AXIS_DIM)
    ang_r = row[:, None] * inv[None, :]
    ang_c = col[:, None] * inv[None, :]
    cos = np.concatenate([np.cos(ang_r)] * 2 + [np.cos(ang_c)] * 2, axis=-1)
    sin = np.concatenate([-np.sin(ang_r), np.sin(ang_r), -np.sin(ang_c), np.sin(ang_c)], axis=-1)
    return jnp.asarray(cos, dtype=F32), jnp.asarray(sin, dtype=F32)


def kernel(x, c, ctx, c_ctx, w_mod, b_mod, norm1_g, norm2_g, ab_w_in, a_q_norm_g, a_k_norm_g, a_sink,
           b_w_pool, b_pool_scale, ab_w_out, cd_w_in, c_v_norm_g, c_w_spatial, c_b_spatial, d_w_fourier,
           cd_w_out, f_w_up, f_conv_w, f_conv_b, f_w_down):
    batch, n, _ = x.shape
    ctx_len = ctx.shape[1]
    assert batch == 1 and DEPTH == 2
    x2 = x.reshape(n, D_MODEL)
    ctx2 = ctx.reshape(ctx_len, D_MODEL)

    mod = _mod_vectors(c, c_ctx, w_mod, b_mod)

    def split6(v):
        return [v[:, k * D_MODEL:(k + 1) * D_MODEL] for k in range(6)]

    row1 = lambda v: v.reshape(1, -1)
    w_up_bf = f_w_up.astype(BF16)
    w_down_bf = f_w_down.astype(BF16)

    ml = split6(mod[0, 0:1])
    mc = split6(mod[0, 1:2])
    g1 = row1(norm1_g[0])
    w_in = ab_w_in[0].astype(BF16)
    qn = row1(a_q_norm_g[0])
    kn = row1(a_k_norm_g[0])
    cos, sin = _rope_tables(n)
    qkv, z = _ab_in_proj(x2, g1, ml[0], ml[1], w_in, cos, sin, qn, kn, tm=PROJ_TM)
    ones = jnp.ones((ctx_len, HEAD_DIM), F32)
    zeros = jnp.zeros((ctx_len, HEAD_DIM), F32)
    qkv_ctx, _ = _ab_in_proj(ctx2, g1, mc[0], mc[1], w_in, ones, zeros, qn, kn, tm=ctx_len)
    attn = _window_attention(qkv, qkv_ctx, a_sink[0])
    pooled = _pool_mix(z, b_w_pool[0].astype(BF16), row1(b_pool_scale[0]), tm=POOL_TM)
    x2, h2 = _out_proj(attn, pooled, ab_w_out[0].astype(BF16), x2, ml[2],
                       row1(norm2_g[0]), ml[3], ml[4], tm=PROJ_TM)
    x2 = _conv_ffn(x2, h2, w_up_bf, f_conv_w[0], f_conv_b[0], w_down_bf, ml[5], layer=0, tm=FFN_TM)

    ml = split6(mod[1, 0:1])
    uv, f = _cd_in_proj(x2, row1(norm1_g[1]), ml[0], ml[1], cd_w_in[0].astype(BF16),
                        row1(c_v_norm_g[0]), tm=PROJ_TM)
    bias_full = jnp.repeat(c_b_spatial[0].T, C_GROUP_DIM, axis=1)
    c_out = _spatial_gate(uv, c_w_spatial[0].astype(BF16), bias_full)
    d_out = _fourier_mix(f, _fourier_tables(n), d_w_fourier[0].astype(BF16))
    x2, h2 = _out_proj(c_out, d_out, cd_w_out[0].astype(BF16), x2, ml[2],
                       row1(norm2_g[1]), ml[3], ml[4], tm=PROJ_TM)
    x2 = _conv_ffn(x2, h2, w_up_bf, f_conv_w[1], f_conv_b[1], w_down_bf, ml[5], layer=1, tm=FFN_TM)
    return x2.reshape(batch, n, D_MODEL)
```

```python
import functools

import numpy as np
import jax
import jax.numpy as jnp
from jax import lax
from jax.experimental import pallas as pl
from jax.experimental.pallas import tpu as pltpu

F32 = jnp.float32
BF16 = jnp.bfloat16

D_MODEL = 2048
DEPTH = 2
GRID_W = 64
HEAD_DIM = 128
A_Q_HEADS = 8
A_KV_HEADS = 2
A_GROUP = A_Q_HEADS // A_KV_HEADS
A_Q_DIM = A_Q_HEADS * HEAD_DIM
A_KV_DIM = A_KV_HEADS * HEAD_DIM
A_QKV_DIM = A_Q_DIM + 2 * A_KV_DIM
WINDOW = 128
BLOCK = 128
ROPE_THETA = 10000.0
AXIS_DIM = HEAD_DIM // 2
ATTN_SCALE = HEAD_DIM ** -0.5
NEG_INF = -1e30
B_GROUPS = 4
B_WIDTH = 1024
B_GROUP_DIM = B_WIDTH // B_GROUPS
POOL_WINDOWS = (2, 4, 8, 16)
AB_IN = A_QKV_DIM + B_WIDTH
C_WIDTH = 1024
C_GROUPS = 4
C_GROUP_DIM = C_WIDTH // C_GROUPS
CHUNK = 128
D_WIDTH = 1024
D_GROUPS = 8
D_GROUP_DIM = D_WIDTH // D_GROUPS
CD_IN = 2 * C_WIDTH + D_WIDTH
D_FF = 5632
EPS = 1e-6

V7X_SUBLANES_F32 = 8
V7X_SUBLANES_BF16 = 16
V7X_LANES = 128
V7X_VMEM_BYTES = 64 * 1024 * 1024
VMEM_LIMIT = 56 * 1024 * 1024

PROJ_TM = 512
POOL_TM = 256
FFN_TM = 1024

FFT_A = 64
FFT_B = 128


def _params(sem):
    return pltpu.CompilerParams(dimension_semantics=sem, vmem_limit_bytes=VMEM_LIMIT)


def _dot(a, b):
    return jnp.dot(a, b, preferred_element_type=F32)


class _CastJob:
    def __init__(self, src, layer, rows, n_steps):
        _, total_rows, self.cols = src.shape
        assert total_rows % rows == 0 and total_rows // rows <= n_steps
        self.src, self.layer, self.rows = src, layer, rows
        self.last = total_rows // rows - 1
        self.out_shape = jax.ShapeDtypeStruct((total_rows, self.cols), BF16)

    def specs(self, step_of):
        blk = lambda *ids: jnp.minimum(step_of(*ids), self.last)
        return (pl.BlockSpec((None, self.rows, self.cols), lambda *ids: (self.layer, blk(*ids), 0)),
                pl.BlockSpec((self.rows, self.cols), lambda *ids: (blk(*ids), 0)))


def _host_cast(kernel_fn, n_in, n_out):
    def body(*refs):
        src = refs[n_in]
        dst = refs[n_in + 1 + n_out]
        dst[...] = src[...].astype(BF16)
        kernel_fn(*refs[:n_in], *refs[n_in + 1:n_in + 1 + n_out], *refs[n_in + 2 + n_out:])
    return body


def _pallas(kernel_fn, *, grid, in_specs, out_specs, out_shape, args, name, scratch_shapes=(),
            cast=None, step_of=None):
    out_specs, out_shape = list(out_specs), list(out_shape)
    if cast is not None:
        src_spec, dst_spec = cast.specs(step_of)
        kernel_fn = _host_cast(kernel_fn, len(in_specs), len(out_specs))
        in_specs = [*in_specs, src_spec]
        out_specs.append(dst_spec)
        out_shape.append(cast.out_shape)
        args = (*args, cast.src)
    return pl.pallas_call(
        kernel_fn, grid=grid, in_specs=list(in_specs), out_specs=out_specs, out_shape=out_shape,
        scratch_shapes=list(scratch_shapes), compiler_params=_params(("arbitrary",) * len(grid)),
        name=name)(*args)


def _norm_mod(x, g, shift, scale):
    ms = jnp.mean(x * x, axis=-1, keepdims=True)
    y = x * lax.rsqrt(ms + EPS) * g
    return y * (1.0 + scale) + shift


def _mod_kernel(cv_ref, w_ref, b_ref, o_ref):
    a = cv_ref[...]
    a = a * jax.nn.sigmoid(a)
    w = w_ref[0]
    a_hi = a.astype(BF16)
    a_lo = (a - a_hi.astype(F32)).astype(BF16)
    w_hi = w.astype(BF16)
    w_lo = (w - w_hi.astype(F32)).astype(BF16)
    acc = _dot(a_hi, w_hi) + _dot(a_lo, w_hi) + _dot(a_hi, w_lo)
    o_ref[0] = acc + b_ref[0]


def _mod_vectors(c, c_ctx, w_mod, b_mod):
    tn = 1024
    cv = jnp.concatenate(
        [c.reshape(1, D_MODEL), c_ctx.reshape(1, D_MODEL),
         jnp.zeros((V7X_SUBLANES_F32 - 2, D_MODEL), F32)], axis=0)
    b3 = b_mod.reshape(DEPTH, 1, 6 * D_MODEL)
    return pl.pallas_call(
        _mod_kernel,
        grid=(DEPTH, 6 * D_MODEL // tn),
        in_specs=[
            pl.BlockSpec((V7X_SUBLANES_F32, D_MODEL), lambda l, j: (0, 0)),
            pl.BlockSpec((1, D_MODEL, tn), lambda l, j: (l, 0, j)),
            pl.BlockSpec((1, 1, tn), lambda l, j: (l, 0, j)),
        ],
        out_specs=pl.BlockSpec((1, V7X_SUBLANES_F32, tn), lambda l, j: (l, 0, j)),
        out_shape=jax.ShapeDtypeStruct((DEPTH, V7X_SUBLANES_F32, 6 * D_MODEL), F32),
        compiler_params=_params(("arbitrary", "arbitrary")),
        name="mod_vectors",
    )(cv, w_mod, b3)


AB_TN = 512
ROW_SPLIT = 2


def _rope(t, cos, sin_signed):
    lane = lax.broadcasted_iota(jnp.int32, t.shape, 1)
    first = (lane % AXIS_DIM) < (AXIS_DIM // 2)
    partner = jnp.where(first,
                        pltpu.roll(t, HEAD_DIM - AXIS_DIM // 2, 1),
                        pltpu.roll(t, AXIS_DIM // 2, 1))
    return t * cos + partner * sin_signed


def _head_norm_rope(t, g, cos, sin_signed):
    ms = jnp.mean(t * t, axis=-1, keepdims=True)
    return _rope(t * lax.rsqrt(ms + EPS) * g, cos, sin_signed)


def _ab_in_kernel(x_ref, g_ref, sh_ref, sc_ref, w_ref, cos_ref, sin_ref, qn_ref, kn_ref,
                  qkv_ref, z_ref, *, tm):
    rows_per = tm // ROW_SPLIT
    n_q_tiles = A_Q_DIM // AB_TN
    for part in range(ROW_SPLIT):
        rs = slice(part * rows_per, (part + 1) * rows_per)
        h = _norm_mod(x_ref[rs, :], g_ref[...], sh_ref[...], sc_ref[...]).astype(BF16)
        cos = cos_ref[rs, :]
        sin = sin_ref[rs, :]
        for t in range(AB_IN // AB_TN):
            p = _dot(h, w_ref[:, t * AB_TN:(t + 1) * AB_TN])
            if t < n_q_tiles:
                for hh in range(AB_TN // HEAD_DIM):
                    c0 = t * AB_TN + hh * HEAD_DIM
                    qkv_ref[rs, c0:c0 + HEAD_DIM] = _head_norm_rope(
                        p[:, hh * HEAD_DIM:(hh + 1) * HEAD_DIM], qn_ref[...], cos, sin).astype(BF16)
            elif t == n_q_tiles:
                for hh in range(A_KV_HEADS):
                    c0 = A_Q_DIM + hh * HEAD_DIM
                    qkv_ref[rs, c0:c0 + HEAD_DIM] = _head_norm_rope(
                        p[:, hh * HEAD_DIM:(hh + 1) * HEAD_DIM], kn_ref[...], cos, sin).astype(BF16)
                qkv_ref[rs, A_Q_DIM + A_KV_DIM:] = p[:, A_KV_DIM:].astype(BF16)
            else:
                c0 = (t - n_q_tiles - 1) * AB_TN
                z_ref[rs, c0:c0 + AB_TN] = p


def _ab_in_proj(x2, g, shift, scale, w_in_bf, cos, sin, qn, kn, tm, cast=None):
    n = x2.shape[0]
    assert A_Q_DIM % AB_TN == 0 and 2 * A_KV_DIM == AB_TN and B_WIDTH % AB_TN == 0
    row = lambda i: (0, 0)
    return _pallas(
        functools.partial(_ab_in_kernel, tm=tm),
        grid=(n // tm,),
        cast=cast, step_of=lambda i: i,
        in_specs=[
            pl.BlockSpec((tm, D_MODEL), lambda i: (i, 0)),
            pl.BlockSpec((1, D_MODEL), row),
            pl.BlockSpec((1, D_MODEL), row),
            pl.BlockSpec((1, D_MODEL), row),
            pl.BlockSpec((D_MODEL, AB_IN), row, pipeline_mode=pl.Buffered(1)),
            pl.BlockSpec((tm, HEAD_DIM), lambda i: (i, 0)),
            pl.BlockSpec((tm, HEAD_DIM), lambda i: (i, 0)),
            pl.BlockSpec((1, HEAD_DIM), row),
            pl.BlockSpec((1, HEAD_DIM), row),
        ],
        out_specs=[
            pl.BlockSpec((tm, A_QKV_DIM), lambda i: (i, 0)),
            pl.BlockSpec((tm, B_WIDTH), lambda i: (i, 0)),
        ],
        out_shape=[jax.ShapeDtypeStruct((n, A_QKV_DIM), BF16),
                   jax.ShapeDtypeStruct((n, B_WIDTH), F32)],
        name="ab_in_proj",
        args=(x2, g, shift, scale, w_in_bf, cos, sin, qn, kn))


def _attn_kernel(sink_ref, bias_ref, q_ref, kp_ref, kc_ref, kn_ref, vp_ref, vc_ref, vn_ref,
                 kx_ref, vx_ref, o_ref):
    rows = A_GROUP * BLOCK
    bias = bias_ref[...]
    r1 = lax.broadcasted_iota(jnp.int32, (rows, 1), 0) // BLOCK
    for hk in range(A_KV_HEADS):
        hs = slice(hk * HEAD_DIM, (hk + 1) * HEAD_DIM)
        kcat = jnp.concatenate([kp_ref[:, hs], kc_ref[:, hs], kn_ref[:, hs], kx_ref[:, hs]], axis=0)
        vcat = jnp.concatenate([vp_ref[:, hs], vc_ref[:, hs], vn_ref[:, hs], vx_ref[:, hs]], axis=0)
        q0 = hk * A_GROUP * HEAD_DIM
        q4 = jnp.concatenate(
            [q_ref[:, q0 + g * HEAD_DIM:q0 + (g + 1) * HEAD_DIM] for g in range(A_GROUP)], axis=0)
        s = lax.dot_general(q4, kcat, (((1,), (1,)), ((), ())), preferred_element_type=F32)
        s = s * ATTN_SCALE + bias
        sink = jnp.zeros((rows, 1), F32)
        for g in range(A_GROUP):
            sink = jnp.where(r1 == g, sink_ref[hk * A_GROUP + g], sink)
        m = jnp.maximum(jnp.max(s, axis=-1, keepdims=True), sink)
        e = jnp.exp(s - m)
        den = jnp.sum(e, axis=-1, keepdims=True) + jnp.exp(sink - m)
        pr = (e / den).astype(BF16)
        o4 = _dot(pr, vcat)
        for g in range(A_GROUP):
            o_ref[:, q0 + g * HEAD_DIM:q0 + (g + 1) * HEAD_DIM] = (
                o4[g * BLOCK:(g + 1) * BLOCK].astype(BF16))


def _attn_bias(ctx_len):
    rows = A_GROUP * BLOCK
    cols = 3 * BLOCK + ctx_len
    r = np.arange(rows)[:, None] % BLOCK
    c = np.arange(cols)[None, :]
    band = (np.abs(BLOCK + r - c) <= WINDOW) | (c >= 3 * BLOCK)
    first = band & (c >= BLOCK)
    last = band & ((c < 2 * BLOCK) | (c >= 3 * BLOCK))
    masks = np.stack([first, band, last])
    return jnp.asarray(np.where(masks, 0.0, NEG_INF), dtype=F32)


def _window_attention(qkv, qkv_ctx, sink, cast=None):
    n = qkv.shape[0]
    ctx_len = qkv_ctx.shape[0]
    n_blocks = n // BLOCK
    assert n_blocks >= 2
    k_col = A_Q_DIM // A_KV_DIM
    v_col = k_col + 1
    prev = lambda i: jnp.maximum(i - 1, 0)
    nxt = lambda i: jnp.minimum(i + 1, n_blocks - 1)
    which = lambda i: jnp.where(i == 0, 0, jnp.where(i == n_blocks - 1, 2, 1))
    blk = (BLOCK, A_KV_DIM)
    rows = A_GROUP * BLOCK
    cols = 3 * BLOCK + ctx_len
    return _pallas(
        _attn_kernel,
        grid=(n_blocks,),
        cast=cast, step_of=lambda i: i,
        in_specs=[
            pl.BlockSpec(memory_space=pltpu.SMEM),
            pl.BlockSpec((None, rows, cols), lambda i: (which(i), 0, 0)),
            pl.BlockSpec((BLOCK, A_Q_DIM), lambda i: (i, 0)),
            pl.BlockSpec(blk, lambda i: (prev(i), k_col)),
            pl.BlockSpec(blk, lambda i: (i, k_col)),
            pl.BlockSpec(blk, lambda i: (nxt(i), k_col)),
            pl.BlockSpec(blk, lambda i: (prev(i), v_col)),
            pl.BlockSpec(blk, lambda i: (i, v_col)),
            pl.BlockSpec(blk, lambda i: (nxt(i), v_col)),
            pl.BlockSpec((ctx_len, A_KV_DIM), lambda i: (0, k_col)),
            pl.BlockSpec((ctx_len, A_KV_DIM), lambda i: (0, v_col)),
        ],
        out_specs=[pl.BlockSpec((BLOCK, A_Q_DIM), lambda i: (i, 0))],
        out_shape=[jax.ShapeDtypeStruct((n, A_Q_DIM), BF16)],
        name="window_attention",
        args=(sink, _attn_bias(ctx_len), qkv, qkv, qkv, qkv, qkv, qkv, qkv, qkv_ctx, qkv_ctx))


POOL_HALO = 8
assert max(POOL_WINDOWS) // 2 <= POOL_HALO


def _pool_kernel(zm_ref, zp_ref, zn_ref, w_ref, ps_ref, o_ref, z_scr, *, n_rows, tm):
    i = pl.program_id(0)
    last = pl.num_programs(0) - 1
    z_scr[0:POOL_HALO, :] = jnp.where(i > 0, zp_ref[...], 0.0)
    z_scr[POOL_HALO:POOL_HALO + tm, :] = zm_ref[...]
    z_scr[POOL_HALO + tm:, :] = jnp.where(i < last, zn_ref[...], 0.0)
    t = i * tm + lax.broadcasted_iota(jnp.int32, (tm, B_GROUP_DIM), 0)
    for g in range(B_GROUPS):
        half = POOL_WINDOWS[g] // 2
        cs = slice(g * B_GROUP_DIM, (g + 1) * B_GROUP_DIM)
        acc = z_scr[pl.ds(POOL_HALO - half, tm), cs]
        for off in range(-half + 1, half):
            acc = acc + z_scr[pl.ds(POOL_HALO + off, tm), cs]
        cnt = (jnp.minimum(t + half, n_rows) - jnp.maximum(t - half, 0)).astype(F32)
        d = (acc / cnt - zm_ref[:, cs]).astype(BF16)
        y = _dot(d, w_ref[g]) * ps_ref[:, cs]
        o_ref[:, cs] = y.astype(BF16)


def _pool_mix(z, w_pool_bf, pool_scale, tm):
    n = z.shape[0]
    hb = tm // POOL_HALO
    n_halo_blocks = n // POOL_HALO
    return pl.pallas_call(
        functools.partial(_pool_kernel, n_rows=n, tm=tm),
        grid=(n // tm,),
        in_specs=[
            pl.BlockSpec((tm, B_WIDTH), lambda i: (i, 0)),
            pl.BlockSpec((POOL_HALO, B_WIDTH), lambda i: (jnp.maximum(i * hb - 1, 0), 0)),
            pl.BlockSpec((POOL_HALO, B_WIDTH), lambda i: (jnp.minimum((i + 1) * hb, n_halo_blocks - 1), 0)),
            pl.BlockSpec((B_GROUPS, B_GROUP_DIM, B_GROUP_DIM), lambda i: (0, 0, 0)),
            pl.BlockSpec((1, B_WIDTH), lambda i: (0, 0)),
        ],
        out_specs=pl.BlockSpec((tm, B_WIDTH), lambda i: (i, 0)),
        out_shape=jax.ShapeDtypeStruct((n, B_WIDTH), BF16),
        scratch_shapes=[pltpu.VMEM((tm + 2 * POOL_HALO, B_WIDTH), F32)],
        compiler_params=_params(("arbitrary",)),
        name="pool_mix",
    )(z, z, z, w_pool_bf, pool_scale)


OUT_TN = 512


def _out_proj_kernel(a1_ref, a2_ref, w_ref, x_ref, gate_ref, g_ref, sh_ref, sc_ref, o_ref, h_ref, *, tm):
    rows_per = tm // ROW_SPLIT
    for part in range(ROW_SPLIT):
        rs = slice(part * rows_per, (part + 1) * rows_per)
        a = jnp.concatenate([a1_ref[rs, :].astype(BF16), a2_ref[rs, :].astype(BF16)], axis=1)
        for t in range(D_MODEL // OUT_TN):
            cs = slice(t * OUT_TN, (t + 1) * OUT_TN)
            o_ref[rs, cs] = x_ref[rs, cs] + gate_ref[:, cs] * _dot(a, w_ref[:, cs])
        h_ref[rs, :] = _norm_mod(o_ref[rs, :], g_ref[...], sh_ref[...], sc_ref[...])


def _out_proj(a1, a2, w_out_bf, x2, gate, g2, shift2, scale2, tm):
    n = x2.shape[0]
    k1 = a1.shape[1]
    k2 = a2.shape[1]
    assert k1 + k2 == w_out_bf.shape[0] and k1 % V7X_LANES == 0
    row = lambda i: (0, 0)
    return pl.pallas_call(
        functools.partial(_out_proj_kernel, tm=tm),
        grid=(n // tm,),
        in_specs=[
            pl.BlockSpec((tm, k1), lambda i: (i, 0)),
            pl.BlockSpec((tm, k2), lambda i: (i, 0)),
            pl.BlockSpec((k1 + k2, D_MODEL), row, pipeline_mode=pl.Buffered(1)),
            pl.BlockSpec((tm, D_MODEL), lambda i: (i, 0)),
            pl.BlockSpec((1, D_MODEL), row),
            pl.BlockSpec((1, D_MODEL), row),
            pl.BlockSpec((1, D_MODEL), row),
            pl.BlockSpec((1, D_MODEL), row),
        ],
        out_specs=[pl.BlockSpec((tm, D_MODEL), lambda i: (i, 0)),
                   pl.BlockSpec((tm, D_MODEL), lambda i: (i, 0))],
        out_shape=[jax.ShapeDtypeStruct((n, D_MODEL), F32),
                   jax.ShapeDtypeStruct((n, D_MODEL), F32)],
        compiler_params=_params(("arbitrary",)),
        name="out_proj",
    )(a1, a2, w_out_bf, x2, gate, g2, shift2, scale2)


FFN_SLABS = V7X_SUBLANES_F32
FFN_EDGE_ROWS = V7X_SUBLANES_BF16
FFN_PIECE_ROWS = 16
FFN_CW = 256
FFN_TF = 2 * FFN_CW
FFN_TN = 512
FFN_CHUNKS = D_FF // FFN_CW
FFN_UP_STEPS = D_FF // FFN_TF
FFN_DOWN_STEPS = D_MODEL // FFN_TN


def _ffn_up_kernel(hm_ref, hp_ref, hn_ref, wg_ref, wv_ref, cw_ref, cb_ref, o_ref,
                   h_scr, carry_a, carry_b, *, tm):
    i = pl.program_id(0)
    j = pl.program_id(1)
    last_i = pl.num_programs(0) - 1
    nc = FFN_CHUNKS
    ns = FFN_SLABS
    sr = tm // ns

    def finish(carry, k, r, step):
        c = jnp.maximum(2 * step + k, 0)

        def rows_of(idx, s, q0, n):
            if q0 < 0:
                return jnp.concatenate([carry[idx, tm:tm + 1, :],
                                        carry[idx, pl.ds(s * sr, n - 1), :]], axis=0)
            if q0 + n > sr:
                return jnp.concatenate([carry[idx, pl.ds(s * sr + q0, n - 1), :],
                                        carry[idx, tm + 1:tm + 2, :]], axis=0)
            return carry[idx, pl.ds(s * sr + q0, n), :]

        def conv(idx, kk, q0):
            n = FFN_PIECE_ROWS
            mid = rows_of(idx, r, q0, n)
            prev = rows_of(idx, r - 1, q0, n) if r > 0 else rows_of(idx, ns - 1, q0 - 1, n)
            nxt = rows_of(idx, r + 1, q0, n) if r < ns - 1 else rows_of(idx, 0, q0 + 1, n)
            return prev * cw_ref[kk, 0] + mid * cw_ref[kk, 1] + nxt * cw_ref[kk, 2] + cb_ref[kk]

        for q0 in range(0, sr, FFN_PIECE_ROWS):
            gg = conv(k, c, q0)
            vv = conv(2 + k, c + nc, q0)
            o_ref[pl.ds(r * sr + q0, FFN_PIECE_ROWS), k * FFN_CW:(k + 1) * FFN_CW] = (
                gg * jax.nn.sigmoid(gg) * vv).astype(BF16)

    @pl.when(j == 0)
    def _():
        hm = hm_ref[...].reshape(sr, ns, D_MODEL)
        h_scr[0:tm, :] = jnp.swapaxes(hm, 0, 1).reshape(tm, D_MODEL).astype(BF16)
        before = jnp.where(i > 0, hp_ref[ns - 1:ns, :], 0.0)
        after = jnp.where(i < last_i, hn_ref[0:1, :], 0.0)
        pad = jnp.zeros((FFN_EDGE_ROWS - 2, D_MODEL), F32)
        h_scr[tm:, :] = jnp.concatenate([before, after, pad], axis=0).astype(BF16)
        carry_b[...] = jnp.zeros_like(carry_b)

    def up_step(carry_w, carry_r):
        for k in range(2):
            for r in range(ns):
                finish(carry_r, k, r, j - 1)
        for k in range(2):
            cs = slice(k * FFN_CW, (k + 1) * FFN_CW)
            carry_w[k] = _dot(h_scr[...], wg_ref[:, cs])
            carry_w[2 + k] = _dot(h_scr[...], wv_ref[:, cs])

    @pl.when((j < FFN_UP_STEPS) & (j % 2 == 0))
    def _():
        up_step(carry_a, carry_b)

    @pl.when((j < FFN_UP_STEPS) & (j % 2 == 1))
    def _():
        up_step(carry_b, carry_a)

    @pl.when(j == FFN_UP_STEPS)
    def _():
        last = carry_a if (FFN_UP_STEPS - 1) % 2 == 0 else carry_b
        for k in range(2):
            for r in range(ns):
                finish(last, k, r, FFN_UP_STEPS - 1)


def _ffn_down_kernel(a_ref, wd_ref, x_ref, gate_ref, o_ref, *, tm):
    ns = FFN_SLABS
    sr = tm // ns
    y = _dot(a_ref[:, FFN_TF:], wd_ref[...])
    y = jnp.swapaxes(y.reshape(ns, sr, FFN_TN), 0, 1).reshape(tm, FFN_TN)
    o_ref[...] = x_ref[...] + gate_ref[...] * y


def _conv_ffn(x2, h2, w_up_bf, conv_w, conv_b, w_down_bf, gate, tm, up_cast=None, down_cast=None):
    n = x2.shape[0]
    nj = FFN_UP_STEPS
    hb = tm // FFN_SLABS
    n_halo_blocks = n // FFN_SLABS
    act_cols = D_FF + FFN_TF
    cw3 = conv_w.reshape(3, 2 * FFN_CHUNKS, FFN_CW).transpose(1, 0, 2)
    cw3 = jnp.broadcast_to(cw3[:, :, None, :], (2 * FFN_CHUNKS, 3, FFN_PIECE_ROWS, FFN_CW))
    cb3 = jnp.broadcast_to(conv_b.reshape(2 * FFN_CHUNKS, 1, FFN_CW),
                           (2 * FFN_CHUNKS, FFN_PIECE_ROWS, FFN_CW))
    once = pl.Buffered(1)
    act, *up_cast = _pallas(
        functools.partial(_ffn_up_kernel, tm=tm),
        grid=(n // tm, nj + 1),
        cast=up_cast, step_of=lambda i, j: i * (nj + 1) + j,
        in_specs=[
            pl.BlockSpec((tm, D_MODEL), lambda i, j: (i, 0)),
            pl.BlockSpec((FFN_SLABS, D_MODEL), lambda i, j: (jnp.maximum(i * hb - 1, 0), 0)),
            pl.BlockSpec((FFN_SLABS, D_MODEL), lambda i, j: (jnp.minimum((i + 1) * hb, n_halo_blocks - 1), 0)),
            pl.BlockSpec((D_MODEL, FFN_TF), lambda i, j: (0, jnp.minimum(j, nj - 1))),
            pl.BlockSpec((D_MODEL, FFN_TF), lambda i, j: (0, jnp.minimum(j, nj - 1) + nj)),
            pl.BlockSpec((2 * FFN_CHUNKS, 3, FFN_PIECE_ROWS, FFN_CW), lambda i, j: (0, 0, 0, 0),
                         pipeline_mode=once),
            pl.BlockSpec((2 * FFN_CHUNKS, FFN_PIECE_ROWS, FFN_CW), lambda i, j: (0, 0, 0),
                         pipeline_mode=once),
        ],
        out_specs=[pl.BlockSpec((tm, FFN_TF), lambda i, j: (i, j))],
        out_shape=[jax.ShapeDtypeStruct((n, act_cols), BF16)],
        scratch_shapes=[pltpu.VMEM((tm + FFN_EDGE_ROWS, D_MODEL), BF16),
                        pltpu.VMEM((4, tm + FFN_EDGE_ROWS, FFN_CW), F32),
                        pltpu.VMEM((4, tm + FFN_EDGE_ROWS, FFN_CW), F32)],
        name="ffn_up",
        args=(h2, h2, h2, w_up_bf, w_up_bf, cw3, cb3))
    out, *down_cast = _pallas(
        functools.partial(_ffn_down_kernel, tm=tm),
        grid=(n // tm, FFN_DOWN_STEPS),
        cast=down_cast, step_of=lambda i, j: i * FFN_DOWN_STEPS + j,
        in_specs=[
            pl.BlockSpec((tm, act_cols), lambda i, j: (i, 0)),
            pl.BlockSpec((D_FF, FFN_TN), lambda i, j: (0, j)),
            pl.BlockSpec((tm, FFN_TN), lambda i, j: (i, j)),
            pl.BlockSpec((1, FFN_TN), lambda i, j: (0, j)),
        ],
        out_specs=[pl.BlockSpec((tm, FFN_TN), lambda i, j: (i, j))],
        out_shape=[jax.ShapeDtypeStruct((n, D_MODEL), F32)],
        name="ffn_down",
        args=(act, w_down_bf, x2, gate))
    return out, up_cast, down_cast


def _gelu(x):
    return 0.5 * x * (1.0 + lax.erf(x * (2.0 ** -0.5)))


def _cd_in_kernel(x_ref, g_ref, sh_ref, sc_ref, w_ref, vg_ref, uv_ref, f_ref, *, tm):
    rows_per = tm // ROW_SPLIT
    for part in range(ROW_SPLIT):
        rs = slice(part * rows_per, (part + 1) * rows_per)
        h = _norm_mod(x_ref[rs, :], g_ref[...], sh_ref[...], sc_ref[...]).astype(BF16)
        uv_ref[rs, 0:C_WIDTH] = _gelu(_dot(h, w_ref[:, 0:C_WIDTH])).astype(BF16)
        v = _gelu(_dot(h, w_ref[:, C_WIDTH:2 * C_WIDTH]))
        ms = jnp.mean(v * v, axis=-1, keepdims=True)
        uv_ref[rs, C_WIDTH:] = (v * lax.rsqrt(ms + EPS) * vg_ref[...]).astype(BF16)
        f_ref[rs, :] = _dot(h, w_ref[:, 2 * C_WIDTH:])


def _cd_in_proj(x2, g, shift, scale, w_in_bf, v_norm_g, tm):
    n = x2.shape[0]
    row = lambda i: (0, 0)
    return pl.pallas_call(
        functools.partial(_cd_in_kernel, tm=tm),
        grid=(n // tm,),
        in_specs=[
            pl.BlockSpec((tm, D_MODEL), lambda i: (i, 0)),
            pl.BlockSpec((1, D_MODEL), row),
            pl.BlockSpec((1, D_MODEL), row),
            pl.BlockSpec((1, D_MODEL), row),
            pl.BlockSpec((D_MODEL, CD_IN), row, pipeline_mode=pl.Buffered(1)),
            pl.BlockSpec((1, C_WIDTH), row),
        ],
        out_specs=[
            pl.BlockSpec((tm, 2 * C_WIDTH), lambda i: (i, 0)),
            pl.BlockSpec((tm, D_WIDTH), lambda i: (i, 0)),
        ],
        out_shape=[jax.ShapeDtypeStruct((n, 2 * C_WIDTH), BF16),
                   jax.ShapeDtypeStruct((n, D_WIDTH), F32)],
        compiler_params=_params(("arbitrary",)),
        name="cd_in_proj",
    )(x2, g, shift, scale, w_in_bf, v_norm_g)


def _spatial_kernel(u_ref, v_ref, ws_ref, bias_ref, o_ref, *, chunks):
    for k in range(chunks):
        rs = slice(k * CHUNK, (k + 1) * CHUNK)
        for g in range(C_GROUPS):
            cs = slice(g * C_GROUP_DIM, (g + 1) * C_GROUP_DIM)
            s = _dot(ws_ref[g], v_ref[rs, cs]) + bias_ref[:, cs]
            o_ref[rs, cs] = (u_ref[rs, cs].astype(F32) * s).astype(BF16)


def _spatial_gate(uv, w_spatial_bf, bias_full, chunks=4):
    n = uv.shape[0]
    rows = chunks * CHUNK
    return pl.pallas_call(
        functools.partial(_spatial_kernel, chunks=chunks),
        grid=(n // rows,),
        in_specs=[
            pl.BlockSpec((rows, C_WIDTH), lambda i: (i, 0)),
            pl.BlockSpec((rows, C_WIDTH), lambda i: (i, 1)),
            pl.BlockSpec((C_GROUPS, CHUNK, CHUNK), lambda i: (0, 0, 0)),
            pl.BlockSpec((CHUNK, C_WIDTH), lambda i: (0, 0)),
        ],
        out_specs=pl.BlockSpec((rows, C_WIDTH), lambda i: (i, 0)),
        out_shape=jax.ShapeDtypeStruct((n, C_WIDTH), BF16),
        compiler_params=_params(("arbitrary",)),
        name="spatial_gate",
    )(uv, uv, w_spatial_bf, bias_full)


FFT_SUB = V7X_SUBLANES_F32
FFT1_CT = 512
FFT2_CT = 512


def _fourier_tables(n):
    a_len, b_len, sub = FFT_A, FFT_B, FFT_SUB
    assert a_len * b_len == n
    ch = np.arange(D_GROUP_DIM)
    ang_c = 2.0 * np.pi * ((ch[:, None] * ch[None, :]) % D_GROUP_DIM) / D_GROUP_DIM
    a = np.arange(a_len)
    f_a = np.exp(-2j * np.pi * ((a[:, None] * a[None, :]) % a_len) / a_len)
    m1 = np.kron(f_a, np.eye(sub))
    b = np.arange(b_len)
    tw = np.exp(-2j * np.pi * ((a[:, None] * b[None, :]) % n) / n)
    f_b = np.exp(-2j * np.pi * ((b[:, None] * b[None, :]) % b_len) / b_len)
    m2 = np.einsum('db,pq->dpqb', f_b, np.eye(sub)).reshape(b_len * sub, sub * b_len)
    norm = 1.0 / np.sqrt(float(n) * D_GROUP_DIM)
    m2 = m2 * norm
    tw3 = np.broadcast_to(tw[:, :, None], (a_len, b_len, V7X_LANES))
    f32 = lambda v: jnp.asarray(np.ascontiguousarray(v), dtype=F32)
    return dict(cos_c=f32(np.cos(ang_c)), sin_c=f32(np.sin(ang_c)),
                m1r=f32(m1.real), m1i=f32(m1.imag), m2r=f32(m2.real), m2i=f32(m2.imag),
                twr=f32(tw3.real), twi=f32(tw3.imag))


def _fft1_kernel(f_ref, cc_ref, sc_ref, m1r_ref, m1i_ref, twr_ref, twi_ref, tr_ref, ti_ref):
    rows = FFT_A * FFT_SUB
    ct = FFT1_CT
    fb = f_ref[...].reshape(rows, ct).astype(BF16)
    xr_parts = []
    xi_parts = []
    for q in range(ct // D_GROUP_DIM):
        blk = fb[:, q * D_GROUP_DIM:(q + 1) * D_GROUP_DIM]
        xr_parts.append(_dot(blk, cc_ref[...]))
        xi_parts.append(-_dot(blk, sc_ref[...]))
    xr = jnp.concatenate(xr_parts, axis=1).astype(BF16)
    xi = jnp.concatenate(xi_parts, axis=1).astype(BF16)
    m1r = m1r_ref[...]
    m1i = m1i_ref[...]
    tr = _dot(m1r, xr) - _dot(m1i, xi)
    ti = _dot(m1r, xi) + _dot(m1i, xr)
    reps = ct // V7X_LANES
    twr = jnp.tile(twr_ref[...].reshape(rows, V7X_LANES), (1, reps))
    twi = jnp.tile(twi_ref[...].reshape(rows, V7X_LANES), (1, reps))
    tr_ref[...] = (tr * twr - ti * twi).reshape(FFT_A, FFT_SUB, ct)
    ti_ref[...] = (tr * twi + ti * twr).reshape(FFT_A, FFT_SUB, ct)


def _fft2_kernel(tr_ref, ti_ref, m2r_ref, m2i_ref, wf_ref, o_ref):
    j = pl.program_id(1)
    tr = tr_ref[...].astype(BF16)
    ti = ti_ref[...].astype(BF16)
    z = _dot(m2r_ref[...], tr) - _dot(m2i_ref[...], ti)
    contrib = _dot(z.astype(BF16), wf_ref[...]).reshape(FFT_B, FFT_SUB, D_WIDTH)

    @pl.when(j == 0)
    def _():
        o_ref[...] = contrib

    @pl.when(j > 0)
    def _():
        o_ref[...] += contrib


def _fourier_mix(f, tabs, w_fourier_bf):
    n = f.shape[0]
    a_len, b_len, sub = FFT_A, FFT_B, FFT_SUB
    f3 = f.reshape(a_len, b_len, D_WIDTH)
    rows1 = a_len * sub
    const2 = lambda i, j: (0, 0)
    tr, ti = pl.pallas_call(
        _fft1_kernel,
        grid=(b_len // sub, D_WIDTH // FFT1_CT),
        in_specs=[
            pl.BlockSpec((a_len, sub, FFT1_CT), lambda i, j: (0, i, j)),
            pl.BlockSpec((D_GROUP_DIM, D_GROUP_DIM), const2),
            pl.BlockSpec((D_GROUP_DIM, D_GROUP_DIM), const2),
            pl.BlockSpec((rows1, rows1), const2),
            pl.BlockSpec((rows1, rows1), const2),
            pl.BlockSpec((a_len, sub, V7X_LANES), lambda i, j: (0, i, 0)),
            pl.BlockSpec((a_len, sub, V7X_LANES), lambda i, j: (0, i, 0)),
        ],
        out_specs=[pl.BlockSpec((a_len, sub, FFT1_CT), lambda i, j: (0, i, j))] * 2,
        out_shape=[jax.ShapeDtypeStruct((a_len, b_len, D_WIDTH), F32)] * 2,
        compiler_params=_params(("arbitrary", "arbitrary")),
        name="fourier_stage1",
    )(f3, tabs['cos_c'].astype(BF16), tabs['sin_c'].astype(BF16),
      tabs['m1r'].astype(BF16), tabs['m1i'].astype(BF16), tabs['twr'], tabs['twi'])

    rows2 = sub * b_len
    tr2 = tr.reshape(n, D_WIDTH)
    ti2 = ti.reshape(n, D_WIDTH)
    out = pl.pallas_call(
        _fft2_kernel,
        grid=(a_len // sub, D_WIDTH // FFT2_CT),
        in_specs=[
            pl.BlockSpec((rows2, FFT2_CT), lambda i, j: (i, j)),
            pl.BlockSpec((rows2, FFT2_CT), lambda i, j: (i, j)),
            pl.BlockSpec((rows2, rows2), const2),
            pl.BlockSpec((rows2, rows2), const2),
            pl.BlockSpec((FFT2_CT, D_WIDTH), lambda i, j: (j, 0)),
        ],
        out_specs=pl.BlockSpec((b_len, sub, D_WIDTH), lambda i, j: (0, i, 0)),
        out_shape=jax.ShapeDtypeStruct((b_len, a_len, D_WIDTH), F32),
        compiler_params=_params(("arbitrary", "arbitrary")),
        name="fourier_stage2",
    )(tr2, ti2, tabs['m2r'].astype(BF16), tabs['m2i'].astype(BF16), w_fourier_bf)
    return out.reshape(n, D_WIDTH)


def _rope_tables(n):
    rows = n // GRID_W
    row = np.repeat(np.arange(rows, dtype=np.float64), GRID_W)
    col = np.tile(np.arange(GRID_W, dtype=np.float64), rows)
    inv = ROPE_THETA ** (-np.arange(0, AXIS_DIM, 2, dtype=np.float64) / AXIS_DIM)
    ang_r = row[:, None] * inv[None, :]
    ang_c = col[:, None] * inv[None, :]
    cos = np.concatenate([np.cos(ang_r)] * 2 + [np.cos(ang_c)] * 2, axis=-1)
    sin = np.concatenate([-np.sin(ang_r), np.sin(ang_r), -np.sin(ang_c), np.sin(ang_c)], axis=-1)
    return jnp.asarray(cos, dtype=F32), jnp.asarray(sin, dtype=F32)


def kernel(x, c, ctx, c_ctx, w_mod, b_mod, norm1_g, norm2_g, ab_w_in, a_q_norm_g, a_k_norm_g, a_sink,
           b_w_pool, b_pool_scale, ab_w_out, cd_w_in, c_v_norm_g, c_w_spatial, c_b_spatial, d_w_fourier,
           cd_w_out, f_w_up, f_conv_w, f_conv_b, f_w_down):
    batch, n, _ = x.shape
    ctx_len = ctx.shape[1]
    assert batch == 1 and DEPTH == 2
    x2 = x.reshape(n, D_MODEL)
    ctx2 = ctx.reshape(ctx_len, D_MODEL)

    mod = _mod_vectors(c, c_ctx, w_mod, b_mod)

    def split6(v):
        return [v[:, k * D_MODEL:(k + 1) * D_MODEL] for k in range(6)]

    row1 = lambda v: v.reshape(1, -1)
    n_tiles = n // PROJ_TM
    n_blocks = n // BLOCK
    ffn_tiles = n // FFN_TM
    cast_down0 = _CastJob(f_w_down, 0, D_FF // n_tiles, n_tiles)
    cast_up0 = _CastJob(f_w_up, 0, D_MODEL // n_blocks, n_blocks)
    cast_up1 = _CastJob(f_w_up, 1, D_MODEL // n_blocks, ffn_tiles * (FFN_UP_STEPS + 1))
    cast_down1 = _CastJob(f_w_down, 1, D_FF // (ffn_tiles * FFN_DOWN_STEPS), ffn_tiles * FFN_DOWN_STEPS)

    ml = split6(mod[0, 0:1])
    mc = split6(mod[0, 1:2])
    g1 = row1(norm1_g[0])
    w_in = ab_w_in[0].astype(BF16)
    qn = row1(a_q_norm_g[0])
    kn = row1(a_k_norm_g[0])
    cos, sin = _rope_tables(n)
    qkv, z, w_down0 = _ab_in_proj(x2, g1, ml[0], ml[1], w_in, cos, sin, qn, kn, tm=PROJ_TM,
                                  cast=cast_down0)
    ones = jnp.ones((ctx_len, HEAD_DIM), F32)
    zeros = jnp.zeros((ctx_len, HEAD_DIM), F32)
    qkv_ctx, _ = _ab_in_proj(ctx2, g1, mc[0], mc[1], w_in, ones, zeros, qn, kn, tm=ctx_len)
    attn, w_up0 = _window_attention(qkv, qkv_ctx, a_sink[0], cast=cast_up0)
    pooled = _pool_mix(z, b_w_pool[0].astype(BF16), row1(b_pool_scale[0]), tm=POOL_TM)
    x2, h2 = _out_proj(attn, pooled, ab_w_out[0].astype(BF16), x2, ml[2],
                       row1(norm2_g[0]), ml[3], ml[4], tm=PROJ_TM)
    x2, (w_up1,), (w_down1,) = _conv_ffn(x2, h2, w_up0, f_conv_w[0], f_conv_b[0], w_down0, ml[5],
                                         tm=FFN_TM, up_cast=cast_up1, down_cast=cast_down1)

    ml = split6(mod[1, 0:1])
    uv, f = _cd_in_proj(x2, row1(norm1_g[1]), ml[0], ml[1], cd_w_in[0].astype(BF16),
                        row1(c_v_norm_g[0]), tm=PROJ_TM)
    bias_full = jnp.repeat(c_b_spatial[0].T, C_GROUP_DIM, axis=1)
    c_out = _spatial_gate(uv, c_w_spatial[0].astype(BF16), bias_full)
    d_out = _fourier_mix(f, _fourier_tables(n), d_w_fourier[0].astype(BF16))
    x2, h2 = _out_proj(c_out, d_out, cd_w_out[0].astype(BF16), x2, ml[2],
                       row1(norm2_g[1]), ml[3], ml[4], tm=PROJ_TM)
    x2, _, _ = _conv_ffn(x2, h2, w_up1, f_conv_w[1], f_conv_b[1], w_down1, ml[5], tm=FFN_TM)
    return x2.reshape(batch, n, D_MODEL)
```

```python
import functools

import numpy as np
import jax
import jax.numpy as jnp
from jax import lax
from jax.experimental import pallas as pl
from jax.experimental.pallas import tpu as pltpu

F32 = jnp.float32
BF16 = jnp.bfloat16

D_MODEL = 2048
DEPTH = 2
GRID_W = 64
HEAD_DIM = 128
A_Q_HEADS = 8
A_KV_HEADS = 2
A_GROUP = A_Q_HEADS // A_KV_HEADS
A_Q_DIM = A_Q_HEADS * HEAD_DIM
A_KV_DIM = A_KV_HEADS * HEAD_DIM
A_QKV_DIM = A_Q_DIM + 2 * A_KV_DIM
WINDOW = 128
BLOCK = 128
ROPE_THETA = 10000.0
AXIS_DIM = HEAD_DIM // 2
ATTN_SCALE = HEAD_DIM ** -0.5
NEG_INF = -1e30
B_GROUPS = 4
B_WIDTH = 1024
B_GROUP_DIM = B_WIDTH // B_GROUPS
POOL_WINDOWS = (2, 4, 8, 16)
AB_IN = A_QKV_DIM + B_WIDTH
C_WIDTH = 1024
C_GROUPS = 4
C_GROUP_DIM = C_WIDTH // C_GROUPS
CHUNK = 128
D_WIDTH = 1024
D_GROUPS = 8
D_GROUP_DIM = D_WIDTH // D_GROUPS
CD_IN = 2 * C_WIDTH + D_WIDTH
D_FF = 5632
EPS = 1e-6

V7X_SUBLANES_F32 = 8
V7X_SUBLANES_BF16 = 16
V7X_LANES = 128
V7X_VMEM_BYTES = 64 * 1024 * 1024
VMEM_LIMIT = 56 * 1024 * 1024

PROJ_TM = 512
POOL_TM = 256
FFN_TM = 1024

FFT_A = 64
FFT_B = 128


def _params(sem):
    return pltpu.CompilerParams(dimension_semantics=sem, vmem_limit_bytes=VMEM_LIMIT)


def _dot(a, b):
    return jnp.dot(a, b, preferred_element_type=F32)


class _CastJob:
    def __init__(self, src, layer, rows, n_steps, col_block=None):
        _, total_rows, self.cols = src.shape
        assert total_rows % rows == 0 and total_rows // rows <= n_steps
        self.src, self.layer, self.rows, self.col_block = src, layer, rows, col_block
        self.last = total_rows // rows - 1
        if col_block is None:
            self.out_shape = jax.ShapeDtypeStruct((total_rows, self.cols), BF16)
        else:
            assert self.cols % col_block == 0
            self.out_shape = jax.ShapeDtypeStruct((self.cols // col_block, total_rows, col_block), BF16)

    def specs(self, step_of):
        blk = lambda *ids: jnp.minimum(step_of(*ids), self.last)
        src = pl.BlockSpec((None, self.rows, self.cols), lambda *ids: (self.layer, blk(*ids), 0))
        if self.col_block is None:
            return src, pl.BlockSpec((self.rows, self.cols), lambda *ids: (blk(*ids), 0))
        return src, pl.BlockSpec((self.cols // self.col_block, self.rows, self.col_block),
                                 lambda *ids: (0, blk(*ids), 0))

    def run(self, src_ref, dst_ref):
        if self.col_block is None:
            dst_ref[...] = src_ref[...].astype(BF16)
        else:
            for b in range(self.cols // self.col_block):
                dst_ref[b] = src_ref[:, b * self.col_block:(b + 1) * self.col_block].astype(BF16)


def _host_cast(kernel_fn, cast, n_in, n_out):
    def body(*refs):
        cast.run(refs[n_in], refs[n_in + 1 + n_out])
        kernel_fn(*refs[:n_in], *refs[n_in + 1:n_in + 1 + n_out], *refs[n_in + 2 + n_out:])
    return body


def _pallas(kernel_fn, *, grid, in_specs, out_specs, out_shape, args, name, scratch_shapes=(),
            cast=None, step_of=None):
    out_specs, out_shape = list(out_specs), list(out_shape)
    if cast is not None:
        src_spec, dst_spec = cast.specs(step_of)
        kernel_fn = _host_cast(kernel_fn, cast, len(in_specs), len(out_specs))
        in_specs = [*in_specs, src_spec]
        out_specs.append(dst_spec)
        out_shape.append(cast.out_shape)
        args = (*args, cast.src)
    return pl.pallas_call(
        kernel_fn, grid=grid, in_specs=list(in_specs), out_specs=out_specs, out_shape=out_shape,
        scratch_shapes=list(scratch_shapes), compiler_params=_params(("arbitrary",) * len(grid)),
        name=name)(*args)


def _norm_mod(x, g, shift, scale):
    ms = jnp.mean(x * x, axis=-1, keepdims=True)
    y = x * lax.rsqrt(ms + EPS) * g
    return y * (1.0 + scale) + shift


def _mod_kernel(cv_ref, w_ref, b_ref, o_ref):
    a = cv_ref[...]
    a = a * jax.nn.sigmoid(a)
    w = w_ref[0]
    a_hi = a.astype(BF16)
    a_lo = (a - a_hi.astype(F32)).astype(BF16)
    w_hi = w.astype(BF16)
    w_lo = (w - w_hi.astype(F32)).astype(BF16)
    acc = _dot(a_hi, w_hi) + _dot(a_lo, w_hi) + _dot(a_hi, w_lo)
    o_ref[0] = acc + b_ref[0]


def _mod_vectors(c, c_ctx, w_mod, b_mod):
    tn = 1024
    cv = jnp.concatenate(
        [c.reshape(1, D_MODEL), c_ctx.reshape(1, D_MODEL),
         jnp.zeros((V7X_SUBLANES_F32 - 2, D_MODEL), F32)], axis=0)
    b3 = b_mod.reshape(DEPTH, 1, 6 * D_MODEL)
    return pl.pallas_call(
        _mod_kernel,
        grid=(DEPTH, 6 * D_MODEL // tn),
        in_specs=[
            pl.BlockSpec((V7X_SUBLANES_F32, D_MODEL), lambda l, j: (0, 0)),
            pl.BlockSpec((1, D_MODEL, tn), lambda l, j: (l, 0, j)),
            pl.BlockSpec((1, 1, tn), lambda l, j: (l, 0, j)),
        ],
        out_specs=pl.BlockSpec((1, V7X_SUBLANES_F32, tn), lambda l, j: (l, 0, j)),
        out_shape=jax.ShapeDtypeStruct((DEPTH, V7X_SUBLANES_F32, 6 * D_MODEL), F32),
        compiler_params=_params(("arbitrary", "arbitrary")),
        name="mod_vectors",
    )(cv, w_mod, b3)


AB_TN = 512
ROW_SPLIT = 2


def _rope(t, cos, sin_signed):
    lane = lax.broadcasted_iota(jnp.int32, t.shape, 1)
    first = (lane % AXIS_DIM) < (AXIS_DIM // 2)
    partner = jnp.where(first,
                        pltpu.roll(t, HEAD_DIM - AXIS_DIM // 2, 1),
                        pltpu.roll(t, AXIS_DIM // 2, 1))
    return t * cos + partner * sin_signed


def _head_norm_rope(t, g, cos, sin_signed):
    ms = jnp.mean(t * t, axis=-1, keepdims=True)
    return _rope(t * lax.rsqrt(ms + EPS) * g, cos, sin_signed)


def _ab_in_kernel(x_ref, g_ref, sh_ref, sc_ref, w_ref, cos_ref, sin_ref, qn_ref, kn_ref,
                  qkv_ref, z_ref, *, tm):
    rows_per = tm // ROW_SPLIT
    n_q_tiles = A_Q_DIM // AB_TN
    for part in range(ROW_SPLIT):
        rs = slice(part * rows_per, (part + 1) * rows_per)
        h = _norm_mod(x_ref[rs, :], g_ref[...], sh_ref[...], sc_ref[...]).astype(BF16)
        cos = cos_ref[rs, :]
        sin = sin_ref[rs, :]
        for t in range(AB_IN // AB_TN):
            p = _dot(h, w_ref[:, t * AB_TN:(t + 1) * AB_TN])
            if t < n_q_tiles:
                for hh in range(AB_TN // HEAD_DIM):
                    c0 = t * AB_TN + hh * HEAD_DIM
                    qkv_ref[rs, c0:c0 + HEAD_DIM] = _head_norm_rope(
                        p[:, hh * HEAD_DIM:(hh + 1) * HEAD_DIM], qn_ref[...], cos, sin).astype(BF16)
            elif t == n_q_tiles:
                for hh in range(A_KV_HEADS):
                    c0 = A_Q_DIM + hh * HEAD_DIM
                    qkv_ref[rs, c0:c0 + HEAD_DIM] = _head_norm_rope(
                        p[:, hh * HEAD_DIM:(hh + 1) * HEAD_DIM], kn_ref[...], cos, sin).astype(BF16)
                qkv_ref[rs, A_Q_DIM + A_KV_DIM:] = p[:, A_KV_DIM:].astype(BF16)
            else:
                c0 = (t - n_q_tiles - 1) * AB_TN
                z_ref[rs, c0:c0 + AB_TN] = p


def _ab_in_proj(x2, g, shift, scale, w_in_bf, cos, sin, qn, kn, tm, cast=None):
    n = x2.shape[0]
    assert A_Q_DIM % AB_TN == 0 and 2 * A_KV_DIM == AB_TN and B_WIDTH % AB_TN == 0
    row = lambda i: (0, 0)
    return _pallas(
        functools.partial(_ab_in_kernel, tm=tm),
        grid=(n // tm,),
        cast=cast, step_of=lambda i: i,
        in_specs=[
            pl.BlockSpec((tm, D_MODEL), lambda i: (i, 0)),
            pl.BlockSpec((1, D_MODEL), row),
            pl.BlockSpec((1, D_MODEL), row),
            pl.BlockSpec((1, D_MODEL), row),
            pl.BlockSpec((D_MODEL, AB_IN), row, pipeline_mode=pl.Buffered(1)),
            pl.BlockSpec((tm, HEAD_DIM), lambda i: (i, 0)),
            pl.BlockSpec((tm, HEAD_DIM), lambda i: (i, 0)),
            pl.BlockSpec((1, HEAD_DIM), row),
            pl.BlockSpec((1, HEAD_DIM), row),
        ],
        out_specs=[
            pl.BlockSpec((tm, A_QKV_DIM), lambda i: (i, 0)),
            pl.BlockSpec((tm, B_WIDTH), lambda i: (i, 0)),
        ],
        out_shape=[jax.ShapeDtypeStruct((n, A_QKV_DIM), BF16),
                   jax.ShapeDtypeStruct((n, B_WIDTH), F32)],
        name="ab_in_proj",
        args=(x2, g, shift, scale, w_in_bf, cos, sin, qn, kn))


def _attn_kernel(sink_ref, bias_ref, q_ref, kp_ref, kc_ref, kn_ref, vp_ref, vc_ref, vn_ref,
                 kx_ref, vx_ref, o_ref):
    rows = A_GROUP * BLOCK
    bias = bias_ref[...]
    r1 = lax.broadcasted_iota(jnp.int32, (rows, 1), 0) // BLOCK
    for hk in range(A_KV_HEADS):
        hs = slice(hk * HEAD_DIM, (hk + 1) * HEAD_DIM)
        kcat = jnp.concatenate([kp_ref[:, hs], kc_ref[:, hs], kn_ref[:, hs], kx_ref[:, hs]], axis=0)
        vcat = jnp.concatenate([vp_ref[:, hs], vc_ref[:, hs], vn_ref[:, hs], vx_ref[:, hs]], axis=0)
        q0 = hk * A_GROUP * HEAD_DIM
        q4 = jnp.concatenate(
            [q_ref[:, q0 + g * HEAD_DIM:q0 + (g + 1) * HEAD_DIM] for g in range(A_GROUP)], axis=0)
        s = lax.dot_general(q4, kcat, (((1,), (1,)), ((), ())), preferred_element_type=F32)
        s = s * ATTN_SCALE + bias
        sink = jnp.zeros((rows, 1), F32)
        for g in range(A_GROUP):
            sink = jnp.where(r1 == g, sink_ref[hk * A_GROUP + g], sink)
        m = jnp.maximum(jnp.max(s, axis=-1, keepdims=True), sink)
        e = jnp.exp(s - m)
        den = jnp.sum(e, axis=-1, keepdims=True) + jnp.exp(sink - m)
        pr = (e / den).astype(BF16)
        o4 = _dot(pr, vcat)
        for g in range(A_GROUP):
            o_ref[:, q0 + g * HEAD_DIM:q0 + (g + 1) * HEAD_DIM] = (
                o4[g * BLOCK:(g + 1) * BLOCK].astype(BF16))


def _attn_bias(ctx_len):
    rows = A_GROUP * BLOCK
    cols = 3 * BLOCK + ctx_len
    r = np.arange(rows)[:, None] % BLOCK
    c = np.arange(cols)[None, :]
    band = (np.abs(BLOCK + r - c) <= WINDOW) | (c >= 3 * BLOCK)
    first = band & (c >= BLOCK)
    last = band & ((c < 2 * BLOCK) | (c >= 3 * BLOCK))
    masks = np.stack([first, band, last])
    return jnp.asarray(np.where(masks, 0.0, NEG_INF), dtype=F32)


def _window_attention(qkv, qkv_ctx, sink, cast=None):
    n = qkv.shape[0]
    ctx_len = qkv_ctx.shape[0]
    n_blocks = n // BLOCK
    assert n_blocks >= 2
    k_col = A_Q_DIM // A_KV_DIM
    v_col = k_col + 1
    prev = lambda i: jnp.maximum(i - 1, 0)
    nxt = lambda i: jnp.minimum(i + 1, n_blocks - 1)
    which = lambda i: jnp.where(i == 0, 0, jnp.where(i == n_blocks - 1, 2, 1))
    blk = (BLOCK, A_KV_DIM)
    rows = A_GROUP * BLOCK
    cols = 3 * BLOCK + ctx_len
    return _pallas(
        _attn_kernel,
        grid=(n_blocks,),
        cast=cast, step_of=lambda i: i,
        in_specs=[
            pl.BlockSpec(memory_space=pltpu.SMEM),
            pl.BlockSpec((None, rows, cols), lambda i: (which(i), 0, 0)),
            pl.BlockSpec((BLOCK, A_Q_DIM), lambda i: (i, 0)),
            pl.BlockSpec(blk, lambda i: (prev(i), k_col)),
            pl.BlockSpec(blk, lambda i: (i, k_col)),
            pl.BlockSpec(blk, lambda i: (nxt(i), k_col)),
            pl.BlockSpec(blk, lambda i: (prev(i), v_col)),
            pl.BlockSpec(blk, lambda i: (i, v_col)),
            pl.BlockSpec(blk, lambda i: (nxt(i), v_col)),
            pl.BlockSpec((ctx_len, A_KV_DIM), lambda i: (0, k_col)),
            pl.BlockSpec((ctx_len, A_KV_DIM), lambda i: (0, v_col)),
        ],
        out_specs=[pl.BlockSpec((BLOCK, A_Q_DIM), lambda i: (i, 0))],
        out_shape=[jax.ShapeDtypeStruct((n, A_Q_DIM), BF16)],
        name="window_attention",
        args=(sink, _attn_bias(ctx_len), qkv, qkv, qkv, qkv, qkv, qkv, qkv, qkv_ctx, qkv_ctx))


POOL_HALO = 8
assert max(POOL_WINDOWS) // 2 <= POOL_HALO


def _pool_kernel(zm_ref, zp_ref, zn_ref, w_ref, ps_ref, o_ref, z_scr, *, n_rows, tm):
    i = pl.program_id(0)
    last = pl.num_programs(0) - 1
    z_scr[0:POOL_HALO, :] = jnp.where(i > 0, zp_ref[...], 0.0)
    z_scr[POOL_HALO:POOL_HALO + tm, :] = zm_ref[...]
    z_scr[POOL_HALO + tm:, :] = jnp.where(i < last, zn_ref[...], 0.0)
    t = i * tm + lax.broadcasted_iota(jnp.int32, (tm, B_GROUP_DIM), 0)
    for g in range(B_GROUPS):
        half = POOL_WINDOWS[g] // 2
        cs = slice(g * B_GROUP_DIM, (g + 1) * B_GROUP_DIM)
        acc = z_scr[pl.ds(POOL_HALO - half, tm), cs]
        for off in range(-half + 1, half):
            acc = acc + z_scr[pl.ds(POOL_HALO + off, tm), cs]
        cnt = (jnp.minimum(t + half, n_rows) - jnp.maximum(t - half, 0)).astype(F32)
        d = (acc / cnt - zm_ref[:, cs]).astype(BF16)
        y = _dot(d, w_ref[g]) * ps_ref[:, cs]
        o_ref[:, cs] = y.astype(BF16)


def _pool_mix(z, w_pool_bf, pool_scale, tm, cast=None):
    n = z.shape[0]
    hb = tm // POOL_HALO
    n_halo_blocks = n // POOL_HALO
    return _pallas(
        functools.partial(_pool_kernel, n_rows=n, tm=tm),
        grid=(n // tm,),
        cast=cast, step_of=lambda i: i,
        in_specs=[
            pl.BlockSpec((tm, B_WIDTH), lambda i: (i, 0)),
            pl.BlockSpec((POOL_HALO, B_WIDTH), lambda i: (jnp.maximum(i * hb - 1, 0), 0)),
            pl.BlockSpec((POOL_HALO, B_WIDTH), lambda i: (jnp.minimum((i + 1) * hb, n_halo_blocks - 1), 0)),
            pl.BlockSpec((B_GROUPS, B_GROUP_DIM, B_GROUP_DIM), lambda i: (0, 0, 0)),
            pl.BlockSpec((1, B_WIDTH), lambda i: (0, 0)),
        ],
        out_specs=[pl.BlockSpec((tm, B_WIDTH), lambda i: (i, 0))],
        out_shape=[jax.ShapeDtypeStruct((n, B_WIDTH), BF16)],
        scratch_shapes=[pltpu.VMEM((tm + 2 * POOL_HALO, B_WIDTH), F32)],
        name="pool_mix",
        args=(z, z, z, w_pool_bf, pool_scale))


OUT_TN = 512


def _out_proj_kernel(a1_ref, a2_ref, w_ref, x_ref, gate_ref, g_ref, sh_ref, sc_ref, o_ref, h_ref, *, tm):
    rows_per = tm // ROW_SPLIT
    for part in range(ROW_SPLIT):
        rs = slice(part * rows_per, (part + 1) * rows_per)
        a = jnp.concatenate([a1_ref[rs, :].astype(BF16), a2_ref[rs, :].astype(BF16)], axis=1)
        for t in range(D_MODEL // OUT_TN):
            cs = slice(t * OUT_TN, (t + 1) * OUT_TN)
            o_ref[rs, cs] = x_ref[rs, cs] + gate_ref[:, cs] * _dot(a, w_ref[:, cs])
        h_ref[rs, :] = _norm_mod(o_ref[rs, :], g_ref[...], sh_ref[...], sc_ref[...])


def _out_proj(a1, a2, w_out_bf, x2, gate, g2, shift2, scale2, tm, cast=None):
    n = x2.shape[0]
    k1 = a1.shape[1]
    k2 = a2.shape[1]
    assert k1 + k2 == w_out_bf.shape[0] and k1 % V7X_LANES == 0
    row = lambda i: (0, 0)
    return _pallas(
        functools.partial(_out_proj_kernel, tm=tm),
        grid=(n // tm,),
        cast=cast, step_of=lambda i: i,
        in_specs=[
            pl.BlockSpec((tm, k1), lambda i: (i, 0)),
            pl.BlockSpec((tm, k2), lambda i: (i, 0)),
            pl.BlockSpec((k1 + k2, D_MODEL), row, pipeline_mode=pl.Buffered(1)),
            pl.BlockSpec((tm, D_MODEL), lambda i: (i, 0)),
            pl.BlockSpec((1, D_MODEL), row),
            pl.BlockSpec((1, D_MODEL), row),
            pl.BlockSpec((1, D_MODEL), row),
            pl.BlockSpec((1, D_MODEL), row),
        ],
        out_specs=[pl.BlockSpec((tm, D_MODEL), lambda i: (i, 0)),
                   pl.BlockSpec((tm, D_MODEL), lambda i: (i, 0))],
        out_shape=[jax.ShapeDtypeStruct((n, D_MODEL), F32),
                   jax.ShapeDtypeStruct((n, D_MODEL), F32)],
        name="out_proj",
        args=(a1, a2, w_out_bf, x2, gate, g2, shift2, scale2))


FFN_SLABS = V7X_SUBLANES_F32
FFN_EDGE_ROWS = V7X_SUBLANES_BF16
FFN_PIECE_ROWS = 16
FFN_CW = 256
FFN_TF = 2 * FFN_CW
FFN_TN = 512
FFN_CHUNKS = D_FF // FFN_CW
FFN_UP_STEPS = D_FF // FFN_TF
FFN_DOWN_STEPS = D_MODEL // FFN_TN


def _ffn_up_kernel(hm_ref, hp_ref, hn_ref, wg_ref, wv_ref, cw_ref, cb_ref, o_ref,
                   h_scr, carry_a, carry_b, *, tm):
    i = pl.program_id(0)
    j = pl.program_id(1)
    last_i = pl.num_programs(0) - 1
    nc = FFN_CHUNKS
    ns = FFN_SLABS
    sr = tm // ns

    def finish(carry, k, r, step):
        c = jnp.maximum(2 * step + k, 0)

        def rows_of(idx, s, q0, n):
            if q0 < 0:
                return jnp.concatenate([carry[idx, tm:tm + 1, :],
                                        carry[idx, pl.ds(s * sr, n - 1), :]], axis=0)
            if q0 + n > sr:
                return jnp.concatenate([carry[idx, pl.ds(s * sr + q0, n - 1), :],
                                        carry[idx, tm + 1:tm + 2, :]], axis=0)
            return carry[idx, pl.ds(s * sr + q0, n), :]

        def conv(idx, kk, q0):
            n = FFN_PIECE_ROWS
            mid = rows_of(idx, r, q0, n)
            prev = rows_of(idx, r - 1, q0, n) if r > 0 else rows_of(idx, ns - 1, q0 - 1, n)
            nxt = rows_of(idx, r + 1, q0, n) if r < ns - 1 else rows_of(idx, 0, q0 + 1, n)
            return prev * cw_ref[kk, 0] + mid * cw_ref[kk, 1] + nxt * cw_ref[kk, 2] + cb_ref[kk]

        for q0 in range(0, sr, FFN_PIECE_ROWS):
            gg = conv(k, c, q0)
            vv = conv(2 + k, c + nc, q0)
            o_ref[pl.ds(r * sr + q0, FFN_PIECE_ROWS), k * FFN_CW:(k + 1) * FFN_CW] = (
                gg * jax.nn.sigmoid(gg) * vv).astype(BF16)

    @pl.when(j == 0)
    def _():
        hm = hm_ref[...].reshape(sr, ns, D_MODEL)
        h_scr[0:tm, :] = jnp.swapaxes(hm, 0, 1).reshape(tm, D_MODEL).astype(BF16)
        before = jnp.where(i > 0, hp_ref[ns - 1:ns, :], 0.0)
        after = jnp.where(i < last_i, hn_ref[0:1, :], 0.0)
        pad = jnp.zeros((FFN_EDGE_ROWS - 2, D_MODEL), F32)
        h_scr[tm:, :] = jnp.concatenate([before, after, pad], axis=0).astype(BF16)
        carry_b[...] = jnp.zeros_like(carry_b)

    def up_step(carry_w, carry_r):
        for k in range(2):
            for r in range(ns):
                finish(carry_r, k, r, j - 1)
        for k in range(2):
            cs = slice(k * FFN_CW, (k + 1) * FFN_CW)
            carry_w[k] = _dot(h_scr[...], wg_ref[:, cs])
            carry_w[2 + k] = _dot(h_scr[...], wv_ref[:, cs])

    @pl.when((j < FFN_UP_STEPS) & (j % 2 == 0))
    def _():
        up_step(carry_a, carry_b)

    @pl.when((j < FFN_UP_STEPS) & (j % 2 == 1))
    def _():
        up_step(carry_b, carry_a)

    @pl.when(j == FFN_UP_STEPS)
    def _():
        last = carry_a if (FFN_UP_STEPS - 1) % 2 == 0 else carry_b
        for k in range(2):
            for r in range(ns):
                finish(last, k, r, FFN_UP_STEPS - 1)


def _ffn_down_kernel(a_ref, wd_ref, x_ref, gate_ref, o_ref, *, tm):
    ns = FFN_SLABS
    sr = tm // ns
    y = _dot(a_ref[:, FFN_TF:], wd_ref[...])
    y = jnp.swapaxes(y.reshape(ns, sr, FFN_TN), 0, 1).reshape(tm, FFN_TN)
    o_ref[...] = x_ref[...] + gate_ref[...] * y


def _conv_ffn(x2, h2, w_up_bf, conv_w, conv_b, w_down_bf, gate, tm, up_cast=None, down_cast=None):
    n = x2.shape[0]
    nj = FFN_UP_STEPS
    hb = tm // FFN_SLABS
    n_halo_blocks = n // FFN_SLABS
    act_cols = D_FF + FFN_TF
    cw3 = conv_w.reshape(3, 2 * FFN_CHUNKS, FFN_CW).transpose(1, 0, 2)
    cw3 = jnp.broadcast_to(cw3[:, :, None, :], (2 * FFN_CHUNKS, 3, FFN_PIECE_ROWS, FFN_CW))
    cb3 = jnp.broadcast_to(conv_b.reshape(2 * FFN_CHUNKS, 1, FFN_CW),
                           (2 * FFN_CHUNKS, FFN_PIECE_ROWS, FFN_CW))
    once = pl.Buffered(1)
    act, *up_cast = _pallas(
        functools.partial(_ffn_up_kernel, tm=tm),
        grid=(n // tm, nj + 1),
        cast=up_cast, step_of=lambda i, j: i * (nj + 1) + j,
        in_specs=[
            pl.BlockSpec((tm, D_MODEL), lambda i, j: (i, 0)),
            pl.BlockSpec((FFN_SLABS, D_MODEL), lambda i, j: (jnp.maximum(i * hb - 1, 0), 0)),
            pl.BlockSpec((FFN_SLABS, D_MODEL), lambda i, j: (jnp.minimum((i + 1) * hb, n_halo_blocks - 1), 0)),
            pl.BlockSpec((None, D_MODEL, FFN_TF), lambda i, j: (jnp.minimum(j, nj - 1), 0, 0)),
            pl.BlockSpec((None, D_MODEL, FFN_TF), lambda i, j: (jnp.minimum(j, nj - 1) + nj, 0, 0)),
            pl.BlockSpec((2 * FFN_CHUNKS, 3, FFN_PIECE_ROWS, FFN_CW), lambda i, j: (0, 0, 0, 0),
                         pipeline_mode=once),
            pl.BlockSpec((2 * FFN_CHUNKS, FFN_PIECE_ROWS, FFN_CW), lambda i, j: (0, 0, 0),
                         pipeline_mode=once),
        ],
        out_specs=[pl.BlockSpec((tm, FFN_TF), lambda i, j: (i, j))],
        out_shape=[jax.ShapeDtypeStruct((n, act_cols), BF16)],
        scratch_shapes=[pltpu.VMEM((tm + FFN_EDGE_ROWS, D_MODEL), BF16),
                        pltpu.VMEM((4, tm + FFN_EDGE_ROWS, FFN_CW), F32),
                        pltpu.VMEM((4, tm + FFN_EDGE_ROWS, FFN_CW), F32)],
        name="ffn_up",
        args=(h2, h2, h2, w_up_bf, w_up_bf, cw3, cb3))
    out, *down_cast = _pallas(
        functools.partial(_ffn_down_kernel, tm=tm),
        grid=(n // tm, FFN_DOWN_STEPS),
        cast=down_cast, step_of=lambda i, j: i * FFN_DOWN_STEPS + j,
        in_specs=[
            pl.BlockSpec((tm, act_cols), lambda i, j: (i, 0)),
            pl.BlockSpec((None, D_FF, FFN_TN), lambda i, j: (j, 0, 0)),
            pl.BlockSpec((tm, FFN_TN), lambda i, j: (i, j)),
            pl.BlockSpec((1, FFN_TN), lambda i, j: (0, j)),
        ],
        out_specs=[pl.BlockSpec((tm, FFN_TN), lambda i, j: (i, j))],
        out_shape=[jax.ShapeDtypeStruct((n, D_MODEL), F32)],
        name="ffn_down",
        args=(act, w_down_bf, x2, gate))
    return out, up_cast, down_cast


def _gelu(x):
    return 0.5 * x * (1.0 + lax.erf(x * (2.0 ** -0.5)))


def _cd_in_kernel(x_ref, g_ref, sh_ref, sc_ref, w_ref, vg_ref, uv_ref, f_ref, *, tm):
    rows_per = tm // ROW_SPLIT
    for part in range(ROW_SPLIT):
        rs = slice(part * rows_per, (part + 1) * rows_per)
        h = _norm_mod(x_ref[rs, :], g_ref[...], sh_ref[...], sc_ref[...]).astype(BF16)
        uv_ref[rs, 0:C_WIDTH] = _gelu(_dot(h, w_ref[:, 0:C_WIDTH])).astype(BF16)
        v = _gelu(_dot(h, w_ref[:, C_WIDTH:2 * C_WIDTH]))
        ms = jnp.mean(v * v, axis=-1, keepdims=True)
        uv_ref[rs, C_WIDTH:] = (v * lax.rsqrt(ms + EPS) * vg_ref[...]).astype(BF16)
        f_ref[rs, :] = _dot(h, w_ref[:, 2 * C_WIDTH:])


def _cd_in_proj(x2, g, shift, scale, w_in_bf, v_norm_g, tm, cast=None):
    n = x2.shape[0]
    row = lambda i: (0, 0)
    return _pallas(
        functools.partial(_cd_in_kernel, tm=tm),
        grid=(n // tm,),
        cast=cast, step_of=lambda i: i,
        in_specs=[
            pl.BlockSpec((tm, D_MODEL), lambda i: (i, 0)),
            pl.BlockSpec((1, D_MODEL), row),
            pl.BlockSpec((1, D_MODEL), row),
            pl.BlockSpec((1, D_MODEL), row),
            pl.BlockSpec((D_MODEL, CD_IN), row, pipeline_mode=pl.Buffered(1)),
            pl.BlockSpec((1, C_WIDTH), row),
        ],
        out_specs=[
            pl.BlockSpec((tm, 2 * C_WIDTH), lambda i: (i, 0)),
            pl.BlockSpec((tm, D_WIDTH), lambda i: (i, 0)),
        ],
        out_shape=[jax.ShapeDtypeStruct((n, 2 * C_WIDTH), BF16),
                   jax.ShapeDtypeStruct((n, D_WIDTH), F32)],
        name="cd_in_proj",
        args=(x2, g, shift, scale, w_in_bf, v_norm_g))


def _spatial_kernel(u_ref, v_ref, ws_ref, bias_ref, o_ref, *, chunks):
    for k in range(chunks):
        rs = slice(k * CHUNK, (k + 1) * CHUNK)
        for g in range(C_GROUPS):
            cs = slice(g * C_GROUP_DIM, (g + 1) * C_GROUP_DIM)
            s = _dot(ws_ref[g], v_ref[rs, cs]) + bias_ref[:, cs]
            o_ref[rs, cs] = (u_ref[rs, cs].astype(F32) * s).astype(BF16)


def _spatial_gate(uv, w_spatial_bf, bias_full, chunks=4):
    n = uv.shape[0]
    rows = chunks * CHUNK
    return pl.pallas_call(
        functools.partial(_spatial_kernel, chunks=chunks),
        grid=(n // rows,),
        in_specs=[
            pl.BlockSpec((rows, C_WIDTH), lambda i: (i, 0)),
            pl.BlockSpec((rows, C_WIDTH), lambda i: (i, 1)),
            pl.BlockSpec((C_GROUPS, CHUNK, CHUNK), lambda i: (0, 0, 0)),
            pl.BlockSpec((CHUNK, C_WIDTH), lambda i: (0, 0)),
        ],
        out_specs=pl.BlockSpec((rows, C_WIDTH), lambda i: (i, 0)),
        out_shape=jax.ShapeDtypeStruct((n, C_WIDTH), BF16),
        compiler_params=_params(("arbitrary",)),
        name="spatial_gate",
    )(uv, uv, w_spatial_bf, bias_full)


FFT_SUB = V7X_SUBLANES_F32
FFT1_CT = 512
FFT2_CT = 512


def _fourier_tables(n):
    a_len, b_len, sub = FFT_A, FFT_B, FFT_SUB
    assert a_len * b_len == n
    ch = np.arange(D_GROUP_DIM)
    ang_c = 2.0 * np.pi * ((ch[:, None] * ch[None, :]) % D_GROUP_DIM) / D_GROUP_DIM
    a = np.arange(a_len)
    f_a = np.exp(-2j * np.pi * ((a[:, None] * a[None, :]) % a_len) / a_len)
    m1 = np.kron(f_a, np.eye(sub))
    b = np.arange(b_len)
    tw = np.exp(-2j * np.pi * ((a[:, None] * b[None, :]) % n) / n)
    f_b = np.exp(-2j * np.pi * ((b[:, None] * b[None, :]) % b_len) / b_len)
    m2 = np.einsum('db,pq->dpqb', f_b, np.eye(sub)).reshape(b_len * sub, sub * b_len)
    norm = 1.0 / np.sqrt(float(n) * D_GROUP_DIM)
    m2 = m2 * norm
    tw3 = np.broadcast_to(tw[:, :, None], (a_len, b_len, V7X_LANES))
    f32 = lambda v: jnp.asarray(np.ascontiguousarray(v), dtype=F32)
    return dict(cos_c=f32(np.cos(ang_c)), sin_c=f32(np.sin(ang_c)),
                m1r=f32(m1.real), m1i=f32(m1.imag), m2r=f32(m2.real), m2i=f32(m2.imag),
                twr=f32(tw3.real), twi=f32(tw3.imag))


def _fft1_kernel(f_ref, cc_ref, sc_ref, m1r_ref, m1i_ref, twr_ref, twi_ref, tr_ref, ti_ref):
    rows = FFT_A * FFT_SUB
    ct = FFT1_CT
    fb = f_ref[...].reshape(rows, ct).astype(BF16)
    xr_parts = []
    xi_parts = []
    for q in range(ct // D_GROUP_DIM):
        blk = fb[:, q * D_GROUP_DIM:(q + 1) * D_GROUP_DIM]
        xr_parts.append(_dot(blk, cc_ref[...]))
        xi_parts.append(-_dot(blk, sc_ref[...]))
    xr = jnp.concatenate(xr_parts, axis=1).astype(BF16)
    xi = jnp.concatenate(xi_parts, axis=1).astype(BF16)
    m1r = m1r_ref[...]
    m1i = m1i_ref[...]
    tr = _dot(m1r, xr) - _dot(m1i, xi)
    ti = _dot(m1r, xi) + _dot(m1i, xr)
    reps = ct // V7X_LANES
    twr = jnp.tile(twr_ref[...].reshape(rows, V7X_LANES), (1, reps))
    twi = jnp.tile(twi_ref[...].reshape(rows, V7X_LANES), (1, reps))
    tr_ref[...] = (tr * twr - ti * twi).reshape(FFT_A, FFT_SUB, ct)
    ti_ref[...] = (tr * twi + ti * twr).reshape(FFT_A, FFT_SUB, ct)


def _fft2_kernel(tr_ref, ti_ref, m2r_ref, m2i_ref, wf_ref, o_ref):
    j = pl.program_id(1)
    tr = tr_ref[...].astype(BF16)
    ti = ti_ref[...].astype(BF16)
    z = _dot(m2r_ref[...], tr) - _dot(m2i_ref[...], ti)
    contrib = _dot(z.astype(BF16), wf_ref[...]).reshape(FFT_B, FFT_SUB, D_WIDTH)

    @pl.when(j == 0)
    def _():
        o_ref[...] = contrib

    @pl.when(j > 0)
    def _():
        o_ref[...] += contrib


def _fourier_mix(f, tabs, w_fourier_bf):
    n = f.shape[0]
    a_len, b_len, sub = FFT_A, FFT_B, FFT_SUB
    f3 = f.reshape(a_len, b_len, D_WIDTH)
    rows1 = a_len * sub
    const2 = lambda i, j: (0, 0)
    tr, ti = pl.pallas_call(
        _fft1_kernel,
        grid=(b_len // sub, D_WIDTH // FFT1_CT),
        in_specs=[
            pl.BlockSpec((a_len, sub, FFT1_CT), lambda i, j: (0, i, j)),
            pl.BlockSpec((D_GROUP_DIM, D_GROUP_DIM), const2),
            pl.BlockSpec((D_GROUP_DIM, D_GROUP_DIM), const2),
            pl.BlockSpec((rows1, rows1), const2),
            pl.BlockSpec((rows1, rows1), const2),
            pl.BlockSpec((a_len, sub, V7X_LANES), lambda i, j: (0, i, 0)),
            pl.BlockSpec((a_len, sub, V7X_LANES), lambda i, j: (0, i, 0)),
        ],
        out_specs=[pl.BlockSpec((a_len, sub, FFT1_CT), lambda i, j: (0, i, j))] * 2,
        out_shape=[jax.ShapeDtypeStruct((a_len, b_len, D_WIDTH), F32)] * 2,
        compiler_params=_params(("arbitrary", "arbitrary")),
        name="fourier_stage1",
    )(f3, tabs['cos_c'].astype(BF16), tabs['sin_c'].astype(BF16),
      tabs['m1r'].astype(BF16), tabs['m1i'].astype(BF16), tabs['twr'], tabs['twi'])

    rows2 = sub * b_len
    tr2 = tr.reshape(n, D_WIDTH)
    ti2 = ti.reshape(n, D_WIDTH)
    out = pl.pallas_call(
        _fft2_kernel,
        grid=(a_len // sub, D_WIDTH // FFT2_CT),
        in_specs=[
            pl.BlockSpec((rows2, FFT2_CT), lambda i, j: (i, j)),
            pl.BlockSpec((rows2, FFT2_CT), lambda i, j: (i, j)),
            pl.BlockSpec((rows2, rows2), const2),
            pl.BlockSpec((rows2, rows2), const2),
            pl.BlockSpec((FFT2_CT, D_WIDTH), lambda i, j: (j, 0)),
        ],
        out_specs=pl.BlockSpec((b_len, sub, D_WIDTH), lambda i, j: (0, i, 0)),
        out_shape=jax.ShapeDtypeStruct((b_len, a_len, D_WIDTH), F32),
        compiler_params=_params(("arbitrary", "arbitrary")),
        name="fourier_stage2",
    )(tr2, ti2, tabs['m2r'].astype(BF16), tabs['m2i'].astype(BF16), w_fourier_bf)
    return out.reshape(n, D_WIDTH)


def _rope_tables(n):
    rows = n // GRID_W
    row = np.repeat(np.arange(rows, dtype=np.float64), GRID_W)
    col = np.tile(np.arange(GRID_W, dtype=np.float64), rows)
    inv = ROPE_THETA ** (-np.arange(0, AXIS_DIM, 2, dtype=np.float64) / AXIS_DIM)
    ang_r = row[:, None] * inv[None, :]
    ang_c = col[:, None] * inv[None, :]
    cos = np.concatenate([np.cos(ang_r)] * 2 + [np.cos(ang_c)] * 2, axis=-1)
    sin = np.concatenate([-np.sin(ang_r), np.sin(ang_r), -np.sin(ang_c), np.sin(ang_c)], axis=-1)
    return jnp.asarray(cos, dtype=F32), jnp.asarray(sin, dtype=F32)


def kernel(x, c, ctx, c_ctx, w_mod, b_mod, norm1_g, norm2_g, ab_w_in, a_q_norm_g, a_k_norm_g, a_sink,
           b_w_pool, b_pool_scale, ab_w_out, cd_w_in, c_v_norm_g, c_w_spatial, c_b_spatial, d_w_fourier,
           cd_w_out, f_w_up, f_conv_w, f_conv_b, f_w_down):
    batch, n, _ = x.shape
    ctx_len = ctx.shape[1]
    assert batch == 1 and DEPTH == 2
    x2 = x.reshape(n, D_MODEL)
    ctx2 = ctx.reshape(ctx_len, D_MODEL)

    mod = _mod_vectors(c, c_ctx, w_mod, b_mod)

    def split6(v):
        return [v[:, k * D_MODEL:(k + 1) * D_MODEL] for k in range(6)]

    row1 = lambda v: v.reshape(1, -1)
    n_tiles = n // PROJ_TM
    n_blocks = n // BLOCK
    ffn_tiles = n // FFN_TM
    cast_down0 = _CastJob(f_w_down, 0, D_FF // n_tiles, n_tiles, col_block=FFN_TN)
    cast_up0 = _CastJob(f_w_up, 0, D_MODEL // n_blocks, n_blocks, col_block=FFN_TF)
    cast_up1 = _CastJob(f_w_up, 1, D_MODEL // n_blocks, ffn_tiles * (FFN_UP_STEPS + 1), col_block=FFN_TF)
    cast_down1 = _CastJob(f_w_down, 1, D_FF // (ffn_tiles * FFN_DOWN_STEPS), ffn_tiles * FFN_DOWN_STEPS,
                          col_block=FFN_TN)

    ml = split6(mod[0, 0:1])
    mc = split6(mod[0, 1:2])
    g1 = row1(norm1_g[0])
    w_in = ab_w_in[0].astype(BF16)
    qn = row1(a_q_norm_g[0])
    kn = row1(a_k_norm_g[0])
    cos, sin = _rope_tables(n)
    qkv, z, w_down0 = _ab_in_proj(x2, g1, ml[0], ml[1], w_in, cos, sin, qn, kn, tm=PROJ_TM,
                                  cast=cast_down0)
    ones = jnp.ones((ctx_len, HEAD_DIM), F32)
    zeros = jnp.zeros((ctx_len, HEAD_DIM), F32)
    qkv_ctx, _ = _ab_in_proj(ctx2, g1, mc[0], mc[1], w_in, ones, zeros, qn, kn, tm=ctx_len)
    attn, w_up0 = _window_attention(qkv, qkv_ctx, a_sink[0], cast=cast_up0)
    pool_tiles = n // POOL_TM
    pooled, w_out0 = _pool_mix(z, b_w_pool[0].astype(BF16), row1(b_pool_scale[0]), tm=POOL_TM,
                               cast=_CastJob(ab_w_out, 0, D_MODEL // pool_tiles, pool_tiles))
    x2, h2, cd_w_in_bf = _out_proj(attn, pooled, w_out0, x2, ml[2], row1(norm2_g[0]), ml[3], ml[4],
                                   tm=PROJ_TM, cast=_CastJob(cd_w_in, 0, D_MODEL // n_tiles, n_tiles))
    x2, (w_up1,), (w_down1,) = _conv_ffn(x2, h2, w_up0, f_conv_w[0], f_conv_b[0], w_down0, ml[5],
                                         tm=FFN_TM, up_cast=cast_up1, down_cast=cast_down1)

    ml = split6(mod[1, 0:1])
    uv, f, w_out1 = _cd_in_proj(x2, row1(norm1_g[1]), ml[0], ml[1], cd_w_in_bf, row1(c_v_norm_g[0]),
                                tm=PROJ_TM, cast=_CastJob(cd_w_out, 0, D_MODEL // n_tiles, n_tiles))
    bias_full = jnp.repeat(c_b_spatial[0].T, C_GROUP_DIM, axis=1)
    c_out = _spatial_gate(uv, c_w_spatial[0].astype(BF16), bias_full)
    d_out = _fourier_mix(f, _fourier_tables(n), d_w_fourier[0].astype(BF16))
    x2, h2 = _out_proj(c_out, d_out, w_out1, x2, ml[2], row1(norm2_g[1]), ml[3], ml[4], tm=PROJ_TM)
    x2, _, _ = _conv_ffn(x2, h2, w_up1, f_conv_w[1], f_conv_b[1], w_down1, ml[5], tm=FFN_TM)
    return x2.reshape(batch, n, D_MODEL)
```

```python
import functools

import numpy as np
import jax
import jax.numpy as jnp
from jax import lax
from jax.experimental import pallas as pl
from jax.experimental.pallas import tpu as pltpu

F32 = jnp.float32
BF16 = jnp.bfloat16

D_MODEL = 2048
DEPTH = 2
GRID_W = 64
HEAD_DIM = 128
A_Q_HEADS = 8
A_KV_HEADS = 2
A_GROUP = A_Q_HEADS // A_KV_HEADS
A_Q_DIM = A_Q_HEADS * HEAD_DIM
A_KV_DIM = A_KV_HEADS * HEAD_DIM
A_QKV_DIM = A_Q_DIM + 2 * A_KV_DIM
WINDOW = 128
BLOCK = 128
ROPE_THETA = 10000.0
AXIS_DIM = HEAD_DIM // 2
ATTN_SCALE = HEAD_DIM ** -0.5
NEG_INF = -1e30
B_GROUPS = 4
B_WIDTH = 1024
B_GROUP_DIM = B_WIDTH // B_GROUPS
POOL_WINDOWS = (2, 4, 8, 16)
AB_IN = A_QKV_DIM + B_WIDTH
C_WIDTH = 1024
C_GROUPS = 4
C_GROUP_DIM = C_WIDTH // C_GROUPS
CHUNK = 128
D_WIDTH = 1024
D_GROUPS = 8
D_GROUP_DIM = D_WIDTH // D_GROUPS
CD_IN = 2 * C_WIDTH + D_WIDTH
D_FF = 5632
EPS = 1e-6

V7X_SUBLANES_F32 = 8
V7X_SUBLANES_BF16 = 16
V7X_LANES = 128
V7X_VMEM_BYTES = 64 * 1024 * 1024
VMEM_LIMIT = 56 * 1024 * 1024

PROJ_TM = 512
POOL_TM = 256
FFN_UP_TM = 512
FFN_DOWN_TM = 1024

FFT_A = 64
FFT_B = 128


def _params(sem):
    return pltpu.CompilerParams(dimension_semantics=sem, vmem_limit_bytes=VMEM_LIMIT)


def _dot(a, b):
    return jnp.dot(a, b, preferred_element_type=F32)


class _CastJob:
    def __init__(self, src, layer, rows, n_steps, col_block=None):
        _, total_rows, self.cols = src.shape
        assert total_rows % rows == 0 and total_rows // rows <= n_steps
        self.src, self.layer, self.rows, self.col_block = src, layer, rows, col_block
        self.last = total_rows // rows - 1
        if col_block is None:
            self.out_shape = jax.ShapeDtypeStruct((total_rows, self.cols), BF16)
        else:
            assert self.cols % col_block == 0
            self.out_shape = jax.ShapeDtypeStruct((self.cols // col_block, total_rows, col_block), BF16)

    def specs(self, step_of):
        blk = lambda *ids: jnp.minimum(step_of(*ids), self.last)
        src = pl.BlockSpec((None, self.rows, self.cols), lambda *ids: (self.layer, blk(*ids), 0))
        if self.col_block is None:
            return src, pl.BlockSpec((self.rows, self.cols), lambda *ids: (blk(*ids), 0))
        return src, pl.BlockSpec((self.cols // self.col_block, self.rows, self.col_block),
                                 lambda *ids: (0, blk(*ids), 0))

    def run(self, src_ref, dst_ref):
        if self.col_block is None:
            dst_ref[...] = src_ref[...].astype(BF16)
        else:
            for b in range(self.cols // self.col_block):
                dst_ref[b] = src_ref[:, b * self.col_block:(b + 1) * self.col_block].astype(BF16)


def _host_cast(kernel_fn, cast, n_in, n_out):
    def body(*refs):
        cast.run(refs[n_in], refs[n_in + 1 + n_out])
        kernel_fn(*refs[:n_in], *refs[n_in + 1:n_in + 1 + n_out], *refs[n_in + 2 + n_out:])
    return body


def _pallas(kernel_fn, *, grid, in_specs, out_specs, out_shape, args, name, scratch_shapes=(),
            cast=None, step_of=None):
    out_specs, out_shape = list(out_specs), list(out_shape)
    if cast is not None:
        src_spec, dst_spec = cast.specs(step_of)
        kernel_fn = _host_cast(kernel_fn, cast, len(in_specs), len(out_specs))
        in_specs = [*in_specs, src_spec]
        out_specs.append(dst_spec)
        out_shape.append(cast.out_shape)
        args = (*args, cast.src)
    return pl.pallas_call(
        kernel_fn, grid=grid, in_specs=list(in_specs), out_specs=out_specs, out_shape=out_shape,
        scratch_shapes=list(scratch_shapes), compiler_params=_params(("arbitrary",) * len(grid)),
        name=name)(*args)


def _norm_mod(x, g, shift, scale):
    ms = jnp.mean(x * x, axis=-1, keepdims=True)
    y = x * lax.rsqrt(ms + EPS) * g
    return y * (1.0 + scale) + shift


def _mod_kernel(cv_ref, w_ref, b_ref, o_ref):
    a = cv_ref[...]
    a = a * jax.nn.sigmoid(a)
    w = w_ref[0]
    a_hi = a.astype(BF16)
    a_lo = (a - a_hi.astype(F32)).astype(BF16)
    w_hi = w.astype(BF16)
    w_lo = (w - w_hi.astype(F32)).astype(BF16)
    acc = _dot(a_hi, w_hi) + _dot(a_lo, w_hi) + _dot(a_hi, w_lo)
    o_ref[0] = acc + b_ref[0]


def _mod_vectors(c, c_ctx, w_mod, b_mod):
    tn = 1024
    cv = jnp.concatenate(
        [c.reshape(1, D_MODEL), c_ctx.reshape(1, D_MODEL),
         jnp.zeros((V7X_SUBLANES_F32 - 2, D_MODEL), F32)], axis=0)
    b3 = b_mod.reshape(DEPTH, 1, 6 * D_MODEL)
    return pl.pallas_call(
        _mod_kernel,
        grid=(DEPTH, 6 * D_MODEL // tn),
        in_specs=[
            pl.BlockSpec((V7X_SUBLANES_F32, D_MODEL), lambda l, j: (0, 0)),
            pl.BlockSpec((1, D_MODEL, tn), lambda l, j: (l, 0, j)),
            pl.BlockSpec((1, 1, tn), lambda l, j: (l, 0, j)),
        ],
        out_specs=pl.BlockSpec((1, V7X_SUBLANES_F32, tn), lambda l, j: (l, 0, j)),
        out_shape=jax.ShapeDtypeStruct((DEPTH, V7X_SUBLANES_F32, 6 * D_MODEL), F32),
        compiler_params=_params(("arbitrary", "arbitrary")),
        name="mod_vectors",
    )(cv, w_mod, b3)


AB_TN = 512
ROW_SPLIT = 2


def _rope(t, cos, sin_signed):
    lane = lax.broadcasted_iota(jnp.int32, t.shape, 1)
    first = (lane % AXIS_DIM) < (AXIS_DIM // 2)
    partner = jnp.where(first,
                        pltpu.roll(t, HEAD_DIM - AXIS_DIM // 2, 1),
                        pltpu.roll(t, AXIS_DIM // 2, 1))
    return t * cos + partner * sin_signed


def _head_norm_rope(t, g, cos, sin_signed):
    ms = jnp.mean(t * t, axis=-1, keepdims=True)
    return _rope(t * lax.rsqrt(ms + EPS) * g, cos, sin_signed)


def _ab_in_kernel(x_ref, g_ref, sh_ref, sc_ref, w_ref, cos_ref, sin_ref, qn_ref, kn_ref,
                  qkv_ref, z_ref, *, tm):
    rows_per = tm // ROW_SPLIT
    n_q_tiles = A_Q_DIM // AB_TN
    for part in range(ROW_SPLIT):
        rs = slice(part * rows_per, (part + 1) * rows_per)
        h = _norm_mod(x_ref[rs, :], g_ref[...], sh_ref[...], sc_ref[...]).astype(BF16)
        cos = cos_ref[rs, :]
        sin = sin_ref[rs, :]
        for t in range(AB_IN // AB_TN):
            p = _dot(h, w_ref[:, t * AB_TN:(t + 1) * AB_TN])
            if t < n_q_tiles:
                for hh in range(AB_TN // HEAD_DIM):
                    c0 = t * AB_TN + hh * HEAD_DIM
                    qkv_ref[rs, c0:c0 + HEAD_DIM] = _head_norm_rope(
                        p[:, hh * HEAD_DIM:(hh + 1) * HEAD_DIM], qn_ref[...], cos, sin).astype(BF16)
            elif t == n_q_tiles:
                for hh in range(A_KV_HEADS):
                    c0 = A_Q_DIM + hh * HEAD_DIM
                    qkv_ref[rs, c0:c0 + HEAD_DIM] = _head_norm_rope(
                        p[:, hh * HEAD_DIM:(hh + 1) * HEAD_DIM], kn_ref[...], cos, sin).astype(BF16)
                qkv_ref[rs, A_Q_DIM + A_KV_DIM:] = p[:, A_KV_DIM:].astype(BF16)
            else:
                c0 = (t - n_q_tiles - 1) * AB_TN
                z_ref[rs, c0:c0 + AB_TN] = p


def _ab_in_proj(x2, g, shift, scale, w_in_bf, cos, sin, qn, kn, tm, cast=None):
    n = x2.shape[0]
    assert A_Q_DIM % AB_TN == 0 and 2 * A_KV_DIM == AB_TN and B_WIDTH % AB_TN == 0
    row = lambda i: (0, 0)
    return _pallas(
        functools.partial(_ab_in_kernel, tm=tm),
        grid=(n // tm,),
        cast=cast, step_of=lambda i: i,
        in_specs=[
            pl.BlockSpec((tm, D_MODEL), lambda i: (i, 0)),
            pl.BlockSpec((1, D_MODEL), row),
            pl.BlockSpec((1, D_MODEL), row),
            pl.BlockSpec((1, D_MODEL), row),
            pl.BlockSpec((D_MODEL, AB_IN), row, pipeline_mode=pl.Buffered(1)),
            pl.BlockSpec((tm, HEAD_DIM), lambda i: (i, 0)),
            pl.BlockSpec((tm, HEAD_DIM), lambda i: (i, 0)),
            pl.BlockSpec((1, HEAD_DIM), row),
            pl.BlockSpec((1, HEAD_DIM), row),
        ],
        out_specs=[
            pl.BlockSpec((tm, A_QKV_DIM), lambda i: (i, 0)),
            pl.BlockSpec((tm, B_WIDTH), lambda i: (i, 0)),
        ],
        out_shape=[jax.ShapeDtypeStruct((n, A_QKV_DIM), BF16),
                   jax.ShapeDtypeStruct((n, B_WIDTH), F32)],
        name="ab_in_proj",
        args=(x2, g, shift, scale, w_in_bf, cos, sin, qn, kn))


def _attn_kernel(sink_ref, bias_ref, q_ref, kp_ref, kc_ref, kn_ref, vp_ref, vc_ref, vn_ref,
                 kx_ref, vx_ref, o_ref):
    rows = A_GROUP * BLOCK
    bias = bias_ref[...]
    r1 = lax.broadcasted_iota(jnp.int32, (rows, 1), 0) // BLOCK
    for hk in range(A_KV_HEADS):
        hs = slice(hk * HEAD_DIM, (hk + 1) * HEAD_DIM)
        kcat = jnp.concatenate([kp_ref[:, hs], kc_ref[:, hs], kn_ref[:, hs], kx_ref[:, hs]], axis=0)
        vcat = jnp.concatenate([vp_ref[:, hs], vc_ref[:, hs], vn_ref[:, hs], vx_ref[:, hs]], axis=0)
        q0 = hk * A_GROUP * HEAD_DIM
        q4 = jnp.concatenate(
            [q_ref[:, q0 + g * HEAD_DIM:q0 + (g + 1) * HEAD_DIM] for g in range(A_GROUP)], axis=0)
        s = lax.dot_general(q4, kcat, (((1,), (1,)), ((), ())), preferred_element_type=F32)
        s = s * ATTN_SCALE + bias
        sink = jnp.zeros((rows, 1), F32)
        for g in range(A_GROUP):
            sink = jnp.where(r1 == g, sink_ref[hk * A_GROUP + g], sink)
        m = jnp.maximum(jnp.max(s, axis=-1, keepdims=True), sink)
        e = jnp.exp(s - m)
        den = jnp.sum(e, axis=-1, keepdims=True) + jnp.exp(sink - m)
        pr = (e / den).astype(BF16)
        o4 = _dot(pr, vcat)
        for g in range(A_GROUP):
            o_ref[:, q0 + g * HEAD_DIM:q0 + (g + 1) * HEAD_DIM] = (
                o4[g * BLOCK:(g + 1) * BLOCK].astype(BF16))


def _attn_bias(ctx_len):
    rows = A_GROUP * BLOCK
    cols = 3 * BLOCK + ctx_len
    r = np.arange(rows)[:, None] % BLOCK
    c = np.arange(cols)[None, :]
    band = (np.abs(BLOCK + r - c) <= WINDOW) | (c >= 3 * BLOCK)
    first = band & (c >= BLOCK)
    last = band & ((c < 2 * BLOCK) | (c >= 3 * BLOCK))
    masks = np.stack([first, band, last])
    return jnp.asarray(np.where(masks, 0.0, NEG_INF), dtype=F32)


def _window_attention(qkv, qkv_ctx, sink, cast=None):
    n = qkv.shape[0]
    ctx_len = qkv_ctx.shape[0]
    n_blocks = n // BLOCK
    assert n_blocks >= 2
    k_col = A_Q_DIM // A_KV_DIM
    v_col = k_col + 1
    prev = lambda i: jnp.maximum(i - 1, 0)
    nxt = lambda i: jnp.minimum(i + 1, n_blocks - 1)
    which = lambda i: jnp.where(i == 0, 0, jnp.where(i == n_blocks - 1, 2, 1))
    blk = (BLOCK, A_KV_DIM)
    rows = A_GROUP * BLOCK
    cols = 3 * BLOCK + ctx_len
    return _pallas(
        _attn_kernel,
        grid=(n_blocks,),
        cast=cast, step_of=lambda i: i,
        in_specs=[
            pl.BlockSpec(memory_space=pltpu.SMEM),
            pl.BlockSpec((None, rows, cols), lambda i: (which(i), 0, 0)),
            pl.BlockSpec((BLOCK, A_Q_DIM), lambda i: (i, 0)),
            pl.BlockSpec(blk, lambda i: (prev(i), k_col)),
            pl.BlockSpec(blk, lambda i: (i, k_col)),
            pl.BlockSpec(blk, lambda i: (nxt(i), k_col)),
            pl.BlockSpec(blk, lambda i: (prev(i), v_col)),
            pl.BlockSpec(blk, lambda i: (i, v_col)),
            pl.BlockSpec(blk, lambda i: (nxt(i), v_col)),
            pl.BlockSpec((ctx_len, A_KV_DIM), lambda i: (0, k_col)),
            pl.BlockSpec((ctx_len, A_KV_DIM), lambda i: (0, v_col)),
        ],
        out_specs=[pl.BlockSpec((BLOCK, A_Q_DIM), lambda i: (i, 0))],
        out_shape=[jax.ShapeDtypeStruct((n, A_Q_DIM), BF16)],
        name="window_attention",
        args=(sink, _attn_bias(ctx_len), qkv, qkv, qkv, qkv, qkv, qkv, qkv, qkv_ctx, qkv_ctx))


POOL_HALO = 8
assert max(POOL_WINDOWS) // 2 <= POOL_HALO


def _pool_kernel(zm_ref, zp_ref, zn_ref, w_ref, ps_ref, o_ref, z_scr, *, n_rows, tm):
    i = pl.program_id(0)
    last = pl.num_programs(0) - 1
    z_scr[0:POOL_HALO, :] = jnp.where(i > 0, zp_ref[...], 0.0)
    z_scr[POOL_HALO:POOL_HALO + tm, :] = zm_ref[...]
    z_scr[POOL_HALO + tm:, :] = jnp.where(i < last, zn_ref[...], 0.0)
    t = i * tm + lax.broadcasted_iota(jnp.int32, (tm, B_GROUP_DIM), 0)
    for g in range(B_GROUPS):
        half = POOL_WINDOWS[g] // 2
        cs = slice(g * B_GROUP_DIM, (g + 1) * B_GROUP_DIM)
        acc = z_scr[pl.ds(POOL_HALO - half, tm), cs]
        for off in range(-half + 1, half):
            acc = acc + z_scr[pl.ds(POOL_HALO + off, tm), cs]
        cnt = (jnp.minimum(t + half, n_rows) - jnp.maximum(t - half, 0)).astype(F32)
        d = (acc / cnt - zm_ref[:, cs]).astype(BF16)
        y = _dot(d, w_ref[g]) * ps_ref[:, cs]
        o_ref[:, cs] = y.astype(BF16)


def _pool_mix(z, w_pool_bf, pool_scale, tm, cast=None):
    n = z.shape[0]
    hb = tm // POOL_HALO
    n_halo_blocks = n // POOL_HALO
    return _pallas(
        functools.partial(_pool_kernel, n_rows=n, tm=tm),
        grid=(n // tm,),
        cast=cast, step_of=lambda i: i,
        in_specs=[
            pl.BlockSpec((tm, B_WIDTH), lambda i: (i, 0)),
            pl.BlockSpec((POOL_HALO, B_WIDTH), lambda i: (jnp.maximum(i * hb - 1, 0), 0)),
            pl.BlockSpec((POOL_HALO, B_WIDTH), lambda i: (jnp.minimum((i + 1) * hb, n_halo_blocks - 1), 0)),
            pl.BlockSpec((B_GROUPS, B_GROUP_DIM, B_GROUP_DIM), lambda i: (0, 0, 0)),
            pl.BlockSpec((1, B_WIDTH), lambda i: (0, 0)),
        ],
        out_specs=[pl.BlockSpec((tm, B_WIDTH), lambda i: (i, 0))],
        out_shape=[jax.ShapeDtypeStruct((n, B_WIDTH), BF16)],
        scratch_shapes=[pltpu.VMEM((tm + 2 * POOL_HALO, B_WIDTH), F32)],
        name="pool_mix",
        args=(z, z, z, w_pool_bf, pool_scale))


OUT_TN = 512


def _out_proj_kernel(a1_ref, a2_ref, w_ref, x_ref, gate_ref, g_ref, sh_ref, sc_ref, o_ref, h_ref, *, tm):
    rows_per = tm // ROW_SPLIT
    for part in range(ROW_SPLIT):
        rs = slice(part * rows_per, (part + 1) * rows_per)
        a = jnp.concatenate([a1_ref[rs, :].astype(BF16), a2_ref[rs, :].astype(BF16)], axis=1)
        for t in range(D_MODEL // OUT_TN):
            cs = slice(t * OUT_TN, (t + 1) * OUT_TN)
            o_ref[rs, cs] = x_ref[rs, cs] + gate_ref[:, cs] * _dot(a, w_ref[:, cs])
        h_ref[rs, :] = _norm_mod(o_ref[rs, :], g_ref[...], sh_ref[...], sc_ref[...])


def _out_proj(a1, a2, w_out_bf, x2, gate, g2, shift2, scale2, tm, cast=None):
    n = x2.shape[0]
    k1 = a1.shape[1]
    k2 = a2.shape[1]
    assert k1 + k2 == w_out_bf.shape[0] and k1 % V7X_LANES == 0
    row = lambda i: (0, 0)
    return _pallas(
        functools.partial(_out_proj_kernel, tm=tm),
        grid=(n // tm,),
        cast=cast, step_of=lambda i: i,
        in_specs=[
            pl.BlockSpec((tm, k1), lambda i: (i, 0)),
            pl.BlockSpec((tm, k2), lambda i: (i, 0)),
            pl.BlockSpec((k1 + k2, D_MODEL), row, pipeline_mode=pl.Buffered(1)),
            pl.BlockSpec((tm, D_MODEL), lambda i: (i, 0)),
            pl.BlockSpec((1, D_MODEL), row),
            pl.BlockSpec((1, D_MODEL), row),
            pl.BlockSpec((1, D_MODEL), row),
            pl.BlockSpec((1, D_MODEL), row),
        ],
        out_specs=[pl.BlockSpec((tm, D_MODEL), lambda i: (i, 0)),
                   pl.BlockSpec((tm, D_MODEL), lambda i: (i, 0))],
        out_shape=[jax.ShapeDtypeStruct((n, D_MODEL), F32),
                   jax.ShapeDtypeStruct((n, D_MODEL), F32)],
        name="out_proj",
        args=(a1, a2, w_out_bf, x2, gate, g2, shift2, scale2))


FFN_SLABS = V7X_SUBLANES_F32
FFN_EDGE_ROWS = V7X_SUBLANES_BF16
FFN_PIECE_ROWS = 16
FFN_CW = 256
FFN_TF = 2 * FFN_CW
FFN_TN = 512
FFN_CHUNKS = D_FF // FFN_CW
FFN_UP_STEPS = D_FF // FFN_TF
FFN_DOWN_STEPS = D_MODEL // FFN_TN


def _ffn_up_kernel(hm_ref, hp_ref, hn_ref, wg_ref, wv_ref, cw_ref, cb_ref, o_ref,
                   h_scr, carry_a, carry_b, *, tm):
    i = pl.program_id(0)
    j = pl.program_id(1)
    last_i = pl.num_programs(0) - 1
    nc = FFN_CHUNKS
    ns = FFN_SLABS
    sr = tm // ns

    def finish(carry, k, r, step):
        c = jnp.maximum(2 * step + k, 0)

        def rows_of(idx, s, q0, n):
            if q0 < 0:
                return jnp.concatenate([carry[idx, tm:tm + 1, :],
                                        carry[idx, pl.ds(s * sr, n - 1), :]], axis=0)
            if q0 + n > sr:
                return jnp.concatenate([carry[idx, pl.ds(s * sr + q0, n - 1), :],
                                        carry[idx, tm + 1:tm + 2, :]], axis=0)
            return carry[idx, pl.ds(s * sr + q0, n), :]

        def conv(idx, kk, q0):
            n = FFN_PIECE_ROWS
            mid = rows_of(idx, r, q0, n)
            prev = rows_of(idx, r - 1, q0, n) if r > 0 else rows_of(idx, ns - 1, q0 - 1, n)
            nxt = rows_of(idx, r + 1, q0, n) if r < ns - 1 else rows_of(idx, 0, q0 + 1, n)
            return prev * cw_ref[kk, 0] + mid * cw_ref[kk, 1] + nxt * cw_ref[kk, 2] + cb_ref[kk]

        for q0 in range(0, sr, FFN_PIECE_ROWS):
            gg = conv(k, c, q0)
            vv = conv(2 + k, c + nc, q0)
            o_ref[pl.ds(r * sr + q0, FFN_PIECE_ROWS), k * FFN_CW:(k + 1) * FFN_CW] = (
                gg * jax.nn.sigmoid(gg) * vv).astype(BF16)

    @pl.when(j == 0)
    def _():
        hm = hm_ref[...].reshape(sr, ns, D_MODEL)
        h_scr[0:tm, :] = jnp.swapaxes(hm, 0, 1).reshape(tm, D_MODEL).astype(BF16)
        before = jnp.where(i > 0, hp_ref[ns - 1:ns, :], 0.0)
        after = jnp.where(i < last_i, hn_ref[0:1, :], 0.0)
        pad = jnp.zeros((FFN_EDGE_ROWS - 2, D_MODEL), F32)
        h_scr[tm:, :] = jnp.concatenate([before, after, pad], axis=0).astype(BF16)
        carry_b[...] = jnp.zeros_like(carry_b)

    def up_step(carry_w, carry_r):
        for k in range(2):
            for r in range(ns):
                finish(carry_r, k, r, j - 1)
        for k in range(2):
            cs = slice(k * FFN_CW, (k + 1) * FFN_CW)
            carry_w[k] = _dot(h_scr[...], wg_ref[:, cs])
            carry_w[2 + k] = _dot(h_scr[...], wv_ref[:, cs])

    @pl.when((j < FFN_UP_STEPS) & (j % 2 == 0))
    def _():
        up_step(carry_a, carry_b)

    @pl.when((j < FFN_UP_STEPS) & (j % 2 == 1))
    def _():
        up_step(carry_b, carry_a)

    @pl.when(j == FFN_UP_STEPS)
    def _():
        last = carry_a if (FFN_UP_STEPS - 1) % 2 == 0 else carry_b
        for k in range(2):
            for r in range(ns):
                finish(last, k, r, FFN_UP_STEPS - 1)


def _ffn_down_kernel(a_ref, wd_ref, x_ref, gate_ref, o_ref, *, tm, up_tm):
    ns = FFN_SLABS
    sr = up_tm // ns
    y = _dot(a_ref[:, FFN_TF:], wd_ref[...])
    y = jnp.swapaxes(y.reshape(tm // up_tm, ns, sr, FFN_TN), 1, 2).reshape(tm, FFN_TN)
    o_ref[...] = x_ref[...] + gate_ref[...] * y


def _conv_ffn(x2, h2, w_up_bf, conv_w, conv_b, w_down_bf, gate, tm, down_tm, up_cast=None,
              down_cast=None):
    n = x2.shape[0]
    nj = FFN_UP_STEPS
    hb = tm // FFN_SLABS
    n_halo_blocks = n // FFN_SLABS
    act_cols = D_FF + FFN_TF
    cw3 = conv_w.reshape(3, 2 * FFN_CHUNKS, FFN_CW).transpose(1, 0, 2)
    cw3 = jnp.broadcast_to(cw3[:, :, None, :], (2 * FFN_CHUNKS, 3, FFN_PIECE_ROWS, FFN_CW))
    cb3 = jnp.broadcast_to(conv_b.reshape(2 * FFN_CHUNKS, 1, FFN_CW),
                           (2 * FFN_CHUNKS, FFN_PIECE_ROWS, FFN_CW))
    once = pl.Buffered(1)
    act, *up_cast = _pallas(
        functools.partial(_ffn_up_kernel, tm=tm),
        grid=(n // tm, nj + 1),
        cast=up_cast, step_of=lambda i, j: i * (nj + 1) + j,
        in_specs=[
            pl.BlockSpec((tm, D_MODEL), lambda i, j: (i, 0)),
            pl.BlockSpec((FFN_SLABS, D_MODEL), lambda i, j: (jnp.maximum(i * hb - 1, 0), 0)),
            pl.BlockSpec((FFN_SLABS, D_MODEL), lambda i, j: (jnp.minimum((i + 1) * hb, n_halo_blocks - 1), 0)),
            pl.BlockSpec((None, D_MODEL, FFN_TF), lambda i, j: (jnp.minimum(j, nj - 1), 0, 0)),
            pl.BlockSpec((None, D_MODEL, FFN_TF), lambda i, j: (jnp.minimum(j, nj - 1) + nj, 0, 0)),
            pl.BlockSpec((2 * FFN_CHUNKS, 3, FFN_PIECE_ROWS, FFN_CW), lambda i, j: (0, 0, 0, 0),
                         pipeline_mode=once),
            pl.BlockSpec((2 * FFN_CHUNKS, FFN_PIECE_ROWS, FFN_CW), lambda i, j: (0, 0, 0),
                         pipeline_mode=once),
        ],
        out_specs=[pl.BlockSpec((tm, FFN_TF), lambda i, j: (i, j))],
        out_shape=[jax.ShapeDtypeStruct((n, act_cols), BF16)],
        scratch_shapes=[pltpu.VMEM((tm + FFN_EDGE_ROWS, D_MODEL), BF16),
                        pltpu.VMEM((4, tm + FFN_EDGE_ROWS, FFN_CW), F32),
                        pltpu.VMEM((4, tm + FFN_EDGE_ROWS, FFN_CW), F32)],
        name="ffn_up",
        args=(h2, h2, h2, w_up_bf, w_up_bf, cw3, cb3))
    out, *down_cast = _pallas(
        functools.partial(_ffn_down_kernel, tm=down_tm, up_tm=tm),
        grid=(n // down_tm, FFN_DOWN_STEPS),
        cast=down_cast, step_of=lambda i, j: i * FFN_DOWN_STEPS + j,
        in_specs=[
            pl.BlockSpec((down_tm, act_cols), lambda i, j: (i, 0)),
            pl.BlockSpec((None, D_FF, FFN_TN), lambda i, j: (j, 0, 0)),
            pl.BlockSpec((down_tm, FFN_TN), lambda i, j: (i, j)),
            pl.BlockSpec((1, FFN_TN), lambda i, j: (0, j)),
        ],
        out_specs=[pl.BlockSpec((down_tm, FFN_TN), lambda i, j: (i, j))],
        out_shape=[jax.ShapeDtypeStruct((n, D_MODEL), F32)],
        name="ffn_down",
        args=(act, w_down_bf, x2, gate))
    return out, up_cast, down_cast


def _gelu(x):
    return 0.5 * x * (1.0 + lax.erf(x * (2.0 ** -0.5)))


def _cd_in_kernel(x_ref, g_ref, sh_ref, sc_ref, w_ref, vg_ref, uv_ref, f_ref, *, tm):
    rows_per = tm // ROW_SPLIT
    for part in range(ROW_SPLIT):
        rs = slice(part * rows_per, (part + 1) * rows_per)
        h = _norm_mod(x_ref[rs, :], g_ref[...], sh_ref[...], sc_ref[...]).astype(BF16)
        uv_ref[rs, 0:C_WIDTH] = _gelu(_dot(h, w_ref[:, 0:C_WIDTH])).astype(BF16)
        v = _gelu(_dot(h, w_ref[:, C_WIDTH:2 * C_WIDTH]))
        ms = jnp.mean(v * v, axis=-1, keepdims=True)
        uv_ref[rs, C_WIDTH:] = (v * lax.rsqrt(ms + EPS) * vg_ref[...]).astype(BF16)
        f_ref[rs, :] = _dot(h, w_ref[:, 2 * C_WIDTH:])


def _cd_in_proj(x2, g, shift, scale, w_in_bf, v_norm_g, tm, cast=None):
    n = x2.shape[0]
    row = lambda i: (0, 0)
    return _pallas(
        functools.partial(_cd_in_kernel, tm=tm),
        grid=(n // tm,),
        cast=cast, step_of=lambda i: i,
        in_specs=[
            pl.BlockSpec((tm, D_MODEL), lambda i: (i, 0)),
            pl.BlockSpec((1, D_MODEL), row),
            pl.BlockSpec((1, D_MODEL), row),
            pl.BlockSpec((1, D_MODEL), row),
            pl.BlockSpec((D_MODEL, CD_IN), row, pipeline_mode=pl.Buffered(1)),
            pl.BlockSpec((1, C_WIDTH), row),
        ],
        out_specs=[
            pl.BlockSpec((tm, 2 * C_WIDTH), lambda i: (i, 0)),
            pl.BlockSpec((tm, D_WIDTH), lambda i: (i, 0)),
        ],
        out_shape=[jax.ShapeDtypeStruct((n, 2 * C_WIDTH), BF16),
                   jax.ShapeDtypeStruct((n, D_WIDTH), F32)],
        name="cd_in_proj",
        args=(x2, g, shift, scale, w_in_bf, v_norm_g))


def _spatial_kernel(u_ref, v_ref, ws_ref, bias_ref, o_ref, *, chunks):
    for k in range(chunks):
        rs = slice(k * CHUNK, (k + 1) * CHUNK)
        for g in range(C_GROUPS):
            cs = slice(g * C_GROUP_DIM, (g + 1) * C_GROUP_DIM)
            s = _dot(ws_ref[g], v_ref[rs, cs]) + bias_ref[:, cs]
            o_ref[rs, cs] = (u_ref[rs, cs].astype(F32) * s).astype(BF16)


def _spatial_gate(uv, w_spatial_bf, bias_full, chunks=4):
    n = uv.shape[0]
    rows = chunks * CHUNK
    return pl.pallas_call(
        functools.partial(_spatial_kernel, chunks=chunks),
        grid=(n // rows,),
        in_specs=[
            pl.BlockSpec((rows, C_WIDTH), lambda i: (i, 0)),
            pl.BlockSpec((rows, C_WIDTH), lambda i: (i, 1)),
            pl.BlockSpec((C_GROUPS, CHUNK, CHUNK), lambda i: (0, 0, 0)),
            pl.BlockSpec((CHUNK, C_WIDTH), lambda i: (0, 0)),
        ],
        out_specs=pl.BlockSpec((rows, C_WIDTH), lambda i: (i, 0)),
        out_shape=jax.ShapeDtypeStruct((n, C_WIDTH), BF16),
        compiler_params=_params(("arbitrary",)),
        name="spatial_gate",
    )(uv, uv, w_spatial_bf, bias_full)


FFT_SUB = V7X_SUBLANES_F32
FFT1_CT = 512
FFT2_CT = 512


def _fourier_tables(n):
    a_len, b_len, sub = FFT_A, FFT_B, FFT_SUB
    assert a_len * b_len == n
    ch = np.arange(D_GROUP_DIM)
    ang_c = 2.0 * np.pi * ((ch[:, None] * ch[None, :]) % D_GROUP_DIM) / D_GROUP_DIM
    a = np.arange(a_len)
    f_a = np.exp(-2j * np.pi * ((a[:, None] * a[None, :]) % a_len) / a_len)
    m1 = np.kron(f_a, np.eye(sub))
    b = np.arange(b_len)
    tw = np.exp(-2j * np.pi * ((a[:, None] * b[None, :]) % n) / n)
    f_b = np.exp(-2j * np.pi * ((b[:, None] * b[None, :]) % b_len) / b_len)
    m2 = np.einsum('db,pq->dpqb', f_b, np.eye(sub)).reshape(b_len * sub, sub * b_len)
    norm = 1.0 / np.sqrt(float(n) * D_GROUP_DIM)
    m2 = m2 * norm
    tw3 = np.broadcast_to(tw[:, :, None], (a_len, b_len, V7X_LANES))
    f32 = lambda v: jnp.asarray(np.ascontiguousarray(v), dtype=F32)
    return dict(cos_c=f32(np.cos(ang_c)), sin_c=f32(np.sin(ang_c)),
                m1r=f32(m1.real), m1i=f32(m1.imag), m2r=f32(m2.real), m2i=f32(m2.imag),
                twr=f32(tw3.real), twi=f32(tw3.imag))


def _fft1_kernel(f_ref, cc_ref, sc_ref, m1r_ref, m1i_ref, twr_ref, twi_ref, tr_ref, ti_ref):
    rows = FFT_A * FFT_SUB
    ct = FFT1_CT
    fb = f_ref[...].reshape(rows, ct).astype(BF16)
    xr_parts = []
    xi_parts = []
    for q in range(ct // D_GROUP_DIM):
        blk = fb[:, q * D_GROUP_DIM:(q + 1) * D_GROUP_DIM]
        xr_parts.append(_dot(blk, cc_ref[...]))
        xi_parts.append(-_dot(blk, sc_ref[...]))
    xr = jnp.concatenate(xr_parts, axis=1).astype(BF16)
    xi = jnp.concatenate(xi_parts, axis=1).astype(BF16)
    m1r = m1r_ref[...]
    m1i = m1i_ref[...]
    tr = _dot(m1r, xr) - _dot(m1i, xi)
    ti = _dot(m1r, xi) + _dot(m1i, xr)
    reps = ct // V7X_LANES
    twr = jnp.tile(twr_ref[...].reshape(rows, V7X_LANES), (1, reps))
    twi = jnp.tile(twi_ref[...].reshape(rows, V7X_LANES), (1, reps))
    tr_ref[...] = (tr * twr - ti * twi).reshape(FFT_A, FFT_SUB, ct)
    ti_ref[...] = (tr * twi + ti * twr).reshape(FFT_A, FFT_SUB, ct)


def _fft2_kernel(tr_ref, ti_ref, m2r_ref, m2i_ref, wf_ref, o_ref):
    j = pl.program_id(1)
    tr = tr_ref[...].astype(BF16)
    ti = ti_ref[...].astype(BF16)
    z = _dot(m2r_ref[...], tr) - _dot(m2i_ref[...], ti)
    contrib = _dot(z.astype(BF16), wf_ref[...]).reshape(FFT_B, FFT_SUB, D_WIDTH)

    @pl.when(j == 0)
    def _():
        o_ref[...] = contrib

    @pl.when(j > 0)
    def _():
        o_ref[...] += contrib


def _fourier_mix(f, tabs, w_fourier_bf):
    n = f.shape[0]
    a_len, b_len, sub = FFT_A, FFT_B, FFT_SUB
    f3 = f.reshape(a_len, b_len, D_WIDTH)
    rows1 = a_len * sub
    const2 = lambda i, j: (0, 0)
    tr, ti = pl.pallas_call(
        _fft1_kernel,
        grid=(b_len // sub, D_WIDTH // FFT1_CT),
        in_specs=[
            pl.BlockSpec((a_len, sub, FFT1_CT), lambda i, j: (0, i, j)),
            pl.BlockSpec((D_GROUP_DIM, D_GROUP_DIM), const2),
            pl.BlockSpec((D_GROUP_DIM, D_GROUP_DIM), const2),
            pl.BlockSpec((rows1, rows1), const2),
            pl.BlockSpec((rows1, rows1), const2),
            pl.BlockSpec((a_len, sub, V7X_LANES), lambda i, j: (0, i, 0)),
            pl.BlockSpec((a_len, sub, V7X_LANES), lambda i, j: (0, i, 0)),
        ],
        out_specs=[pl.BlockSpec((a_len, sub, FFT1_CT), lambda i, j: (0, i, j))] * 2,
        out_shape=[jax.ShapeDtypeStruct((a_len, b_len, D_WIDTH), F32)] * 2,
        compiler_params=_params(("arbitrary", "arbitrary")),
        name="fourier_stage1",
    )(f3, tabs['cos_c'].astype(BF16), tabs['sin_c'].astype(BF16),
      tabs['m1r'].astype(BF16), tabs['m1i'].astype(BF16), tabs['twr'], tabs['twi'])

    rows2 = sub * b_len
    tr2 = tr.reshape(n, D_WIDTH)
    ti2 = ti.reshape(n, D_WIDTH)
    out = pl.pallas_call(
        _fft2_kernel,
        grid=(a_len // sub, D_WIDTH // FFT2_CT),
        in_specs=[
            pl.BlockSpec((rows2, FFT2_CT), lambda i, j: (i, j)),
            pl.BlockSpec((rows2, FFT2_CT), lambda i, j: (i, j)),
            pl.BlockSpec((rows2, rows2), const2),
            pl.BlockSpec((rows2, rows2), const2),
            pl.BlockSpec((FFT2_CT, D_WIDTH), lambda i, j: (j, 0)),
        ],
        out_specs=pl.BlockSpec((b_len, sub, D_WIDTH), lambda i, j: (0, i, 0)),
        out_shape=jax.ShapeDtypeStruct((b_len, a_len, D_WIDTH), F32),
        compiler_params=_params(("arbitrary", "arbitrary")),
        name="fourier_stage2",
    )(tr2, ti2, tabs['m2r'].astype(BF16), tabs['m2i'].astype(BF16), w_fourier_bf)
    return out.reshape(n, D_WIDTH)


def _rope_tables(n):
    rows = n // GRID_W
    row = np.repeat(np.arange(rows, dtype=np.float64), GRID_W)
    col = np.tile(np.arange(GRID_W, dtype=np.float64), rows)
    inv = ROPE_THETA ** (-np.arange(0, AXIS_DIM, 2, dtype=np.float64) / AXIS_DIM)
    ang_r = row[:, None] * inv[None, :]
    ang_c = col[:, None] * inv[None, :]
    cos = np.concatenate([np.cos(ang_r)] * 2 + [np.cos(ang_c)] * 2, axis=-1)
    sin = np.concatenate([-np.sin(ang_r), np.sin(ang_r), -np.sin(ang_c), np.sin(ang_c)], axis=-1)
    return jnp.asarray(cos, dtype=F32), jnp.asarray(sin, dtype=F32)


def kernel(x, c, ctx, c_ctx, w_mod, b_mod, norm1_g, norm2_g, ab_w_in, a_q_norm_g, a_k_norm_g, a_sink,
           b_w_pool, b_pool_scale, ab_w_out, cd_w_in, c_v_norm_g, c_w_spatial, c_b_spatial, d_w_fourier,
           cd_w_out, f_w_up, f_conv_w, f_conv_b, f_w_down):
    batch, n, _ = x.shape
    ctx_len = ctx.shape[1]
    assert batch == 1 and DEPTH == 2
    x2 = x.reshape(n, D_MODEL)
    ctx2 = ctx.reshape(ctx_len, D_MODEL)

    mod = _mod_vectors(c, c_ctx, w_mod, b_mod)

    def split6(v):
        return [v[:, k * D_MODEL:(k + 1) * D_MODEL] for k in range(6)]

    row1 = lambda v: v.reshape(1, -1)
    n_tiles = n // PROJ_TM
    n_blocks = n // BLOCK
    up_steps = (n // FFN_UP_TM) * (FFN_UP_STEPS + 1)
    down_steps = (n // FFN_DOWN_TM) * FFN_DOWN_STEPS
    cast_down0 = _CastJob(f_w_down, 0, D_FF // n_tiles, n_tiles, col_block=FFN_TN)
    cast_up0 = _CastJob(f_w_up, 0, D_MODEL // n_blocks, n_blocks, col_block=FFN_TF)
    cast_up1 = _CastJob(f_w_up, 1, D_MODEL // n_blocks, up_steps, col_block=FFN_TF)
    cast_down1 = _CastJob(f_w_down, 1, D_FF // down_steps, down_steps, col_block=FFN_TN)

    ml = split6(mod[0, 0:1])
    mc = split6(mod[0, 1:2])
    g1 = row1(norm1_g[0])
    w_in = ab_w_in[0].astype(BF16)
    qn = row1(a_q_norm_g[0])
    kn = row1(a_k_norm_g[0])
    cos, sin = _rope_tables(n)
    qkv, z, w_down0 = _ab_in_proj(x2, g1, ml[0], ml[1], w_in, cos, sin, qn, kn, tm=PROJ_TM,
                                  cast=cast_down0)
    ones = jnp.ones((ctx_len, HEAD_DIM), F32)
    zeros = jnp.zeros((ctx_len, HEAD_DIM), F32)
    qkv_ctx, _ = _ab_in_proj(ctx2, g1, mc[0], mc[1], w_in, ones, zeros, qn, kn, tm=ctx_len)
    attn, w_up0 = _window_attention(qkv, qkv_ctx, a_sink[0], cast=cast_up0)
    pool_tiles = n // POOL_TM
    pooled, w_out0 = _pool_mix(z, b_w_pool[0].astype(BF16), row1(b_pool_scale[0]), tm=POOL_TM,
                               cast=_CastJob(ab_w_out, 0, D_MODEL // pool_tiles, pool_tiles))
    x2, h2, cd_w_in_bf = _out_proj(attn, pooled, w_out0, x2, ml[2], row1(norm2_g[0]), ml[3], ml[4],
                                   tm=PROJ_TM, cast=_CastJob(cd_w_in, 0, D_MODEL // n_tiles, n_tiles))
    x2, (w_up1,), (w_down1,) = _conv_ffn(x2, h2, w_up0, f_conv_w[0], f_conv_b[0], w_down0, ml[5],
                                         tm=FFN_UP_TM, down_tm=FFN_DOWN_TM,
                                         up_cast=cast_up1, down_cast=cast_down1)

    ml = split6(mod[1, 0:1])
    uv, f, w_out1 = _cd_in_proj(x2, row1(norm1_g[1]), ml[0], ml[1], cd_w_in_bf, row1(c_v_norm_g[0]),
                                tm=PROJ_TM, cast=_CastJob(cd_w_out, 0, D_MODEL // n_tiles, n_tiles))
    bias_full = jnp.repeat(c_b_spatial[0].T, C_GROUP_DIM, axis=1)
    c_out = _spatial_gate(uv, c_w_spatial[0].astype(BF16), bias_full)
    d_out = _fourier_mix(f, _fourier_tables(n), d_w_fourier[0].astype(BF16))
    x2, h2 = _out_proj(c_out, d_out, w_out1, x2, ml[2], row1(norm2_g[1]), ml[3], ml[4], tm=PROJ_TM)
    x2, _, _ = _conv_ffn(x2, h2, w_up1, f_conv_w[1], f_conv_b[1], w_down1, ml[5],
                         tm=FFN_UP_TM, down_tm=FFN_DOWN_TM)
    return x2.reshape(batch, n, D_MODEL)
```

```python
import functools

import numpy as np
import jax
import jax.numpy as jnp
from jax import lax
from jax.experimental import pallas as pl
from jax.experimental.pallas import tpu as pltpu

F32 = jnp.float32
BF16 = jnp.bfloat16

D_MODEL = 2048
DEPTH = 2
GRID_W = 64
HEAD_DIM = 128
A_Q_HEADS = 8
A_KV_HEADS = 2
A_GROUP = A_Q_HEADS // A_KV_HEADS
A_Q_DIM = A_Q_HEADS * HEAD_DIM
A_KV_DIM = A_KV_HEADS * HEAD_DIM
A_QKV_DIM = A_Q_DIM + 2 * A_KV_DIM
WINDOW = 128
BLOCK = 128
ROPE_THETA = 10000.0
AXIS_DIM = HEAD_DIM // 2
ATTN_SCALE = HEAD_DIM ** -0.5
NEG_INF = -1e30
B_GROUPS = 4
B_WIDTH = 1024
B_GROUP_DIM = B_WIDTH // B_GROUPS
POOL_WINDOWS = (2, 4, 8, 16)
AB_IN = A_QKV_DIM + B_WIDTH
C_WIDTH = 1024
C_GROUPS = 4
C_GROUP_DIM = C_WIDTH // C_GROUPS
CHUNK = 128
D_WIDTH = 1024
D_GROUPS = 8
D_GROUP_DIM = D_WIDTH // D_GROUPS
CD_IN = 2 * C_WIDTH + D_WIDTH
D_FF = 5632
EPS = 1e-6

V7X_SUBLANES_F32 = 8
V7X_SUBLANES_BF16 = 16
V7X_LANES = 128
V7X_VMEM_BYTES = 64 * 1024 * 1024
VMEM_LIMIT = 56 * 1024 * 1024

PROJ_TM = 512
POOL_TM = 256
FFN_UP_TM = 1024
FFN_DOWN_TM = 1024

FFT_A = 64
FFT_B = 128


def _params(sem):
    return pltpu.CompilerParams(dimension_semantics=sem, vmem_limit_bytes=VMEM_LIMIT)


def _dot(a, b):
    return jnp.dot(a, b, preferred_element_type=F32)


class _CastJob:
    def __init__(self, src, layer, rows, n_steps, col_block=None):
        _, total_rows, self.cols = src.shape
        assert total_rows % rows == 0 and total_rows // rows <= n_steps
        self.src, self.layer, self.rows, self.col_block = src, layer, rows, col_block
        self.last = total_rows // rows - 1
        if col_block is None:
            self.out_shape = jax.ShapeDtypeStruct((total_rows, self.cols), BF16)
        else:
            assert self.cols % col_block == 0
            self.out_shape = jax.ShapeDtypeStruct((self.cols // col_block, total_rows, col_block), BF16)

    def specs(self, step_of):
        blk = lambda *ids: jnp.minimum(step_of(*ids), self.last)
        src = pl.BlockSpec((None, self.rows, self.cols), lambda *ids: (self.layer, blk(*ids), 0))
        if self.col_block is None:
            return src, pl.BlockSpec((self.rows, self.cols), lambda *ids: (blk(*ids), 0))
        return src, pl.BlockSpec((self.cols // self.col_block, self.rows, self.col_block),
                                 lambda *ids: (0, blk(*ids), 0))

    def run(self, src_ref, dst_ref):
        if self.col_block is None:
            dst_ref[...] = src_ref[...].astype(BF16)
        else:
            for b in range(self.cols // self.col_block):
                dst_ref[b] = src_ref[:, b * self.col_block:(b + 1) * self.col_block].astype(BF16)


def _host_cast(kernel_fn, cast, n_in, n_out):
    def body(*refs):
        cast.run(refs[n_in], refs[n_in + 1 + n_out])
        kernel_fn(*refs[:n_in], *refs[n_in + 1:n_in + 1 + n_out], *refs[n_in + 2 + n_out:])
    return body


def _pallas(kernel_fn, *, grid, in_specs, out_specs, out_shape, args, name, scratch_shapes=(),
            cast=None, step_of=None):
    out_specs, out_shape = list(out_specs), list(out_shape)
    if cast is not None:
        src_spec, dst_spec = cast.specs(step_of)
        kernel_fn = _host_cast(kernel_fn, cast, len(in_specs), len(out_specs))
        in_specs = [*in_specs, src_spec]
        out_specs.append(dst_spec)
        out_shape.append(cast.out_shape)
        args = (*args, cast.src)
    return pl.pallas_call(
        kernel_fn, grid=grid, in_specs=list(in_specs), out_specs=out_specs, out_shape=out_shape,
        scratch_shapes=list(scratch_shapes), compiler_params=_params(("arbitrary",) * len(grid)),
        name=name)(*args)


def _norm_mod(x, g, shift, scale):
    ms = jnp.mean(x * x, axis=-1, keepdims=True)
    y = x * lax.rsqrt(ms + EPS) * g
    return y * (1.0 + scale) + shift


MOD_TK = 128
MOD_STREAMS = 2


def _mod_kernel(cv_ref, w_ref, b_ref, o_ref, acc_scr):
    k = pl.program_id(1)
    sub = V7X_SUBLANES_F32
    groups = MOD_TK // sub
    n_out = 6 * D_MODEL
    lane_tiles = n_out // V7X_LANES

    @pl.when(k == 0)
    def _():
        acc_scr[...] = jnp.zeros_like(acc_scr)

    w = w_ref[0].reshape(groups, sub, n_out)
    for s in range(MOD_STREAMS):
        a = cv_ref[s]
        a = (a * jax.nn.sigmoid(a)).reshape(groups, sub, V7X_LANES)
        a = jnp.concatenate([a] * lane_tiles, axis=-1)
        acc_scr[s] += jnp.sum(w * a, axis=0)

    @pl.when(k == pl.num_programs(1) - 1)
    def _():
        rows = [jnp.sum(acc_scr[s], axis=0, keepdims=True) for s in range(MOD_STREAMS)]
        rows.append(jnp.zeros((sub - MOD_STREAMS, n_out), F32))
        o_ref[0] = jnp.concatenate(rows, axis=0) + b_ref[0]


def _mod_vectors(c, c_ctx, w_mod, b_mod):
    n_out = 6 * D_MODEL
    cv = jnp.stack([c.reshape(D_MODEL), c_ctx.reshape(D_MODEL)])
    cv = jnp.broadcast_to(cv[:, :, None], (MOD_STREAMS, D_MODEL, V7X_LANES))
    b3 = b_mod.reshape(DEPTH, 1, n_out)
    return pl.pallas_call(
        _mod_kernel,
        grid=(DEPTH, D_MODEL // MOD_TK),
        in_specs=[
            pl.BlockSpec((MOD_STREAMS, MOD_TK, V7X_LANES), lambda l, k: (0, k, 0)),
            pl.BlockSpec((1, MOD_TK, n_out), lambda l, k: (l, k, 0)),
            pl.BlockSpec((1, 1, n_out), lambda l, k: (l, 0, 0)),
        ],
        out_specs=pl.BlockSpec((1, V7X_SUBLANES_F32, n_out), lambda l, k: (l, 0, 0)),
        out_shape=jax.ShapeDtypeStruct((DEPTH, V7X_SUBLANES_F32, n_out), F32),
        scratch_shapes=[pltpu.VMEM((MOD_STREAMS, V7X_SUBLANES_F32, n_out), F32)],
        compiler_params=_params(("arbitrary", "arbitrary")),
        name="mod_vectors",
    )(cv, w_mod, b3)


AB_TN = 512
ROW_SPLIT = 2


def _rope(t, cos, sin_signed):
    lane = lax.broadcasted_iota(jnp.int32, t.shape, 1)
    first = (lane % AXIS_DIM) < (AXIS_DIM // 2)
    partner = jnp.where(first,
                        pltpu.roll(t, HEAD_DIM - AXIS_DIM // 2, 1),
                        pltpu.roll(t, AXIS_DIM // 2, 1))
    return t * cos + partner * sin_signed


def _head_norm_rope(t, g, cos, sin_signed):
    ms = jnp.mean(t * t, axis=-1, keepdims=True)
    return _rope(t * lax.rsqrt(ms + EPS) * g, cos, sin_signed)


def _ab_in_kernel(x_ref, g_ref, sh_ref, sc_ref, w_ref, cos_ref, sin_ref, qn_ref, kn_ref,
                  qkv_ref, z_ref, *, tm):
    rows_per = tm // ROW_SPLIT
    n_q_tiles = A_Q_DIM // AB_TN
    for part in range(ROW_SPLIT):
        rs = slice(part * rows_per, (part + 1) * rows_per)
        h = _norm_mod(x_ref[rs, :], g_ref[...], sh_ref[...], sc_ref[...]).astype(BF16)
        cos = cos_ref[rs, :]
        sin = sin_ref[rs, :]
        for t in range(AB_IN // AB_TN):
            p = _dot(h, w_ref[:, t * AB_TN:(t + 1) * AB_TN])
            if t < n_q_tiles:
                for hh in range(AB_TN // HEAD_DIM):
                    c0 = t * AB_TN + hh * HEAD_DIM
                    qkv_ref[rs, c0:c0 + HEAD_DIM] = _head_norm_rope(
                        p[:, hh * HEAD_DIM:(hh + 1) * HEAD_DIM], qn_ref[...], cos, sin).astype(BF16)
            elif t == n_q_tiles:
                for hh in range(A_KV_HEADS):
                    c0 = A_Q_DIM + hh * HEAD_DIM
                    qkv_ref[rs, c0:c0 + HEAD_DIM] = _head_norm_rope(
                        p[:, hh * HEAD_DIM:(hh + 1) * HEAD_DIM], kn_ref[...], cos, sin).astype(BF16)
                qkv_ref[rs, A_Q_DIM + A_KV_DIM:] = p[:, A_KV_DIM:].astype(BF16)
            else:
                c0 = (t - n_q_tiles - 1) * AB_TN
                z_ref[rs, c0:c0 + AB_TN] = p


def _ab_in_proj(x2, g, shift, scale, w_in_bf, cos, sin, qn, kn, tm, cast=None):
    n = x2.shape[0]
    assert A_Q_DIM % AB_TN == 0 and 2 * A_KV_DIM == AB_TN and B_WIDTH % AB_TN == 0
    row = lambda i: (0, 0)
    return _pallas(
        functools.partial(_ab_in_kernel, tm=tm),
        grid=(n // tm,),
        cast=cast, step_of=lambda i: i,
        in_specs=[
            pl.BlockSpec((tm, D_MODEL), lambda i: (i, 0)),
            pl.BlockSpec((1, D_MODEL), row),
            pl.BlockSpec((1, D_MODEL), row),
            pl.BlockSpec((1, D_MODEL), row),
            pl.BlockSpec((D_MODEL, AB_IN), row, pipeline_mode=pl.Buffered(1)),
            pl.BlockSpec((tm, HEAD_DIM), lambda i: (i, 0)),
            pl.BlockSpec((tm, HEAD_DIM), lambda i: (i, 0)),
            pl.BlockSpec((1, HEAD_DIM), row),
            pl.BlockSpec((1, HEAD_DIM), row),
        ],
        out_specs=[
            pl.BlockSpec((tm, A_QKV_DIM), lambda i: (i, 0)),
            pl.BlockSpec((tm, B_WIDTH), lambda i: (i, 0)),
        ],
        out_shape=[jax.ShapeDtypeStruct((n, A_QKV_DIM), BF16),
                   jax.ShapeDtypeStruct((n, B_WIDTH), F32)],
        name="ab_in_proj",
        args=(x2, g, shift, scale, w_in_bf, cos, sin, qn, kn))


def _attn_kernel(sink_ref, bias_ref, q_ref, kp_ref, kc_ref, kn_ref, vp_ref, vc_ref, vn_ref,
                 kx_ref, vx_ref, o_ref):
    rows = A_GROUP * BLOCK
    bias = bias_ref[...]
    r1 = lax.broadcasted_iota(jnp.int32, (rows, 1), 0) // BLOCK
    for hk in range(A_KV_HEADS):
        hs = slice(hk * HEAD_DIM, (hk + 1) * HEAD_DIM)
        kcat = jnp.concatenate([kp_ref[:, hs], kc_ref[:, hs], kn_ref[:, hs], kx_ref[:, hs]], axis=0)
        vcat = jnp.concatenate([vp_ref[:, hs], vc_ref[:, hs], vn_ref[:, hs], vx_ref[:, hs]], axis=0)
        q0 = hk * A_GROUP * HEAD_DIM
        q4 = jnp.concatenate(
            [q_ref[:, q0 + g * HEAD_DIM:q0 + (g + 1) * HEAD_DIM] for g in range(A_GROUP)], axis=0)
        s = lax.dot_general(q4, kcat, (((1,), (1,)), ((), ())), preferred_element_type=F32)
        s = s * ATTN_SCALE + bias
        sink = jnp.zeros((rows, 1), F32)
        for g in range(A_GROUP):
            sink = jnp.where(r1 == g, sink_ref[hk * A_GROUP + g], sink)
        m = jnp.maximum(jnp.max(s, axis=-1, keepdims=True), sink)
        e = jnp.exp(s - m)
        den = jnp.sum(e, axis=-1, keepdims=True) + jnp.exp(sink - m)
        o4 = _dot(e.astype(BF16), vcat) / den
        for g in range(A_GROUP):
            o_ref[:, q0 + g * HEAD_DIM:q0 + (g + 1) * HEAD_DIM] = (
                o4[g * BLOCK:(g + 1) * BLOCK].astype(BF16))


def _attn_bias(ctx_len):
    rows = A_GROUP * BLOCK
    cols = 3 * BLOCK + ctx_len
    r = np.arange(rows)[:, None] % BLOCK
    c = np.arange(cols)[None, :]
    band = (np.abs(BLOCK + r - c) <= WINDOW) | (c >= 3 * BLOCK)
    first = band & (c >= BLOCK)
    last = band & ((c < 2 * BLOCK) | (c >= 3 * BLOCK))
    masks = np.stack([first, band, last])
    return jnp.asarray(np.where(masks, 0.0, NEG_INF), dtype=F32)


def _window_attention(qkv, qkv_ctx, sink, cast=None):
    n = qkv.shape[0]
    ctx_len = qkv_ctx.shape[0]
    n_blocks = n // BLOCK
    assert n_blocks >= 2
    k_col = A_Q_DIM // A_KV_DIM
    v_col = k_col + 1
    prev = lambda i: jnp.maximum(i - 1, 0)
    nxt = lambda i: jnp.minimum(i + 1, n_blocks - 1)
    which = lambda i: jnp.where(i == 0, 0, jnp.where(i == n_blocks - 1, 2, 1))
    blk = (BLOCK, A_KV_DIM)
    rows = A_GROUP * BLOCK
    cols = 3 * BLOCK + ctx_len
    return _pallas(
        _attn_kernel,
        grid=(n_blocks,),
        cast=cast, step_of=lambda i: i,
        in_specs=[
            pl.BlockSpec(memory_space=pltpu.SMEM),
            pl.BlockSpec((None, rows, cols), lambda i: (which(i), 0, 0)),
            pl.BlockSpec((BLOCK, A_Q_DIM), lambda i: (i, 0)),
            pl.BlockSpec(blk, lambda i: (prev(i), k_col)),
            pl.BlockSpec(blk, lambda i: (i, k_col)),
            pl.BlockSpec(blk, lambda i: (nxt(i), k_col)),
            pl.BlockSpec(blk, lambda i: (prev(i), v_col)),
            pl.BlockSpec(blk, lambda i: (i, v_col)),
            pl.BlockSpec(blk, lambda i: (nxt(i), v_col)),
            pl.BlockSpec((ctx_len, A_KV_DIM), lambda i: (0, k_col)),
            pl.BlockSpec((ctx_len, A_KV_DIM), lambda i: (0, v_col)),
        ],
        out_specs=[pl.BlockSpec((BLOCK, A_Q_DIM), lambda i: (i, 0))],
        out_shape=[jax.ShapeDtypeStruct((n, A_Q_DIM), BF16)],
        name="window_attention",
        args=(sink, _attn_bias(ctx_len), qkv, qkv, qkv, qkv, qkv, qkv, qkv, qkv_ctx, qkv_ctx))


POOL_HALO = 8
assert max(POOL_WINDOWS) // 2 <= POOL_HALO


def _pool_kernel(zm_ref, zp_ref, zn_ref, w_ref, ps_ref, o_ref, z_scr, *, n_rows, tm):
    i = pl.program_id(0)
    last = pl.num_programs(0) - 1
    pad = POOL_HALO
    span_rows = tm + 2 * pad
    z_scr[0:pad, :] = jnp.zeros((pad, B_WIDTH), F32)
    z_scr[pad:2 * pad, :] = jnp.where(i > 0, zp_ref[...], 0.0)
    z_scr[2 * pad:2 * pad + tm, :] = zm_ref[...]
    z_scr[2 * pad + tm:3 * pad + tm, :] = jnp.where(i < last, zn_ref[...], 0.0)
    z_scr[3 * pad + tm:, :] = jnp.zeros((pad, B_WIDTH), F32)
    t = i * tm + lax.broadcasted_iota(jnp.int32, (tm, B_GROUP_DIM), 0)
    for g in range(B_GROUPS):
        window = POOL_WINDOWS[g]
        half = window // 2
        cs = slice(g * B_GROUP_DIM, (g + 1) * B_GROUP_DIM)
        s = z_scr[pl.ds(2 * pad - half, span_rows), cs]
        width = 1
        while width < window:
            s = s + pltpu.roll(s, span_rows - width, 0)
            width *= 2
        acc = s[0:tm]
        cnt = (jnp.minimum(t + half, n_rows) - jnp.maximum(t - half, 0)).astype(F32)
        d = (acc / cnt - zm_ref[:, cs]).astype(BF16)
        y = _dot(d, w_ref[g]) * ps_ref[:, cs]
        o_ref[:, cs] = y.astype(BF16)


def _pool_mix(z, w_pool_bf, pool_scale, tm, cast=None):
    n = z.shape[0]
    hb = tm // POOL_HALO
    n_halo_blocks = n // POOL_HALO
    return _pallas(
        functools.partial(_pool_kernel, n_rows=n, tm=tm),
        grid=(n // tm,),
        cast=cast, step_of=lambda i: i,
        in_specs=[
            pl.BlockSpec((tm, B_WIDTH), lambda i: (i, 0)),
            pl.BlockSpec((POOL_HALO, B_WIDTH), lambda i: (jnp.maximum(i * hb - 1, 0), 0)),
            pl.BlockSpec((POOL_HALO, B_WIDTH), lambda i: (jnp.minimum((i + 1) * hb, n_halo_blocks - 1), 0)),
            pl.BlockSpec((B_GROUPS, B_GROUP_DIM, B_GROUP_DIM), lambda i: (0, 0, 0)),
            pl.BlockSpec((1, B_WIDTH), lambda i: (0, 0)),
        ],
        out_specs=[pl.BlockSpec((tm, B_WIDTH), lambda i: (i, 0))],
        out_shape=[jax.ShapeDtypeStruct((n, B_WIDTH), BF16)],
        scratch_shapes=[pltpu.VMEM((tm + 4 * POOL_HALO, B_WIDTH), F32)],
        name="pool_mix",
        args=(z, z, z, w_pool_bf, pool_scale))


OUT_TN = 512


def _out_proj_kernel(a1_ref, a2_ref, w_ref, x_ref, gate_ref, g_ref, sh_ref, sc_ref, o_ref, h_ref, *, tm):
    rows_per = tm // ROW_SPLIT
    for part in range(ROW_SPLIT):
        rs = slice(part * rows_per, (part + 1) * rows_per)
        a = jnp.concatenate([a1_ref[rs, :].astype(BF16), a2_ref[rs, :].astype(BF16)], axis=1)
        for t in range(D_MODEL // OUT_TN):
            cs = slice(t * OUT_TN, (t + 1) * OUT_TN)
            o_ref[rs, cs] = x_ref[rs, cs] + gate_ref[:, cs] * _dot(a, w_ref[:, cs])
        h_ref[rs, :] = _norm_mod(o_ref[rs, :], g_ref[...], sh_ref[...], sc_ref[...])


def _out_proj(a1, a2, w_out_bf, x2, gate, g2, shift2, scale2, tm, cast=None):
    n = x2.shape[0]
    k1 = a1.shape[1]
    k2 = a2.shape[1]
    assert k1 + k2 == w_out_bf.shape[0] and k1 % V7X_LANES == 0
    row = lambda i: (0, 0)
    return _pallas(
        functools.partial(_out_proj_kernel, tm=tm),
        grid=(n // tm,),
        cast=cast, step_of=lambda i: i,
        in_specs=[
            pl.BlockSpec((tm, k1), lambda i: (i, 0)),
            pl.BlockSpec((tm, k2), lambda i: (i, 0)),
            pl.BlockSpec((k1 + k2, D_MODEL), row, pipeline_mode=pl.Buffered(1)),
            pl.BlockSpec((tm, D_MODEL), lambda i: (i, 0)),
            pl.BlockSpec((1, D_MODEL), row),
            pl.BlockSpec((1, D_MODEL), row),
            pl.BlockSpec((1, D_MODEL), row),
            pl.BlockSpec((1, D_MODEL), row),
        ],
        out_specs=[pl.BlockSpec((tm, D_MODEL), lambda i: (i, 0)),
                   pl.BlockSpec((tm, D_MODEL), lambda i: (i, 0))],
        out_shape=[jax.ShapeDtypeStruct((n, D_MODEL), F32),
                   jax.ShapeDtypeStruct((n, D_MODEL), F32)],
        name="out_proj",
        args=(a1, a2, w_out_bf, x2, gate, g2, shift2, scale2))


FFN_SLABS = V7X_SUBLANES_F32
FFN_EDGE_ROWS = V7X_SUBLANES_BF16
FFN_PIECE_ROWS = 16
FFN_CW = 256
FFN_TF = 2 * FFN_CW
FFN_TN = 512
FFN_CHUNKS = D_FF // FFN_CW
FFN_UP_STEPS = D_FF // FFN_TF
FFN_DOWN_STEPS = D_MODEL // FFN_TN


def _ffn_up_kernel(hm_ref, hp_ref, hn_ref, wg_ref, wv_ref, cw_ref, cb_ref, o_ref,
                   h_scr, carry_a, carry_b, *, tm):
    i = pl.program_id(0)
    j = pl.program_id(1)
    last_i = pl.num_programs(0) - 1
    nc = FFN_CHUNKS
    ns = FFN_SLABS
    sr = tm // ns

    def finish(carry, k, r, step):
        c = jnp.maximum(2 * step + k, 0)

        def rows_of(idx, s, q0, n):
            if q0 < 0:
                return jnp.concatenate([carry[idx, tm:tm + 1, :],
                                        carry[idx, pl.ds(s * sr, n - 1), :]], axis=0)
            if q0 + n > sr:
                return jnp.concatenate([carry[idx, pl.ds(s * sr + q0, n - 1), :],
                                        carry[idx, tm + 1:tm + 2, :]], axis=0)
            return carry[idx, pl.ds(s * sr + q0, n), :]

        def conv(idx, kk, q0):
            n = FFN_PIECE_ROWS
            mid = rows_of(idx, r, q0, n)
            prev = rows_of(idx, r - 1, q0, n) if r > 0 else rows_of(idx, ns - 1, q0 - 1, n)
            nxt = rows_of(idx, r + 1, q0, n) if r < ns - 1 else rows_of(idx, 0, q0 + 1, n)
            return prev * cw_ref[kk, 0] + mid * cw_ref[kk, 1] + nxt * cw_ref[kk, 2] + cb_ref[kk]

        for q0 in range(0, sr, FFN_PIECE_ROWS):
            gg = conv(k, c, q0)
            vv = conv(2 + k, c + nc, q0)
            o_ref[pl.ds(r * sr + q0, FFN_PIECE_ROWS), k * FFN_CW:(k + 1) * FFN_CW] = (
                gg * jax.nn.sigmoid(gg) * vv).astype(BF16)

    @pl.when(j == 0)
    def _():
        hm = hm_ref[...].reshape(sr, ns, D_MODEL)
        h_scr[0:tm, :] = jnp.swapaxes(hm, 0, 1).reshape(tm, D_MODEL).astype(BF16)
        before = jnp.where(i > 0, hp_ref[ns - 1:ns, :], 0.0)
        after = jnp.where(i < last_i, hn_ref[0:1, :], 0.0)
        pad = jnp.zeros((FFN_EDGE_ROWS - 2, D_MODEL), F32)
        h_scr[tm:, :] = jnp.concatenate([before, after, pad], axis=0).astype(BF16)
        carry_b[...] = jnp.zeros_like(carry_b)

    def up_step(carry_w, carry_r):
        for k in range(2):
            for r in range(ns):
                finish(carry_r, k, r, j - 1)
        for k in range(2):
            cs = slice(k * FFN_CW, (k + 1) * FFN_CW)
            carry_w[k] = _dot(h_scr[...], wg_ref[:, cs])
            carry_w[2 + k] = _dot(h_scr[...], wv_ref[:, cs])

    @pl.when((j < FFN_UP_STEPS) & (j % 2 == 0))
    def _():
        up_step(carry_a, carry_b)

    @pl.when((j < FFN_UP_STEPS) & (j % 2 == 1))
    def _():
        up_step(carry_b, carry_a)

    @pl.when(j == FFN_UP_STEPS)
    def _():
        last = carry_a if (FFN_UP_STEPS - 1) % 2 == 0 else carry_b
        for k in range(2):
            for r in range(ns):
                finish(last, k, r, FFN_UP_STEPS - 1)


def _ffn_down_kernel(a_ref, wd_ref, x_ref, gate_ref, o_ref, *, tm, up_tm):
    ns = FFN_SLABS
    sr = up_tm // ns
    y = _dot(a_ref[:, FFN_TF:], wd_ref[...])
    y = jnp.swapaxes(y.reshape(tm // up_tm, ns, sr, FFN_TN), 1, 2).reshape(tm, FFN_TN)
    o_ref[...] = x_ref[...] + gate_ref[...] * y


def _conv_ffn(x2, h2, w_up_bf, conv_w, conv_b, w_down_bf, gate, tm, down_tm, up_cast=None,
              down_cast=None):
    n = x2.shape[0]
    nj = FFN_UP_STEPS
    hb = tm // FFN_SLABS
    n_halo_blocks = n // FFN_SLABS
    act_cols = D_FF + FFN_TF
    cw3 = conv_w.reshape(3, 2 * FFN_CHUNKS, FFN_CW).transpose(1, 0, 2)
    cw3 = jnp.broadcast_to(cw3[:, :, None, :], (2 * FFN_CHUNKS, 3, FFN_PIECE_ROWS, FFN_CW))
    cb3 = jnp.broadcast_to(conv_b.reshape(2 * FFN_CHUNKS, 1, FFN_CW),
                           (2 * FFN_CHUNKS, FFN_PIECE_ROWS, FFN_CW))
    once = pl.Buffered(1)
    act, *up_cast = _pallas(
        functools.partial(_ffn_up_kernel, tm=tm),
        grid=(n // tm, nj + 1),
        cast=up_cast, step_of=lambda i, j: i * (nj + 1) + j,
        in_specs=[
            pl.BlockSpec((tm, D_MODEL), lambda i, j: (i, 0)),
            pl.BlockSpec((FFN_SLABS, D_MODEL), lambda i, j: (jnp.maximum(i * hb - 1, 0), 0)),
            pl.BlockSpec((FFN_SLABS, D_MODEL), lambda i, j: (jnp.minimum((i + 1) * hb, n_halo_blocks - 1), 0)),
            pl.BlockSpec((None, D_MODEL, FFN_TF), lambda i, j: (jnp.minimum(j, nj - 1), 0, 0)),
            pl.BlockSpec((None, D_MODEL, FFN_TF), lambda i, j: (jnp.minimum(j, nj - 1) + nj, 0, 0)),
            pl.BlockSpec((2 * FFN_CHUNKS, 3, FFN_PIECE_ROWS, FFN_CW), lambda i, j: (0, 0, 0, 0),
                         pipeline_mode=once),
            pl.BlockSpec((2 * FFN_CHUNKS, FFN_PIECE_ROWS, FFN_CW), lambda i, j: (0, 0, 0),
                         pipeline_mode=once),
        ],
        out_specs=[pl.BlockSpec((tm, FFN_TF), lambda i, j: (i, j))],
        out_shape=[jax.ShapeDtypeStruct((n, act_cols), BF16)],
        scratch_shapes=[pltpu.VMEM((tm + FFN_EDGE_ROWS, D_MODEL), BF16),
                        pltpu.VMEM((4, tm + FFN_EDGE_ROWS, FFN_CW), F32),
                        pltpu.VMEM((4, tm + FFN_EDGE_ROWS, FFN_CW), F32)],
        name="ffn_up",
        args=(h2, h2, h2, w_up_bf, w_up_bf, cw3, cb3))
    out, *down_cast = _pallas(
        functools.partial(_ffn_down_kernel, tm=down_tm, up_tm=tm),
        grid=(n // down_tm, FFN_DOWN_STEPS),
        cast=down_cast, step_of=lambda i, j: i * FFN_DOWN_STEPS + j,
        in_specs=[
            pl.BlockSpec((down_tm, act_cols), lambda i, j: (i, 0)),
            pl.BlockSpec((None, D_FF, FFN_TN), lambda i, j: (j, 0, 0)),
            pl.BlockSpec((down_tm, FFN_TN), lambda i, j: (i, j)),
            pl.BlockSpec((1, FFN_TN), lambda i, j: (0, j)),
        ],
        out_specs=[pl.BlockSpec((down_tm, FFN_TN), lambda i, j: (i, j))],
        out_shape=[jax.ShapeDtypeStruct((n, D_MODEL), F32)],
        name="ffn_down",
        args=(act, w_down_bf, x2, gate))
    return out, up_cast, down_cast


def _gelu(x):
    return 0.5 * x * (1.0 + lax.erf(x * (2.0 ** -0.5)))


def _cd_in_kernel(x_ref, g_ref, sh_ref, sc_ref, w_ref, vg_ref, uv_ref, f_ref, *, tm):
    rows_per = tm // ROW_SPLIT
    for part in range(ROW_SPLIT):
        rs = slice(part * rows_per, (part + 1) * rows_per)
        h = _norm_mod(x_ref[rs, :], g_ref[...], sh_ref[...], sc_ref[...]).astype(BF16)
        uv_ref[rs, 0:C_WIDTH] = _gelu(_dot(h, w_ref[:, 0:C_WIDTH])).astype(BF16)
        v = _gelu(_dot(h, w_ref[:, C_WIDTH:2 * C_WIDTH]))
        ms = jnp.mean(v * v, axis=-1, keepdims=True)
        uv_ref[rs, C_WIDTH:] = (v * lax.rsqrt(ms + EPS) * vg_ref[...]).astype(BF16)
        f_ref[rs, :] = _dot(h, w_ref[:, 2 * C_WIDTH:])


def _cd_in_proj(x2, g, shift, scale, w_in_bf, v_norm_g, tm, cast=None):
    n = x2.shape[0]
    row = lambda i: (0, 0)
    return _pallas(
        functools.partial(_cd_in_kernel, tm=tm),
        grid=(n // tm,),
        cast=cast, step_of=lambda i: i,
        in_specs=[
            pl.BlockSpec((tm, D_MODEL), lambda i: (i, 0)),
            pl.BlockSpec((1, D_MODEL), row),
            pl.BlockSpec((1, D_MODEL), row),
            pl.BlockSpec((1, D_MODEL), row),
            pl.BlockSpec((D_MODEL, CD_IN), row, pipeline_mode=pl.Buffered(1)),
            pl.BlockSpec((1, C_WIDTH), row),
        ],
        out_specs=[
            pl.BlockSpec((tm, 2 * C_WIDTH), lambda i: (i, 0)),
            pl.BlockSpec((tm, D_WIDTH), lambda i: (i, 0)),
        ],
        out_shape=[jax.ShapeDtypeStruct((n, 2 * C_WIDTH), BF16),
                   jax.ShapeDtypeStruct((n, D_WIDTH), F32)],
        name="cd_in_proj",
        args=(x2, g, shift, scale, w_in_bf, v_norm_g))


def _spatial_kernel(u_ref, v_ref, ws_ref, bias_ref, o_ref, *, chunks):
    for k in range(chunks):
        rs = slice(k * CHUNK, (k + 1) * CHUNK)
        for g in range(C_GROUPS):
            cs = slice(g * C_GROUP_DIM, (g + 1) * C_GROUP_DIM)
            s = _dot(ws_ref[g], v_ref[rs, cs]) + bias_ref[:, cs]
            o_ref[rs, cs] = (u_ref[rs, cs].astype(F32) * s).astype(BF16)


def _spatial_gate(uv, w_spatial_bf, bias_full, chunks=4):
    n = uv.shape[0]
    rows = chunks * CHUNK
    return pl.pallas_call(
        functools.partial(_spatial_kernel, chunks=chunks),
        grid=(n // rows,),
        in_specs=[
            pl.BlockSpec((rows, C_WIDTH), lambda i: (i, 0)),
            pl.BlockSpec((rows, C_WIDTH), lambda i: (i, 1)),
            pl.BlockSpec((C_GROUPS, CHUNK, CHUNK), lambda i: (0, 0, 0)),
            pl.BlockSpec((CHUNK, C_WIDTH), lambda i: (0, 0)),
        ],
        out_specs=pl.BlockSpec((rows, C_WIDTH), lambda i: (i, 0)),
        out_shape=jax.ShapeDtypeStruct((n, C_WIDTH), BF16),
        compiler_params=_params(("arbitrary",)),
        name="spatial_gate",
    )(uv, uv, w_spatial_bf, bias_full)


FFT_SUB = V7X_SUBLANES_F32
FFT1_CT = 512
FFT2_CT = 512


def _fourier_tables(n):
    a_len, b_len, sub = FFT_A, FFT_B, FFT_SUB
    assert a_len * b_len == n
    ch = np.arange(D_GROUP_DIM)
    ang_c = 2.0 * np.pi * ((ch[:, None] * ch[None, :]) % D_GROUP_DIM) / D_GROUP_DIM
    a = np.arange(a_len)
    f_a = np.exp(-2j * np.pi * ((a[:, None] * a[None, :]) % a_len) / a_len)
    m1 = np.kron(f_a, np.eye(sub))
    b = np.arange(b_len)
    tw = np.exp(-2j * np.pi * ((a[:, None] * b[None, :]) % n) / n)
    f_b = np.exp(-2j * np.pi * ((b[:, None] * b[None, :]) % b_len) / b_len)
    m2 = np.einsum('db,pq->dpqb', f_b, np.eye(sub)).reshape(b_len * sub, sub * b_len)
    norm = 1.0 / np.sqrt(float(n) * D_GROUP_DIM)
    m2 = m2 * norm
    tw3 = np.broadcast_to(tw[:, :, None], (a_len, b_len, V7X_LANES))
    f32 = lambda v: jnp.asarray(np.ascontiguousarray(v), dtype=F32)
    return dict(cos_c=f32(np.cos(ang_c)), sin_c=f32(np.sin(ang_c)),
                m1r=f32(m1.real), m1i=f32(m1.imag), m2r=f32(m2.real), m2i=f32(m2.imag),
                twr=f32(tw3.real), twi=f32(tw3.imag))


def _fft1_kernel(f_ref, cc_ref, sc_ref, m1r_ref, m1i_ref, twr_ref, twi_ref, tr_ref, ti_ref):
    rows = FFT_A * FFT_SUB
    ct = FFT1_CT
    fb = f_ref[...].reshape(rows, ct).astype(BF16)
    xr_parts = []
    xi_parts = []
    for q in range(ct // D_GROUP_DIM):
        blk = fb[:, q * D_GROUP_DIM:(q + 1) * D_GROUP_DIM]
        xr_parts.append(_dot(blk, cc_ref[...]))
        xi_parts.append(-_dot(blk, sc_ref[...]))
    xr = jnp.concatenate(xr_parts, axis=1).astype(BF16)
    xi = jnp.concatenate(xi_parts, axis=1).astype(BF16)
    m1r = m1r_ref[...]
    m1i = m1i_ref[...]
    tr = _dot(m1r, xr) - _dot(m1i, xi)
    ti = _dot(m1r, xi) + _dot(m1i, xr)
    reps = ct // V7X_LANES
    twr = jnp.tile(twr_ref[...].reshape(rows, V7X_LANES), (1, reps))
    twi = jnp.tile(twi_ref[...].reshape(rows, V7X_LANES), (1, reps))
    tr_ref[...] = (tr * twr - ti * twi).reshape(FFT_A, FFT_SUB, ct)
    ti_ref[...] = (tr * twi + ti * twr).reshape(FFT_A, FFT_SUB, ct)


def _fft2_kernel(tr_ref, ti_ref, m2r_ref, m2i_ref, wf_ref, o_ref):
    j = pl.program_id(1)
    tr = tr_ref[...].astype(BF16)
    ti = ti_ref[...].astype(BF16)
    z = _dot(m2r_ref[...], tr) - _dot(m2i_ref[...], ti)
    contrib = _dot(z.astype(BF16), wf_ref[...]).reshape(FFT_B, FFT_SUB, D_WIDTH)

    @pl.when(j == 0)
    def _():
        o_ref[...] = contrib

    @pl.when(j > 0)
    def _():
        o_ref[...] += contrib


def _fourier_mix(f, tabs, w_fourier_bf):
    n = f.shape[0]
    a_len, b_len, sub = FFT_A, FFT_B, FFT_SUB
    f3 = f.reshape(a_len, b_len, D_WIDTH)
    rows1 = a_len * sub
    const2 = lambda i, j: (0, 0)
    tr, ti = pl.pallas_call(
        _fft1_kernel,
        grid=(b_len // sub, D_WIDTH // FFT1_CT),
        in_specs=[
            pl.BlockSpec((a_len, sub, FFT1_CT), lambda i, j: (0, i, j)),
            pl.BlockSpec((D_GROUP_DIM, D_GROUP_DIM), const2),
            pl.BlockSpec((D_GROUP_DIM, D_GROUP_DIM), const2),
            pl.BlockSpec((rows1, rows1), const2),
            pl.BlockSpec((rows1, rows1), const2),
            pl.BlockSpec((a_len, sub, V7X_LANES), lambda i, j: (0, i, 0)),
            pl.BlockSpec((a_len, sub, V7X_LANES), lambda i, j: (0, i, 0)),
        ],
        out_specs=[pl.BlockSpec((a_len, sub, FFT1_CT), lambda i, j: (0, i, j))] * 2,
        out_shape=[jax.ShapeDtypeStruct((a_len, b_len, D_WIDTH), F32)] * 2,
        compiler_params=_params(("arbitrary", "arbitrary")),
        name="fourier_stage1",
    )(f3, tabs['cos_c'].astype(BF16), tabs['sin_c'].astype(BF16),
      tabs['m1r'].astype(BF16), tabs['m1i'].astype(BF16), tabs['twr'], tabs['twi'])

    rows2 = sub * b_len
    tr2 = tr.reshape(n, D_WIDTH)
    ti2 = ti.reshape(n, D_WIDTH)
    out = pl.pallas_call(
        _fft2_kernel,
        grid=(a_len // sub, D_WIDTH // FFT2_CT),
        in_specs=[
            pl.BlockSpec((rows2, FFT2_CT), lambda i, j: (i, j)),
            pl.BlockSpec((rows2, FFT2_CT), lambda i, j: (i, j)),
            pl.BlockSpec((rows2, rows2), const2),
            pl.BlockSpec((rows2, rows2), const2),
            pl.BlockSpec((FFT2_CT, D_WIDTH), lambda i, j: (j, 0)),
        ],
        out_specs=pl.BlockSpec((b_len, sub, D_WIDTH), lambda i, j: (0, i, 0)),
        out_shape=jax.ShapeDtypeStruct((b_len, a_len, D_WIDTH), F32),
        compiler_params=_params(("arbitrary", "arbitrary")),
        name="fourier_stage2",
    )(tr2, ti2, tabs['m2r'].astype(BF16), tabs['m2i'].astype(BF16), w_fourier_bf)
    return out.reshape(n, D_WIDTH)


def _rope_tables(n):
    rows = n // GRID_W
    row = np.repeat(np.arange(rows, dtype=np.float64), GRID_W)
    col = np.tile(np.arange(GRID_W, dtype=np.float64), rows)
    inv = ROPE_THETA ** (-np.arange(0, AXIS_DIM, 2, dtype=np.float64) / AXIS_DIM)
    ang_r = row[:, None] * inv[None, :]
    ang_c = col[:, None] * inv[None, :]
    cos = np.concatenate([np.cos(ang_r)] * 2 + [np.cos(ang_c)] * 2, axis=-1)
    sin = np.concatenate([-np.sin(ang_r), np.sin(ang_r), -np.sin(ang_c), np.sin(ang_c)], axis=-1)
    return jnp.asarray(cos, dtype=F32), jnp.asarray(sin, dtype=F32)


def kernel(x, c, ctx, c_ctx, w_mod, b_mod, norm1_g, norm2_g, ab_w_in, a_q_norm_g, a_k_norm_g, a_sink,
           b_w_pool, b_pool_scale, ab_w_out, cd_w_in, c_v_norm_g, c_w_spatial, c_b_spatial, d_w_fourier,
           cd_w_out, f_w_up, f_conv_w, f_conv_b, f_w_down):
    batch, n, _ = x.shape
    ctx_len = ctx.shape[1]
    assert batch == 1 and DEPTH == 2
    x2 = x.reshape(n, D_MODEL)
    ctx2 = ctx.reshape(ctx_len, D_MODEL)

    mod = _mod_vectors(c, c_ctx, w_mod, b_mod)

    def split6(v):
        return [v[:, k * D_MODEL:(k + 1) * D_MODEL] for k in range(6)]

    row1 = lambda v: v.reshape(1, -1)
    n_tiles = n // PROJ_TM
    n_blocks = n // BLOCK
    up_steps = (n // FFN_UP_TM) * (FFN_UP_STEPS + 1)
    down_steps = (n // FFN_DOWN_TM) * FFN_DOWN_STEPS
    cast_down0 = _CastJob(f_w_down, 0, D_FF // n_tiles, n_tiles, col_block=FFN_TN)
    cast_up0 = _CastJob(f_w_up, 0, D_MODEL // n_blocks, n_blocks, col_block=FFN_TF)
    cast_up1 = _CastJob(f_w_up, 1, D_MODEL // n_blocks, up_steps, col_block=FFN_TF)
    cast_down1 = _CastJob(f_w_down, 1, D_FF // down_steps, down_steps, col_block=FFN_TN)

    ml = split6(mod[0, 0:1])
    mc = split6(mod[0, 1:2])
    g1 = row1(norm1_g[0])
    w_in = ab_w_in[0].astype(BF16)
    qn = row1(a_q_norm_g[0])
    kn = row1(a_k_norm_g[0])
    cos, sin = _rope_tables(n)
    qkv, z, w_down0 = _ab_in_proj(x2, g1, ml[0], ml[1], w_in, cos, sin, qn, kn, tm=PROJ_TM,
                                  cast=cast_down0)
    ones = jnp.ones((ctx_len, HEAD_DIM), F32)
    zeros = jnp.zeros((ctx_len, HEAD_DIM), F32)
    qkv_ctx, _ = _ab_in_proj(ctx2, g1, mc[0], mc[1], w_in, ones, zeros, qn, kn, tm=ctx_len)
    attn, w_up0 = _window_attention(qkv, qkv_ctx, a_sink[0], cast=cast_up0)
    pool_tiles = n // POOL_TM
    pooled, w_out0 = _pool_mix(z, b_w_pool[0].astype(BF16), row1(b_pool_scale[0]), tm=POOL_TM,
                               cast=_CastJob(ab_w_out, 0, D_MODEL // pool_tiles, pool_tiles))
    x2, h2, cd_w_in_bf = _out_proj(attn, pooled, w_out0, x2, ml[2], row1(norm2_g[0]), ml[3], ml[4],
                                   tm=PROJ_TM, cast=_CastJob(cd_w_in, 0, D_MODEL // n_tiles, n_tiles))
    x2, (w_up1,), (w_down1,) = _conv_ffn(x2, h2, w_up0, f_conv_w[0], f_conv_b[0], w_down0, ml[5],
                                         tm=FFN_UP_TM, down_tm=FFN_DOWN_TM,
                                         up_cast=cast_up1, down_cast=cast_down1)

    ml = split6(mod[1, 0:1])
    uv, f, w_out1 = _cd_in_proj(x2, row1(norm1_g[1]), ml[0], ml[1], cd_w_in_bf, row1(c_v_norm_g[0]),
                                tm=PROJ_TM, cast=_CastJob(cd_w_out, 0, D_MODEL // n_tiles, n_tiles))
    bias_full = jnp.repeat(c_b_spatial[0].T, C_GROUP_DIM, axis=1)
    c_out = _spatial_gate(uv, c_w_spatial[0].astype(BF16), bias_full)
    d_out = _fourier_mix(f, _fourier_tables(n), d_w_fourier[0].astype(BF16))
    x2, h2 = _out_proj(c_out, d_out, w_out1, x2, ml[2], row1(norm2_g[1]), ml[3], ml[4], tm=PROJ_TM)
    x2, _, _ = _conv_ffn(x2, h2, w_up1, f_conv_w[1], f_conv_b[1], w_down1, ml[5],
                         tm=FFN_UP_TM, down_tm=FFN_DOWN_TM)
    return x2.reshape(batch, n, D_MODEL)
```

```python
import functools

import numpy as np
import jax
import jax.numpy as jnp
from jax import lax
from jax.experimental import pallas as pl
from jax.experimental.pallas import tpu as pltpu

F32 = jnp.float32
BF16 = jnp.bfloat16

D_MODEL = 2048
DEPTH = 2
GRID_W = 64
HEAD_DIM = 128
A_Q_HEADS = 8
A_KV_HEADS = 2
A_GROUP = A_Q_HEADS // A_KV_HEADS
A_Q_DIM = A_Q_HEADS * HEAD_DIM
A_KV_DIM = A_KV_HEADS * HEAD_DIM
A_QKV_DIM = A_Q_DIM + 2 * A_KV_DIM
WINDOW = 128
BLOCK = 128
ROPE_THETA = 10000.0
AXIS_DIM = HEAD_DIM // 2
ATTN_SCALE = HEAD_DIM ** -0.5
NEG_INF = -1e30
B_GROUPS = 4
B_WIDTH = 1024
B_GROUP_DIM = B_WIDTH // B_GROUPS
POOL_WINDOWS = (2, 4, 8, 16)
AB_IN = A_QKV_DIM + B_WIDTH
C_WIDTH = 1024
C_GROUPS = 4
C_GROUP_DIM = C_WIDTH // C_GROUPS
CHUNK = 128
D_WIDTH = 1024
D_GROUPS = 8
D_GROUP_DIM = D_WIDTH // D_GROUPS
CD_IN = 2 * C_WIDTH + D_WIDTH
D_FF = 5632
EPS = 1e-6

V7X_SUBLANES_F32 = 8
V7X_SUBLANES_BF16 = 16
V7X_LANES = 128
V7X_VMEM_BYTES = 64 * 1024 * 1024
VMEM_LIMIT = 56 * 1024 * 1024

PROJ_TM = 512
POOL_TM = 256
FFN_UP_TM = 1024
FFN_DOWN_TM = 1024

FFT_A = 64
FFT_B = 128


def _params(sem):
    return pltpu.CompilerParams(dimension_semantics=sem, vmem_limit_bytes=VMEM_LIMIT)


def _dot(a, b):
    return jnp.dot(a, b, preferred_element_type=F32)


class _CastJob:
    def __init__(self, src, layer, rows, n_steps, col_block=None):
        _, total_rows, self.cols = src.shape
        assert total_rows % rows == 0 and total_rows // rows <= n_steps
        self.src, self.layer, self.rows, self.col_block = src, layer, rows, col_block
        self.last = total_rows // rows - 1
        if col_block is None:
            self.out_shape = jax.ShapeDtypeStruct((total_rows, self.cols), BF16)
        else:
            assert self.cols % col_block == 0
            self.out_shape = jax.ShapeDtypeStruct((self.cols // col_block, total_rows, col_block), BF16)

    def specs(self, step_of):
        blk = lambda *ids: jnp.minimum(step_of(*ids), self.last)
        src = pl.BlockSpec((None, self.rows, self.cols), lambda *ids: (self.layer, blk(*ids), 0))
        if self.col_block is None:
            return src, pl.BlockSpec((self.rows, self.cols), lambda *ids: (blk(*ids), 0))
        return src, pl.BlockSpec((self.cols // self.col_block, self.rows, self.col_block),
                                 lambda *ids: (0, blk(*ids), 0))

    def run(self, src_ref, dst_ref):
        if self.col_block is None:
            dst_ref[...] = src_ref[...].astype(BF16)
        else:
            for b in range(self.cols // self.col_block):
                dst_ref[b] = src_ref[:, b * self.col_block:(b + 1) * self.col_block].astype(BF16)


def _host_cast(kernel_fn, cast, n_in, n_out):
    def body(*refs):
        cast.run(refs[n_in], refs[n_in + 1 + n_out])
        kernel_fn(*refs[:n_in], *refs[n_in + 1:n_in + 1 + n_out], *refs[n_in + 2 + n_out:])
    return body


def _pallas(kernel_fn, *, grid, in_specs, out_specs, out_shape, args, name, scratch_shapes=(),
            cast=None, step_of=None):
    out_specs, out_shape = list(out_specs), list(out_shape)
    if cast is not None:
        src_spec, dst_spec = cast.specs(step_of)
        kernel_fn = _host_cast(kernel_fn, cast, len(in_specs), len(out_specs))
        in_specs = [*in_specs, src_spec]
        out_specs.append(dst_spec)
        out_shape.append(cast.out_shape)
        args = (*args, cast.src)
    return pl.pallas_call(
        kernel_fn, grid=grid, in_specs=list(in_specs), out_specs=out_specs, out_shape=out_shape,
        scratch_shapes=list(scratch_shapes), compiler_params=_params(("arbitrary",) * len(grid)),
        name=name)(*args)


def _norm_mod(x, g, shift, scale):
    ms = jnp.mean(x * x, axis=-1, keepdims=True)
    y = x * lax.rsqrt(ms + EPS) * g
    return y * (1.0 + scale) + shift


MOD_TK = 128
MOD_STREAMS = 2


def _mod_kernel(cv_ref, w_ref, b_ref, o_ref, acc_scr):
    k = pl.program_id(1)
    sub = V7X_SUBLANES_F32
    groups = MOD_TK // sub
    n_out = 6 * D_MODEL
    lane_tiles = n_out // V7X_LANES

    @pl.when(k == 0)
    def _():
        acc_scr[...] = jnp.zeros_like(acc_scr)

    w = w_ref[0].reshape(groups, sub, n_out)
    for s in range(MOD_STREAMS):
        a = cv_ref[s]
        a = (a * jax.nn.sigmoid(a)).reshape(groups, sub, V7X_LANES)
        a = jnp.concatenate([a] * lane_tiles, axis=-1)
        acc_scr[s] += jnp.sum(w * a, axis=0)

    @pl.when(k == pl.num_programs(1) - 1)
    def _():
        rows = [jnp.sum(acc_scr[s], axis=0, keepdims=True) for s in range(MOD_STREAMS)]
        rows.append(jnp.zeros((sub - MOD_STREAMS, n_out), F32))
        o_ref[0] = jnp.concatenate(rows, axis=0) + b_ref[0]


def _mod_vectors(c, c_ctx, w_mod, b_mod):
    n_out = 6 * D_MODEL
    cv = jnp.stack([c.reshape(D_MODEL), c_ctx.reshape(D_MODEL)])
    cv = jnp.broadcast_to(cv[:, :, None], (MOD_STREAMS, D_MODEL, V7X_LANES))
    b3 = b_mod.reshape(DEPTH, 1, n_out)
    return pl.pallas_call(
        _mod_kernel,
        grid=(DEPTH, D_MODEL // MOD_TK),
        in_specs=[
            pl.BlockSpec((MOD_STREAMS, MOD_TK, V7X_LANES), lambda l, k: (0, k, 0)),
            pl.BlockSpec((1, MOD_TK, n_out), lambda l, k: (l, k, 0)),
            pl.BlockSpec((1, 1, n_out), lambda l, k: (l, 0, 0)),
        ],
        out_specs=pl.BlockSpec((1, V7X_SUBLANES_F32, n_out), lambda l, k: (l, 0, 0)),
        out_shape=jax.ShapeDtypeStruct((DEPTH, V7X_SUBLANES_F32, n_out), F32),
        scratch_shapes=[pltpu.VMEM((MOD_STREAMS, V7X_SUBLANES_F32, n_out), F32)],
        compiler_params=_params(("arbitrary", "arbitrary")),
        name="mod_vectors",
    )(cv, w_mod, b3)


AB_TN = 512
ROW_SPLIT = 2


def _rope(t, cos, sin_signed):
    lane = lax.broadcasted_iota(jnp.int32, t.shape, 1)
    first = (lane % AXIS_DIM) < (AXIS_DIM // 2)
    partner = jnp.where(first,
                        pltpu.roll(t, HEAD_DIM - AXIS_DIM // 2, 1),
                        pltpu.roll(t, AXIS_DIM // 2, 1))
    return t * cos + partner * sin_signed


def _head_norm_rope(t, g, cos, sin_signed):
    ms = jnp.mean(t * t, axis=-1, keepdims=True)
    return _rope(t * lax.rsqrt(ms + EPS) * g, cos, sin_signed)


def _ab_in_kernel(x_ref, g_ref, sh_ref, sc_ref, w_ref, cos_ref, sin_ref, qn_ref, kn_ref,
                  qkv_ref, z_ref, *, tm):
    rows_per = tm // ROW_SPLIT
    n_q_tiles = A_Q_DIM // AB_TN
    for part in range(ROW_SPLIT):
        rs = slice(part * rows_per, (part + 1) * rows_per)
        h = _norm_mod(x_ref[rs, :], g_ref[...], sh_ref[...], sc_ref[...]).astype(BF16)
        cos = cos_ref[rs, :]
        sin = sin_ref[rs, :]
        for t in range(AB_IN // AB_TN):
            p = _dot(h, w_ref[:, t * AB_TN:(t + 1) * AB_TN])
            if t < n_q_tiles:
                for hh in range(AB_TN // HEAD_DIM):
                    c0 = t * AB_TN + hh * HEAD_DIM
                    qkv_ref[rs, c0:c0 + HEAD_DIM] = _head_norm_rope(
                        p[:, hh * HEAD_DIM:(hh + 1) * HEAD_DIM], qn_ref[...], cos, sin).astype(BF16)
            elif t == n_q_tiles:
                for hh in range(A_KV_HEADS):
                    c0 = A_Q_DIM + hh * HEAD_DIM
                    qkv_ref[rs, c0:c0 + HEAD_DIM] = _head_norm_rope(
                        p[:, hh * HEAD_DIM:(hh + 1) * HEAD_DIM], kn_ref[...], cos, sin).astype(BF16)
                qkv_ref[rs, A_Q_DIM + A_KV_DIM:] = p[:, A_KV_DIM:].astype(BF16)
            else:
                c0 = (t - n_q_tiles - 1) * AB_TN
                z_ref[rs, c0:c0 + AB_TN] = p


def _ab_in_proj(x2, g, shift, scale, w_in_bf, cos, sin, qn, kn, tm, cast=None):
    n = x2.shape[0]
    assert A_Q_DIM % AB_TN == 0 and 2 * A_KV_DIM == AB_TN and B_WIDTH % AB_TN == 0
    row = lambda i: (0, 0)
    return _pallas(
        functools.partial(_ab_in_kernel, tm=tm),
        grid=(n // tm,),
        cast=cast, step_of=lambda i: i,
        in_specs=[
            pl.BlockSpec((tm, D_MODEL), lambda i: (i, 0)),
            pl.BlockSpec((1, D_MODEL), row),
            pl.BlockSpec((1, D_MODEL), row),
            pl.BlockSpec((1, D_MODEL), row),
            pl.BlockSpec((D_MODEL, AB_IN), row, pipeline_mode=pl.Buffered(1)),
            pl.BlockSpec((tm, HEAD_DIM), lambda i: (i, 0)),
            pl.BlockSpec((tm, HEAD_DIM), lambda i: (i, 0)),
            pl.BlockSpec((1, HEAD_DIM), row),
            pl.BlockSpec((1, HEAD_DIM), row),
        ],
        out_specs=[
            pl.BlockSpec((tm, A_QKV_DIM), lambda i: (i, 0)),
            pl.BlockSpec((tm, B_WIDTH), lambda i: (i, 0)),
        ],
        out_shape=[jax.ShapeDtypeStruct((n, A_QKV_DIM), BF16),
                   jax.ShapeDtypeStruct((n, B_WIDTH), F32)],
        name="ab_in_proj",
        args=(x2, g, shift, scale, w_in_bf, cos, sin, qn, kn))


def _attn_kernel(sink_ref, bias_ref, q_ref, kp_ref, kc_ref, kn_ref, vp_ref, vc_ref, vn_ref,
                 kx_ref, vx_ref, o_ref):
    rows = A_GROUP * BLOCK
    bias = bias_ref[...]
    r1 = lax.broadcasted_iota(jnp.int32, (rows, 1), 0) // BLOCK
    for hk in range(A_KV_HEADS):
        hs = slice(hk * HEAD_DIM, (hk + 1) * HEAD_DIM)
        kcat = jnp.concatenate([kp_ref[:, hs], kc_ref[:, hs], kn_ref[:, hs], kx_ref[:, hs]], axis=0)
        vcat = jnp.concatenate([vp_ref[:, hs], vc_ref[:, hs], vn_ref[:, hs], vx_ref[:, hs]], axis=0)
        q0 = hk * A_GROUP * HEAD_DIM
        q4 = jnp.concatenate(
            [q_ref[:, q0 + g * HEAD_DIM:q0 + (g + 1) * HEAD_DIM] for g in range(A_GROUP)], axis=0)
        s = lax.dot_general(q4, kcat, (((1,), (1,)), ((), ())), preferred_element_type=F32)
        s = s * ATTN_SCALE + bias
        sink = jnp.zeros((rows, 1), F32)
        for g in range(A_GROUP):
            sink = jnp.where(r1 == g, sink_ref[hk * A_GROUP + g], sink)
        m = jnp.maximum(jnp.max(s, axis=-1, keepdims=True), sink)
        e = jnp.exp(s - m)
        den = jnp.sum(e, axis=-1, keepdims=True) + jnp.exp(sink - m)
        o4 = _dot(e.astype(BF16), vcat) / den
        for g in range(A_GROUP):
            o_ref[:, q0 + g * HEAD_DIM:q0 + (g + 1) * HEAD_DIM] = (
                o4[g * BLOCK:(g + 1) * BLOCK].astype(BF16))


def _attn_bias(ctx_len):
    rows = A_GROUP * BLOCK
    cols = 3 * BLOCK + ctx_len
    r = np.arange(rows)[:, None] % BLOCK
    c = np.arange(cols)[None, :]
    band = (np.abs(BLOCK + r - c) <= WINDOW) | (c >= 3 * BLOCK)
    first = band & (c >= BLOCK)
    last = band & ((c < 2 * BLOCK) | (c >= 3 * BLOCK))
    masks = np.stack([first, band, last])
    return jnp.asarray(np.where(masks, 0.0, NEG_INF), dtype=F32)


def _window_attention(qkv, qkv_ctx, sink, cast=None):
    n = qkv.shape[0]
    ctx_len = qkv_ctx.shape[0]
    n_blocks = n // BLOCK
    assert n_blocks >= 2
    k_col = A_Q_DIM // A_KV_DIM
    v_col = k_col + 1
    prev = lambda i: jnp.maximum(i - 1, 0)
    nxt = lambda i: jnp.minimum(i + 1, n_blocks - 1)
    which = lambda i: jnp.where(i == 0, 0, jnp.where(i == n_blocks - 1, 2, 1))
    blk = (BLOCK, A_KV_DIM)
    rows = A_GROUP * BLOCK
    cols = 3 * BLOCK + ctx_len
    return _pallas(
        _attn_kernel,
        grid=(n_blocks,),
        cast=cast, step_of=lambda i: i,
        in_specs=[
            pl.BlockSpec(memory_space=pltpu.SMEM),
            pl.BlockSpec((None, rows, cols), lambda i: (which(i), 0, 0)),
            pl.BlockSpec((BLOCK, A_Q_DIM), lambda i: (i, 0)),
            pl.BlockSpec(blk, lambda i: (prev(i), k_col)),
            pl.BlockSpec(blk, lambda i: (i, k_col)),
            pl.BlockSpec(blk, lambda i: (nxt(i), k_col)),
            pl.BlockSpec(blk, lambda i: (prev(i), v_col)),
            pl.BlockSpec(blk, lambda i: (i, v_col)),
            pl.BlockSpec(blk, lambda i: (nxt(i), v_col)),
            pl.BlockSpec((ctx_len, A_KV_DIM), lambda i: (0, k_col)),
            pl.BlockSpec((ctx_len, A_KV_DIM), lambda i: (0, v_col)),
        ],
        out_specs=[pl.BlockSpec((BLOCK, A_Q_DIM), lambda i: (i, 0))],
        out_shape=[jax.ShapeDtypeStruct((n, A_Q_DIM), BF16)],
        name="window_attention",
        args=(sink, _attn_bias(ctx_len), qkv, qkv, qkv, qkv, qkv, qkv, qkv, qkv_ctx, qkv_ctx))


POOL_HALO = 8
assert max(POOL_WINDOWS) // 2 <= POOL_HALO


def _pool_kernel(zm_ref, zp_ref, zn_ref, w_ref, ps_ref, o_ref, z_scr, *, n_rows, tm):
    i = pl.program_id(0)
    last = pl.num_programs(0) - 1
    pad = POOL_HALO
    span_rows = tm + 2 * pad
    z_scr[0:pad, :] = jnp.zeros((pad, B_WIDTH), F32)
    z_scr[pad:2 * pad, :] = jnp.where(i > 0, zp_ref[...], 0.0)
    z_scr[2 * pad:2 * pad + tm, :] = zm_ref[...]
    z_scr[2 * pad + tm:3 * pad + tm, :] = jnp.where(i < last, zn_ref[...], 0.0)
    z_scr[3 * pad + tm:, :] = jnp.zeros((pad, B_WIDTH), F32)
    t = i * tm + lax.broadcasted_iota(jnp.int32, (tm, B_GROUP_DIM), 0)
    for g in range(B_GROUPS):
        window = POOL_WINDOWS[g]
        half = window // 2
        cs = slice(g * B_GROUP_DIM, (g + 1) * B_GROUP_DIM)
        s = z_scr[pl.ds(2 * pad - half, span_rows), cs]
        width = 1
        while width < window:
            s = s + pltpu.roll(s, span_rows - width, 0)
            width *= 2
        acc = s[0:tm]
        cnt = (jnp.minimum(t + half, n_rows) - jnp.maximum(t - half, 0)).astype(F32)
        d = (acc / cnt - zm_ref[:, cs]).astype(BF16)
        y = _dot(d, w_ref[g]) * ps_ref[:, cs]
        o_ref[:, cs] = y.astype(BF16)


def _pool_mix(z, w_pool_bf, pool_scale, tm, cast=None):
    n = z.shape[0]
    hb = tm // POOL_HALO
    n_halo_blocks = n // POOL_HALO
    return _pallas(
        functools.partial(_pool_kernel, n_rows=n, tm=tm),
        grid=(n // tm,),
        cast=cast, step_of=lambda i: i,
        in_specs=[
            pl.BlockSpec((tm, B_WIDTH), lambda i: (i, 0)),
            pl.BlockSpec((POOL_HALO, B_WIDTH), lambda i: (jnp.maximum(i * hb - 1, 0), 0)),
            pl.BlockSpec((POOL_HALO, B_WIDTH), lambda i: (jnp.minimum((i + 1) * hb, n_halo_blocks - 1), 0)),
            pl.BlockSpec((B_GROUPS, B_GROUP_DIM, B_GROUP_DIM), lambda i: (0, 0, 0)),
            pl.BlockSpec((1, B_WIDTH), lambda i: (0, 0)),
        ],
        out_specs=[pl.BlockSpec((tm, B_WIDTH), lambda i: (i, 0))],
        out_shape=[jax.ShapeDtypeStruct((n, B_WIDTH), BF16)],
        scratch_shapes=[pltpu.VMEM((tm + 4 * POOL_HALO, B_WIDTH), F32)],
        name="pool_mix",
        args=(z, z, z, w_pool_bf, pool_scale))


OUT_TN = 512


def _out_proj_kernel(a1_ref, a2_ref, w_ref, x_ref, gate_ref, g_ref, sh_ref, sc_ref, o_ref, h_ref, *, tm):
    rows_per = tm // ROW_SPLIT
    for part in range(ROW_SPLIT):
        rs = slice(part * rows_per, (part + 1) * rows_per)
        a = jnp.concatenate([a1_ref[rs, :].astype(BF16), a2_ref[rs, :].astype(BF16)], axis=1)
        for t in range(D_MODEL // OUT_TN):
            cs = slice(t * OUT_TN, (t + 1) * OUT_TN)
            o_ref[rs, cs] = x_ref[rs, cs] + gate_ref[:, cs] * _dot(a, w_ref[:, cs])
        h_ref[rs, :] = _norm_mod(o_ref[rs, :], g_ref[...], sh_ref[...], sc_ref[...])


def _out_proj(a1, a2, w_out_bf, x2, gate, g2, shift2, scale2, tm, cast=None):
    n = x2.shape[0]
    k1 = a1.shape[1]
    k2 = a2.shape[1]
    assert k1 + k2 == w_out_bf.shape[0] and k1 % V7X_LANES == 0
    row = lambda i: (0, 0)
    return _pallas(
        functools.partial(_out_proj_kernel, tm=tm),
        grid=(n // tm,),
        cast=cast, step_of=lambda i: i,
        in_specs=[
            pl.BlockSpec((tm, k1), lambda i: (i, 0)),
            pl.BlockSpec((tm, k2), lambda i: (i, 0)),
            pl.BlockSpec((k1 + k2, D_MODEL), row, pipeline_mode=pl.Buffered(1)),
            pl.BlockSpec((tm, D_MODEL), lambda i: (i, 0)),
            pl.BlockSpec((1, D_MODEL), row),
            pl.BlockSpec((1, D_MODEL), row),
            pl.BlockSpec((1, D_MODEL), row),
            pl.BlockSpec((1, D_MODEL), row),
        ],
        out_specs=[pl.BlockSpec((tm, D_MODEL), lambda i: (i, 0)),
                   pl.BlockSpec((tm, D_MODEL), lambda i: (i, 0))],
        out_shape=[jax.ShapeDtypeStruct((n, D_MODEL), F32),
                   jax.ShapeDtypeStruct((n, D_MODEL), F32)],
        name="out_proj",
        args=(a1, a2, w_out_bf, x2, gate, g2, shift2, scale2))


FFN_SLABS = V7X_SUBLANES_F32
FFN_EDGE_ROWS = V7X_SUBLANES_BF16
FFN_PIECE_ROWS = 16
FFN_ELEMENTWISE_DTYPE = BF16
FFN_CW = 256
FFN_TF = 2 * FFN_CW
FFN_TN = 512
FFN_CHUNKS = D_FF // FFN_CW
FFN_UP_STEPS = D_FF // FFN_TF
FFN_DOWN_STEPS = D_MODEL // FFN_TN


def _ffn_up_kernel(hm_ref, hp_ref, hn_ref, wg_ref, wv_ref, cw_ref, cb_ref, o_ref,
                   h_scr, carry_a, carry_b, edge_a, edge_b, *, tm):
    i = pl.program_id(0)
    j = pl.program_id(1)
    last_i = pl.num_programs(0) - 1
    nc = FFN_CHUNKS
    ns = FFN_SLABS
    sr = tm // ns

    def finish(carries, k, r, step):
        carry, edge = carries
        c = jnp.maximum(2 * step + k, 0)
        dt = FFN_ELEMENTWISE_DTYPE

        def moved_rows(idx, s, q0, n):
            base = 0 if s == 0 else sr
            if q0 < 0:
                rows = jnp.concatenate([edge[idx, 2 * sr:2 * sr + 1, :],
                                        edge[idx, pl.ds(base, n - 1), :]], axis=0)
            elif q0 + n > sr:
                rows = jnp.concatenate([edge[idx, pl.ds(base + q0, n - 1), :],
                                        edge[idx, 2 * sr + 1:2 * sr + 2, :]], axis=0)
            else:
                rows = edge[idx, pl.ds(base + q0, n), :]
            return rows.astype(dt)

        def conv(idx, kk, q0):
            n = FFN_PIECE_ROWS
            slab = lambda s: carry[idx, pl.ds(s * sr + q0, n), :]
            prev = slab(r - 1) if r > 0 else moved_rows(idx, ns - 1, q0 - 1, n)
            nxt = slab(r + 1) if r < ns - 1 else moved_rows(idx, 0, q0 + 1, n)
            return prev * cw_ref[kk, 0] + slab(r) * cw_ref[kk, 1] + nxt * cw_ref[kk, 2] + cb_ref[kk]

        for q0 in range(0, sr, FFN_PIECE_ROWS):
            gg = conv(k, c, q0)
            vv = conv(2 + k, c + nc, q0)
            o_ref[pl.ds(r * sr + q0, FFN_PIECE_ROWS), k * FFN_CW:(k + 1) * FFN_CW] = (
                gg * jax.nn.sigmoid(gg) * vv).astype(BF16)

    @pl.when(j == 0)
    def _():
        hm = hm_ref[...].reshape(sr, ns, D_MODEL)
        h_scr[0:tm, :] = jnp.swapaxes(hm, 0, 1).reshape(tm, D_MODEL).astype(BF16)
        before = jnp.where(i > 0, hp_ref[ns - 1:ns, :], 0.0)
        after = jnp.where(i < last_i, hn_ref[0:1, :], 0.0)
        pad = jnp.zeros((FFN_EDGE_ROWS - 2, D_MODEL), F32)
        h_scr[tm:, :] = jnp.concatenate([before, after, pad], axis=0).astype(BF16)
        carry_b[...] = jnp.zeros_like(carry_b)
        edge_b[...] = jnp.zeros_like(edge_b)

    def keep(carries, idx, u):
        carry, edge = carries
        carry[idx] = u.astype(FFN_ELEMENTWISE_DTYPE)
        edge[idx, 0:sr, :] = u[0:sr]
        edge[idx, sr:2 * sr, :] = u[(ns - 1) * sr:ns * sr]
        edge[idx, 2 * sr:, :] = u[tm:]

    def up_step(write, read):
        for k in range(2):
            for r in range(ns):
                finish(read, k, r, j - 1)
        for k in range(2):
            cs = slice(k * FFN_CW, (k + 1) * FFN_CW)
            keep(write, k, _dot(h_scr[...], wg_ref[:, cs]))
            keep(write, 2 + k, _dot(h_scr[...], wv_ref[:, cs]))

    buf_a = (carry_a, edge_a)
    buf_b = (carry_b, edge_b)

    @pl.when((j < FFN_UP_STEPS) & (j % 2 == 0))
    def _():
        up_step(buf_a, buf_b)

    @pl.when((j < FFN_UP_STEPS) & (j % 2 == 1))
    def _():
        up_step(buf_b, buf_a)

    @pl.when(j == FFN_UP_STEPS)
    def _():
        last = buf_a if (FFN_UP_STEPS - 1) % 2 == 0 else buf_b
        for k in range(2):
            for r in range(ns):
                finish(last, k, r, FFN_UP_STEPS - 1)


def _ffn_down_kernel(a_ref, wd_ref, x_ref, gate_ref, o_ref, *, tm, up_tm):
    ns = FFN_SLABS
    sr = up_tm // ns
    y = _dot(a_ref[:, FFN_TF:], wd_ref[...])
    y = jnp.swapaxes(y.reshape(tm // up_tm, ns, sr, FFN_TN), 1, 2).reshape(tm, FFN_TN)
    o_ref[...] = x_ref[...] + gate_ref[...] * y


def _conv_ffn(x2, h2, w_up_bf, conv_w, conv_b, w_down_bf, gate, tm, down_tm, up_cast=None,
              down_cast=None):
    n = x2.shape[0]
    nj = FFN_UP_STEPS
    hb = tm // FFN_SLABS
    n_halo_blocks = n // FFN_SLABS
    act_cols = D_FF + FFN_TF
    cw3 = conv_w.reshape(3, 2 * FFN_CHUNKS, FFN_CW).transpose(1, 0, 2)
    cw3 = jnp.broadcast_to(cw3[:, :, None, :], (2 * FFN_CHUNKS, 3, FFN_PIECE_ROWS, FFN_CW))
    cb3 = jnp.broadcast_to(conv_b.reshape(2 * FFN_CHUNKS, 1, FFN_CW),
                           (2 * FFN_CHUNKS, FFN_PIECE_ROWS, FFN_CW))
    cw3 = cw3.astype(FFN_ELEMENTWISE_DTYPE)
    cb3 = cb3.astype(FFN_ELEMENTWISE_DTYPE)
    once = pl.Buffered(1)
    act, *up_cast = _pallas(
        functools.partial(_ffn_up_kernel, tm=tm),
        grid=(n // tm, nj + 1),
        cast=up_cast, step_of=lambda i, j: i * (nj + 1) + j,
        in_specs=[
            pl.BlockSpec((tm, D_MODEL), lambda i, j: (i, 0)),
            pl.BlockSpec((FFN_SLABS, D_MODEL), lambda i, j: (jnp.maximum(i * hb - 1, 0), 0)),
            pl.BlockSpec((FFN_SLABS, D_MODEL), lambda i, j: (jnp.minimum((i + 1) * hb, n_halo_blocks - 1), 0)),
            pl.BlockSpec((None, D_MODEL, FFN_TF), lambda i, j: (jnp.minimum(j, nj - 1), 0, 0)),
            pl.BlockSpec((None, D_MODEL, FFN_TF), lambda i, j: (jnp.minimum(j, nj - 1) + nj, 0, 0)),
            pl.BlockSpec((2 * FFN_CHUNKS, 3, FFN_PIECE_ROWS, FFN_CW), lambda i, j: (0, 0, 0, 0),
                         pipeline_mode=once),
            pl.BlockSpec((2 * FFN_CHUNKS, FFN_PIECE_ROWS, FFN_CW), lambda i, j: (0, 0, 0),
                         pipeline_mode=once),
        ],
        out_specs=[pl.BlockSpec((tm, FFN_TF), lambda i, j: (i, j))],
        out_shape=[jax.ShapeDtypeStruct((n, act_cols), BF16)],
        scratch_shapes=[pltpu.VMEM((tm + FFN_EDGE_ROWS, D_MODEL), BF16),
                        pltpu.VMEM((4, tm + FFN_EDGE_ROWS, FFN_CW), FFN_ELEMENTWISE_DTYPE),
                        pltpu.VMEM((4, tm + FFN_EDGE_ROWS, FFN_CW), FFN_ELEMENTWISE_DTYPE),
                        pltpu.VMEM((4, 2 * (tm // FFN_SLABS) + FFN_EDGE_ROWS, FFN_CW), F32),
                        pltpu.VMEM((4, 2 * (tm // FFN_SLABS) + FFN_EDGE_ROWS, FFN_CW), F32)],
        name="ffn_up",
        args=(h2, h2, h2, w_up_bf, w_up_bf, cw3, cb3))
    out, *down_cast = _pallas(
        functools.partial(_ffn_down_kernel, tm=down_tm, up_tm=tm),
        grid=(n // down_tm, FFN_DOWN_STEPS),
        cast=down_cast, step_of=lambda i, j: i * FFN_DOWN_STEPS + j,
        in_specs=[
            pl.BlockSpec((down_tm, act_cols), lambda i, j: (i, 0)),
            pl.BlockSpec((None, D_FF, FFN_TN), lambda i, j: (j, 0, 0)),
            pl.BlockSpec((down_tm, FFN_TN), lambda i, j: (i, j)),
            pl.BlockSpec((1, FFN_TN), lambda i, j: (0, j)),
        ],
        out_specs=[pl.BlockSpec((down_tm, FFN_TN), lambda i, j: (i, j))],
        out_shape=[jax.ShapeDtypeStruct((n, D_MODEL), F32)],
        name="ffn_down",
        args=(act, w_down_bf, x2, gate))
    return out, up_cast, down_cast


def _gelu(x):
    return 0.5 * x * (1.0 + lax.erf(x * (2.0 ** -0.5)))


def _cd_in_kernel(x_ref, g_ref, sh_ref, sc_ref, w_ref, vg_ref, uv_ref, f_ref, *, tm):
    rows_per = tm // ROW_SPLIT
    for part in range(ROW_SPLIT):
        rs = slice(part * rows_per, (part + 1) * rows_per)
        h = _norm_mod(x_ref[rs, :], g_ref[...], sh_ref[...], sc_ref[...]).astype(BF16)
        uv_ref[rs, 0:C_WIDTH] = _gelu(_dot(h, w_ref[:, 0:C_WIDTH])).astype(BF16)
        v = _gelu(_dot(h, w_ref[:, C_WIDTH:2 * C_WIDTH]))
        ms = jnp.mean(v * v, axis=-1, keepdims=True)
        uv_ref[rs, C_WIDTH:] = (v * lax.rsqrt(ms + EPS) * vg_ref[...]).astype(BF16)
        f_ref[rs, :] = _dot(h, w_ref[:, 2 * C_WIDTH:])


def _cd_in_proj(x2, g, shift, scale, w_in_bf, v_norm_g, tm, cast=None):
    n = x2.shape[0]
    row = lambda i: (0, 0)
    return _pallas(
        functools.partial(_cd_in_kernel, tm=tm),
        grid=(n // tm,),
        cast=cast, step_of=lambda i: i,
        in_specs=[
            pl.BlockSpec((tm, D_MODEL), lambda i: (i, 0)),
            pl.BlockSpec((1, D_MODEL), row),
            pl.BlockSpec((1, D_MODEL), row),
            pl.BlockSpec((1, D_MODEL), row),
            pl.BlockSpec((D_MODEL, CD_IN), row, pipeline_mode=pl.Buffered(1)),
            pl.BlockSpec((1, C_WIDTH), row),
        ],
        out_specs=[
            pl.BlockSpec((tm, 2 * C_WIDTH), lambda i: (i, 0)),
            pl.BlockSpec((tm, D_WIDTH), lambda i: (i, 0)),
        ],
        out_shape=[jax.ShapeDtypeStruct((n, 2 * C_WIDTH), BF16),
                   jax.ShapeDtypeStruct((n, D_WIDTH), F32)],
        name="cd_in_proj",
        args=(x2, g, shift, scale, w_in_bf, v_norm_g))


def _spatial_kernel(u_ref, v_ref, ws_ref, bias_ref, o_ref, *, chunks):
    for k in range(chunks):
        rs = slice(k * CHUNK, (k + 1) * CHUNK)
        for g in range(C_GROUPS):
            cs = slice(g * C_GROUP_DIM, (g + 1) * C_GROUP_DIM)
            s = _dot(ws_ref[g], v_ref[rs, cs]) + bias_ref[:, cs]
            o_ref[rs, cs] = (u_ref[rs, cs].astype(F32) * s).astype(BF16)


def _spatial_gate(uv, w_spatial_bf, bias_full, chunks=4):
    n = uv.shape[0]
    rows = chunks * CHUNK
    return pl.pallas_call(
        functools.partial(_spatial_kernel, chunks=chunks),
        grid=(n // rows,),
        in_specs=[
            pl.BlockSpec((rows, C_WIDTH), lambda i: (i, 0)),
            pl.BlockSpec((rows, C_WIDTH), lambda i: (i, 1)),
            pl.BlockSpec((C_GROUPS, CHUNK, CHUNK), lambda i: (0, 0, 0)),
            pl.BlockSpec((CHUNK, C_WIDTH), lambda i: (0, 0)),
        ],
        out_specs=pl.BlockSpec((rows, C_WIDTH), lambda i: (i, 0)),
        out_shape=jax.ShapeDtypeStruct((n, C_WIDTH), BF16),
        compiler_params=_params(("arbitrary",)),
        name="spatial_gate",
    )(uv, uv, w_spatial_bf, bias_full)


FFT_SUB = V7X_SUBLANES_F32
FFT1_CT = 512
FFT2_CT = 512


def _fourier_tables(n):
    a_len, b_len, sub = FFT_A, FFT_B, FFT_SUB
    assert a_len * b_len == n
    ch = np.arange(D_GROUP_DIM)
    ang_c = 2.0 * np.pi * ((ch[:, None] * ch[None, :]) % D_GROUP_DIM) / D_GROUP_DIM
    a = np.arange(a_len)
    f_a = np.exp(-2j * np.pi * ((a[:, None] * a[None, :]) % a_len) / a_len)
    m1 = np.kron(f_a, np.eye(sub))
    b = np.arange(b_len)
    tw = np.exp(-2j * np.pi * ((a[:, None] * b[None, :]) % n) / n)
    f_b = np.exp(-2j * np.pi * ((b[:, None] * b[None, :]) % b_len) / b_len)
    m2 = np.einsum('db,pq->dpqb', f_b, np.eye(sub)).reshape(b_len * sub, sub * b_len)
    norm = 1.0 / np.sqrt(float(n) * D_GROUP_DIM)
    m2 = m2 * norm
    tw3 = np.broadcast_to(tw[:, :, None], (a_len, b_len, V7X_LANES))
    f32 = lambda v: jnp.asarray(np.ascontiguousarray(v), dtype=F32)
    return dict(cos_c=f32(np.cos(ang_c)), sin_c=f32(np.sin(ang_c)),
                m1r=f32(m1.real), m1i=f32(m1.imag), m2r=f32(m2.real), m2i=f32(m2.imag),
                twr=f32(tw3.real), twi=f32(tw3.imag))


def _fft1_kernel(f_ref, cc_ref, sc_ref, m1r_ref, m1i_ref, twr_ref, twi_ref, tr_ref, ti_ref):
    rows = FFT_A * FFT_SUB
    ct = FFT1_CT
    fb = f_ref[...].reshape(rows, ct).astype(BF16)
    xr_parts = []
    xi_parts = []
    for q in range(ct // D_GROUP_DIM):
        blk = fb[:, q * D_GROUP_DIM:(q + 1) * D_GROUP_DIM]
        xr_parts.append(_dot(blk, cc_ref[...]))
        xi_parts.append(-_dot(blk, sc_ref[...]))
    xr = jnp.concatenate(xr_parts, axis=1).astype(BF16)
    xi = jnp.concatenate(xi_parts, axis=1).astype(BF16)
    m1r = m1r_ref[...]
    m1i = m1i_ref[...]
    tr = _dot(m1r, xr) - _dot(m1i, xi)
    ti = _dot(m1r, xi) + _dot(m1i, xr)
    reps = ct // V7X_LANES
    twr = jnp.tile(twr_ref[...].reshape(rows, V7X_LANES), (1, reps))
    twi = jnp.tile(twi_ref[...].reshape(rows, V7X_LANES), (1, reps))
    tr_ref[...] = (tr * twr - ti * twi).reshape(FFT_A, FFT_SUB, ct)
    ti_ref[...] = (tr * twi + ti * twr).reshape(FFT_A, FFT_SUB, ct)


def _fft2_kernel(tr_ref, ti_ref, m2r_ref, m2i_ref, wf_ref, o_ref):
    j = pl.program_id(1)
    tr = tr_ref[...].astype(BF16)
    ti = ti_ref[...].astype(BF16)
    z = _dot(m2r_ref[...], tr) - _dot(m2i_ref[...], ti)
    contrib = _dot(z.astype(BF16), wf_ref[...]).reshape(FFT_B, FFT_SUB, D_WIDTH)

    @pl.when(j == 0)
    def _():
        o_ref[...] = contrib

    @pl.when(j > 0)
    def _():
        o_ref[...] += contrib


def _fourier_mix(f, tabs, w_fourier_bf):
    n = f.shape[0]
    a_len, b_len, sub = FFT_A, FFT_B, FFT_SUB
    f3 = f.reshape(a_len, b_len, D_WIDTH)
    rows1 = a_len * sub
    const2 = lambda i, j: (0, 0)
    tr, ti = pl.pallas_call(
        _fft1_kernel,
        grid=(b_len // sub, D_WIDTH // FFT1_CT),
        in_specs=[
            pl.BlockSpec((a_len, sub, FFT1_CT), lambda i, j: (0, i, j)),
            pl.BlockSpec((D_GROUP_DIM, D_GROUP_DIM), const2),
            pl.BlockSpec((D_GROUP_DIM, D_GROUP_DIM), const2),
            pl.BlockSpec((rows1, rows1), const2),
            pl.BlockSpec((rows1, rows1), const2),
            pl.BlockSpec((a_len, sub, V7X_LANES), lambda i, j: (0, i, 0)),
            pl.BlockSpec((a_len, sub, V7X_LANES), lambda i, j: (0, i, 0)),
        ],
        out_specs=[pl.BlockSpec((a_len, sub, FFT1_CT), lambda i, j: (0, i, j))] * 2,
        out_shape=[jax.ShapeDtypeStruct((a_len, b_len, D_WIDTH), F32)] * 2,
        compiler_params=_params(("arbitrary", "arbitrary")),
        name="fourier_stage1",
    )(f3, tabs['cos_c'].astype(BF16), tabs['sin_c'].astype(BF16),
      tabs['m1r'].astype(BF16), tabs['m1i'].astype(BF16), tabs['twr'], tabs['twi'])

    rows2 = sub * b_len
    tr2 = tr.reshape(n, D_WIDTH)
    ti2 = ti.reshape(n, D_WIDTH)
    out = pl.pallas_call(
        _fft2_kernel,
        grid=(a_len // sub, D_WIDTH // FFT2_CT),
        in_specs=[
            pl.BlockSpec((rows2, FFT2_CT), lambda i, j: (i, j)),
            pl.BlockSpec((rows2, FFT2_CT), lambda i, j: (i, j)),
            pl.BlockSpec((rows2, rows2), const2),
            pl.BlockSpec((rows2, rows2), const2),
            pl.BlockSpec((FFT2_CT, D_WIDTH), lambda i, j: (j, 0)),
        ],
        out_specs=pl.BlockSpec((b_len, sub, D_WIDTH), lambda i, j: (0, i, 0)),
        out_shape=jax.ShapeDtypeStruct((b_len, a_len, D_WIDTH), F32),
        compiler_params=_params(("arbitrary", "arbitrary")),
        name="fourier_stage2",
    )(tr2, ti2, tabs['m2r'].astype(BF16), tabs['m2i'].astype(BF16), w_fourier_bf)
    return out.reshape(n, D_WIDTH)


def _rope_tables(n):
    rows = n // GRID_W
    row = np.repeat(np.arange(rows, dtype=np.float64), GRID_W)
    col = np.tile(np.arange(GRID_W, dtype=np.float64), rows)
    inv = ROPE_THETA ** (-np.arange(0, AXIS_DIM, 2, dtype=np.float64) / AXIS_DIM)
    ang_r = row[:, None] * inv[None, :]
    ang_c = col[:, None] * inv[None, :]
    cos = np.concatenate([np.cos(ang_r)] * 2 + [np.cos(ang_c)] * 2, axis=-1)
    sin = np.concatenate([-np.sin(ang_r), np.sin(ang_r), -np.sin(ang_c), np.sin(ang_c)], axis=-1)
    return jnp.asarray(cos, dtype=F32), jnp.asarray(sin, dtype=F32)


def kernel(x, c, ctx, c_ctx, w_mod, b_mod, norm1_g, norm2_g, ab_w_in, a_q_norm_g, a_k_norm_g, a_sink,
           b_w_pool, b_pool_scale, ab_w_out, cd_w_in, c_v_norm_g, c_w_spatial, c_b_spatial, d_w_fourier,
           cd_w_out, f_w_up, f_conv_w, f_conv_b, f_w_down):
    batch, n, _ = x.shape
    ctx_len = ctx.shape[1]
    assert batch == 1 and DEPTH == 2
    x2 = x.reshape(n, D_MODEL)
    ctx2 = ctx.reshape(ctx_len, D_MODEL)

    mod = _mod_vectors(c, c_ctx, w_mod, b_mod)

    def split6(v):
        return [v[:, k * D_MODEL:(k + 1) * D_MODEL] for k in range(6)]

    row1 = lambda v: v.reshape(1, -1)
    n_tiles = n // PROJ_TM
    n_blocks = n // BLOCK
    up_steps = (n // FFN_UP_TM) * (FFN_UP_STEPS + 1)
    down_steps = (n // FFN_DOWN_TM) * FFN_DOWN_STEPS
    cast_down0 = _CastJob(f_w_down, 0, D_FF // n_tiles, n_tiles, col_block=FFN_TN)
    cast_up0 = _CastJob(f_w_up, 0, D_MODEL // n_blocks, n_blocks, col_block=FFN_TF)
    cast_up1 = _CastJob(f_w_up, 1, D_MODEL // n_blocks, up_steps, col_block=FFN_TF)
    cast_down1 = _CastJob(f_w_down, 1, D_FF // down_steps, down_steps, col_block=FFN_TN)

    ml = split6(mod[0, 0:1])
    mc = split6(mod[0, 1:2])
    g1 = row1(norm1_g[0])
    w_in = ab_w_in[0].astype(BF16)
    qn = row1(a_q_norm_g[0])
    kn = row1(a_k_norm_g[0])
    cos, sin = _rope_tables(n)
    qkv, z, w_down0 = _ab_in_proj(x2, g1, ml[0], ml[1], w_in, cos, sin, qn, kn, tm=PROJ_TM,
                                  cast=cast_down0)
    ones = jnp.ones((ctx_len, HEAD_DIM), F32)
    zeros = jnp.zeros((ctx_len, HEAD_DIM), F32)
    qkv_ctx, _ = _ab_in_proj(ctx2, g1, mc[0], mc[1], w_in, ones, zeros, qn, kn, tm=ctx_len)
    attn, w_up0 = _window_attention(qkv, qkv_ctx, a_sink[0], cast=cast_up0)
    pool_tiles = n // POOL_TM
    pooled, w_out0 = _pool_mix(z, b_w_pool[0].astype(BF16), row1(b_pool_scale[0]), tm=POOL_TM,
                               cast=_CastJob(ab_w_out, 0, D_MODEL // pool_tiles, pool_tiles))
    x2, h2, cd_w_in_bf = _out_proj(attn, pooled, w_out0, x2, ml[2], row1(norm2_g[0]), ml[3], ml[4],
                                   tm=PROJ_TM, cast=_CastJob(cd_w_in, 0, D_MODEL // n_tiles, n_tiles))
    x2, (w_up1,), (w_down1,) = _conv_ffn(x2, h2, w_up0, f_conv_w[0], f_conv_b[0], w_down0, ml[5],
                                         tm=FFN_UP_TM, down_tm=FFN_DOWN_TM,
                                         up_cast=cast_up1, down_cast=cast_down1)

    ml = split6(mod[1, 0:1])
    uv, f, w_out1 = _cd_in_proj(x2, row1(norm1_g[1]), ml[0], ml[1], cd_w_in_bf, row1(c_v_norm_g[0]),
                                tm=PROJ_TM, cast=_CastJob(cd_w_out, 0, D_MODEL // n_tiles, n_tiles))
    bias_full = jnp.repeat(c_b_spatial[0].T, C_GROUP_DIM, axis=1)
    c_out = _spatial_gate(uv, c_w_spatial[0].astype(BF16), bias_full)
    d_out = _fourier_mix(f, _fourier_tables(n), d_w_fourier[0].astype(BF16))
    x2, h2 = _out_proj(c_out, d_out, w_out1, x2, ml[2], row1(norm2_g[1]), ml[3], ml[4], tm=PROJ_TM)
    x2, _, _ = _conv_ffn(x2, h2, w_up1, f_conv_w[1], f_conv_b[1], w_down1, ml[5],
                         tm=FFN_UP_TM, down_tm=FFN_DOWN_TM)
    return x2.reshape(batch, n, D_MODEL)
```

```python
import functools

import numpy as np
import jax
import jax.numpy as jnp
from jax import lax
from jax.experimental import pallas as pl
from jax.experimental.pallas import tpu as pltpu

F32 = jnp.float32
BF16 = jnp.bfloat16

D_MODEL = 2048
DEPTH = 2
GRID_W = 64
HEAD_DIM = 128
A_Q_HEADS = 8
A_KV_HEADS = 2
A_GROUP = A_Q_HEADS // A_KV_HEADS
A_Q_DIM = A_Q_HEADS * HEAD_DIM
A_KV_DIM = A_KV_HEADS * HEAD_DIM
A_QKV_DIM = A_Q_DIM + 2 * A_KV_DIM
WINDOW = 128
BLOCK = 128
ROPE_THETA = 10000.0
AXIS_DIM = HEAD_DIM // 2
ATTN_SCALE = HEAD_DIM ** -0.5
NEG_INF = -1e30
B_GROUPS = 4
B_WIDTH = 1024
B_GROUP_DIM = B_WIDTH // B_GROUPS
POOL_WINDOWS = (2, 4, 8, 16)
AB_IN = A_QKV_DIM + B_WIDTH
C_WIDTH = 1024
C_GROUPS = 4
C_GROUP_DIM = C_WIDTH // C_GROUPS
CHUNK = 128
D_WIDTH = 1024
D_GROUPS = 8
D_GROUP_DIM = D_WIDTH // D_GROUPS
CD_IN = 2 * C_WIDTH + D_WIDTH
D_FF = 5632
EPS = 1e-6

V7X_SUBLANES_F32 = 8
V7X_SUBLANES_BF16 = 16
V7X_LANES = 128
V7X_VMEM_BYTES = 64 * 1024 * 1024
VMEM_LIMIT = 56 * 1024 * 1024

PROJ_TM = 512
POOL_TM = 256
FFN_UP_TM = 1024
FFN_DOWN_TM = 1024

FFT_A = 64
FFT_B = 128


def _params(sem):
    return pltpu.CompilerParams(dimension_semantics=sem, vmem_limit_bytes=VMEM_LIMIT)


def _dot(a, b):
    return jnp.dot(a, b, preferred_element_type=F32)


class _CastJob:
    def __init__(self, src, layer, rows, n_steps, col_block=None):
        _, total_rows, self.cols = src.shape
        assert total_rows % rows == 0 and total_rows // rows <= n_steps
        self.src, self.layer, self.rows, self.col_block = src, layer, rows, col_block
        self.last = total_rows // rows - 1
        if col_block is None:
            self.out_shape = jax.ShapeDtypeStruct((total_rows, self.cols), BF16)
        else:
            assert self.cols % col_block == 0
            self.out_shape = jax.ShapeDtypeStruct((self.cols // col_block, total_rows, col_block), BF16)

    def specs(self, step_of):
        blk = lambda *ids: jnp.minimum(step_of(*ids), self.last)
        src = pl.BlockSpec((None, self.rows, self.cols), lambda *ids: (self.layer, blk(*ids), 0))
        if self.col_block is None:
            return src, pl.BlockSpec((self.rows, self.cols), lambda *ids: (blk(*ids), 0))
        return src, pl.BlockSpec((self.cols // self.col_block, self.rows, self.col_block),
                                 lambda *ids: (0, blk(*ids), 0))

    def run(self, src_ref, dst_ref):
        if self.col_block is None:
            dst_ref[...] = src_ref[...].astype(BF16)
        else:
            for b in range(self.cols // self.col_block):
                dst_ref[b] = src_ref[:, b * self.col_block:(b + 1) * self.col_block].astype(BF16)


def _host_cast(kernel_fn, cast, n_in, n_out):
    def body(*refs):
        cast.run(refs[n_in], refs[n_in + 1 + n_out])
        kernel_fn(*refs[:n_in], *refs[n_in + 1:n_in + 1 + n_out], *refs[n_in + 2 + n_out:])
    return body


def _pallas(kernel_fn, *, grid, in_specs, out_specs, out_shape, args, name, scratch_shapes=(),
            cast=None, step_of=None):
    out_specs, out_shape = list(out_specs), list(out_shape)
    if cast is not None:
        src_spec, dst_spec = cast.specs(step_of)
        kernel_fn = _host_cast(kernel_fn, cast, len(in_specs), len(out_specs))
        in_specs = [*in_specs, src_spec]
        out_specs.append(dst_spec)
        out_shape.append(cast.out_shape)
        args = (*args, cast.src)
    return pl.pallas_call(
        kernel_fn, grid=grid, in_specs=list(in_specs), out_specs=out_specs, out_shape=out_shape,
        scratch_shapes=list(scratch_shapes), compiler_params=_params(("arbitrary",) * len(grid)),
        name=name)(*args)


def _norm_mod(x, g, shift, scale):
    ms = jnp.mean(x * x, axis=-1, keepdims=True)
    y = x * lax.rsqrt(ms + EPS) * g
    return y * (1.0 + scale) + shift


MOD_TK = 128
MOD_STREAMS = 2


def _mod_kernel(cv_ref, w_ref, b_ref, o_ref, acc_scr):
    k = pl.program_id(1)
    sub = V7X_SUBLANES_F32
    groups = MOD_TK // sub
    n_out = 6 * D_MODEL
    lane_tiles = n_out // V7X_LANES

    @pl.when(k == 0)
    def _():
        acc_scr[...] = jnp.zeros_like(acc_scr)

    w = w_ref[0].reshape(groups, sub, n_out)
    for s in range(MOD_STREAMS):
        a = cv_ref[s]
        a = (a * jax.nn.sigmoid(a)).reshape(groups, sub, V7X_LANES)
        a = jnp.concatenate([a] * lane_tiles, axis=-1)
        acc_scr[s] += jnp.sum(w * a, axis=0)

    @pl.when(k == pl.num_programs(1) - 1)
    def _():
        rows = [jnp.sum(acc_scr[s], axis=0, keepdims=True) for s in range(MOD_STREAMS)]
        rows.append(jnp.zeros((sub - MOD_STREAMS, n_out), F32))
        o_ref[0] = jnp.concatenate(rows, axis=0) + b_ref[0]


def _mod_vectors(c, c_ctx, w_mod, b_mod):
    n_out = 6 * D_MODEL
    cv = jnp.stack([c.reshape(D_MODEL), c_ctx.reshape(D_MODEL)])
    cv = jnp.broadcast_to(cv[:, :, None], (MOD_STREAMS, D_MODEL, V7X_LANES))
    b3 = b_mod.reshape(DEPTH, 1, n_out)
    return pl.pallas_call(
        _mod_kernel,
        grid=(DEPTH, D_MODEL // MOD_TK),
        in_specs=[
            pl.BlockSpec((MOD_STREAMS, MOD_TK, V7X_LANES), lambda l, k: (0, k, 0)),
            pl.BlockSpec((1, MOD_TK, n_out), lambda l, k: (l, k, 0)),
            pl.BlockSpec((1, 1, n_out), lambda l, k: (l, 0, 0)),
        ],
        out_specs=pl.BlockSpec((1, V7X_SUBLANES_F32, n_out), lambda l, k: (l, 0, 0)),
        out_shape=jax.ShapeDtypeStruct((DEPTH, V7X_SUBLANES_F32, n_out), F32),
        scratch_shapes=[pltpu.VMEM((MOD_STREAMS, V7X_SUBLANES_F32, n_out), F32)],
        compiler_params=_params(("arbitrary", "arbitrary")),
        name="mod_vectors",
    )(cv, w_mod, b3)


AB_TN = 512
ROW_SPLIT = 2


def _rope(t, cos, sin_signed):
    lane = lax.broadcasted_iota(jnp.int32, t.shape, 1)
    first = (lane % AXIS_DIM) < (AXIS_DIM // 2)
    partner = jnp.where(first,
                        pltpu.roll(t, HEAD_DIM - AXIS_DIM // 2, 1),
                        pltpu.roll(t, AXIS_DIM // 2, 1))
    return t * cos + partner * sin_signed


def _head_norm_rope(t, g, cos, sin_signed):
    ms = jnp.mean(t * t, axis=-1, keepdims=True)
    return _rope(t * lax.rsqrt(ms + EPS) * g, cos, sin_signed)


def _ab_in_kernel(x_ref, g_ref, sh_ref, sc_ref, w_ref, cos_ref, sin_ref, qn_ref, kn_ref,
                  qkv_ref, z_ref, *, tm):
    rows_per = tm // ROW_SPLIT
    n_q_tiles = A_Q_DIM // AB_TN
    for part in range(ROW_SPLIT):
        rs = slice(part * rows_per, (part + 1) * rows_per)
        h = _norm_mod(x_ref[rs, :], g_ref[...], sh_ref[...], sc_ref[...]).astype(BF16)
        cos = cos_ref[rs, :]
        sin = sin_ref[rs, :]
        for t in range(AB_IN // AB_TN):
            p = _dot(h, w_ref[:, t * AB_TN:(t + 1) * AB_TN])
            if t < n_q_tiles:
                for hh in range(AB_TN // HEAD_DIM):
                    c0 = t * AB_TN + hh * HEAD_DIM
                    qkv_ref[rs, c0:c0 + HEAD_DIM] = _head_norm_rope(
                        p[:, hh * HEAD_DIM:(hh + 1) * HEAD_DIM], qn_ref[...], cos, sin).astype(BF16)
            elif t == n_q_tiles:
                for hh in range(A_KV_HEADS):
                    c0 = A_Q_DIM + hh * HEAD_DIM
                    qkv_ref[rs, c0:c0 + HEAD_DIM] = _head_norm_rope(
                        p[:, hh * HEAD_DIM:(hh + 1) * HEAD_DIM], kn_ref[...], cos, sin).astype(BF16)
                qkv_ref[rs, A_Q_DIM + A_KV_DIM:] = p[:, A_KV_DIM:].astype(BF16)
            else:
                c0 = (t - n_q_tiles - 1) * AB_TN
                z_ref[rs, c0:c0 + AB_TN] = p


def _ab_in_proj(x2, g, shift, scale, w_in_bf, cos, sin, qn, kn, tm, cast=None):
    n = x2.shape[0]
    assert A_Q_DIM % AB_TN == 0 and 2 * A_KV_DIM == AB_TN and B_WIDTH % AB_TN == 0
    row = lambda i: (0, 0)
    return _pallas(
        functools.partial(_ab_in_kernel, tm=tm),
        grid=(n // tm,),
        cast=cast, step_of=lambda i: i,
        in_specs=[
            pl.BlockSpec((tm, D_MODEL), lambda i: (i, 0)),
            pl.BlockSpec((1, D_MODEL), row),
            pl.BlockSpec((1, D_MODEL), row),
            pl.BlockSpec((1, D_MODEL), row),
            pl.BlockSpec((D_MODEL, AB_IN), row, pipeline_mode=pl.Buffered(1)),
            pl.BlockSpec((tm, HEAD_DIM), lambda i: (i, 0)),
            pl.BlockSpec((tm, HEAD_DIM), lambda i: (i, 0)),
            pl.BlockSpec((1, HEAD_DIM), row),
            pl.BlockSpec((1, HEAD_DIM), row),
        ],
        out_specs=[
            pl.BlockSpec((tm, A_QKV_DIM), lambda i: (i, 0)),
            pl.BlockSpec((tm, B_WIDTH), lambda i: (i, 0)),
        ],
        out_shape=[jax.ShapeDtypeStruct((n, A_QKV_DIM), BF16),
                   jax.ShapeDtypeStruct((n, B_WIDTH), F32)],
        name="ab_in_proj",
        args=(x2, g, shift, scale, w_in_bf, cos, sin, qn, kn))


def _attn_kernel(sink_ref, bias_ref, q_ref, kp_ref, kc_ref, kn_ref, vp_ref, vc_ref, vn_ref,
                 kx_ref, vx_ref, o_ref):
    rows = A_GROUP * BLOCK
    bias = bias_ref[...]
    r1 = lax.broadcasted_iota(jnp.int32, (rows, 1), 0) // BLOCK
    for hk in range(A_KV_HEADS):
        hs = slice(hk * HEAD_DIM, (hk + 1) * HEAD_DIM)
        kcat = jnp.concatenate([kp_ref[:, hs], kc_ref[:, hs], kn_ref[:, hs], kx_ref[:, hs]], axis=0)
        vcat = jnp.concatenate([vp_ref[:, hs], vc_ref[:, hs], vn_ref[:, hs], vx_ref[:, hs]], axis=0)
        q0 = hk * A_GROUP * HEAD_DIM
        q4 = jnp.concatenate(
            [q_ref[:, q0 + g * HEAD_DIM:q0 + (g + 1) * HEAD_DIM] for g in range(A_GROUP)], axis=0)
        s = lax.dot_general(q4, kcat, (((1,), (1,)), ((), ())), preferred_element_type=F32)
        s = s * ATTN_SCALE + bias
        sink = jnp.zeros((rows, 1), F32)
        for g in range(A_GROUP):
            sink = jnp.where(r1 == g, sink_ref[hk * A_GROUP + g], sink)
        m = jnp.maximum(jnp.max(s, axis=-1, keepdims=True), sink)
        e = jnp.exp(s - m)
        den = jnp.sum(e, axis=-1, keepdims=True) + jnp.exp(sink - m)
        o4 = _dot(e.astype(BF16), vcat) / den
        for g in range(A_GROUP):
            o_ref[:, q0 + g * HEAD_DIM:q0 + (g + 1) * HEAD_DIM] = (
                o4[g * BLOCK:(g + 1) * BLOCK].astype(BF16))


def _attn_bias(ctx_len):
    rows = A_GROUP * BLOCK
    cols = 3 * BLOCK + ctx_len
    r = np.arange(rows)[:, None] % BLOCK
    c = np.arange(cols)[None, :]
    band = (np.abs(BLOCK + r - c) <= WINDOW) | (c >= 3 * BLOCK)
    first = band & (c >= BLOCK)
    last = band & ((c < 2 * BLOCK) | (c >= 3 * BLOCK))
    masks = np.stack([first, band, last])
    return jnp.asarray(np.where(masks, 0.0, NEG_INF), dtype=F32)


def _window_attention(qkv, qkv_ctx, sink, cast=None):
    n = qkv.shape[0]
    ctx_len = qkv_ctx.shape[0]
    n_blocks = n // BLOCK
    assert n_blocks >= 2
    k_col = A_Q_DIM // A_KV_DIM
    v_col = k_col + 1
    prev = lambda i: jnp.maximum(i - 1, 0)
    nxt = lambda i: jnp.minimum(i + 1, n_blocks - 1)
    which = lambda i: jnp.where(i == 0, 0, jnp.where(i == n_blocks - 1, 2, 1))
    blk = (BLOCK, A_KV_DIM)
    rows = A_GROUP * BLOCK
    cols = 3 * BLOCK + ctx_len
    return _pallas(
        _attn_kernel,
        grid=(n_blocks,),
        cast=cast, step_of=lambda i: i,
        in_specs=[
            pl.BlockSpec(memory_space=pltpu.SMEM),
            pl.BlockSpec((None, rows, cols), lambda i: (which(i), 0, 0)),
            pl.BlockSpec((BLOCK, A_Q_DIM), lambda i: (i, 0)),
            pl.BlockSpec(blk, lambda i: (prev(i), k_col)),
            pl.BlockSpec(blk, lambda i: (i, k_col)),
            pl.BlockSpec(blk, lambda i: (nxt(i), k_col)),
            pl.BlockSpec(blk, lambda i: (prev(i), v_col)),
            pl.BlockSpec(blk, lambda i: (i, v_col)),
            pl.BlockSpec(blk, lambda i: (nxt(i), v_col)),
            pl.BlockSpec((ctx_len, A_KV_DIM), lambda i: (0, k_col)),
            pl.BlockSpec((ctx_len, A_KV_DIM), lambda i: (0, v_col)),
        ],
        out_specs=[pl.BlockSpec((BLOCK, A_Q_DIM), lambda i: (i, 0))],
        out_shape=[jax.ShapeDtypeStruct((n, A_Q_DIM), BF16)],
        name="window_attention",
        args=(sink, _attn_bias(ctx_len), qkv, qkv, qkv, qkv, qkv, qkv, qkv, qkv_ctx, qkv_ctx))


POOL_HALO = 8
assert max(POOL_WINDOWS) // 2 <= POOL_HALO


def _pool_kernel(zm_ref, zp_ref, zn_ref, w_ref, ps_ref, o_ref, z_scr, *, n_rows, tm):
    i = pl.program_id(0)
    last = pl.num_programs(0) - 1
    pad = POOL_HALO
    span_rows = tm + 2 * pad
    z_scr[0:pad, :] = jnp.zeros((pad, B_WIDTH), F32)
    z_scr[pad:2 * pad, :] = jnp.where(i > 0, zp_ref[...], 0.0)
    z_scr[2 * pad:2 * pad + tm, :] = zm_ref[...]
    z_scr[2 * pad + tm:3 * pad + tm, :] = jnp.where(i < last, zn_ref[...], 0.0)
    z_scr[3 * pad + tm:, :] = jnp.zeros((pad, B_WIDTH), F32)
    t = i * tm + lax.broadcasted_iota(jnp.int32, (tm, B_GROUP_DIM), 0)
    for g in range(B_GROUPS):
        window = POOL_WINDOWS[g]
        half = window // 2
        cs = slice(g * B_GROUP_DIM, (g + 1) * B_GROUP_DIM)
        s = z_scr[pl.ds(2 * pad - half, span_rows), cs]
        width = 1
        while width < window:
            s = s + pltpu.roll(s, span_rows - width, 0)
            width *= 2
        acc = s[0:tm]
        cnt = (jnp.minimum(t + half, n_rows) - jnp.maximum(t - half, 0)).astype(F32)
        d = (acc / cnt - zm_ref[:, cs]).astype(BF16)
        y = _dot(d, w_ref[g]) * ps_ref[:, cs]
        o_ref[:, cs] = y.astype(BF16)


def _pool_mix(z, w_pool_bf, pool_scale, tm, cast=None):
    n = z.shape[0]
    hb = tm // POOL_HALO
    n_halo_blocks = n // POOL_HALO
    return _pallas(
        functools.partial(_pool_kernel, n_rows=n, tm=tm),
        grid=(n // tm,),
        cast=cast, step_of=lambda i: i,
        in_specs=[
            pl.BlockSpec((tm, B_WIDTH), lambda i: (i, 0)),
            pl.BlockSpec((POOL_HALO, B_WIDTH), lambda i: (jnp.maximum(i * hb - 1, 0), 0)),
            pl.BlockSpec((POOL_HALO, B_WIDTH), lambda i: (jnp.minimum((i + 1) * hb, n_halo_blocks - 1), 0)),
            pl.BlockSpec((B_GROUPS, B_GROUP_DIM, B_GROUP_DIM), lambda i: (0, 0, 0)),
            pl.BlockSpec((1, B_WIDTH), lambda i: (0, 0)),
        ],
        out_specs=[pl.BlockSpec((tm, B_WIDTH), lambda i: (i, 0))],
        out_shape=[jax.ShapeDtypeStruct((n, B_WIDTH), BF16)],
        scratch_shapes=[pltpu.VMEM((tm + 4 * POOL_HALO, B_WIDTH), F32)],
        name="pool_mix",
        args=(z, z, z, w_pool_bf, pool_scale))


OUT_TN = 512


def _out_proj_kernel(a1_ref, a2_ref, w_ref, x_ref, gate_ref, g_ref, sh_ref, sc_ref, o_ref, h_ref, *, tm):
    rows_per = tm // ROW_SPLIT
    for part in range(ROW_SPLIT):
        rs = slice(part * rows_per, (part + 1) * rows_per)
        a = jnp.concatenate([a1_ref[rs, :].astype(BF16), a2_ref[rs, :].astype(BF16)], axis=1)
        for t in range(D_MODEL // OUT_TN):
            cs = slice(t * OUT_TN, (t + 1) * OUT_TN)
            o_ref[rs, cs] = x_ref[rs, cs] + gate_ref[:, cs] * _dot(a, w_ref[:, cs])
        h_ref[rs, :] = _norm_mod(o_ref[rs, :], g_ref[...], sh_ref[...], sc_ref[...])


def _out_proj(a1, a2, w_out_bf, x2, gate, g2, shift2, scale2, tm, cast=None):
    n = x2.shape[0]
    k1 = a1.shape[1]
    k2 = a2.shape[1]
    assert k1 + k2 == w_out_bf.shape[0] and k1 % V7X_LANES == 0
    row = lambda i: (0, 0)
    return _pallas(
        functools.partial(_out_proj_kernel, tm=tm),
        grid=(n // tm,),
        cast=cast, step_of=lambda i: i,
        in_specs=[
            pl.BlockSpec((tm, k1), lambda i: (i, 0)),
            pl.BlockSpec((tm, k2), lambda i: (i, 0)),
            pl.BlockSpec((k1 + k2, D_MODEL), row, pipeline_mode=pl.Buffered(1)),
            pl.BlockSpec((tm, D_MODEL), lambda i: (i, 0)),
            pl.BlockSpec((1, D_MODEL), row),
            pl.BlockSpec((1, D_MODEL), row),
            pl.BlockSpec((1, D_MODEL), row),
            pl.BlockSpec((1, D_MODEL), row),
        ],
        out_specs=[pl.BlockSpec((tm, D_MODEL), lambda i: (i, 0)),
                   pl.BlockSpec((tm, D_MODEL), lambda i: (i, 0))],
        out_shape=[jax.ShapeDtypeStruct((n, D_MODEL), F32),
                   jax.ShapeDtypeStruct((n, D_MODEL), F32)],
        name="out_proj",
        args=(a1, a2, w_out_bf, x2, gate, g2, shift2, scale2))


FFN_SLABS = V7X_SUBLANES_F32
FFN_EDGE_ROWS = V7X_SUBLANES_BF16
FFN_PIECE_ROWS = 16
FFN_ELEMENTWISE_DTYPE = BF16
FFN_CW = 256
FFN_TF = 2 * FFN_CW
FFN_TN = 512
FFN_CHUNKS = D_FF // FFN_CW
FFN_UP_STEPS = D_FF // FFN_TF
FFN_DOWN_STEPS = D_MODEL // FFN_TN


def _ffn_up_kernel(hm_ref, hp_ref, hn_ref, wg_ref, wv_ref, cw_ref, cb_ref, o_ref,
                   h_scr, carry_a, carry_b, *, tm):
    i = pl.program_id(0)
    j = pl.program_id(1)
    last_i = pl.num_programs(0) - 1
    nc = FFN_CHUNKS
    ns = FFN_SLABS
    sr = tm // ns

    def finish(carry, k, r, step):
        c = jnp.maximum(2 * step + k, 0)

        def conv(idx, kk, q0):
            slab = lambda s: carry[idx, pl.ds(s * sr + q0, FFN_PIECE_ROWS), :]
            prev = slab(r - 1) if r > 0 else slab(ns)
            nxt = slab(r + 1) if r < ns - 1 else slab(ns + 1)
            return prev * cw_ref[kk, 0] + slab(r) * cw_ref[kk, 1] + nxt * cw_ref[kk, 2] + cb_ref[kk]

        for q0 in range(0, sr, FFN_PIECE_ROWS):
            gg = conv(k, c, q0)
            vv = conv(2 + k, c + nc, q0)
            o_ref[pl.ds(r * sr + q0, FFN_PIECE_ROWS), k * FFN_CW:(k + 1) * FFN_CW] = (
                gg * jax.nn.sigmoid(gg) * vv).astype(BF16)

    @pl.when(j == 0)
    def _():
        hm = hm_ref[...].reshape(sr, ns, D_MODEL)
        h_scr[0:tm, :] = jnp.swapaxes(hm, 0, 1).reshape(tm, D_MODEL).astype(BF16)
        before = jnp.where(i > 0, hp_ref[ns - 1:ns, :], 0.0)
        after = jnp.where(i < last_i, hn_ref[0:1, :], 0.0)
        pad = jnp.zeros((FFN_EDGE_ROWS - 2, D_MODEL), F32)
        h_scr[tm:, :] = jnp.concatenate([before, after, pad], axis=0).astype(BF16)
        carry_b[...] = jnp.zeros_like(carry_b)

    def keep(carry, idx, u):
        dt = FFN_ELEMENTWISE_DTYPE
        carry[idx, 0:tm, :] = u[0:tm].astype(dt)
        carry[idx, tm:tm + sr, :] = jnp.concatenate(
            [u[tm:tm + 1], u[(ns - 1) * sr:ns * sr - 1]], axis=0).astype(dt)
        carry[idx, tm + sr:, :] = jnp.concatenate([u[1:sr], u[tm + 1:tm + 2]], axis=0).astype(dt)

    def up_step(write, read):
        for k in range(2):
            for r in range(ns):
                finish(read, k, r, j - 1)
        for k in range(2):
            cs = slice(k * FFN_CW, (k + 1) * FFN_CW)
            keep(write, k, _dot(h_scr[...], wg_ref[:, cs]))
            keep(write, 2 + k, _dot(h_scr[...], wv_ref[:, cs]))

    @pl.when((j < FFN_UP_STEPS) & (j % 2 == 0))
    def _():
        up_step(carry_a, carry_b)

    @pl.when((j < FFN_UP_STEPS) & (j % 2 == 1))
    def _():
        up_step(carry_b, carry_a)

    @pl.when(j == FFN_UP_STEPS)
    def _():
        last = carry_a if (FFN_UP_STEPS - 1) % 2 == 0 else carry_b
        for k in range(2):
            for r in range(ns):
                finish(last, k, r, FFN_UP_STEPS - 1)


def _ffn_down_kernel(a_ref, wd_ref, x_ref, gate_ref, o_ref, *, tm, up_tm):
    ns = FFN_SLABS
    sr = up_tm // ns
    y = _dot(a_ref[:, FFN_TF:], wd_ref[...])
    y = jnp.swapaxes(y.reshape(tm // up_tm, ns, sr, FFN_TN), 1, 2).reshape(tm, FFN_TN)
    o_ref[...] = x_ref[...] + gate_ref[...] * y


def _conv_ffn(x2, h2, w_up_bf, conv_w, conv_b, w_down_bf, gate, tm, down_tm, up_cast=None,
              down_cast=None):
    n = x2.shape[0]
    nj = FFN_UP_STEPS
    hb = tm // FFN_SLABS
    n_halo_blocks = n // FFN_SLABS
    act_cols = D_FF + FFN_TF
    cw3 = conv_w.reshape(3, 2 * FFN_CHUNKS, FFN_CW).transpose(1, 0, 2)
    cw3 = jnp.broadcast_to(cw3[:, :, None, :], (2 * FFN_CHUNKS, 3, FFN_PIECE_ROWS, FFN_CW))
    cb3 = jnp.broadcast_to(conv_b.reshape(2 * FFN_CHUNKS, 1, FFN_CW),
                           (2 * FFN_CHUNKS, FFN_PIECE_ROWS, FFN_CW))
    cw3 = cw3.astype(FFN_ELEMENTWISE_DTYPE)
    cb3 = cb3.astype(FFN_ELEMENTWISE_DTYPE)
    once = pl.Buffered(1)
    act, *up_cast = _pallas(
        functools.partial(_ffn_up_kernel, tm=tm),
        grid=(n // tm, nj + 1),
        cast=up_cast, step_of=lambda i, j: i * (nj + 1) + j,
        in_specs=[
            pl.BlockSpec((tm, D_MODEL), lambda i, j: (i, 0)),
            pl.BlockSpec((FFN_SLABS, D_MODEL), lambda i, j: (jnp.maximum(i * hb - 1, 0), 0)),
            pl.BlockSpec((FFN_SLABS, D_MODEL), lambda i, j: (jnp.minimum((i + 1) * hb, n_halo_blocks - 1), 0)),
            pl.BlockSpec((None, D_MODEL, FFN_TF), lambda i, j: (jnp.minimum(j, nj - 1), 0, 0)),
            pl.BlockSpec((None, D_MODEL, FFN_TF), lambda i, j: (jnp.minimum(j, nj - 1) + nj, 0, 0)),
            pl.BlockSpec((2 * FFN_CHUNKS, 3, FFN_PIECE_ROWS, FFN_CW), lambda i, j: (0, 0, 0, 0),
                         pipeline_mode=once),
            pl.BlockSpec((2 * FFN_CHUNKS, FFN_PIECE_ROWS, FFN_CW), lambda i, j: (0, 0, 0),
                         pipeline_mode=once),
        ],
        out_specs=[pl.BlockSpec((tm, FFN_TF), lambda i, j: (i, j))],
        out_shape=[jax.ShapeDtypeStruct((n, act_cols), BF16)],
        scratch_shapes=[pltpu.VMEM((tm + FFN_EDGE_ROWS, D_MODEL), BF16),
                        pltpu.VMEM((4, tm + 2 * (tm // FFN_SLABS), FFN_CW), FFN_ELEMENTWISE_DTYPE),
                        pltpu.VMEM((4, tm + 2 * (tm // FFN_SLABS), FFN_CW), FFN_ELEMENTWISE_DTYPE)],
        name="ffn_up",
        args=(h2, h2, h2, w_up_bf, w_up_bf, cw3, cb3))
    out, *down_cast = _pallas(
        functools.partial(_ffn_down_kernel, tm=down_tm, up_tm=tm),
        grid=(n // down_tm, FFN_DOWN_STEPS),
        cast=down_cast, step_of=lambda i, j: i * FFN_DOWN_STEPS + j,
        in_specs=[
            pl.BlockSpec((down_tm, act_cols), lambda i, j: (i, 0)),
            pl.BlockSpec((None, D_FF, FFN_TN), lambda i, j: (j, 0, 0)),
            pl.BlockSpec((down_tm, FFN_TN), lambda i, j: (i, j)),
            pl.BlockSpec((1, FFN_TN), lambda i, j: (0, j)),
        ],
        out_specs=[pl.BlockSpec((down_tm, FFN_TN), lambda i, j: (i, j))],
        out_shape=[jax.ShapeDtypeStruct((n, D_MODEL), F32)],
        name="ffn_down",
        args=(act, w_down_bf, x2, gate))
    return out, up_cast, down_cast


def _gelu(x):
    return 0.5 * x * (1.0 + lax.erf(x * (2.0 ** -0.5)))


def _cd_in_kernel(x_ref, g_ref, sh_ref, sc_ref, w_ref, vg_ref, uv_ref, f_ref, *, tm):
    rows_per = tm // ROW_SPLIT
    for part in range(ROW_SPLIT):
        rs = slice(part * rows_per, (part + 1) * rows_per)
        h = _norm_mod(x_ref[rs, :], g_ref[...], sh_ref[...], sc_ref[...]).astype(BF16)
        uv_ref[rs, 0:C_WIDTH] = _gelu(_dot(h, w_ref[:, 0:C_WIDTH])).astype(BF16)
        v = _gelu(_dot(h, w_ref[:, C_WIDTH:2 * C_WIDTH]))
        ms = jnp.mean(v * v, axis=-1, keepdims=True)
        uv_ref[rs, C_WIDTH:] = (v * lax.rsqrt(ms + EPS) * vg_ref[...]).astype(BF16)
        f_ref[rs, :] = _dot(h, w_ref[:, 2 * C_WIDTH:])


def _cd_in_proj(x2, g, shift, scale, w_in_bf, v_norm_g, tm, cast=None):
    n = x2.shape[0]
    row = lambda i: (0, 0)
    return _pallas(
        functools.partial(_cd_in_kernel, tm=tm),
        grid=(n // tm,),
        cast=cast, step_of=lambda i: i,
        in_specs=[
            pl.BlockSpec((tm, D_MODEL), lambda i: (i, 0)),
            pl.BlockSpec((1, D_MODEL), row),
            pl.BlockSpec((1, D_MODEL), row),
            pl.BlockSpec((1, D_MODEL), row),
            pl.BlockSpec((D_MODEL, CD_IN), row, pipeline_mode=pl.Buffered(1)),
            pl.BlockSpec((1, C_WIDTH), row),
        ],
        out_specs=[
            pl.BlockSpec((tm, 2 * C_WIDTH), lambda i: (i, 0)),
            pl.BlockSpec((tm, D_WIDTH), lambda i: (i, 0)),
        ],
        out_shape=[jax.ShapeDtypeStruct((n, 2 * C_WIDTH), BF16),
                   jax.ShapeDtypeStruct((n, D_WIDTH), F32)],
        name="cd_in_proj",
        args=(x2, g, shift, scale, w_in_bf, v_norm_g))


def _spatial_kernel(u_ref, v_ref, ws_ref, bias_ref, o_ref, *, chunks):
    for k in range(chunks):
        rs = slice(k * CHUNK, (k + 1) * CHUNK)
        for g in range(C_GROUPS):
            cs = slice(g * C_GROUP_DIM, (g + 1) * C_GROUP_DIM)
            s = _dot(ws_ref[g], v_ref[rs, cs]) + bias_ref[:, cs]
            o_ref[rs, cs] = (u_ref[rs, cs].astype(F32) * s).astype(BF16)


def _spatial_gate(uv, w_spatial_bf, bias_full, chunks=4):
    n = uv.shape[0]
    rows = chunks * CHUNK
    return pl.pallas_call(
        functools.partial(_spatial_kernel, chunks=chunks),
        grid=(n // rows,),
        in_specs=[
            pl.BlockSpec((rows, C_WIDTH), lambda i: (i, 0)),
            pl.BlockSpec((rows, C_WIDTH), lambda i: (i, 1)),
            pl.BlockSpec((C_GROUPS, CHUNK, CHUNK), lambda i: (0, 0, 0)),
            pl.BlockSpec((CHUNK, C_WIDTH), lambda i: (0, 0)),
        ],
        out_specs=pl.BlockSpec((rows, C_WIDTH), lambda i: (i, 0)),
        out_shape=jax.ShapeDtypeStruct((n, C_WIDTH), BF16),
        compiler_params=_params(("arbitrary",)),
        name="spatial_gate",
    )(uv, uv, w_spatial_bf, bias_full)


FFT_SUB = V7X_SUBLANES_F32
FFT1_CT = 1024
FFT2_CT = 512


def _fourier_tables(n):
    a_len, b_len, sub = FFT_A, FFT_B, FFT_SUB
    assert a_len * b_len == n
    ch = np.arange(D_GROUP_DIM)
    ang_c = 2.0 * np.pi * ((ch[:, None] * ch[None, :]) % D_GROUP_DIM) / D_GROUP_DIM
    a = np.arange(a_len)
    f_a = np.exp(-2j * np.pi * ((a[:, None] * a[None, :]) % a_len) / a_len)
    m1 = np.kron(f_a, np.eye(sub))
    b = np.arange(b_len)
    tw = np.exp(-2j * np.pi * ((a[:, None] * b[None, :]) % n) / n)
    f_b = np.exp(-2j * np.pi * ((b[:, None] * b[None, :]) % b_len) / b_len)
    m2 = np.einsum('db,pq->dpqb', f_b, np.eye(sub)).reshape(b_len * sub, sub * b_len)
    norm = 1.0 / np.sqrt(float(n) * D_GROUP_DIM)
    m2 = m2 * norm
    tw3 = np.broadcast_to(tw[:, :, None], (a_len, b_len, V7X_LANES))
    f32 = lambda v: jnp.asarray(np.ascontiguousarray(v), dtype=F32)
    return dict(cos_c=f32(np.cos(ang_c)), sin_c=f32(np.sin(ang_c)),
                m1r=f32(m1.real), m1i=f32(m1.imag), m2r=f32(m2.real), m2i=f32(m2.imag),
                twr=f32(tw3.real), twi=f32(tw3.imag))


def _fft1_kernel(f_ref, cc_ref, sc_ref, m1r_ref, m1i_ref, twr_ref, twi_ref, tr_ref, ti_ref):
    rows = FFT_A * FFT_SUB
    ct = FFT1_CT
    fb = f_ref[...].reshape(rows, ct).astype(BF16)
    xr_parts = []
    xi_parts = []
    for q in range(ct // D_GROUP_DIM):
        blk = fb[:, q * D_GROUP_DIM:(q + 1) * D_GROUP_DIM]
        xr_parts.append(_dot(blk, cc_ref[...]))
        xi_parts.append(-_dot(blk, sc_ref[...]))
    xr = jnp.concatenate(xr_parts, axis=1).astype(BF16)
    xi = jnp.concatenate(xi_parts, axis=1).astype(BF16)
    m1r = m1r_ref[...]
    m1i = m1i_ref[...]
    tr = _dot(m1r, xr) - _dot(m1i, xi)
    ti = _dot(m1r, xi) + _dot(m1i, xr)
    reps = ct // V7X_LANES
    twr = jnp.tile(twr_ref[...].reshape(rows, V7X_LANES), (1, reps))
    twi = jnp.tile(twi_ref[...].reshape(rows, V7X_LANES), (1, reps))
    tr_ref[...] = (tr * twr - ti * twi).reshape(FFT_A, FFT_SUB, ct)
    ti_ref[...] = (tr * twi + ti * twr).reshape(FFT_A, FFT_SUB, ct)


def _fft2_kernel(tr_ref, ti_ref, m2r_ref, m2i_ref, wf_ref, o_ref):
    j = pl.program_id(1)
    tr = tr_ref[...].astype(BF16)
    ti = ti_ref[...].astype(BF16)
    z = _dot(m2r_ref[...], tr) - _dot(m2i_ref[...], ti)
    contrib = _dot(z.astype(BF16), wf_ref[...]).reshape(FFT_B, FFT_SUB, D_WIDTH)

    @pl.when(j == 0)
    def _():
        o_ref[...] = contrib

    @pl.when(j > 0)
    def _():
        o_ref[...] += contrib


def _fourier_mix(f, tabs, w_fourier_bf):
    n = f.shape[0]
    a_len, b_len, sub = FFT_A, FFT_B, FFT_SUB
    f3 = f.reshape(a_len, b_len, D_WIDTH)
    rows1 = a_len * sub
    const2 = lambda i, j: (0, 0)
    tr, ti = pl.pallas_call(
        _fft1_kernel,
        grid=(b_len // sub, D_WIDTH // FFT1_CT),
        in_specs=[
            pl.BlockSpec((a_len, sub, FFT1_CT), lambda i, j: (0, i, j)),
            pl.BlockSpec((D_GROUP_DIM, D_GROUP_DIM), const2),
            pl.BlockSpec((D_GROUP_DIM, D_GROUP_DIM), const2),
            pl.BlockSpec((rows1, rows1), const2),
            pl.BlockSpec((rows1, rows1), const2),
            pl.BlockSpec((a_len, sub, V7X_LANES), lambda i, j: (0, i, 0)),
            pl.BlockSpec((a_len, sub, V7X_LANES), lambda i, j: (0, i, 0)),
        ],
        out_specs=[pl.BlockSpec((a_len, sub, FFT1_CT), lambda i, j: (0, i, j))] * 2,
        out_shape=[jax.ShapeDtypeStruct((a_len, b_len, D_WIDTH), F32)] * 2,
        compiler_params=_params(("arbitrary", "arbitrary")),
        name="fourier_stage1",
    )(f3, tabs['cos_c'].astype(BF16), tabs['sin_c'].astype(BF16),
      tabs['m1r'].astype(BF16), tabs['m1i'].astype(BF16), tabs['twr'], tabs['twi'])

    rows2 = sub * b_len
    tr2 = tr.reshape(n, D_WIDTH)
    ti2 = ti.reshape(n, D_WIDTH)
    out = pl.pallas_call(
        _fft2_kernel,
        grid=(a_len // sub, D_WIDTH // FFT2_CT),
        in_specs=[
            pl.BlockSpec((rows2, FFT2_CT), lambda i, j: (i, j)),
            pl.BlockSpec((rows2, FFT2_CT), lambda i, j: (i, j)),
            pl.BlockSpec((rows2, rows2), const2),
            pl.BlockSpec((rows2, rows2), const2),
            pl.BlockSpec((FFT2_CT, D_WIDTH), lambda i, j: (j, 0)),
        ],
        out_specs=pl.BlockSpec((b_len, sub, D_WIDTH), lambda i, j: (0, i, 0)),
        out_shape=jax.ShapeDtypeStruct((b_len, a_len, D_WIDTH), F32),
        compiler_params=_params(("arbitrary", "arbitrary")),
        name="fourier_stage2",
    )(tr2, ti2, tabs['m2r'].astype(BF16), tabs['m2i'].astype(BF16), w_fourier_bf)
    return out.reshape(n, D_WIDTH)


def _rope_tables(n):
    rows = n // GRID_W
    row = np.repeat(np.arange(rows, dtype=np.float64), GRID_W)
    col = np.tile(np.arange(GRID_W, dtype=np.float64), rows)
    inv = ROPE_THETA ** (-np.arange(0, AXIS_DIM, 2, dtype=np.float64) / AXIS_DIM)
    ang_r = row[:, None] * inv[None, :]
    ang_c = col[:, None] * inv[None, :]
    cos = np.concatenate([np.cos(ang_r)] * 2 + [np.cos(ang_c)] * 2, axis=-1)
    sin = np.concatenate([-np.sin(ang_r), np.sin(ang_r), -np.sin(ang_c), np.sin(ang_c)], axis=-1)
    return jnp.asarray(cos, dtype=F32), jnp.asarray(sin, dtype=F32)


def kernel(x, c, ctx, c_ctx, w_mod, b_mod, norm1_g, norm2_g, ab_w_in, a_q_norm_g, a_k_norm_g, a_sink,
           b_w_pool, b_pool_scale, ab_w_out, cd_w_in, c_v_norm_g, c_w_spatial, c_b_spatial, d_w_fourier,
           cd_w_out, f_w_up, f_conv_w, f_conv_b, f_w_down):
    batch, n, _ = x.shape
    ctx_len = ctx.shape[1]
    assert batch == 1 and DEPTH == 2
    x2 = x.reshape(n, D_MODEL)
    ctx2 = ctx.reshape(ctx_len, D_MODEL)

    mod = _mod_vectors(c, c_ctx, w_mod, b_mod)

    def split6(v):
        return [v[:, k * D_MODEL:(k + 1) * D_MODEL] for k in range(6)]

    row1 = lambda v: v.reshape(1, -1)
    n_tiles = n // PROJ_TM
    n_blocks = n // BLOCK
    up_steps = (n // FFN_UP_TM) * (FFN_UP_STEPS + 1)
    down_steps = (n // FFN_DOWN_TM) * FFN_DOWN_STEPS
    cast_down0 = _CastJob(f_w_down, 0, D_FF // n_tiles, n_tiles, col_block=FFN_TN)
    cast_up0 = _CastJob(f_w_up, 0, D_MODEL // n_blocks, n_blocks, col_block=FFN_TF)
    cast_up1 = _CastJob(f_w_up, 1, D_MODEL // n_blocks, up_steps, col_block=FFN_TF)
    cast_down1 = _CastJob(f_w_down, 1, D_FF // down_steps, down_steps, col_block=FFN_TN)

    ml = split6(mod[0, 0:1])
    mc = split6(mod[0, 1:2])
    g1 = row1(norm1_g[0])
    w_in = ab_w_in[0].astype(BF16)
    qn = row1(a_q_norm_g[0])
    kn = row1(a_k_norm_g[0])
    cos, sin = _rope_tables(n)
    qkv, z, w_down0 = _ab_in_proj(x2, g1, ml[0], ml[1], w_in, cos, sin, qn, kn, tm=PROJ_TM,
                                  cast=cast_down0)
    ones = jnp.ones((ctx_len, HEAD_DIM), F32)
    zeros = jnp.zeros((ctx_len, HEAD_DIM), F32)
    qkv_ctx, _ = _ab_in_proj(ctx2, g1, mc[0], mc[1], w_in, ones, zeros, qn, kn, tm=ctx_len)
    attn, w_up0 = _window_attention(qkv, qkv_ctx, a_sink[0], cast=cast_up0)
    pool_tiles = n // POOL_TM
    pooled, w_out0 = _pool_mix(z, b_w_pool[0].astype(BF16), row1(b_pool_scale[0]), tm=POOL_TM,
                               cast=_CastJob(ab_w_out, 0, D_MODEL // pool_tiles, pool_tiles))
    x2, h2, cd_w_in_bf = _out_proj(attn, pooled, w_out0, x2, ml[2], row1(norm2_g[0]), ml[3], ml[4],
                                   tm=PROJ_TM, cast=_CastJob(cd_w_in, 0, D_MODEL // n_tiles, n_tiles))
    x2, (w_up1,), (w_down1,) = _conv_ffn(x2, h2, w_up0, f_conv_w[0], f_conv_b[0], w_down0, ml[5],
                                         tm=FFN_UP_TM, down_tm=FFN_DOWN_TM,
                                         up_cast=cast_up1, down_cast=cast_down1)

    ml = split6(mod[1, 0:1])
    uv, f, w_out1 = _cd_in_proj(x2, row1(norm1_g[1]), ml[0], ml[1], cd_w_in_bf, row1(c_v_norm_g[0]),
                                tm=PROJ_TM, cast=_CastJob(cd_w_out, 0, D_MODEL // n_tiles, n_tiles))
    bias_full = jnp.repeat(c_b_spatial[0].T, C_GROUP_DIM, axis=1)
    c_out = _spatial_gate(uv, c_w_spatial[0].astype(BF16), bias_full)
    d_out = _fourier_mix(f, _fourier_tables(n), d_w_fourier[0].astype(BF16))
    x2, h2 = _out_proj(c_out, d_out, w_out1, x2, ml[2], row1(norm2_g[1]), ml[3], ml[4], tm=PROJ_TM)
    x2, _, _ = _conv_ffn(x2, h2, w_up1, f_conv_w[1], f_conv_b[1], w_down1, ml[5],
                         tm=FFN_UP_TM, down_tm=FFN_DOWN_TM)
    return x2.reshape(batch, n, D_MODEL)
```

```python
import functools

import numpy as np
import jax
import jax.numpy as jnp
from jax import lax
from jax.experimental import pallas as pl
from jax.experimental.pallas import tpu as pltpu

F32 = jnp.float32
BF16 = jnp.bfloat16

D_MODEL = 2048
DEPTH = 2
GRID_W = 64
HEAD_DIM = 128
A_Q_HEADS = 8
A_KV_HEADS = 2
A_GROUP = A_Q_HEADS // A_KV_HEADS
A_Q_DIM = A_Q_HEADS * HEAD_DIM
A_KV_DIM = A_KV_HEADS * HEAD_DIM
A_QKV_DIM = A_Q_DIM + 2 * A_KV_DIM
WINDOW = 128
BLOCK = 128
ROPE_THETA = 10000.0
AXIS_DIM = HEAD_DIM // 2
ATTN_SCALE = HEAD_DIM ** -0.5
NEG_INF = -1e30
B_GROUPS = 4
B_WIDTH = 1024
B_GROUP_DIM = B_WIDTH // B_GROUPS
POOL_WINDOWS = (2, 4, 8, 16)
AB_IN = A_QKV_DIM + B_WIDTH
C_WIDTH = 1024
C_GROUPS = 4
C_GROUP_DIM = C_WIDTH // C_GROUPS
CHUNK = 128
D_WIDTH = 1024
D_GROUPS = 8
D_GROUP_DIM = D_WIDTH // D_GROUPS
CD_IN = 2 * C_WIDTH + D_WIDTH
D_FF = 5632
EPS = 1e-6

V7X_SUBLANES_F32 = 8
V7X_SUBLANES_BF16 = 16
V7X_LANES = 128
V7X_VMEM_BYTES = 64 * 1024 * 1024
VMEM_LIMIT = 56 * 1024 * 1024

PROJ_TM = 512
POOL_TM = 256
ATTN_TQ = 128
FFN_UP_TM = 1024
FFN_DOWN_TM = 1024

FFT_A = 64
FFT_B = 128


def _params(sem):
    return pltpu.CompilerParams(dimension_semantics=sem, vmem_limit_bytes=VMEM_LIMIT)


def _dot(a, b):
    return jnp.dot(a, b, preferred_element_type=F32)


class _CastJob:
    def __init__(self, src, layer, rows, n_steps, col_block=None):
        _, total_rows, self.cols = src.shape
        assert total_rows % rows == 0 and total_rows // rows <= n_steps
        self.src, self.layer, self.rows, self.col_block = src, layer, rows, col_block
        self.last = total_rows // rows - 1
        if col_block is None:
            self.out_shape = jax.ShapeDtypeStruct((total_rows, self.cols), BF16)
        else:
            assert self.cols % col_block == 0
            self.out_shape = jax.ShapeDtypeStruct((self.cols // col_block, total_rows, col_block), BF16)

    def specs(self, step_of):
        blk = lambda *ids: jnp.minimum(step_of(*ids), self.last)
        src = pl.BlockSpec((None, self.rows, self.cols), lambda *ids: (self.layer, blk(*ids), 0))
        if self.col_block is None:
            return src, pl.BlockSpec((self.rows, self.cols), lambda *ids: (blk(*ids), 0))
        return src, pl.BlockSpec((self.cols // self.col_block, self.rows, self.col_block),
                                 lambda *ids: (0, blk(*ids), 0))

    def run(self, src_ref, dst_ref):
        if self.col_block is None:
            dst_ref[...] = src_ref[...].astype(BF16)
        else:
            for b in range(self.cols // self.col_block):
                dst_ref[b] = src_ref[:, b * self.col_block:(b + 1) * self.col_block].astype(BF16)


def _host_cast(kernel_fn, cast, n_in, n_out):
    def body(*refs):
        cast.run(refs[n_in], refs[n_in + 1 + n_out])
        kernel_fn(*refs[:n_in], *refs[n_in + 1:n_in + 1 + n_out], *refs[n_in + 2 + n_out:])
    return body


def _pallas(kernel_fn, *, grid, in_specs, out_specs, out_shape, args, name, scratch_shapes=(),
            cast=None, step_of=None):
    out_specs, out_shape = list(out_specs), list(out_shape)
    if cast is not None:
        src_spec, dst_spec = cast.specs(step_of)
        kernel_fn = _host_cast(kernel_fn, cast, len(in_specs), len(out_specs))
        in_specs = [*in_specs, src_spec]
        out_specs.append(dst_spec)
        out_shape.append(cast.out_shape)
        args = (*args, cast.src)
    return pl.pallas_call(
        kernel_fn, grid=grid, in_specs=list(in_specs), out_specs=out_specs, out_shape=out_shape,
        scratch_shapes=list(scratch_shapes), compiler_params=_params(("arbitrary",) * len(grid)),
        name=name)(*args)


def _norm_mod(x, g, shift, scale):
    ms = jnp.mean(x * x, axis=-1, keepdims=True)
    y = x * lax.rsqrt(ms + EPS) * g
    return y * (1.0 + scale) + shift


MOD_TK = 128
MOD_STREAMS = 2


def _mod_kernel(cv_ref, w_ref, b_ref, o_ref, acc_scr):
    k = pl.program_id(1)
    sub = V7X_SUBLANES_F32
    groups = MOD_TK // sub
    n_out = 6 * D_MODEL
    lane_tiles = n_out // V7X_LANES

    @pl.when(k == 0)
    def _():
        acc_scr[...] = jnp.zeros_like(acc_scr)

    w = w_ref[0].reshape(groups, sub, n_out)
    for s in range(MOD_STREAMS):
        a = cv_ref[s]
        a = (a * jax.nn.sigmoid(a)).reshape(groups, sub, V7X_LANES)
        a = jnp.concatenate([a] * lane_tiles, axis=-1)
        acc_scr[s] += jnp.sum(w * a, axis=0)

    @pl.when(k == pl.num_programs(1) - 1)
    def _():
        rows = [jnp.sum(acc_scr[s], axis=0, keepdims=True) for s in range(MOD_STREAMS)]
        rows.append(jnp.zeros((sub - MOD_STREAMS, n_out), F32))
        o_ref[0] = jnp.concatenate(rows, axis=0) + b_ref[0]


def _mod_vectors(c, c_ctx, w_mod, b_mod):
    n_out = 6 * D_MODEL
    cv = jnp.stack([c.reshape(D_MODEL), c_ctx.reshape(D_MODEL)])
    cv = jnp.broadcast_to(cv[:, :, None], (MOD_STREAMS, D_MODEL, V7X_LANES))
    b3 = b_mod.reshape(DEPTH, 1, n_out)
    return pl.pallas_call(
        _mod_kernel,
        grid=(DEPTH, D_MODEL // MOD_TK),
        in_specs=[
            pl.BlockSpec((MOD_STREAMS, MOD_TK, V7X_LANES), lambda l, k: (0, k, 0)),
            pl.BlockSpec((1, MOD_TK, n_out), lambda l, k: (l, k, 0)),
            pl.BlockSpec((1, 1, n_out), lambda l, k: (l, 0, 0)),
        ],
        out_specs=pl.BlockSpec((1, V7X_SUBLANES_F32, n_out), lambda l, k: (l, 0, 0)),
        out_shape=jax.ShapeDtypeStruct((DEPTH, V7X_SUBLANES_F32, n_out), F32),
        scratch_shapes=[pltpu.VMEM((MOD_STREAMS, V7X_SUBLANES_F32, n_out), F32)],
        compiler_params=_params(("arbitrary", "arbitrary")),
        name="mod_vectors",
    )(cv, w_mod, b3)


AB_TN = 512
ROW_SPLIT = 2


def _rope(t, cos, sin_signed):
    lane = lax.broadcasted_iota(jnp.int32, t.shape, 1)
    first = (lane % AXIS_DIM) < (AXIS_DIM // 2)
    partner = jnp.where(first,
                        pltpu.roll(t, HEAD_DIM - AXIS_DIM // 2, 1),
                        pltpu.roll(t, AXIS_DIM // 2, 1))
    return t * cos + partner * sin_signed


def _head_norm_rope(t, g, cos, sin_signed):
    ms = jnp.mean(t * t, axis=-1, keepdims=True)
    return _rope(t * lax.rsqrt(ms + EPS) * g, cos, sin_signed)


def _ab_in_kernel(x_ref, g_ref, sh_ref, sc_ref, w_ref, cos_ref, sin_ref, qn_ref, kn_ref,
                  qkv_ref, z_ref, *, tm):
    rows_per = tm // ROW_SPLIT
    n_q_tiles = A_Q_DIM // AB_TN
    for part in range(ROW_SPLIT):
        rs = slice(part * rows_per, (part + 1) * rows_per)
        h = _norm_mod(x_ref[rs, :], g_ref[...], sh_ref[...], sc_ref[...]).astype(BF16)
        cos = cos_ref[rs, :]
        sin = sin_ref[rs, :]
        for t in range(AB_IN // AB_TN):
            p = _dot(h, w_ref[:, t * AB_TN:(t + 1) * AB_TN])
            if t < n_q_tiles:
                for hh in range(AB_TN // HEAD_DIM):
                    c0 = t * AB_TN + hh * HEAD_DIM
                    qkv_ref[rs, c0:c0 + HEAD_DIM] = _head_norm_rope(
                        p[:, hh * HEAD_DIM:(hh + 1) * HEAD_DIM], qn_ref[...], cos, sin).astype(BF16)
            elif t == n_q_tiles:
                for hh in range(A_KV_HEADS):
                    c0 = A_Q_DIM + hh * HEAD_DIM
                    qkv_ref[rs, c0:c0 + HEAD_DIM] = _head_norm_rope(
                        p[:, hh * HEAD_DIM:(hh + 1) * HEAD_DIM], kn_ref[...], cos, sin).astype(BF16)
                qkv_ref[rs, A_Q_DIM + A_KV_DIM:] = p[:, A_KV_DIM:].astype(BF16)
            else:
                c0 = (t - n_q_tiles - 1) * AB_TN
                z_ref[rs, c0:c0 + AB_TN] = p


def _ab_in_proj(x2, g, shift, scale, w_in_bf, cos, sin, qn, kn, tm, cast=None):
    n = x2.shape[0]
    assert A_Q_DIM % AB_TN == 0 and 2 * A_KV_DIM == AB_TN and B_WIDTH % AB_TN == 0
    row = lambda i: (0, 0)
    return _pallas(
        functools.partial(_ab_in_kernel, tm=tm),
        grid=(n // tm,),
        cast=cast, step_of=lambda i: i,
        in_specs=[
            pl.BlockSpec((tm, D_MODEL), lambda i: (i, 0)),
            pl.BlockSpec((1, D_MODEL), row),
            pl.BlockSpec((1, D_MODEL), row),
            pl.BlockSpec((1, D_MODEL), row),
            pl.BlockSpec((D_MODEL, AB_IN), row, pipeline_mode=pl.Buffered(1)),
            pl.BlockSpec((tm, HEAD_DIM), lambda i: (i, 0)),
            pl.BlockSpec((tm, HEAD_DIM), lambda i: (i, 0)),
            pl.BlockSpec((1, HEAD_DIM), row),
            pl.BlockSpec((1, HEAD_DIM), row),
        ],
        out_specs=[
            pl.BlockSpec((tm, A_QKV_DIM), lambda i: (i, 0)),
            pl.BlockSpec((tm, B_WIDTH), lambda i: (i, 0)),
        ],
        out_shape=[jax.ShapeDtypeStruct((n, A_QKV_DIM), BF16),
                   jax.ShapeDtypeStruct((n, B_WIDTH), F32)],
        name="ab_in_proj",
        args=(x2, g, shift, scale, w_in_bf, cos, sin, qn, kn))


def _attn_kernel(sink_ref, bias_ref, q_ref, kp_ref, kc_ref, kn_ref, vp_ref, vc_ref, vn_ref,
                 kx_ref, vx_ref, o_ref):
    rows = A_GROUP * ATTN_TQ
    bias = jnp.concatenate([bias_ref[...]] * A_GROUP, axis=0)
    r1 = lax.broadcasted_iota(jnp.int32, (rows, 1), 0) // ATTN_TQ
    for hk in range(A_KV_HEADS):
        hs = slice(hk * HEAD_DIM, (hk + 1) * HEAD_DIM)
        kcat = jnp.concatenate([kp_ref[:, hs], kc_ref[:, hs], kn_ref[:, hs], kx_ref[:, hs]], axis=0)
        vcat = jnp.concatenate([vp_ref[:, hs], vc_ref[:, hs], vn_ref[:, hs], vx_ref[:, hs]], axis=0)
        q0 = hk * A_GROUP * HEAD_DIM
        q4 = jnp.concatenate(
            [q_ref[:, q0 + g * HEAD_DIM:q0 + (g + 1) * HEAD_DIM] for g in range(A_GROUP)], axis=0)
        s = lax.dot_general(q4, kcat, (((1,), (1,)), ((), ())), preferred_element_type=F32)
        s = s * ATTN_SCALE + bias
        sink = jnp.zeros((rows, 1), F32)
        for g in range(A_GROUP):
            sink = jnp.where(r1 == g, sink_ref[hk * A_GROUP + g], sink)
        m = jnp.maximum(jnp.max(s, axis=-1, keepdims=True), sink)
        e = jnp.exp(s - m)
        den = jnp.sum(e, axis=-1, keepdims=True) + jnp.exp(sink - m)
        o4 = _dot(e.astype(BF16), vcat) / den
        for g in range(A_GROUP):
            o_ref[:, q0 + g * HEAD_DIM:q0 + (g + 1) * HEAD_DIM] = (
                o4[g * ATTN_TQ:(g + 1) * ATTN_TQ].astype(BF16))


def _attn_bias(ctx_len):
    band_cols = ATTN_TQ + 2 * BLOCK
    r = np.arange(ATTN_TQ)[:, None]
    c = np.arange(band_cols + ctx_len)[None, :]
    band = (np.abs(BLOCK + r - c) <= WINDOW) | (c >= band_cols)
    first = band & (c >= BLOCK)
    last = band & ((c < BLOCK + ATTN_TQ) | (c >= band_cols))
    masks = np.stack([first, band, last])
    return jnp.asarray(np.where(masks, 0.0, NEG_INF), dtype=F32)


def _window_attention(qkv, qkv_ctx, sink, cast=None):
    n = qkv.shape[0]
    ctx_len = qkv_ctx.shape[0]
    n_tiles = n // ATTN_TQ
    n_blocks = n // BLOCK
    per_tile = ATTN_TQ // BLOCK
    assert n_tiles >= 2 and WINDOW <= BLOCK
    k_col = A_Q_DIM // A_KV_DIM
    v_col = k_col + 1
    prev = lambda i: jnp.maximum(i * per_tile - 1, 0)
    nxt = lambda i: jnp.minimum((i + 1) * per_tile, n_blocks - 1)
    which = lambda i: jnp.where(i == 0, 0, jnp.where(i == n_tiles - 1, 2, 1))
    blk = (BLOCK, A_KV_DIM)
    cur = (ATTN_TQ, A_KV_DIM)
    cols = ATTN_TQ + 2 * BLOCK + ctx_len
    return _pallas(
        _attn_kernel,
        grid=(n_tiles,),
        cast=cast, step_of=lambda i: i,
        in_specs=[
            pl.BlockSpec(memory_space=pltpu.SMEM),
            pl.BlockSpec((None, ATTN_TQ, cols), lambda i: (which(i), 0, 0)),
            pl.BlockSpec((ATTN_TQ, A_Q_DIM), lambda i: (i, 0)),
            pl.BlockSpec(blk, lambda i: (prev(i), k_col)),
            pl.BlockSpec(cur, lambda i: (i, k_col)),
            pl.BlockSpec(blk, lambda i: (nxt(i), k_col)),
            pl.BlockSpec(blk, lambda i: (prev(i), v_col)),
            pl.BlockSpec(cur, lambda i: (i, v_col)),
            pl.BlockSpec(blk, lambda i: (nxt(i), v_col)),
            pl.BlockSpec((ctx_len, A_KV_DIM), lambda i: (0, k_col)),
            pl.BlockSpec((ctx_len, A_KV_DIM), lambda i: (0, v_col)),
        ],
        out_specs=[pl.BlockSpec((ATTN_TQ, A_Q_DIM), lambda i: (i, 0))],
        out_shape=[jax.ShapeDtypeStruct((n, A_Q_DIM), BF16)],
        name="window_attention",
        args=(sink, _attn_bias(ctx_len), qkv, qkv, qkv, qkv, qkv, qkv, qkv, qkv_ctx, qkv_ctx))


POOL_HALO = 8
assert max(POOL_WINDOWS) // 2 <= POOL_HALO


def _pool_kernel(zm_ref, zp_ref, zn_ref, w_ref, ps_ref, o_ref, z_scr, *, n_rows, tm):
    i = pl.program_id(0)
    last = pl.num_programs(0) - 1
    pad = POOL_HALO
    span_rows = tm + 2 * pad
    z_scr[0:pad, :] = jnp.zeros((pad, B_WIDTH), F32)
    z_scr[pad:2 * pad, :] = jnp.where(i > 0, zp_ref[...], 0.0)
    z_scr[2 * pad:2 * pad + tm, :] = zm_ref[...]
    z_scr[2 * pad + tm:3 * pad + tm, :] = jnp.where(i < last, zn_ref[...], 0.0)
    z_scr[3 * pad + tm:, :] = jnp.zeros((pad, B_WIDTH), F32)
    t = i * tm + lax.broadcasted_iota(jnp.int32, (tm, B_GROUP_DIM), 0)
    for g in range(B_GROUPS):
        window = POOL_WINDOWS[g]
        half = window // 2
        cs = slice(g * B_GROUP_DIM, (g + 1) * B_GROUP_DIM)
        s = z_scr[pl.ds(2 * pad - half, span_rows), cs]
        width = 1
        while width < window:
            s = s + pltpu.roll(s, span_rows - width, 0)
            width *= 2
        acc = s[0:tm]
        cnt = (jnp.minimum(t + half, n_rows) - jnp.maximum(t - half, 0)).astype(F32)
        d = (acc / cnt - zm_ref[:, cs]).astype(BF16)
        y = _dot(d, w_ref[g]) * ps_ref[:, cs]
        o_ref[:, cs] = y.astype(BF16)


def _pool_mix(z, w_pool_bf, pool_scale, tm, cast=None):
    n = z.shape[0]
    hb = tm // POOL_HALO
    n_halo_blocks = n // POOL_HALO
    return _pallas(
        functools.partial(_pool_kernel, n_rows=n, tm=tm),
        grid=(n // tm,),
        cast=cast, step_of=lambda i: i,
        in_specs=[
            pl.BlockSpec((tm, B_WIDTH), lambda i: (i, 0)),
            pl.BlockSpec((POOL_HALO, B_WIDTH), lambda i: (jnp.maximum(i * hb - 1, 0), 0)),
            pl.BlockSpec((POOL_HALO, B_WIDTH), lambda i: (jnp.minimum((i + 1) * hb, n_halo_blocks - 1), 0)),
            pl.BlockSpec((B_GROUPS, B_GROUP_DIM, B_GROUP_DIM), lambda i: (0, 0, 0)),
            pl.BlockSpec((1, B_WIDTH), lambda i: (0, 0)),
        ],
        out_specs=[pl.BlockSpec((tm, B_WIDTH), lambda i: (i, 0))],
        out_shape=[jax.ShapeDtypeStruct((n, B_WIDTH), BF16)],
        scratch_shapes=[pltpu.VMEM((tm + 4 * POOL_HALO, B_WIDTH), F32)],
        name="pool_mix",
        args=(z, z, z, w_pool_bf, pool_scale))


OUT_TN = 512


def _out_proj_kernel(a1_ref, a2_ref, w_ref, x_ref, gate_ref, g_ref, sh_ref, sc_ref, o_ref, h_ref, *, tm):
    rows_per = tm // ROW_SPLIT
    for part in range(ROW_SPLIT):
        rs = slice(part * rows_per, (part + 1) * rows_per)
        a = jnp.concatenate([a1_ref[rs, :].astype(BF16), a2_ref[rs, :].astype(BF16)], axis=1)
        for t in range(D_MODEL // OUT_TN):
            cs = slice(t * OUT_TN, (t + 1) * OUT_TN)
            o_ref[rs, cs] = x_ref[rs, cs] + gate_ref[:, cs] * _dot(a, w_ref[:, cs])
        h_ref[rs, :] = _norm_mod(o_ref[rs, :], g_ref[...], sh_ref[...], sc_ref[...])


def _out_proj(a1, a2, w_out_bf, x2, gate, g2, shift2, scale2, tm, cast=None):
    n = x2.shape[0]
    k1 = a1.shape[1]
    k2 = a2.shape[1]
    assert k1 + k2 == w_out_bf.shape[0] and k1 % V7X_LANES == 0
    row = lambda i: (0, 0)
    return _pallas(
        functools.partial(_out_proj_kernel, tm=tm),
        grid=(n // tm,),
        cast=cast, step_of=lambda i: i,
        in_specs=[
            pl.BlockSpec((tm, k1), lambda i: (i, 0)),
            pl.BlockSpec((tm, k2), lambda i: (i, 0)),
            pl.BlockSpec((k1 + k2, D_MODEL), row, pipeline_mode=pl.Buffered(1)),
            pl.BlockSpec((tm, D_MODEL), lambda i: (i, 0)),
            pl.BlockSpec((1, D_MODEL), row),
            pl.BlockSpec((1, D_MODEL), row),
            pl.BlockSpec((1, D_MODEL), row),
            pl.BlockSpec((1, D_MODEL), row),
        ],
        out_specs=[pl.BlockSpec((tm, D_MODEL), lambda i: (i, 0)),
                   pl.BlockSpec((tm, D_MODEL), lambda i: (i, 0))],
        out_shape=[jax.ShapeDtypeStruct((n, D_MODEL), F32),
                   jax.ShapeDtypeStruct((n, D_MODEL), F32)],
        name="out_proj",
        args=(a1, a2, w_out_bf, x2, gate, g2, shift2, scale2))


FFN_SLABS = V7X_SUBLANES_F32
FFN_EDGE_ROWS = V7X_SUBLANES_BF16
FFN_PIECE_ROWS = 16
FFN_ELEMENTWISE_DTYPE = BF16
FFN_CW = 256
FFN_TF = 2 * FFN_CW
FFN_TN = 512
FFN_CHUNKS = D_FF // FFN_CW
FFN_UP_STEPS = D_FF // FFN_TF
FFN_DOWN_STEPS = D_MODEL // FFN_TN


def _ffn_up_kernel(hm_ref, hp_ref, hn_ref, wg_ref, wv_ref, cw_ref, cb_ref, o_ref,
                   h_scr, carry_a, carry_b, *, tm):
    i = pl.program_id(0)
    j = pl.program_id(1)
    last_i = pl.num_programs(0) - 1
    nc = FFN_CHUNKS
    ns = FFN_SLABS
    sr = tm // ns

    def finish(carry, k, r, step):
        c = jnp.maximum(2 * step + k, 0)

        def conv(idx, kk, q0):
            slab = lambda s: carry[idx, pl.ds(s * sr + q0, FFN_PIECE_ROWS), :]
            prev = slab(r - 1) if r > 0 else slab(ns)
            nxt = slab(r + 1) if r < ns - 1 else slab(ns + 1)
            return prev * cw_ref[kk, 0] + slab(r) * cw_ref[kk, 1] + nxt * cw_ref[kk, 2] + cb_ref[kk]

        for q0 in range(0, sr, FFN_PIECE_ROWS):
            gg = conv(k, c, q0)
            vv = conv(2 + k, c + nc, q0)
            o_ref[pl.ds(r * sr + q0, FFN_PIECE_ROWS), k * FFN_CW:(k + 1) * FFN_CW] = (
                gg * jax.nn.sigmoid(gg) * vv).astype(BF16)

    @pl.when(j == 0)
    def _():
        hm = hm_ref[...].reshape(sr, ns, D_MODEL)
        h_scr[0:tm, :] = jnp.swapaxes(hm, 0, 1).reshape(tm, D_MODEL).astype(BF16)
        before = jnp.where(i > 0, hp_ref[ns - 1:ns, :], 0.0)
        after = jnp.where(i < last_i, hn_ref[0:1, :], 0.0)
        pad = jnp.zeros((FFN_EDGE_ROWS - 2, D_MODEL), F32)
        h_scr[tm:, :] = jnp.concatenate([before, after, pad], axis=0).astype(BF16)
        carry_b[...] = jnp.zeros_like(carry_b)

    def keep(carry, idx, u):
        dt = FFN_ELEMENTWISE_DTYPE
        carry[idx, 0:tm, :] = u[0:tm].astype(dt)
        carry[idx, tm:tm + sr, :] = jnp.concatenate(
            [u[tm:tm + 1], u[(ns - 1) * sr:ns * sr - 1]], axis=0).astype(dt)
        carry[idx, tm + sr:, :] = jnp.concatenate([u[1:sr], u[tm + 1:tm + 2]], axis=0).astype(dt)

    def up_step(write, read):
        for k in range(2):
            cs = slice(k * FFN_CW, (k + 1) * FFN_CW)
            keep(write, k, _dot(h_scr[...], wg_ref[:, cs]))
            keep(write, 2 + k, _dot(h_scr[...], wv_ref[:, cs]))
        for k in range(2):
            for r in range(ns):
                finish(read, k, r, j - 1)

    @pl.when((j < FFN_UP_STEPS) & (j % 2 == 0))
    def _():
        up_step(carry_a, carry_b)

    @pl.when((j < FFN_UP_STEPS) & (j % 2 == 1))
    def _():
        up_step(carry_b, carry_a)

    @pl.when(j == FFN_UP_STEPS)
    def _():
        last = carry_a if (FFN_UP_STEPS - 1) % 2 == 0 else carry_b
        for k in range(2):
            for r in range(ns):
                finish(last, k, r, FFN_UP_STEPS - 1)


def _ffn_down_kernel(a_ref, wd_ref, x_ref, gate_ref, o_ref, *, tm, up_tm):
    ns = FFN_SLABS
    sr = up_tm // ns
    y = _dot(a_ref[:, FFN_TF:], wd_ref[...])
    y = jnp.swapaxes(y.reshape(tm // up_tm, ns, sr, FFN_TN), 1, 2).reshape(tm, FFN_TN)
    o_ref[...] = x_ref[...] + gate_ref[...] * y


def _conv_ffn(x2, h2, w_up_bf, conv_w, conv_b, w_down_bf, gate, tm, down_tm, up_cast=None,
              down_cast=None):
    n = x2.shape[0]
    nj = FFN_UP_STEPS
    hb = tm // FFN_SLABS
    n_halo_blocks = n // FFN_SLABS
    act_cols = D_FF + FFN_TF
    cw3 = conv_w.reshape(3, 2 * FFN_CHUNKS, FFN_CW).transpose(1, 0, 2)
    cw3 = jnp.broadcast_to(cw3[:, :, None, :], (2 * FFN_CHUNKS, 3, FFN_PIECE_ROWS, FFN_CW))
    cb3 = jnp.broadcast_to(conv_b.reshape(2 * FFN_CHUNKS, 1, FFN_CW),
                           (2 * FFN_CHUNKS, FFN_PIECE_ROWS, FFN_CW))
    cw3 = cw3.astype(FFN_ELEMENTWISE_DTYPE)
    cb3 = cb3.astype(FFN_ELEMENTWISE_DTYPE)
    once = pl.Buffered(1)
    act, *up_cast = _pallas(
        functools.partial(_ffn_up_kernel, tm=tm),
        grid=(n // tm, nj + 1),
        cast=up_cast, step_of=lambda i, j: i * (nj + 1) + j,
        in_specs=[
            pl.BlockSpec((tm, D_MODEL), lambda i, j: (i, 0)),
            pl.BlockSpec((FFN_SLABS, D_MODEL), lambda i, j: (jnp.maximum(i * hb - 1, 0), 0)),
            pl.BlockSpec((FFN_SLABS, D_MODEL), lambda i, j: (jnp.minimum((i + 1) * hb, n_halo_blocks - 1), 0)),
            pl.BlockSpec((None, D_MODEL, FFN_TF), lambda i, j: (jnp.minimum(j, nj - 1), 0, 0)),
            pl.BlockSpec((None, D_MODEL, FFN_TF), lambda i, j: (jnp.minimum(j, nj - 1) + nj, 0, 0)),
            pl.BlockSpec((2 * FFN_CHUNKS, 3, FFN_PIECE_ROWS, FFN_CW), lambda i, j: (0, 0, 0, 0),
                         pipeline_mode=once),
            pl.BlockSpec((2 * FFN_CHUNKS, FFN_PIECE_ROWS, FFN_CW), lambda i, j: (0, 0, 0),
                         pipeline_mode=once),
        ],
        out_specs=[pl.BlockSpec((tm, FFN_TF), lambda i, j: (i, j))],
        out_shape=[jax.ShapeDtypeStruct((n, act_cols), BF16)],
        scratch_shapes=[pltpu.VMEM((tm + FFN_EDGE_ROWS, D_MODEL), BF16),
                        pltpu.VMEM((4, tm + 2 * (tm // FFN_SLABS), FFN_CW), FFN_ELEMENTWISE_DTYPE),
                        pltpu.VMEM((4, tm + 2 * (tm // FFN_SLABS), FFN_CW), FFN_ELEMENTWISE_DTYPE)],
        name="ffn_up",
        args=(h2, h2, h2, w_up_bf, w_up_bf, cw3, cb3))
    out, *down_cast = _pallas(
        functools.partial(_ffn_down_kernel, tm=down_tm, up_tm=tm),
        grid=(n // down_tm, FFN_DOWN_STEPS),
        cast=down_cast, step_of=lambda i, j: i * FFN_DOWN_STEPS + j,
        in_specs=[
            pl.BlockSpec((down_tm, act_cols), lambda i, j: (i, 0)),
            pl.BlockSpec((None, D_FF, FFN_TN), lambda i, j: (j, 0, 0)),
            pl.BlockSpec((down_tm, FFN_TN), lambda i, j: (i, j)),
            pl.BlockSpec((1, FFN_TN), lambda i, j: (0, j)),
        ],
        out_specs=[pl.BlockSpec((down_tm, FFN_TN), lambda i, j: (i, j))],
        out_shape=[jax.ShapeDtypeStruct((n, D_MODEL), F32)],
        name="ffn_down",
        args=(act, w_down_bf, x2, gate))
    return out, up_cast, down_cast


def _gelu(x):
    return 0.5 * x * (1.0 + lax.erf(x * (2.0 ** -0.5)))


def _cd_in_kernel(x_ref, g_ref, sh_ref, sc_ref, w_ref, vg_ref, uv_ref, f_ref, *, tm):
    rows_per = tm // ROW_SPLIT
    for part in range(ROW_SPLIT):
        rs = slice(part * rows_per, (part + 1) * rows_per)
        h = _norm_mod(x_ref[rs, :], g_ref[...], sh_ref[...], sc_ref[...]).astype(BF16)
        uv_ref[rs, 0:C_WIDTH] = _gelu(_dot(h, w_ref[:, 0:C_WIDTH])).astype(BF16)
        v = _gelu(_dot(h, w_ref[:, C_WIDTH:2 * C_WIDTH]))
        ms = jnp.mean(v * v, axis=-1, keepdims=True)
        uv_ref[rs, C_WIDTH:] = (v * lax.rsqrt(ms + EPS) * vg_ref[...]).astype(BF16)
        f_ref[rs, :] = _dot(h, w_ref[:, 2 * C_WIDTH:])


def _cd_in_proj(x2, g, shift, scale, w_in_bf, v_norm_g, tm, cast=None):
    n = x2.shape[0]
    row = lambda i: (0, 0)
    return _pallas(
        functools.partial(_cd_in_kernel, tm=tm),
        grid=(n // tm,),
        cast=cast, step_of=lambda i: i,
        in_specs=[
            pl.BlockSpec((tm, D_MODEL), lambda i: (i, 0)),
            pl.BlockSpec((1, D_MODEL), row),
            pl.BlockSpec((1, D_MODEL), row),
            pl.BlockSpec((1, D_MODEL), row),
            pl.BlockSpec((D_MODEL, CD_IN), row, pipeline_mode=pl.Buffered(1)),
            pl.BlockSpec((1, C_WIDTH), row),
        ],
        out_specs=[
            pl.BlockSpec((tm, 2 * C_WIDTH), lambda i: (i, 0)),
            pl.BlockSpec((tm, D_WIDTH), lambda i: (i, 0)),
        ],
        out_shape=[jax.ShapeDtypeStruct((n, 2 * C_WIDTH), BF16),
                   jax.ShapeDtypeStruct((n, D_WIDTH), F32)],
        name="cd_in_proj",
        args=(x2, g, shift, scale, w_in_bf, v_norm_g))


def _spatial_kernel(u_ref, v_ref, ws_ref, bias_ref, o_ref, *, chunks):
    for k in range(chunks):
        rs = slice(k * CHUNK, (k + 1) * CHUNK)
        for g in range(C_GROUPS):
            cs = slice(g * C_GROUP_DIM, (g + 1) * C_GROUP_DIM)
            s = _dot(ws_ref[g], v_ref[rs, cs]) + bias_ref[:, cs]
            o_ref[rs, cs] = (u_ref[rs, cs].astype(F32) * s).astype(BF16)


def _spatial_gate(uv, w_spatial_bf, bias_full, chunks=4):
    n = uv.shape[0]
    rows = chunks * CHUNK
    return pl.pallas_call(
        functools.partial(_spatial_kernel, chunks=chunks),
        grid=(n // rows,),
        in_specs=[
            pl.BlockSpec((rows, C_WIDTH), lambda i: (i, 0)),
            pl.BlockSpec((rows, C_WIDTH), lambda i: (i, 1)),
            pl.BlockSpec((C_GROUPS, CHUNK, CHUNK), lambda i: (0, 0, 0)),
            pl.BlockSpec((CHUNK, C_WIDTH), lambda i: (0, 0)),
        ],
        out_specs=pl.BlockSpec((rows, C_WIDTH), lambda i: (i, 0)),
        out_shape=jax.ShapeDtypeStruct((n, C_WIDTH), BF16),
        compiler_params=_params(("arbitrary",)),
        name="spatial_gate",
    )(uv, uv, w_spatial_bf, bias_full)


FFT_SUB = V7X_SUBLANES_F32
FFT1_CT = 1024
FFT2_CT = 512


def _fourier_tables(n):
    a_len, b_len, sub = FFT_A, FFT_B, FFT_SUB
    assert a_len * b_len == n
    ch = np.arange(D_GROUP_DIM)
    ang_c = 2.0 * np.pi * ((ch[:, None] * ch[None, :]) % D_GROUP_DIM) / D_GROUP_DIM
    a = np.arange(a_len)
    f_a = np.exp(-2j * np.pi * ((a[:, None] * a[None, :]) % a_len) / a_len)
    m1 = np.kron(f_a, np.eye(sub))
    b = np.arange(b_len)
    tw = np.exp(-2j * np.pi * ((a[:, None] * b[None, :]) % n) / n)
    f_b = np.exp(-2j * np.pi * ((b[:, None] * b[None, :]) % b_len) / b_len)
    m2 = np.einsum('db,pq->dpqb', f_b, np.eye(sub)).reshape(b_len * sub, sub * b_len)
    norm = 1.0 / np.sqrt(float(n) * D_GROUP_DIM)
    m2 = m2 * norm
    tw3 = np.broadcast_to(tw[:, :, None], (a_len, b_len, V7X_LANES))
    f32 = lambda v: jnp.asarray(np.ascontiguousarray(v), dtype=F32)
    return dict(cos_c=f32(np.cos(ang_c)), sin_c=f32(np.sin(ang_c)),
                m1r=f32(m1.real), m1i=f32(m1.imag), m2r=f32(m2.real), m2i=f32(m2.imag),
                twr=f32(tw3.real), twi=f32(tw3.imag))


def _fft1_kernel(f_ref, cc_ref, sc_ref, m1r_ref, m1i_ref, twr_ref, twi_ref, tr_ref, ti_ref):
    rows = FFT_A * FFT_SUB
    ct = FFT1_CT
    fb = f_ref[...].reshape(rows, ct).astype(BF16)
    xr_parts = []
    xi_parts = []
    for q in range(ct // D_GROUP_DIM):
        blk = fb[:, q * D_GROUP_DIM:(q + 1) * D_GROUP_DIM]
        xr_parts.append(_dot(blk, cc_ref[...]))
        xi_parts.append(-_dot(blk, sc_ref[...]))
    xr = jnp.concatenate(xr_parts, axis=1).astype(BF16)
    xi = jnp.concatenate(xi_parts, axis=1).astype(BF16)
    m1r = m1r_ref[...]
    m1i = m1i_ref[...]
    tr = _dot(m1r, xr) - _dot(m1i, xi)
    ti = _dot(m1r, xi) + _dot(m1i, xr)
    reps = ct // V7X_LANES
    twr = jnp.tile(twr_ref[...].reshape(rows, V7X_LANES), (1, reps))
    twi = jnp.tile(twi_ref[...].reshape(rows, V7X_LANES), (1, reps))
    tr_ref[...] = (tr * twr - ti * twi).reshape(FFT_A, FFT_SUB, ct)
    ti_ref[...] = (tr * twi + ti * twr).reshape(FFT_A, FFT_SUB, ct)


def _fft2_kernel(tr_ref, ti_ref, m2r_ref, m2i_ref, wf_ref, o_ref):
    j = pl.program_id(1)
    tr = tr_ref[...].astype(BF16)
    ti = ti_ref[...].astype(BF16)
    z = _dot(m2r_ref[...], tr) - _dot(m2i_ref[...], ti)
    contrib = _dot(z.astype(BF16), wf_ref[...]).reshape(FFT_B, FFT_SUB, D_WIDTH)

    @pl.when(j == 0)
    def _():
        o_ref[...] = contrib

    @pl.when(j > 0)
    def _():
        o_ref[...] += contrib


def _fourier_mix(f, tabs, w_fourier_bf):
    n = f.shape[0]
    a_len, b_len, sub = FFT_A, FFT_B, FFT_SUB
    f3 = f.reshape(a_len, b_len, D_WIDTH)
    rows1 = a_len * sub
    const2 = lambda i, j: (0, 0)
    tr, ti = pl.pallas_call(
        _fft1_kernel,
        grid=(b_len // sub, D_WIDTH // FFT1_CT),
        in_specs=[
            pl.BlockSpec((a_len, sub, FFT1_CT), lambda i, j: (0, i, j)),
            pl.BlockSpec((D_GROUP_DIM, D_GROUP_DIM), const2),
            pl.BlockSpec((D_GROUP_DIM, D_GROUP_DIM), const2),
            pl.BlockSpec((rows1, rows1), const2),
            pl.BlockSpec((rows1, rows1), const2),
            pl.BlockSpec((a_len, sub, V7X_LANES), lambda i, j: (0, i, 0)),
            pl.BlockSpec((a_len, sub, V7X_LANES), lambda i, j: (0, i, 0)),
        ],
        out_specs=[pl.BlockSpec((a_len, sub, FFT1_CT), lambda i, j: (0, i, j))] * 2,
        out_shape=[jax.ShapeDtypeStruct((a_len, b_len, D_WIDTH), F32)] * 2,
        compiler_params=_params(("arbitrary", "arbitrary")),
        name="fourier_stage1",
    )(f3, tabs['cos_c'].astype(BF16), tabs['sin_c'].astype(BF16),
      tabs['m1r'].astype(BF16), tabs['m1i'].astype(BF16), tabs['twr'], tabs['twi'])

    rows2 = sub * b_len
    tr2 = tr.reshape(n, D_WIDTH)
    ti2 = ti.reshape(n, D_WIDTH)
    out = pl.pallas_call(
        _fft2_kernel,
        grid=(a_len // sub, D_WIDTH // FFT2_CT),
        in_specs=[
            pl.BlockSpec((rows2, FFT2_CT), lambda i, j: (i, j)),
            pl.BlockSpec((rows2, FFT2_CT), lambda i, j: (i, j)),
            pl.BlockSpec((rows2, rows2), const2),
            pl.BlockSpec((rows2, rows2), const2),
            pl.BlockSpec((FFT2_CT, D_WIDTH), lambda i, j: (j, 0)),
        ],
        out_specs=pl.BlockSpec((b_len, sub, D_WIDTH), lambda i, j: (0, i, 0)),
        out_shape=jax.ShapeDtypeStruct((b_len, a_len, D_WIDTH), F32),
        compiler_params=_params(("arbitrary", "arbitrary")),
        name="fourier_stage2",
    )(tr2, ti2, tabs['m2r'].astype(BF16), tabs['m2i'].astype(BF16), w_fourier_bf)
    return out.reshape(n, D_WIDTH)


def _rope_tables(n):
    rows = n // GRID_W
    row = np.repeat(np.arange(rows, dtype=np.float64), GRID_W)
    col = np.tile(np.arange(GRID_W, dtype=np.float64), rows)
    inv = ROPE_THETA ** (-np.arange(0, AXIS_DIM, 2, dtype=np.float64) / AXIS_DIM)
    ang_r = row[:, None] * inv[None, :]
    ang_c = col[:, None] * inv[None, :]
    cos = np.concatenate([np.cos(ang_r)] * 2 + [np.cos(ang_c)] * 2, axis=-1)
    sin = np.concatenate([-np.sin(ang_r), np.sin(ang_r), -np.sin(ang_c), np.sin(ang_c)], axis=-1)
    return jnp.asarray(cos, dtype=F32), jnp.asarray(sin, dtype=F32)


def kernel(x, c, ctx, c_ctx, w_mod, b_mod, norm1_g, norm2_g, ab_w_in, a_q_norm_g, a_k_norm_g, a_sink,
           b_w_pool, b_pool_scale, ab_w_out, cd_w_in, c_v_norm_g, c_w_spatial, c_b_spatial, d_w_fourier,
           cd_w_out, f_w_up, f_conv_w, f_conv_b, f_w_down):
    batch, n, _ = x.shape
    ctx_len = ctx.shape[1]
    assert batch == 1 and DEPTH == 2
    x2 = x.reshape(n, D_MODEL)
    ctx2 = ctx.reshape(ctx_len, D_MODEL)

    mod = _mod_vectors(c, c_ctx, w_mod, b_mod)

    def split6(v):
        return [v[:, k * D_MODEL:(k + 1) * D_MODEL] for k in range(6)]

    row1 = lambda v: v.reshape(1, -1)
    n_tiles = n // PROJ_TM
    attn_steps = n // ATTN_TQ
    up_steps = (n // FFN_UP_TM) * (FFN_UP_STEPS + 1)
    down_steps = (n // FFN_DOWN_TM) * FFN_DOWN_STEPS
    cast_down0 = _CastJob(f_w_down, 0, D_FF // n_tiles, n_tiles, col_block=FFN_TN)
    cast_up0 = _CastJob(f_w_up, 0, D_MODEL // attn_steps, attn_steps, col_block=FFN_TF)
    cast_up1 = _CastJob(f_w_up, 1, D_MODEL // attn_steps, up_steps, col_block=FFN_TF)
    cast_down1 = _CastJob(f_w_down, 1, D_FF // down_steps, down_steps, col_block=FFN_TN)

    ml = split6(mod[0, 0:1])
    mc = split6(mod[0, 1:2])
    g1 = row1(norm1_g[0])
    w_in = ab_w_in[0].astype(BF16)
    qn = row1(a_q_norm_g[0])
    kn = row1(a_k_norm_g[0])
    cos, sin = _rope_tables(n)
    qkv, z, w_down0 = _ab_in_proj(x2, g1, ml[0], ml[1], w_in, cos, sin, qn, kn, tm=PROJ_TM,
                                  cast=cast_down0)
    ones = jnp.ones((ctx_len, HEAD_DIM), F32)
    zeros = jnp.zeros((ctx_len, HEAD_DIM), F32)
    qkv_ctx, _ = _ab_in_proj(ctx2, g1, mc[0], mc[1], w_in, ones, zeros, qn, kn, tm=ctx_len)
    attn, w_up0 = _window_attention(qkv, qkv_ctx, a_sink[0], cast=cast_up0)
    pool_tiles = n // POOL_TM
    pooled, w_out0 = _pool_mix(z, b_w_pool[0].astype(BF16), row1(b_pool_scale[0]), tm=POOL_TM,
                               cast=_CastJob(ab_w_out, 0, D_MODEL // pool_tiles, pool_tiles))
    x2, h2, cd_w_in_bf = _out_proj(attn, pooled, w_out0, x2, ml[2], row1(norm2_g[0]), ml[3], ml[4],
                                   tm=PROJ_TM, cast=_CastJob(cd_w_in, 0, D_MODEL // n_tiles, n_tiles))
    x2, (w_up1,), (w_down1,) = _conv_ffn(x2, h2, w_up0, f_conv_w[0], f_conv_b[0], w_down0, ml[5],
                                         tm=FFN_UP_TM, down_tm=FFN_DOWN_TM,
                                         up_cast=cast_up1, down_cast=cast_down1)

    ml = split6(mod[1, 0:1])
    uv, f, w_out1 = _cd_in_proj(x2, row1(norm1_g[1]), ml[0], ml[1], cd_w_in_bf, row1(c_v_norm_g[0]),
                                tm=PROJ_TM, cast=_CastJob(cd_w_out, 0, D_MODEL // n_tiles, n_tiles))
    bias_full = jnp.repeat(c_b_spatial[0].T, C_GROUP_DIM, axis=1)
    c_out = _spatial_gate(uv, c_w_spatial[0].astype(BF16), bias_full)
    d_out = _fourier_mix(f, _fourier_tables(n), d_w_fourier[0].astype(BF16))
    x2, h2 = _out_proj(c_out, d_out, w_out1, x2, ml[2], row1(norm2_g[1]), ml[3], ml[4], tm=PROJ_TM)
    x2, _, _ = _conv_ffn(x2, h2, w_up1, f_conv_w[1], f_conv_b[1], w_down1, ml[5],
                         tm=FFN_UP_TM, down_tm=FFN_DOWN_TM)
    return x2.reshape(batch, n, D_MODEL)
```

```python
import functools

import numpy as np
import jax
import jax.numpy as jnp
from jax import lax
from jax.experimental import pallas as pl
from jax.experimental.pallas import tpu as pltpu

F32 = jnp.float32
BF16 = jnp.bfloat16

D_MODEL = 2048
DEPTH = 2
GRID_W = 64
HEAD_DIM = 128
A_Q_HEADS = 8
A_KV_HEADS = 2
A_GROUP = A_Q_HEADS // A_KV_HEADS
A_Q_DIM = A_Q_HEADS * HEAD_DIM
A_KV_DIM = A_KV_HEADS * HEAD_DIM
A_QKV_DIM = A_Q_DIM + 2 * A_KV_DIM
WINDOW = 128
BLOCK = 128
ROPE_THETA = 10000.0
AXIS_DIM = HEAD_DIM // 2
ATTN_SCALE = HEAD_DIM ** -0.5
NEG_INF = -1e30
B_GROUPS = 4
B_WIDTH = 1024
B_GROUP_DIM = B_WIDTH // B_GROUPS
POOL_WINDOWS = (2, 4, 8, 16)
AB_IN = A_QKV_DIM + B_WIDTH
C_WIDTH = 1024
C_GROUPS = 4
C_GROUP_DIM = C_WIDTH // C_GROUPS
CHUNK = 128
D_WIDTH = 1024
D_GROUPS = 8
D_GROUP_DIM = D_WIDTH // D_GROUPS
CD_IN = 2 * C_WIDTH + D_WIDTH
D_FF = 5632
EPS = 1e-6

V7X_SUBLANES_F32 = 8
V7X_SUBLANES_BF16 = 16
V7X_LANES = 128
V7X_VMEM_BYTES = 64 * 1024 * 1024
VMEM_LIMIT = 56 * 1024 * 1024

PROJ_TM = 512
POOL_TM = 256
ATTN_TQ = 128
FFN_UP_TM = 1024
FFN_DOWN_TM = 1024

FFT_A = 64
FFT_B = 128


def _params(sem):
    return pltpu.CompilerParams(dimension_semantics=sem, vmem_limit_bytes=VMEM_LIMIT)


def _dot(a, b):
    return jnp.dot(a, b, preferred_element_type=F32)


class _CastJob:
    def __init__(self, src, layer, rows, n_steps, col_block=None):
        _, total_rows, self.cols = src.shape
        assert total_rows % rows == 0 and total_rows // rows <= n_steps
        self.src, self.layer, self.rows, self.col_block = src, layer, rows, col_block
        self.last = total_rows // rows - 1
        if col_block is None:
            self.out_shape = jax.ShapeDtypeStruct((total_rows, self.cols), BF16)
        else:
            assert self.cols % col_block == 0
            self.out_shape = jax.ShapeDtypeStruct((self.cols // col_block, total_rows, col_block), BF16)

    def specs(self, step_of):
        blk = lambda *ids: jnp.minimum(step_of(*ids), self.last)
        src = pl.BlockSpec((None, self.rows, self.cols), lambda *ids: (self.layer, blk(*ids), 0))
        if self.col_block is None:
            return src, pl.BlockSpec((self.rows, self.cols), lambda *ids: (blk(*ids), 0))
        return src, pl.BlockSpec((self.cols // self.col_block, self.rows, self.col_block),
                                 lambda *ids: (0, blk(*ids), 0))

    def run(self, src_ref, dst_ref):
        if self.col_block is None:
            dst_ref[...] = src_ref[...].astype(BF16)
        else:
            for b in range(self.cols // self.col_block):
                dst_ref[b] = src_ref[:, b * self.col_block:(b + 1) * self.col_block].astype(BF16)


def _host_cast(kernel_fn, cast, n_in, n_out):
    def body(*refs):
        cast.run(refs[n_in], refs[n_in + 1 + n_out])
        kernel_fn(*refs[:n_in], *refs[n_in + 1:n_in + 1 + n_out], *refs[n_in + 2 + n_out:])
    return body


def _pallas(kernel_fn, *, grid, in_specs, out_specs, out_shape, args, name, scratch_shapes=(),
            cast=None, step_of=None):
    out_specs, out_shape = list(out_specs), list(out_shape)
    if cast is not None:
        src_spec, dst_spec = cast.specs(step_of)
        kernel_fn = _host_cast(kernel_fn, cast, len(in_specs), len(out_specs))
        in_specs = [*in_specs, src_spec]
        out_specs.append(dst_spec)
        out_shape.append(cast.out_shape)
        args = (*args, cast.src)
    return pl.pallas_call(
        kernel_fn, grid=grid, in_specs=list(in_specs), out_specs=out_specs, out_shape=out_shape,
        scratch_shapes=list(scratch_shapes), compiler_params=_params(("arbitrary",) * len(grid)),
        name=name)(*args)


def _norm_mod(x, g, shift, scale):
    ms = jnp.mean(x * x, axis=-1, keepdims=True)
    return x * lax.rsqrt(ms + EPS) * (g * (1.0 + scale)) + shift


MOD_TK = 128
MOD_STREAMS = 2


def _mod_kernel(cv_ref, w_ref, b_ref, o_ref, acc_scr):
    k = pl.program_id(1)
    sub = V7X_SUBLANES_F32
    groups = MOD_TK // sub
    n_out = 6 * D_MODEL
    lane_tiles = n_out // V7X_LANES

    @pl.when(k == 0)
    def _():
        acc_scr[...] = jnp.zeros_like(acc_scr)

    w = w_ref[0].reshape(groups, sub, n_out)
    for s in range(MOD_STREAMS):
        a = cv_ref[s]
        a = (a * jax.nn.sigmoid(a)).reshape(groups, sub, V7X_LANES)
        a = jnp.concatenate([a] * lane_tiles, axis=-1)
        acc_scr[s] += jnp.sum(w * a, axis=0)

    @pl.when(k == pl.num_programs(1) - 1)
    def _():
        rows = [jnp.sum(acc_scr[s], axis=0, keepdims=True) for s in range(MOD_STREAMS)]
        rows.append(jnp.zeros((sub - MOD_STREAMS, n_out), F32))
        o_ref[0] = jnp.concatenate(rows, axis=0) + b_ref[0]


def _mod_vectors(c, c_ctx, w_mod, b_mod):
    n_out = 6 * D_MODEL
    cv = jnp.stack([c.reshape(D_MODEL), c_ctx.reshape(D_MODEL)])
    cv = jnp.broadcast_to(cv[:, :, None], (MOD_STREAMS, D_MODEL, V7X_LANES))
    b3 = b_mod.reshape(DEPTH, 1, n_out)
    return pl.pallas_call(
        _mod_kernel,
        grid=(DEPTH, D_MODEL // MOD_TK),
        in_specs=[
            pl.BlockSpec((MOD_STREAMS, MOD_TK, V7X_LANES), lambda l, k: (0, k, 0)),
            pl.BlockSpec((1, MOD_TK, n_out), lambda l, k: (l, k, 0)),
            pl.BlockSpec((1, 1, n_out), lambda l, k: (l, 0, 0)),
        ],
        out_specs=pl.BlockSpec((1, V7X_SUBLANES_F32, n_out), lambda l, k: (l, 0, 0)),
        out_shape=jax.ShapeDtypeStruct((DEPTH, V7X_SUBLANES_F32, n_out), F32),
        scratch_shapes=[pltpu.VMEM((MOD_STREAMS, V7X_SUBLANES_F32, n_out), F32)],
        compiler_params=_params(("arbitrary", "arbitrary")),
        name="mod_vectors",
    )(cv, w_mod, b3)


AB_TN = 512
ROW_SPLIT = 2


def _rope(t, cos, sin_signed):
    lane = lax.broadcasted_iota(jnp.int32, t.shape, 1)
    first = (lane % AXIS_DIM) < (AXIS_DIM // 2)
    partner = jnp.where(first,
                        pltpu.roll(t, HEAD_DIM - AXIS_DIM // 2, 1),
                        pltpu.roll(t, AXIS_DIM // 2, 1))
    return t * cos + partner * sin_signed


def _head_norm_rope(t, g, cos, sin_signed):
    ms = jnp.mean(t * t, axis=-1, keepdims=True)
    return _rope(t * lax.rsqrt(ms + EPS) * g, cos, sin_signed)


def _ab_in_kernel(x_ref, g_ref, sh_ref, sc_ref, w_ref, cos_ref, sin_ref, qn_ref, kn_ref,
                  qkv_ref, z_ref, *, tm):
    rows_per = tm // ROW_SPLIT
    n_q_tiles = A_Q_DIM // AB_TN
    for part in range(ROW_SPLIT):
        rs = slice(part * rows_per, (part + 1) * rows_per)
        h = _norm_mod(x_ref[rs, :], g_ref[...], sh_ref[...], sc_ref[...]).astype(BF16)
        cos = cos_ref[rs, :]
        sin = sin_ref[rs, :]
        for t in range(AB_IN // AB_TN):
            p = _dot(h, w_ref[:, t * AB_TN:(t + 1) * AB_TN])
            if t < n_q_tiles:
                for hh in range(AB_TN // HEAD_DIM):
                    c0 = t * AB_TN + hh * HEAD_DIM
                    qkv_ref[rs, c0:c0 + HEAD_DIM] = _head_norm_rope(
                        p[:, hh * HEAD_DIM:(hh + 1) * HEAD_DIM], qn_ref[...] * ATTN_SCALE,
                        cos, sin).astype(BF16)
            elif t == n_q_tiles:
                for hh in range(A_KV_HEADS):
                    c0 = A_Q_DIM + hh * HEAD_DIM
                    qkv_ref[rs, c0:c0 + HEAD_DIM] = _head_norm_rope(
                        p[:, hh * HEAD_DIM:(hh + 1) * HEAD_DIM], kn_ref[...], cos, sin).astype(BF16)
                qkv_ref[rs, A_Q_DIM + A_KV_DIM:] = p[:, A_KV_DIM:].astype(BF16)
            else:
                c0 = (t - n_q_tiles - 1) * AB_TN
                z_ref[rs, c0:c0 + AB_TN] = p


def _ab_in_proj(x2, g, shift, scale, w_in_bf, cos, sin, qn, kn, tm, cast=None):
    n = x2.shape[0]
    assert A_Q_DIM % AB_TN == 0 and 2 * A_KV_DIM == AB_TN and B_WIDTH % AB_TN == 0
    row = lambda i: (0, 0)
    return _pallas(
        functools.partial(_ab_in_kernel, tm=tm),
        grid=(n // tm,),
        cast=cast, step_of=lambda i: i,
        in_specs=[
            pl.BlockSpec((tm, D_MODEL), lambda i: (i, 0)),
            pl.BlockSpec((1, D_MODEL), row),
            pl.BlockSpec((1, D_MODEL), row),
            pl.BlockSpec((1, D_MODEL), row),
            pl.BlockSpec((D_MODEL, AB_IN), row, pipeline_mode=pl.Buffered(1)),
            pl.BlockSpec((tm, HEAD_DIM), lambda i: (i, 0)),
            pl.BlockSpec((tm, HEAD_DIM), lambda i: (i, 0)),
            pl.BlockSpec((1, HEAD_DIM), row),
            pl.BlockSpec((1, HEAD_DIM), row),
        ],
        out_specs=[
            pl.BlockSpec((tm, A_QKV_DIM), lambda i: (i, 0)),
            pl.BlockSpec((tm, B_WIDTH), lambda i: (i, 0)),
        ],
        out_shape=[jax.ShapeDtypeStruct((n, A_QKV_DIM), BF16),
                   jax.ShapeDtypeStruct((n, B_WIDTH), F32)],
        name="ab_in_proj",
        args=(x2, g, shift, scale, w_in_bf, cos, sin, qn, kn))


def _attn_kernel(sink_ref, bias_ref, q_ref, kp_ref, kc_ref, kn_ref, vp_ref, vc_ref, vn_ref,
                 kx_ref, vx_ref, o_ref):
    rows = A_GROUP * ATTN_TQ
    band_cols = ATTN_TQ + 2 * BLOCK
    n_keys = band_cols + kx_ref.shape[0]
    bias = jnp.concatenate([bias_ref[...]] * A_GROUP, axis=0)
    r1 = lax.broadcasted_iota(jnp.int32, (rows, 1), 0) // ATTN_TQ
    ones_col = (lax.broadcasted_iota(jnp.int32, (n_keys, HEAD_DIM), 1) == 0).astype(BF16)
    for hk in range(A_KV_HEADS):
        hs = slice(hk * HEAD_DIM, (hk + 1) * HEAD_DIM)
        kcat = jnp.concatenate([kp_ref[:, hs], kc_ref[:, hs], kn_ref[:, hs], kx_ref[:, hs]], axis=0)
        vcat = jnp.concatenate([vp_ref[:, hs], vc_ref[:, hs], vn_ref[:, hs], vx_ref[:, hs]], axis=0)
        q0 = hk * A_GROUP * HEAD_DIM
        q4 = jnp.concatenate(
            [q_ref[:, q0 + g * HEAD_DIM:q0 + (g + 1) * HEAD_DIM] for g in range(A_GROUP)], axis=0)
        s = lax.dot_general(q4, kcat, (((1,), (1,)), ((), ())), preferred_element_type=F32)
        s = jnp.concatenate([s[:, :band_cols] + bias, s[:, band_cols:]], axis=1)
        sink = jnp.zeros((rows, 1), F32)
        for g in range(A_GROUP):
            sink = jnp.where(r1 == g, sink_ref[hk * A_GROUP + g], sink)
        m = jnp.maximum(jnp.max(s, axis=-1, keepdims=True), sink)
        e = jnp.exp(s - m).astype(BF16)
        pv = _dot(e, jnp.concatenate([vcat, ones_col], axis=1))
        den = pv[:, HEAD_DIM:HEAD_DIM + 1] + jnp.exp(sink - m)
        o4 = pv[:, :HEAD_DIM] / den
        for g in range(A_GROUP):
            o_ref[:, q0 + g * HEAD_DIM:q0 + (g + 1) * HEAD_DIM] = (
                o4[g * ATTN_TQ:(g + 1) * ATTN_TQ].astype(BF16))


def _attn_bias(ctx_len):
    del ctx_len
    band_cols = ATTN_TQ + 2 * BLOCK
    r = np.arange(ATTN_TQ)[:, None]
    c = np.arange(band_cols)[None, :]
    band = np.abs(BLOCK + r - c) <= WINDOW
    first = band & (c >= BLOCK)
    last = band & (c < BLOCK + ATTN_TQ)
    masks = np.stack([first, band, last])
    return jnp.asarray(np.where(masks, 0.0, NEG_INF), dtype=F32)


def _window_attention(qkv, qkv_ctx, sink, cast=None):
    n = qkv.shape[0]
    ctx_len = qkv_ctx.shape[0]
    n_tiles = n // ATTN_TQ
    n_blocks = n // BLOCK
    per_tile = ATTN_TQ // BLOCK
    assert n_tiles >= 2 and WINDOW <= BLOCK
    k_col = A_Q_DIM // A_KV_DIM
    v_col = k_col + 1
    prev = lambda i: jnp.maximum(i * per_tile - 1, 0)
    nxt = lambda i: jnp.minimum((i + 1) * per_tile, n_blocks - 1)
    which = lambda i: jnp.where(i == 0, 0, jnp.where(i == n_tiles - 1, 2, 1))
    blk = (BLOCK, A_KV_DIM)
    cur = (ATTN_TQ, A_KV_DIM)
    cols = ATTN_TQ + 2 * BLOCK
    return _pallas(
        _attn_kernel,
        grid=(n_tiles,),
        cast=cast, step_of=lambda i: i,
        in_specs=[
            pl.BlockSpec(memory_space=pltpu.SMEM),
            pl.BlockSpec((None, ATTN_TQ, cols), lambda i: (which(i), 0, 0)),
            pl.BlockSpec((ATTN_TQ, A_Q_DIM), lambda i: (i, 0)),
            pl.BlockSpec(blk, lambda i: (prev(i), k_col)),
            pl.BlockSpec(cur, lambda i: (i, k_col)),
            pl.BlockSpec(blk, lambda i: (nxt(i), k_col)),
            pl.BlockSpec(blk, lambda i: (prev(i), v_col)),
            pl.BlockSpec(cur, lambda i: (i, v_col)),
            pl.BlockSpec(blk, lambda i: (nxt(i), v_col)),
            pl.BlockSpec((ctx_len, A_KV_DIM), lambda i: (0, k_col)),
            pl.BlockSpec((ctx_len, A_KV_DIM), lambda i: (0, v_col)),
        ],
        out_specs=[pl.BlockSpec((ATTN_TQ, A_Q_DIM), lambda i: (i, 0))],
        out_shape=[jax.ShapeDtypeStruct((n, A_Q_DIM), BF16)],
        name="window_attention",
        args=(sink, _attn_bias(ctx_len), qkv, qkv, qkv, qkv, qkv, qkv, qkv, qkv_ctx, qkv_ctx))


POOL_HALO = 8
assert max(POOL_WINDOWS) // 2 <= POOL_HALO


def _pool_kernel(zm_ref, zp_ref, zn_ref, w_ref, ps_ref, o_ref, z_scr, *, n_rows, tm):
    i = pl.program_id(0)
    last = pl.num_programs(0) - 1
    pad = POOL_HALO
    span_rows = tm + 2 * pad
    z_scr[0:pad, :] = jnp.zeros((pad, B_WIDTH), F32)
    z_scr[pad:2 * pad, :] = jnp.where(i > 0, zp_ref[...], 0.0)
    z_scr[2 * pad:2 * pad + tm, :] = zm_ref[...]
    z_scr[2 * pad + tm:3 * pad + tm, :] = jnp.where(i < last, zn_ref[...], 0.0)
    z_scr[3 * pad + tm:, :] = jnp.zeros((pad, B_WIDTH), F32)
    t = i * tm + lax.broadcasted_iota(jnp.int32, (tm, B_GROUP_DIM), 0)
    for g in range(B_GROUPS):
        window = POOL_WINDOWS[g]
        half = window // 2
        cs = slice(g * B_GROUP_DIM, (g + 1) * B_GROUP_DIM)
        s = z_scr[pl.ds(2 * pad - half, span_rows), cs]
        width = 1
        while width < window:
            s = s + pltpu.roll(s, span_rows - width, 0)
            width *= 2
        acc = s[0:tm]
        cnt = (jnp.minimum(t + half, n_rows) - jnp.maximum(t - half, 0)).astype(F32)
        d = (acc / cnt - zm_ref[:, cs]).astype(BF16)
        y = _dot(d, w_ref[g]) * ps_ref[:, cs]
        o_ref[:, cs] = y.astype(BF16)


def _pool_mix(z, w_pool_bf, pool_scale, tm, cast=None):
    n = z.shape[0]
    hb = tm // POOL_HALO
    n_halo_blocks = n // POOL_HALO
    return _pallas(
        functools.partial(_pool_kernel, n_rows=n, tm=tm),
        grid=(n // tm,),
        cast=cast, step_of=lambda i: i,
        in_specs=[
            pl.BlockSpec((tm, B_WIDTH), lambda i: (i, 0)),
            pl.BlockSpec((POOL_HALO, B_WIDTH), lambda i: (jnp.maximum(i * hb - 1, 0), 0)),
            pl.BlockSpec((POOL_HALO, B_WIDTH), lambda i: (jnp.minimum((i + 1) * hb, n_halo_blocks - 1), 0)),
            pl.BlockSpec((B_GROUPS, B_GROUP_DIM, B_GROUP_DIM), lambda i: (0, 0, 0)),
            pl.BlockSpec((1, B_WIDTH), lambda i: (0, 0)),
        ],
        out_specs=[pl.BlockSpec((tm, B_WIDTH), lambda i: (i, 0))],
        out_shape=[jax.ShapeDtypeStruct((n, B_WIDTH), BF16)],
        scratch_shapes=[pltpu.VMEM((tm + 4 * POOL_HALO, B_WIDTH), F32)],
        name="pool_mix",
        args=(z, z, z, w_pool_bf, pool_scale))


OUT_TN = 512


def _out_proj_kernel(a1_ref, a2_ref, w_ref, x_ref, gate_ref, g_ref, sh_ref, sc_ref, o_ref, h_ref, *, tm):
    rows_per = tm // ROW_SPLIT
    for part in range(ROW_SPLIT):
        rs = slice(part * rows_per, (part + 1) * rows_per)
        a = jnp.concatenate([a1_ref[rs, :].astype(BF16), a2_ref[rs, :].astype(BF16)], axis=1)
        for t in range(D_MODEL // OUT_TN):
            cs = slice(t * OUT_TN, (t + 1) * OUT_TN)
            o_ref[rs, cs] = x_ref[rs, cs] + gate_ref[:, cs] * _dot(a, w_ref[:, cs])
        h_ref[rs, :] = _norm_mod(o_ref[rs, :], g_ref[...], sh_ref[...], sc_ref[...])


def _out_proj(a1, a2, w_out_bf, x2, gate, g2, shift2, scale2, tm, cast=None):
    n = x2.shape[0]
    k1 = a1.shape[1]
    k2 = a2.shape[1]
    assert k1 + k2 == w_out_bf.shape[0] and k1 % V7X_LANES == 0
    row = lambda i: (0, 0)
    return _pallas(
        functools.partial(_out_proj_kernel, tm=tm),
        grid=(n // tm,),
        cast=cast, step_of=lambda i: i,
        in_specs=[
            pl.BlockSpec((tm, k1), lambda i: (i, 0)),
            pl.BlockSpec((tm, k2), lambda i: (i, 0)),
            pl.BlockSpec((k1 + k2, D_MODEL), row, pipeline_mode=pl.Buffered(1)),
            pl.BlockSpec((tm, D_MODEL), lambda i: (i, 0)),
            pl.BlockSpec((1, D_MODEL), row),
            pl.BlockSpec((1, D_MODEL), row),
            pl.BlockSpec((1, D_MODEL), row),
            pl.BlockSpec((1, D_MODEL), row),
        ],
        out_specs=[pl.BlockSpec((tm, D_MODEL), lambda i: (i, 0)),
                   pl.BlockSpec((tm, D_MODEL), lambda i: (i, 0))],
        out_shape=[jax.ShapeDtypeStruct((n, D_MODEL), F32),
                   jax.ShapeDtypeStruct((n, D_MODEL), F32)],
        name="out_proj",
        args=(a1, a2, w_out_bf, x2, gate, g2, shift2, scale2))


FFN_SLABS = V7X_SUBLANES_F32
FFN_EDGE_ROWS = V7X_SUBLANES_BF16
FFN_PIECE_ROWS = 16
FFN_ELEMENTWISE_DTYPE = BF16
FFN_CW = 256
FFN_TF = 2 * FFN_CW
FFN_TN = 512
FFN_CHUNKS = D_FF // FFN_CW
FFN_UP_STEPS = D_FF // FFN_TF
FFN_DOWN_STEPS = D_MODEL // FFN_TN


def _ffn_up_kernel(hm_ref, hp_ref, hn_ref, wg_ref, wv_ref, cw_ref, cb_ref, o_ref,
                   h_scr, carry_a, carry_b, *, tm):
    i = pl.program_id(0)
    j = pl.program_id(1)
    last_i = pl.num_programs(0) - 1
    nc = FFN_CHUNKS
    ns = FFN_SLABS
    sr = tm // ns

    def finish(carry, k, r, step):
        c = jnp.maximum(2 * step + k, 0)

        def conv(idx, kk, q0):
            slab = lambda s: carry[idx, pl.ds(s * sr + q0, FFN_PIECE_ROWS), :]
            prev = slab(r - 1) if r > 0 else slab(ns)
            nxt = slab(r + 1) if r < ns - 1 else slab(ns + 1)
            return prev * cw_ref[kk, 0] + slab(r) * cw_ref[kk, 1] + nxt * cw_ref[kk, 2] + cb_ref[kk]

        for q0 in range(0, sr, FFN_PIECE_ROWS):
            gg = conv(k, c, q0)
            vv = conv(2 + k, c + nc, q0)
            o_ref[pl.ds(r * sr + q0, FFN_PIECE_ROWS), k * FFN_CW:(k + 1) * FFN_CW] = (
                gg * jax.nn.sigmoid(gg) * vv).astype(BF16)

    @pl.when(j == 0)
    def _():
        hm = hm_ref[...].reshape(sr, ns, D_MODEL)
        h_scr[0:tm, :] = jnp.swapaxes(hm, 0, 1).reshape(tm, D_MODEL).astype(BF16)
        before = jnp.where(i > 0, hp_ref[ns - 1:ns, :], 0.0)
        after = jnp.where(i < last_i, hn_ref[0:1, :], 0.0)
        pad = jnp.zeros((FFN_EDGE_ROWS - 2, D_MODEL), F32)
        h_scr[tm:, :] = jnp.concatenate([before, after, pad], axis=0).astype(BF16)
        carry_b[...] = jnp.zeros_like(carry_b)

    def keep(carry, idx, u):
        dt = FFN_ELEMENTWISE_DTYPE
        carry[idx, 0:tm, :] = u[0:tm].astype(dt)
        carry[idx, tm:tm + sr, :] = jnp.concatenate(
            [u[tm:tm + 1], u[(ns - 1) * sr:ns * sr - 1]], axis=0).astype(dt)
        carry[idx, tm + sr:, :] = jnp.concatenate([u[1:sr], u[tm + 1:tm + 2]], axis=0).astype(dt)

    def up_step(write, read):
        for k in range(2):
            for r in range(ns):
                finish(read, k, r, j - 1)
        for k in range(2):
            cs = slice(k * FFN_CW, (k + 1) * FFN_CW)
            keep(write, k, _dot(h_scr[...], wg_ref[:, cs]))
            keep(write, 2 + k, _dot(h_scr[...], wv_ref[:, cs]))

    @pl.when((j < FFN_UP_STEPS) & (j % 2 == 0))
    def _():
        up_step(carry_a, carry_b)

    @pl.when((j < FFN_UP_STEPS) & (j % 2 == 1))
    def _():
        up_step(carry_b, carry_a)

    @pl.when(j == FFN_UP_STEPS)
    def _():
        last = carry_a if (FFN_UP_STEPS - 1) % 2 == 0 else carry_b
        for k in range(2):
            for r in range(ns):
                finish(last, k, r, FFN_UP_STEPS - 1)


def _ffn_down_kernel(a_ref, wd_ref, x_ref, gate_ref, o_ref, *, tm, up_tm):
    ns = FFN_SLABS
    sr = up_tm // ns
    y = _dot(a_ref[:, FFN_TF:], wd_ref[...])
    y = jnp.swapaxes(y.reshape(tm // up_tm, ns, sr, FFN_TN), 1, 2).reshape(tm, FFN_TN)
    o_ref[...] = x_ref[...] + gate_ref[...] * y


def _conv_ffn(x2, h2, w_up_bf, conv_w, conv_b, w_down_bf, gate, tm, down_tm, up_cast=None,
              down_cast=None):
    n = x2.shape[0]
    nj = FFN_UP_STEPS
    hb = tm // FFN_SLABS
    n_halo_blocks = n // FFN_SLABS
    act_cols = D_FF + FFN_TF
    cw3 = conv_w.reshape(3, 2 * FFN_CHUNKS, FFN_CW).transpose(1, 0, 2)
    cw3 = jnp.broadcast_to(cw3[:, :, None, :], (2 * FFN_CHUNKS, 3, FFN_PIECE_ROWS, FFN_CW))
    cb3 = jnp.broadcast_to(conv_b.reshape(2 * FFN_CHUNKS, 1, FFN_CW),
                           (2 * FFN_CHUNKS, FFN_PIECE_ROWS, FFN_CW))
    cw3 = cw3.astype(FFN_ELEMENTWISE_DTYPE)
    cb3 = cb3.astype(FFN_ELEMENTWISE_DTYPE)
    once = pl.Buffered(1)
    act, *up_cast = _pallas(
        functools.partial(_ffn_up_kernel, tm=tm),
        grid=(n // tm, nj + 1),
        cast=up_cast, step_of=lambda i, j: i * (nj + 1) + j,
        in_specs=[
            pl.BlockSpec((tm, D_MODEL), lambda i, j: (i, 0)),
            pl.BlockSpec((FFN_SLABS, D_MODEL), lambda i, j: (jnp.maximum(i * hb - 1, 0), 0)),
            pl.BlockSpec((FFN_SLABS, D_MODEL), lambda i, j: (jnp.minimum((i + 1) * hb, n_halo_blocks - 1), 0)),
            pl.BlockSpec((None, D_MODEL, FFN_TF), lambda i, j: (jnp.minimum(j, nj - 1), 0, 0)),
            pl.BlockSpec((None, D_MODEL, FFN_TF), lambda i, j: (jnp.minimum(j, nj - 1) + nj, 0, 0)),
            pl.BlockSpec((2 * FFN_CHUNKS, 3, FFN_PIECE_ROWS, FFN_CW), lambda i, j: (0, 0, 0, 0),
                         pipeline_mode=once),
            pl.BlockSpec((2 * FFN_CHUNKS, FFN_PIECE_ROWS, FFN_CW), lambda i, j: (0, 0, 0),
                         pipeline_mode=once),
        ],
        out_specs=[pl.BlockSpec((tm, FFN_TF), lambda i, j: (i, j))],
        out_shape=[jax.ShapeDtypeStruct((n, act_cols), BF16)],
        scratch_shapes=[pltpu.VMEM((tm + FFN_EDGE_ROWS, D_MODEL), BF16),
                        pltpu.VMEM((4, tm + 2 * (tm // FFN_SLABS), FFN_CW), FFN_ELEMENTWISE_DTYPE),
                        pltpu.VMEM((4, tm + 2 * (tm // FFN_SLABS), FFN_CW), FFN_ELEMENTWISE_DTYPE)],
        name="ffn_up",
        args=(h2, h2, h2, w_up_bf, w_up_bf, cw3, cb3))
    out, *down_cast = _pallas(
        functools.partial(_ffn_down_kernel, tm=down_tm, up_tm=tm),
        grid=(n // down_tm, FFN_DOWN_STEPS),
        cast=down_cast, step_of=lambda i, j: i * FFN_DOWN_STEPS + j,
        in_specs=[
            pl.BlockSpec((down_tm, act_cols), lambda i, j: (i, 0)),
            pl.BlockSpec((None, D_FF, FFN_TN), lambda i, j: (j, 0, 0)),
            pl.BlockSpec((down_tm, FFN_TN), lambda i, j: (i, j)),
            pl.BlockSpec((1, FFN_TN), lambda i, j: (0, j)),
        ],
        out_specs=[pl.BlockSpec((down_tm, FFN_TN), lambda i, j: (i, j))],
        out_shape=[jax.ShapeDtypeStruct((n, D_MODEL), F32)],
        name="ffn_down",
        args=(act, w_down_bf, x2, gate))
    return out, up_cast, down_cast


def _gelu(x):
    return 0.5 * x * (1.0 + lax.erf(x * (2.0 ** -0.5)))


def _cd_in_kernel(x_ref, g_ref, sh_ref, sc_ref, w_ref, vg_ref, uv_ref, f_ref, *, tm):
    rows_per = tm // ROW_SPLIT
    for part in range(ROW_SPLIT):
        rs = slice(part * rows_per, (part + 1) * rows_per)
        h = _norm_mod(x_ref[rs, :], g_ref[...], sh_ref[...], sc_ref[...]).astype(BF16)
        uv_ref[rs, 0:C_WIDTH] = _gelu(_dot(h, w_ref[:, 0:C_WIDTH])).astype(BF16)
        v = _gelu(_dot(h, w_ref[:, C_WIDTH:2 * C_WIDTH]))
        ms = jnp.mean(v * v, axis=-1, keepdims=True)
        uv_ref[rs, C_WIDTH:] = (v * lax.rsqrt(ms + EPS) * vg_ref[...]).astype(BF16)
        f_ref[rs, :] = _dot(h, w_ref[:, 2 * C_WIDTH:])


def _cd_in_proj(x2, g, shift, scale, w_in_bf, v_norm_g, tm, cast=None):
    n = x2.shape[0]
    row = lambda i: (0, 0)
    return _pallas(
        functools.partial(_cd_in_kernel, tm=tm),
        grid=(n // tm,),
        cast=cast, step_of=lambda i: i,
        in_specs=[
            pl.BlockSpec((tm, D_MODEL), lambda i: (i, 0)),
            pl.BlockSpec((1, D_MODEL), row),
            pl.BlockSpec((1, D_MODEL), row),
            pl.BlockSpec((1, D_MODEL), row),
            pl.BlockSpec((D_MODEL, CD_IN), row, pipeline_mode=pl.Buffered(1)),
            pl.BlockSpec((1, C_WIDTH), row),
        ],
        out_specs=[
            pl.BlockSpec((tm, 2 * C_WIDTH), lambda i: (i, 0)),
            pl.BlockSpec((tm, D_WIDTH), lambda i: (i, 0)),
        ],
        out_shape=[jax.ShapeDtypeStruct((n, 2 * C_WIDTH), BF16),
                   jax.ShapeDtypeStruct((n, D_WIDTH), F32)],
        name="cd_in_proj",
        args=(x2, g, shift, scale, w_in_bf, v_norm_g))


def _spatial_kernel(u_ref, v_ref, ws_ref, bias_ref, o_ref, *, chunks):
    for k in range(chunks):
        rs = slice(k * CHUNK, (k + 1) * CHUNK)
        for g in range(C_GROUPS):
            cs = slice(g * C_GROUP_DIM, (g + 1) * C_GROUP_DIM)
            s = _dot(ws_ref[g], v_ref[rs, cs]) + bias_ref[:, cs]
            o_ref[rs, cs] = (u_ref[rs, cs].astype(F32) * s).astype(BF16)


def _spatial_gate(uv, w_spatial_bf, bias_full, chunks=4):
    n = uv.shape[0]
    rows = chunks * CHUNK
    return pl.pallas_call(
        functools.partial(_spatial_kernel, chunks=chunks),
        grid=(n // rows,),
        in_specs=[
            pl.BlockSpec((rows, C_WIDTH), lambda i: (i, 0)),
            pl.BlockSpec((rows, C_WIDTH), lambda i: (i, 1)),
            pl.BlockSpec((C_GROUPS, CHUNK, CHUNK), lambda i: (0, 0, 0)),
            pl.BlockSpec((CHUNK, C_WIDTH), lambda i: (0, 0)),
        ],
        out_specs=pl.BlockSpec((rows, C_WIDTH), lambda i: (i, 0)),
        out_shape=jax.ShapeDtypeStruct((n, C_WIDTH), BF16),
        compiler_params=_params(("arbitrary",)),
        name="spatial_gate",
    )(uv, uv, w_spatial_bf, bias_full)


FFT_SUB = V7X_SUBLANES_F32
FFT1_CT = 1024
FFT2_CT = 512


def _fourier_tables(n):
    a_len, b_len, sub = FFT_A, FFT_B, FFT_SUB
    assert a_len * b_len == n
    ch = np.arange(D_GROUP_DIM)
    ang_c = 2.0 * np.pi * ((ch[:, None] * ch[None, :]) % D_GROUP_DIM) / D_GROUP_DIM
    a = np.arange(a_len)
    f_a = np.exp(-2j * np.pi * ((a[:, None] * a[None, :]) % a_len) / a_len)
    m1 = np.kron(f_a, np.eye(sub))
    b = np.arange(b_len)
    tw = np.exp(-2j * np.pi * ((a[:, None] * b[None, :]) % n) / n)
    f_b = np.exp(-2j * np.pi * ((b[:, None] * b[None, :]) % b_len) / b_len)
    m2 = np.einsum('db,pq->dpqb', f_b, np.eye(sub)).reshape(b_len * sub, sub * b_len)
    norm = 1.0 / np.sqrt(float(n) * D_GROUP_DIM)
    m2 = m2 * norm
    tw3 = np.broadcast_to(tw[:, :, None], (a_len, b_len, V7X_LANES))
    f32 = lambda v: jnp.asarray(np.ascontiguousarray(v), dtype=F32)
    return dict(cos_c=f32(np.cos(ang_c)), sin_c=f32(np.sin(ang_c)),
                m1r=f32(m1.real), m1i=f32(m1.imag), m2r=f32(m2.real), m2i=f32(m2.imag),
                twr=f32(tw3.real), twi=f32(tw3.imag))


def _fft1_kernel(f_ref, cc_ref, sc_ref, m1r_ref, m1i_ref, twr_ref, twi_ref, tr_ref, ti_ref):
    rows = FFT_A * FFT_SUB
    ct = FFT1_CT
    fb = f_ref[...].reshape(rows, ct).astype(BF16)
    xr_parts = []
    xi_parts = []
    for q in range(ct // D_GROUP_DIM):
        blk = fb[:, q * D_GROUP_DIM:(q + 1) * D_GROUP_DIM]
        xr_parts.append(_dot(blk, cc_ref[...]))
        xi_parts.append(-_dot(blk, sc_ref[...]))
    xr = jnp.concatenate(xr_parts, axis=1).astype(BF16)
    xi = jnp.concatenate(xi_parts, axis=1).astype(BF16)
    m1r = m1r_ref[...]
    m1i = m1i_ref[...]
    tr = _dot(m1r, xr) - _dot(m1i, xi)
    ti = _dot(m1r, xi) + _dot(m1i, xr)
    reps = ct // V7X_LANES
    twr = jnp.tile(twr_ref[...].reshape(rows, V7X_LANES), (1, reps))
    twi = jnp.tile(twi_ref[...].reshape(rows, V7X_LANES), (1, reps))
    tr_ref[...] = (tr * twr - ti * twi).reshape(FFT_A, FFT_SUB, ct)
    ti_ref[...] = (tr * twi + ti * twr).reshape(FFT_A, FFT_SUB, ct)


def _fft2_kernel(tr_ref, ti_ref, m2r_ref, m2i_ref, wf_ref, o_ref):
    j = pl.program_id(1)
    tr = tr_ref[...].astype(BF16)
    ti = ti_ref[...].astype(BF16)
    z = _dot(m2r_ref[...], tr) - _dot(m2i_ref[...], ti)
    contrib = _dot(z.astype(BF16), wf_ref[...]).reshape(FFT_B, FFT_SUB, D_WIDTH)

    @pl.when(j == 0)
    def _():
        o_ref[...] = contrib

    @pl.when(j > 0)
    def _():
        o_ref[...] += contrib


def _fourier_mix(f, tabs, w_fourier_bf):
    n = f.shape[0]
    a_len, b_len, sub = FFT_A, FFT_B, FFT_SUB
    f3 = f.reshape(a_len, b_len, D_WIDTH)
    rows1 = a_len * sub
    const2 = lambda i, j: (0, 0)
    tr, ti = pl.pallas_call(
        _fft1_kernel,
        grid=(b_len // sub, D_WIDTH // FFT1_CT),
        in_specs=[
            pl.BlockSpec((a_len, sub, FFT1_CT), lambda i, j: (0, i, j)),
            pl.BlockSpec((D_GROUP_DIM, D_GROUP_DIM), const2),
            pl.BlockSpec((D_GROUP_DIM, D_GROUP_DIM), const2),
            pl.BlockSpec((rows1, rows1), const2),
            pl.BlockSpec((rows1, rows1), const2),
            pl.BlockSpec((a_len, sub, V7X_LANES), lambda i, j: (0, i, 0)),
            pl.BlockSpec((a_len, sub, V7X_LANES), lambda i, j: (0, i, 0)),
        ],
        out_specs=[pl.BlockSpec((a_len, sub, FFT1_CT), lambda i, j: (0, i, j))] * 2,
        out_shape=[jax.ShapeDtypeStruct((a_len, b_len, D_WIDTH), F32)] * 2,
        compiler_params=_params(("arbitrary", "arbitrary")),
        name="fourier_stage1",
    )(f3, tabs['cos_c'].astype(BF16), tabs['sin_c'].astype(BF16),
      tabs['m1r'].astype(BF16), tabs['m1i'].astype(BF16), tabs['twr'], tabs['twi'])

    rows2 = sub * b_len
    tr2 = tr.reshape(n, D_WIDTH)
    ti2 = ti.reshape(n, D_WIDTH)
    out = pl.pallas_call(
        _fft2_kernel,
        grid=(a_len // sub, D_WIDTH // FFT2_CT),
        in_specs=[
            pl.BlockSpec((rows2, FFT2_CT), lambda i, j: (i, j)),
            pl.BlockSpec((rows2, FFT2_CT), lambda i, j: (i, j)),
            pl.BlockSpec((rows2, rows2), const2),
            pl.BlockSpec((rows2, rows2), const2),
            pl.BlockSpec((FFT2_CT, D_WIDTH), lambda i, j: (j, 0)),
        ],
        out_specs=pl.BlockSpec((b_len, sub, D_WIDTH), lambda i, j: (0, i, 0)),
        out_shape=jax.ShapeDtypeStruct((b_len, a_len, D_WIDTH), F32),
        compiler_params=_params(("arbitrary", "arbitrary")),
        name="fourier_stage2",
    )(tr2, ti2, tabs['m2r'].astype(BF16), tabs['m2i'].astype(BF16), w_fourier_bf)
    return out.reshape(n, D_WIDTH)


def _rope_tables(n):
    rows = n // GRID_W
    row = np.repeat(np.arange(rows, dtype=np.float64), GRID_W)
    col = np.tile(np.arange(GRID_W, dtype=np.float64), rows)
    inv = ROPE_THETA ** (-np.arange(0, AXIS_DIM, 2, dtype=np.float64) / AXIS_DIM)
    ang_r = row[:, None] * inv[None, :]
    ang_c = col[:, None] * inv[None, :]
    cos = np.concatenate([np.cos(ang_r)] * 2 + [np.cos(ang_c)] * 2, axis=-1)
    sin = np.concatenate([-np.sin(ang_r), np.sin(ang_r), -np.sin(ang_c), np.sin(ang_c)], axis=-1)
    return jnp.asarray(cos, dtype=F32), jnp.asarray(sin, dtype=F32)


def kernel(x, c, ctx, c_ctx, w_mod, b_mod, norm1_g, norm2_g, ab_w_in, a_q_norm_g, a_k_norm_g, a_sink,
           b_w_pool, b_pool_scale, ab_w_out, cd_w_in, c_v_norm_g, c_w_spatial, c_b_spatial, d_w_fourier,
           cd_w_out, f_w_up, f_conv_w, f_conv_b, f_w_down):
    batch, n, _ = x.shape
    ctx_len = ctx.shape[1]
    assert batch == 1 and DEPTH == 2
    x2 = x.reshape(n, D_MODEL)
    ctx2 = ctx.reshape(ctx_len, D_MODEL)

    mod = _mod_vectors(c, c_ctx, w_mod, b_mod)

    def split6(v):
        return [v[:, k * D_MODEL:(k + 1) * D_MODEL] for k in range(6)]

    row1 = lambda v: v.reshape(1, -1)
    n_tiles = n // PROJ_TM
    attn_steps = n // ATTN_TQ
    up_steps = (n // FFN_UP_TM) * (FFN_UP_STEPS + 1)
    down_steps = (n // FFN_DOWN_TM) * FFN_DOWN_STEPS
    cast_down0 = _CastJob(f_w_down, 0, D_FF // n_tiles, n_tiles, col_block=FFN_TN)
    cast_up0 = _CastJob(f_w_up, 0, D_MODEL // attn_steps, attn_steps, col_block=FFN_TF)
    cast_up1 = _CastJob(f_w_up, 1, D_MODEL // attn_steps, up_steps, col_block=FFN_TF)
    cast_down1 = _CastJob(f_w_down, 1, D_FF // down_steps, down_steps, col_block=FFN_TN)

    ml = split6(mod[0, 0:1])
    mc = split6(mod[0, 1:2])
    g1 = row1(norm1_g[0])
    w_in = ab_w_in[0].astype(BF16)
    qn = row1(a_q_norm_g[0])
    kn = row1(a_k_norm_g[0])
    cos, sin = _rope_tables(n)
    qkv, z, w_down0 = _ab_in_proj(x2, g1, ml[0], ml[1], w_in, cos, sin, qn, kn, tm=PROJ_TM,
                                  cast=cast_down0)
    ones = jnp.ones((ctx_len, HEAD_DIM), F32)
    zeros = jnp.zeros((ctx_len, HEAD_DIM), F32)
    qkv_ctx, _ = _ab_in_proj(ctx2, g1, mc[0], mc[1], w_in, ones, zeros, qn, kn, tm=ctx_len)
    attn, w_up0 = _window_attention(qkv, qkv_ctx, a_sink[0], cast=cast_up0)
    pool_tiles = n // POOL_TM
    pooled, w_out0 = _pool_mix(z, b_w_pool[0].astype(BF16), row1(b_pool_scale[0]), tm=POOL_TM,
                               cast=_CastJob(ab_w_out, 0, D_MODEL // pool_tiles, pool_tiles))
    x2, h2, cd_w_in_bf = _out_proj(attn, pooled, w_out0, x2, ml[2], row1(norm2_g[0]), ml[3], ml[4],
                                   tm=PROJ_TM, cast=_CastJob(cd_w_in, 0, D_MODEL // n_tiles, n_tiles))
    x2, (w_up1,), (w_down1,) = _conv_ffn(x2, h2, w_up0, f_conv_w[0], f_conv_b[0], w_down0, ml[5],
                                         tm=FFN_UP_TM, down_tm=FFN_DOWN_TM,
                                         up_cast=cast_up1, down_cast=cast_down1)

    ml = split6(mod[1, 0:1])
    uv, f, w_out1 = _cd_in_proj(x2, row1(norm1_g[1]), ml[0], ml[1], cd_w_in_bf, row1(c_v_norm_g[0]),
                                tm=PROJ_TM, cast=_CastJob(cd_w_out, 0, D_MODEL // n_tiles, n_tiles))
    bias_full = jnp.repeat(c_b_spatial[0].T, C_GROUP_DIM, axis=1)
    c_out = _spatial_gate(uv, c_w_spatial[0].astype(BF16), bias_full)
    d_out = _fourier_mix(f, _fourier_tables(n), d_w_fourier[0].astype(BF16))
    x2, h2 = _out_proj(c_out, d_out, w_out1, x2, ml[2], row1(norm2_g[1]), ml[3], ml[4], tm=PROJ_TM)
    x2, _, _ = _conv_ffn(x2, h2, w_up1, f_conv_w[1], f_conv_b[1], w_down1, ml[5],
                         tm=FFN_UP_TM, down_tm=FFN_DOWN_TM)
    return x2.reshape(batch, n, D_MODEL)
```

```python
import functools

import numpy as np
import jax
import jax.numpy as jnp
from jax import lax
from jax.experimental import pallas as pl
from jax.experimental.pallas import tpu as pltpu

F32 = jnp.float32
BF16 = jnp.bfloat16

D_MODEL = 2048
DEPTH = 2
GRID_W = 64
HEAD_DIM = 128
A_Q_HEADS = 8
A_KV_HEADS = 2
A_GROUP = A_Q_HEADS // A_KV_HEADS
A_Q_DIM = A_Q_HEADS * HEAD_DIM
A_KV_DIM = A_KV_HEADS * HEAD_DIM
A_QKV_DIM = A_Q_DIM + 2 * A_KV_DIM
WINDOW = 128
BLOCK = 128
ROPE_THETA = 10000.0
AXIS_DIM = HEAD_DIM // 2
ATTN_SCALE = HEAD_DIM ** -0.5
NEG_INF = -1e30
B_GROUPS = 4
B_WIDTH = 1024
B_GROUP_DIM = B_WIDTH // B_GROUPS
POOL_WINDOWS = (2, 4, 8, 16)
AB_IN = A_QKV_DIM + B_WIDTH
C_WIDTH = 1024
C_GROUPS = 4
C_GROUP_DIM = C_WIDTH // C_GROUPS
CHUNK = 128
D_WIDTH = 1024
D_GROUPS = 8
D_GROUP_DIM = D_WIDTH // D_GROUPS
CD_IN = 2 * C_WIDTH + D_WIDTH
D_FF = 5632
EPS = 1e-6

V7X_SUBLANES_F32 = 8
V7X_SUBLANES_BF16 = 16
V7X_LANES = 128
V7X_VMEM_BYTES = 64 * 1024 * 1024
VMEM_LIMIT = 56 * 1024 * 1024

PROJ_TM = 512
POOL_TM = 512
ATTN_TQ = 128
FFN_UP_TM = 1024
FFN_DOWN_TM = 1024

FFT_A = 64
FFT_B = 128


def _params(sem):
    return pltpu.CompilerParams(dimension_semantics=sem, vmem_limit_bytes=VMEM_LIMIT)


def _dot(a, b):
    return jnp.dot(a, b, preferred_element_type=F32)


class _CastJob:
    def __init__(self, src, layer, rows, n_steps, col_block=None):
        _, total_rows, self.cols = src.shape
        assert total_rows % rows == 0 and total_rows // rows <= n_steps
        self.src, self.layer, self.rows, self.col_block = src, layer, rows, col_block
        self.last = total_rows // rows - 1
        if col_block is None:
            self.out_shape = jax.ShapeDtypeStruct((total_rows, self.cols), BF16)
        else:
            assert self.cols % col_block == 0
            self.out_shape = jax.ShapeDtypeStruct((self.cols // col_block, total_rows, col_block), BF16)

    def specs(self, step_of):
        blk = lambda *ids: jnp.minimum(step_of(*ids), self.last)
        src = pl.BlockSpec((None, self.rows, self.cols), lambda *ids: (self.layer, blk(*ids), 0))
        if self.col_block is None:
            return src, pl.BlockSpec((self.rows, self.cols), lambda *ids: (blk(*ids), 0))
        return src, pl.BlockSpec((self.cols // self.col_block, self.rows, self.col_block),
                                 lambda *ids: (0, blk(*ids), 0))

    def run(self, src_ref, dst_ref):
        if self.col_block is None:
            dst_ref[...] = src_ref[...].astype(BF16)
        else:
            for b in range(self.cols // self.col_block):
                dst_ref[b] = src_ref[:, b * self.col_block:(b + 1) * self.col_block].astype(BF16)


def _host_cast(kernel_fn, cast, n_in, n_out):
    def body(*refs):
        cast.run(refs[n_in], refs[n_in + 1 + n_out])
        kernel_fn(*refs[:n_in], *refs[n_in + 1:n_in + 1 + n_out], *refs[n_in + 2 + n_out:])
    return body


def _pallas(kernel_fn, *, grid, in_specs, out_specs, out_shape, args, name, scratch_shapes=(),
            cast=None, step_of=None):
    out_specs, out_shape = list(out_specs), list(out_shape)
    if cast is not None:
        src_spec, dst_spec = cast.specs(step_of)
        kernel_fn = _host_cast(kernel_fn, cast, len(in_specs), len(out_specs))
        in_specs = [*in_specs, src_spec]
        out_specs.append(dst_spec)
        out_shape.append(cast.out_shape)
        args = (*args, cast.src)
    return pl.pallas_call(
        kernel_fn, grid=grid, in_specs=list(in_specs), out_specs=out_specs, out_shape=out_shape,
        scratch_shapes=list(scratch_shapes), compiler_params=_params(("arbitrary",) * len(grid)),
        name=name)(*args)


def _norm_mod(x, g, shift, scale):
    ms = jnp.mean(x * x, axis=-1, keepdims=True)
    return x * lax.rsqrt(ms + EPS) * (g * (1.0 + scale)) + shift


MOD_TK = 256
MOD_STREAMS = 2


def _mod_kernel(cv_ref, w_ref, b_ref, o_ref, acc_scr):
    k = pl.program_id(1)
    sub = V7X_SUBLANES_F32
    groups = MOD_TK // sub
    n_out = 6 * D_MODEL
    lane_tiles = n_out // V7X_LANES

    @pl.when(k == 0)
    def _():
        acc_scr[...] = jnp.zeros_like(acc_scr)

    w = w_ref[0].reshape(groups, sub, n_out)
    for s in range(MOD_STREAMS):
        a = cv_ref[s]
        a = (a * jax.nn.sigmoid(a)).reshape(groups, sub, V7X_LANES)
        a = jnp.concatenate([a] * lane_tiles, axis=-1)
        acc_scr[s] += jnp.sum(w * a, axis=0)

    @pl.when(k == pl.num_programs(1) - 1)
    def _():
        rows = [jnp.sum(acc_scr[s], axis=0, keepdims=True) for s in range(MOD_STREAMS)]
        rows.append(jnp.zeros((sub - MOD_STREAMS, n_out), F32))
        o_ref[0] = jnp.concatenate(rows, axis=0) + b_ref[0]


def _mod_vectors(c, c_ctx, w_mod, b_mod):
    n_out = 6 * D_MODEL
    cv = jnp.stack([c.reshape(D_MODEL), c_ctx.reshape(D_MODEL)])
    cv = jnp.broadcast_to(cv[:, :, None], (MOD_STREAMS, D_MODEL, V7X_LANES))
    b3 = b_mod.reshape(DEPTH, 1, n_out)
    return pl.pallas_call(
        _mod_kernel,
        grid=(DEPTH, D_MODEL // MOD_TK),
        in_specs=[
            pl.BlockSpec((MOD_STREAMS, MOD_TK, V7X_LANES), lambda l, k: (0, k, 0)),
            pl.BlockSpec((1, MOD_TK, n_out), lambda l, k: (l, k, 0)),
            pl.BlockSpec((1, 1, n_out), lambda l, k: (l, 0, 0)),
        ],
        out_specs=pl.BlockSpec((1, V7X_SUBLANES_F32, n_out), lambda l, k: (l, 0, 0)),
        out_shape=jax.ShapeDtypeStruct((DEPTH, V7X_SUBLANES_F32, n_out), F32),
        scratch_shapes=[pltpu.VMEM((MOD_STREAMS, V7X_SUBLANES_F32, n_out), F32)],
        compiler_params=_params(("arbitrary", "arbitrary")),
        name="mod_vectors",
    )(cv, w_mod, b3)


AB_TN = 512
ROW_SPLIT = 2


def _rope(t, cos, sin_signed):
    lane = lax.broadcasted_iota(jnp.int32, t.shape, 1)
    first = (lane % AXIS_DIM) < (AXIS_DIM // 2)
    partner = jnp.where(first,
                        pltpu.roll(t, HEAD_DIM - AXIS_DIM // 2, 1),
                        pltpu.roll(t, AXIS_DIM // 2, 1))
    return t * cos + partner * sin_signed


def _head_norm_rope(t, g, cos, sin_signed):
    ms = jnp.mean(t * t, axis=-1, keepdims=True)
    return _rope(t * lax.rsqrt(ms + EPS) * g, cos, sin_signed)


def _ab_in_kernel(x_ref, g_ref, sh_ref, sc_ref, w_ref, cos_ref, sin_ref, qn_ref, kn_ref,
                  qkv_ref, z_ref, *, tm):
    rows_per = tm // ROW_SPLIT
    n_q_tiles = A_Q_DIM // AB_TN
    for part in range(ROW_SPLIT):
        rs = slice(part * rows_per, (part + 1) * rows_per)
        h = _norm_mod(x_ref[rs, :], g_ref[...], sh_ref[...], sc_ref[...]).astype(BF16)
        cos = cos_ref[rs, :]
        sin = sin_ref[rs, :]
        for t in range(AB_IN // AB_TN):
            p = _dot(h, w_ref[:, t * AB_TN:(t + 1) * AB_TN])
            if t < n_q_tiles:
                for hh in range(AB_TN // HEAD_DIM):
                    c0 = t * AB_TN + hh * HEAD_DIM
                    qkv_ref[rs, c0:c0 + HEAD_DIM] = _head_norm_rope(
                        p[:, hh * HEAD_DIM:(hh + 1) * HEAD_DIM], qn_ref[...] * ATTN_SCALE,
                        cos, sin).astype(BF16)
            elif t == n_q_tiles:
                for hh in range(A_KV_HEADS):
                    c0 = A_Q_DIM + hh * HEAD_DIM
                    qkv_ref[rs, c0:c0 + HEAD_DIM] = _head_norm_rope(
                        p[:, hh * HEAD_DIM:(hh + 1) * HEAD_DIM], kn_ref[...], cos, sin).astype(BF16)
                qkv_ref[rs, A_Q_DIM + A_KV_DIM:] = p[:, A_KV_DIM:].astype(BF16)
            else:
                c0 = (t - n_q_tiles - 1) * AB_TN
                z_ref[rs, c0:c0 + AB_TN] = p


def _ab_in_proj(x2, g, shift, scale, w_in_bf, cos, sin, qn, kn, tm, cast=None):
    n = x2.shape[0]
    assert A_Q_DIM % AB_TN == 0 and 2 * A_KV_DIM == AB_TN and B_WIDTH % AB_TN == 0
    row = lambda i: (0, 0)
    return _pallas(
        functools.partial(_ab_in_kernel, tm=tm),
        grid=(n // tm,),
        cast=cast, step_of=lambda i: i,
        in_specs=[
            pl.BlockSpec((tm, D_MODEL), lambda i: (i, 0)),
            pl.BlockSpec((1, D_MODEL), row),
            pl.BlockSpec((1, D_MODEL), row),
            pl.BlockSpec((1, D_MODEL), row),
            pl.BlockSpec((D_MODEL, AB_IN), row, pipeline_mode=pl.Buffered(1)),
            pl.BlockSpec((tm, HEAD_DIM), lambda i: (i, 0)),
            pl.BlockSpec((tm, HEAD_DIM), lambda i: (i, 0)),
            pl.BlockSpec((1, HEAD_DIM), row),
            pl.BlockSpec((1, HEAD_DIM), row),
        ],
        out_specs=[
            pl.BlockSpec((tm, A_QKV_DIM), lambda i: (i, 0)),
            pl.BlockSpec((tm, B_WIDTH), lambda i: (i, 0)),
        ],
        out_shape=[jax.ShapeDtypeStruct((n, A_QKV_DIM), BF16),
                   jax.ShapeDtypeStruct((n, B_WIDTH), F32)],
        name="ab_in_proj",
        args=(x2, g, shift, scale, w_in_bf, cos, sin, qn, kn))


def _attn_kernel(sink_ref, bias_ref, q_ref, kp_ref, kc_ref, kn_ref, vp_ref, vc_ref, vn_ref,
                 kx_ref, vx_ref, o_ref):
    rows = A_GROUP * ATTN_TQ
    band_cols = ATTN_TQ + 2 * BLOCK
    n_keys = band_cols + kx_ref.shape[0]
    bias = jnp.concatenate([bias_ref[...]] * A_GROUP, axis=0)
    r1 = lax.broadcasted_iota(jnp.int32, (rows, 1), 0) // ATTN_TQ
    ones_col = (lax.broadcasted_iota(jnp.int32, (n_keys, HEAD_DIM), 1) == 0).astype(BF16)
    for hk in range(A_KV_HEADS):
        hs = slice(hk * HEAD_DIM, (hk + 1) * HEAD_DIM)
        kcat = jnp.concatenate([kp_ref[:, hs], kc_ref[:, hs], kn_ref[:, hs], kx_ref[:, hs]], axis=0)
        vcat = jnp.concatenate([vp_ref[:, hs], vc_ref[:, hs], vn_ref[:, hs], vx_ref[:, hs]], axis=0)
        q0 = hk * A_GROUP * HEAD_DIM
        q4 = jnp.concatenate(
            [q_ref[:, q0 + g * HEAD_DIM:q0 + (g + 1) * HEAD_DIM] for g in range(A_GROUP)], axis=0)
        s = lax.dot_general(q4, kcat, (((1,), (1,)), ((), ())), preferred_element_type=F32)
        s = jnp.concatenate([s[:, :band_cols] + bias, s[:, band_cols:]], axis=1)
        sink = jnp.zeros((rows, 1), F32)
        for g in range(A_GROUP):
            sink = jnp.where(r1 == g, sink_ref[hk * A_GROUP + g], sink)
        m = jnp.maximum(jnp.max(s, axis=-1, keepdims=True), sink)
        e = jnp.exp(s - m).astype(BF16)
        pv = _dot(e, jnp.concatenate([vcat, ones_col], axis=1))
        den = pv[:, HEAD_DIM:HEAD_DIM + 1] + jnp.exp(sink - m)
        o4 = pv[:, :HEAD_DIM] / den
        for g in range(A_GROUP):
            o_ref[:, q0 + g * HEAD_DIM:q0 + (g + 1) * HEAD_DIM] = (
                o4[g * ATTN_TQ:(g + 1) * ATTN_TQ].astype(BF16))


def _attn_bias(ctx_len):
    del ctx_len
    band_cols = ATTN_TQ + 2 * BLOCK
    r = np.arange(ATTN_TQ)[:, None]
    c = np.arange(band_cols)[None, :]
    band = np.abs(BLOCK + r - c) <= WINDOW
    first = band & (c >= BLOCK)
    last = band & (c < BLOCK + ATTN_TQ)
    masks = np.stack([first, band, last])
    return jnp.asarray(np.where(masks, 0.0, NEG_INF), dtype=F32)


def _window_attention(qkv, qkv_ctx, sink, cast=None):
    n = qkv.shape[0]
    ctx_len = qkv_ctx.shape[0]
    n_tiles = n // ATTN_TQ
    n_blocks = n // BLOCK
    per_tile = ATTN_TQ // BLOCK
    assert n_tiles >= 2 and WINDOW <= BLOCK
    k_col = A_Q_DIM // A_KV_DIM
    v_col = k_col + 1
    prev = lambda i: jnp.maximum(i * per_tile - 1, 0)
    nxt = lambda i: jnp.minimum((i + 1) * per_tile, n_blocks - 1)
    which = lambda i: jnp.where(i == 0, 0, jnp.where(i == n_tiles - 1, 2, 1))
    blk = (BLOCK, A_KV_DIM)
    cur = (ATTN_TQ, A_KV_DIM)
    cols = ATTN_TQ + 2 * BLOCK
    return _pallas(
        _attn_kernel,
        grid=(n_tiles,),
        cast=cast, step_of=lambda i: i,
        in_specs=[
            pl.BlockSpec(memory_space=pltpu.SMEM),
            pl.BlockSpec((None, ATTN_TQ, cols), lambda i: (which(i), 0, 0)),
            pl.BlockSpec((ATTN_TQ, A_Q_DIM), lambda i: (i, 0)),
            pl.BlockSpec(blk, lambda i: (prev(i), k_col)),
            pl.BlockSpec(cur, lambda i: (i, k_col)),
            pl.BlockSpec(blk, lambda i: (nxt(i), k_col)),
            pl.BlockSpec(blk, lambda i: (prev(i), v_col)),
            pl.BlockSpec(cur, lambda i: (i, v_col)),
            pl.BlockSpec(blk, lambda i: (nxt(i), v_col)),
            pl.BlockSpec((ctx_len, A_KV_DIM), lambda i: (0, k_col)),
            pl.BlockSpec((ctx_len, A_KV_DIM), lambda i: (0, v_col)),
        ],
        out_specs=[pl.BlockSpec((ATTN_TQ, A_Q_DIM), lambda i: (i, 0))],
        out_shape=[jax.ShapeDtypeStruct((n, A_Q_DIM), BF16)],
        name="window_attention",
        args=(sink, _attn_bias(ctx_len), qkv, qkv, qkv, qkv, qkv, qkv, qkv, qkv_ctx, qkv_ctx))


POOL_HALO = 8
assert max(POOL_WINDOWS) // 2 <= POOL_HALO


def _pool_kernel(zm_ref, zp_ref, zn_ref, w_ref, ps_ref, o_ref, z_scr, *, n_rows, tm):
    i = pl.program_id(0)
    last = pl.num_programs(0) - 1
    pad = POOL_HALO
    span_rows = tm + 2 * pad
    z_scr[0:pad, :] = jnp.zeros((pad, B_WIDTH), F32)
    z_scr[pad:2 * pad, :] = jnp.where(i > 0, zp_ref[...], 0.0)
    z_scr[2 * pad:2 * pad + tm, :] = zm_ref[...]
    z_scr[2 * pad + tm:3 * pad + tm, :] = jnp.where(i < last, zn_ref[...], 0.0)
    z_scr[3 * pad + tm:, :] = jnp.zeros((pad, B_WIDTH), F32)
    t = i * tm + lax.broadcasted_iota(jnp.int32, (tm, B_GROUP_DIM), 0)
    for g in range(B_GROUPS):
        window = POOL_WINDOWS[g]
        half = window // 2
        cs = slice(g * B_GROUP_DIM, (g + 1) * B_GROUP_DIM)
        s = z_scr[pl.ds(2 * pad - half, span_rows), cs]
        width = 1
        while width < window:
            s = s + pltpu.roll(s, span_rows - width, 0)
            width *= 2
        acc = s[0:tm]
        cnt = (jnp.minimum(t + half, n_rows) - jnp.maximum(t - half, 0)).astype(F32)
        d = (acc / cnt - zm_ref[:, cs]).astype(BF16)
        y = _dot(d, w_ref[g]) * ps_ref[:, cs]
        o_ref[:, cs] = y.astype(BF16)


def _pool_mix(z, w_pool_bf, pool_scale, tm, cast=None):
    n = z.shape[0]
    hb = tm // POOL_HALO
    n_halo_blocks = n // POOL_HALO
    return _pallas(
        functools.partial(_pool_kernel, n_rows=n, tm=tm),
        grid=(n // tm,),
        cast=cast, step_of=lambda i: i,
        in_specs=[
            pl.BlockSpec((tm, B_WIDTH), lambda i: (i, 0)),
            pl.BlockSpec((POOL_HALO, B_WIDTH), lambda i: (jnp.maximum(i * hb - 1, 0), 0)),
            pl.BlockSpec((POOL_HALO, B_WIDTH), lambda i: (jnp.minimum((i + 1) * hb, n_halo_blocks - 1), 0)),
            pl.BlockSpec((B_GROUPS, B_GROUP_DIM, B_GROUP_DIM), lambda i: (0, 0, 0)),
            pl.BlockSpec((1, B_WIDTH), lambda i: (0, 0)),
        ],
        out_specs=[pl.BlockSpec((tm, B_WIDTH), lambda i: (i, 0))],
        out_shape=[jax.ShapeDtypeStruct((n, B_WIDTH), BF16)],
        scratch_shapes=[pltpu.VMEM((tm + 4 * POOL_HALO, B_WIDTH), F32)],
        name="pool_mix",
        args=(z, z, z, w_pool_bf, pool_scale))


OUT_TN = 512


def _out_proj_kernel(a1_ref, a2_ref, w_ref, x_ref, gate_ref, g_ref, sh_ref, sc_ref, o_ref, h_ref, *, tm):
    rows_per = tm // ROW_SPLIT
    for part in range(ROW_SPLIT):
        rs = slice(part * rows_per, (part + 1) * rows_per)
        a = jnp.concatenate([a1_ref[rs, :].astype(BF16), a2_ref[rs, :].astype(BF16)], axis=1)
        for t in range(D_MODEL // OUT_TN):
            cs = slice(t * OUT_TN, (t + 1) * OUT_TN)
            o_ref[rs, cs] = x_ref[rs, cs] + gate_ref[:, cs] * _dot(a, w_ref[:, cs])
        h_ref[rs, :] = _norm_mod(o_ref[rs, :], g_ref[...], sh_ref[...], sc_ref[...])


def _out_proj(a1, a2, w_out_bf, x2, gate, g2, shift2, scale2, tm, cast=None):
    n = x2.shape[0]
    k1 = a1.shape[1]
    k2 = a2.shape[1]
    assert k1 + k2 == w_out_bf.shape[0] and k1 % V7X_LANES == 0
    row = lambda i: (0, 0)
    return _pallas(
        functools.partial(_out_proj_kernel, tm=tm),
        grid=(n // tm,),
        cast=cast, step_of=lambda i: i,
        in_specs=[
            pl.BlockSpec((tm, k1), lambda i: (i, 0)),
            pl.BlockSpec((tm, k2), lambda i: (i, 0)),
            pl.BlockSpec((k1 + k2, D_MODEL), row, pipeline_mode=pl.Buffered(1)),
            pl.BlockSpec((tm, D_MODEL), lambda i: (i, 0)),
            pl.BlockSpec((1, D_MODEL), row),
            pl.BlockSpec((1, D_MODEL), row),
            pl.BlockSpec((1, D_MODEL), row),
            pl.BlockSpec((1, D_MODEL), row),
        ],
        out_specs=[pl.BlockSpec((tm, D_MODEL), lambda i: (i, 0)),
                   pl.BlockSpec((tm, D_MODEL), lambda i: (i, 0))],
        out_shape=[jax.ShapeDtypeStruct((n, D_MODEL), F32),
                   jax.ShapeDtypeStruct((n, D_MODEL), F32)],
        name="out_proj",
        args=(a1, a2, w_out_bf, x2, gate, g2, shift2, scale2))


FFN_SLABS = V7X_SUBLANES_F32
FFN_EDGE_ROWS = V7X_SUBLANES_BF16
FFN_PIECE_ROWS = 16
FFN_ELEMENTWISE_DTYPE = BF16
FFN_CW = 256
FFN_TF = 2 * FFN_CW
FFN_TN = 512
FFN_CHUNKS = D_FF // FFN_CW
FFN_UP_STEPS = D_FF // FFN_TF
FFN_DOWN_STEPS = D_MODEL // FFN_TN


def _ffn_up_kernel(hm_ref, hp_ref, hn_ref, wg_ref, wv_ref, cw_ref, cb_ref, o_ref,
                   h_scr, carry_a, carry_b, *, tm):
    i = pl.program_id(0)
    j = pl.program_id(1)
    last_i = pl.num_programs(0) - 1
    nc = FFN_CHUNKS
    ns = FFN_SLABS
    sr = tm // ns

    def finish(carry, k, r, step):
        c = jnp.maximum(2 * step + k, 0)

        def conv(idx, kk, q0):
            slab = lambda s: carry[idx, pl.ds(s * sr + q0, FFN_PIECE_ROWS), :]
            prev = slab(r - 1) if r > 0 else slab(ns)
            nxt = slab(r + 1) if r < ns - 1 else slab(ns + 1)
            return prev * cw_ref[kk, 0] + slab(r) * cw_ref[kk, 1] + nxt * cw_ref[kk, 2] + cb_ref[kk]

        for q0 in range(0, sr, FFN_PIECE_ROWS):
            gg = conv(k, c, q0)
            vv = conv(2 + k, c + nc, q0)
            o_ref[pl.ds(r * sr + q0, FFN_PIECE_ROWS), k * FFN_CW:(k + 1) * FFN_CW] = (
                gg * jax.nn.sigmoid(gg) * vv).astype(BF16)

    @pl.when(j == 0)
    def _():
        hm = hm_ref[...].reshape(sr, ns, D_MODEL)
        h_scr[0:tm, :] = jnp.swapaxes(hm, 0, 1).reshape(tm, D_MODEL).astype(BF16)
        before = jnp.where(i > 0, hp_ref[ns - 1:ns, :], 0.0)
        after = jnp.where(i < last_i, hn_ref[0:1, :], 0.0)
        pad = jnp.zeros((FFN_EDGE_ROWS - 2, D_MODEL), F32)
        h_scr[tm:, :] = jnp.concatenate([before, after, pad], axis=0).astype(BF16)
        carry_b[...] = jnp.zeros_like(carry_b)

    def keep(carry, idx, u):
        dt = FFN_ELEMENTWISE_DTYPE
        carry[idx, 0:tm, :] = u[0:tm].astype(dt)
        carry[idx, tm:tm + sr, :] = jnp.concatenate(
            [u[tm:tm + 1], u[(ns - 1) * sr:ns * sr - 1]], axis=0).astype(dt)
        carry[idx, tm + sr:, :] = jnp.concatenate([u[1:sr], u[tm + 1:tm + 2]], axis=0).astype(dt)

    def up_step(write, read):
        for k in range(2):
            for r in range(ns):
                finish(read, k, r, j - 1)
        for k in range(2):
            cs = slice(k * FFN_CW, (k + 1) * FFN_CW)
            keep(write, k, _dot(h_scr[...], wg_ref[:, cs]))
            keep(write, 2 + k, _dot(h_scr[...], wv_ref[:, cs]))

    @pl.when((j < FFN_UP_STEPS) & (j % 2 == 0))
    def _():
        up_step(carry_a, carry_b)

    @pl.when((j < FFN_UP_STEPS) & (j % 2 == 1))
    def _():
        up_step(carry_b, carry_a)

    @pl.when(j == FFN_UP_STEPS)
    def _():
        last = carry_a if (FFN_UP_STEPS - 1) % 2 == 0 else carry_b
        for k in range(2):
            for r in range(ns):
                finish(last, k, r, FFN_UP_STEPS - 1)


def _ffn_down_kernel(a_ref, wd_ref, x_ref, gate_ref, o_ref, *, tm, up_tm):
    ns = FFN_SLABS
    sr = up_tm // ns
    y = _dot(a_ref[:, FFN_TF:], wd_ref[...])
    y = jnp.swapaxes(y.reshape(tm // up_tm, ns, sr, FFN_TN), 1, 2).reshape(tm, FFN_TN)
    o_ref[...] = x_ref[...] + gate_ref[...] * y


def _conv_ffn(x2, h2, w_up_bf, conv_w, conv_b, w_down_bf, gate, tm, down_tm, up_cast=None,
              down_cast=None):
    n = x2.shape[0]
    nj = FFN_UP_STEPS
    hb = tm // FFN_SLABS
    n_halo_blocks = n // FFN_SLABS
    act_cols = D_FF + FFN_TF
    cw3 = conv_w.reshape(3, 2 * FFN_CHUNKS, FFN_CW).transpose(1, 0, 2)
    cw3 = jnp.broadcast_to(cw3[:, :, None, :], (2 * FFN_CHUNKS, 3, FFN_PIECE_ROWS, FFN_CW))
    cb3 = jnp.broadcast_to(conv_b.reshape(2 * FFN_CHUNKS, 1, FFN_CW),
                           (2 * FFN_CHUNKS, FFN_PIECE_ROWS, FFN_CW))
    cw3 = cw3.astype(FFN_ELEMENTWISE_DTYPE)
    cb3 = cb3.astype(FFN_ELEMENTWISE_DTYPE)
    once = pl.Buffered(1)
    act, *up_cast = _pallas(
        functools.partial(_ffn_up_kernel, tm=tm),
        grid=(n // tm, nj + 1),
        cast=up_cast, step_of=lambda i, j: i * (nj + 1) + j,
        in_specs=[
            pl.BlockSpec((tm, D_MODEL), lambda i, j: (i, 0)),
            pl.BlockSpec((FFN_SLABS, D_MODEL), lambda i, j: (jnp.maximum(i * hb - 1, 0), 0)),
            pl.BlockSpec((FFN_SLABS, D_MODEL), lambda i, j: (jnp.minimum((i + 1) * hb, n_halo_blocks - 1), 0)),
            pl.BlockSpec((None, D_MODEL, FFN_TF), lambda i, j: (jnp.minimum(j, nj - 1), 0, 0)),
            pl.BlockSpec((None, D_MODEL, FFN_TF), lambda i, j: (jnp.minimum(j, nj - 1) + nj, 0, 0)),
            pl.BlockSpec((2 * FFN_CHUNKS, 3, FFN_PIECE_ROWS, FFN_CW), lambda i, j: (0, 0, 0, 0),
                         pipeline_mode=once),
            pl.BlockSpec((2 * FFN_CHUNKS, FFN_PIECE_ROWS, FFN_CW), lambda i, j: (0, 0, 0),
                         pipeline_mode=once),
        ],
        out_specs=[pl.BlockSpec((tm, FFN_TF), lambda i, j: (i, j))],
        out_shape=[jax.ShapeDtypeStruct((n, act_cols), BF16)],
        scratch_shapes=[pltpu.VMEM((tm + FFN_EDGE_ROWS, D_MODEL), BF16),
                        pltpu.VMEM((4, tm + 2 * (tm // FFN_SLABS), FFN_CW), FFN_ELEMENTWISE_DTYPE),
                        pltpu.VMEM((4, tm + 2 * (tm // FFN_SLABS), FFN_CW), FFN_ELEMENTWISE_DTYPE)],
        name="ffn_up",
        args=(h2, h2, h2, w_up_bf, w_up_bf, cw3, cb3))
    out, *down_cast = _pallas(
        functools.partial(_ffn_down_kernel, tm=down_tm, up_tm=tm),
        grid=(n // down_tm, FFN_DOWN_STEPS),
        cast=down_cast, step_of=lambda i, j: i * FFN_DOWN_STEPS + j,
        in_specs=[
            pl.BlockSpec((down_tm, act_cols), lambda i, j: (i, 0)),
            pl.BlockSpec((None, D_FF, FFN_TN), lambda i, j: (j, 0, 0)),
            pl.BlockSpec((down_tm, FFN_TN), lambda i, j: (i, j)),
            pl.BlockSpec((1, FFN_TN), lambda i, j: (0, j)),
        ],
        out_specs=[pl.BlockSpec((down_tm, FFN_TN), lambda i, j: (i, j))],
        out_shape=[jax.ShapeDtypeStruct((n, D_MODEL), F32)],
        name="ffn_down",
        args=(act, w_down_bf, x2, gate))
    return out, up_cast, down_cast


def _gelu(x):
    return 0.5 * x * (1.0 + lax.erf(x * (2.0 ** -0.5)))


def _cd_in_kernel(x_ref, g_ref, sh_ref, sc_ref, w_ref, vg_ref, uv_ref, f_ref, *, tm):
    rows_per = tm // ROW_SPLIT
    for part in range(ROW_SPLIT):
        rs = slice(part * rows_per, (part + 1) * rows_per)
        h = _norm_mod(x_ref[rs, :], g_ref[...], sh_ref[...], sc_ref[...]).astype(BF16)
        uv_ref[rs, 0:C_WIDTH] = _gelu(_dot(h, w_ref[:, 0:C_WIDTH])).astype(BF16)
        v = _gelu(_dot(h, w_ref[:, C_WIDTH:2 * C_WIDTH]))
        ms = jnp.mean(v * v, axis=-1, keepdims=True)
        uv_ref[rs, C_WIDTH:] = (v * lax.rsqrt(ms + EPS) * vg_ref[...]).astype(BF16)
        f_ref[rs, :] = _dot(h, w_ref[:, 2 * C_WIDTH:])


def _cd_in_proj(x2, g, shift, scale, w_in_bf, v_norm_g, tm, cast=None):
    n = x2.shape[0]
    row = lambda i: (0, 0)
    return _pallas(
        functools.partial(_cd_in_kernel, tm=tm),
        grid=(n // tm,),
        cast=cast, step_of=lambda i: i,
        in_specs=[
            pl.BlockSpec((tm, D_MODEL), lambda i: (i, 0)),
            pl.BlockSpec((1, D_MODEL), row),
            pl.BlockSpec((1, D_MODEL), row),
            pl.BlockSpec((1, D_MODEL), row),
            pl.BlockSpec((D_MODEL, CD_IN), row, pipeline_mode=pl.Buffered(1)),
            pl.BlockSpec((1, C_WIDTH), row),
        ],
        out_specs=[
            pl.BlockSpec((tm, 2 * C_WIDTH), lambda i: (i, 0)),
            pl.BlockSpec((tm, D_WIDTH), lambda i: (i, 0)),
        ],
        out_shape=[jax.ShapeDtypeStruct((n, 2 * C_WIDTH), BF16),
                   jax.ShapeDtypeStruct((n, D_WIDTH), F32)],
        name="cd_in_proj",
        args=(x2, g, shift, scale, w_in_bf, v_norm_g))


def _spatial_kernel(u_ref, v_ref, ws_ref, bias_ref, o_ref, *, chunks):
    for k in range(chunks):
        rs = slice(k * CHUNK, (k + 1) * CHUNK)
        for g in range(C_GROUPS):
            cs = slice(g * C_GROUP_DIM, (g + 1) * C_GROUP_DIM)
            s = _dot(ws_ref[g], v_ref[rs, cs]) + bias_ref[:, cs]
            o_ref[rs, cs] = (u_ref[rs, cs].astype(F32) * s).astype(BF16)


def _spatial_gate(uv, w_spatial_bf, bias_full, chunks=8):
    n = uv.shape[0]
    rows = chunks * CHUNK
    return pl.pallas_call(
        functools.partial(_spatial_kernel, chunks=chunks),
        grid=(n // rows,),
        in_specs=[
            pl.BlockSpec((rows, C_WIDTH), lambda i: (i, 0)),
            pl.BlockSpec((rows, C_WIDTH), lambda i: (i, 1)),
            pl.BlockSpec((C_GROUPS, CHUNK, CHUNK), lambda i: (0, 0, 0)),
            pl.BlockSpec((CHUNK, C_WIDTH), lambda i: (0, 0)),
        ],
        out_specs=pl.BlockSpec((rows, C_WIDTH), lambda i: (i, 0)),
        out_shape=jax.ShapeDtypeStruct((n, C_WIDTH), BF16),
        compiler_params=_params(("arbitrary",)),
        name="spatial_gate",
    )(uv, uv, w_spatial_bf, bias_full)


FFT_SUB = V7X_SUBLANES_F32
FFT1_CT = 1024
FFT2_CT = 512


def _fourier_tables(n):
    a_len, b_len, sub = FFT_A, FFT_B, FFT_SUB
    assert a_len * b_len == n
    ch = np.arange(D_GROUP_DIM)
    ang_c = 2.0 * np.pi * ((ch[:, None] * ch[None, :]) % D_GROUP_DIM) / D_GROUP_DIM
    a = np.arange(a_len)
    f_a = np.exp(-2j * np.pi * ((a[:, None] * a[None, :]) % a_len) / a_len)
    m1 = np.kron(f_a, np.eye(sub))
    b = np.arange(b_len)
    tw = np.exp(-2j * np.pi * ((a[:, None] * b[None, :]) % n) / n)
    f_b = np.exp(-2j * np.pi * ((b[:, None] * b[None, :]) % b_len) / b_len)
    m2 = np.einsum('db,pq->dpqb', f_b, np.eye(sub)).reshape(b_len * sub, sub * b_len)
    norm = 1.0 / np.sqrt(float(n) * D_GROUP_DIM)
    m2 = m2 * norm
    tw3 = np.broadcast_to(tw[:, :, None], (a_len, b_len, V7X_LANES))
    f32 = lambda v: jnp.asarray(np.ascontiguousarray(v), dtype=F32)
    return dict(cos_c=f32(np.cos(ang_c)), sin_c=f32(np.sin(ang_c)),
                m1r=f32(m1.real), m1i=f32(m1.imag), m2r=f32(m2.real), m2i=f32(m2.imag),
                twr=f32(tw3.real), twi=f32(tw3.imag))


def _fft1_kernel(f_ref, cc_ref, sc_ref, m1r_ref, m1i_ref, twr_ref, twi_ref, tr_ref, ti_ref):
    rows = FFT_A * FFT_SUB
    ct = FFT1_CT
    fb = f_ref[...].reshape(rows, ct).astype(BF16)
    xr_parts = []
    xi_parts = []
    for q in range(ct // D_GROUP_DIM):
        blk = fb[:, q * D_GROUP_DIM:(q + 1) * D_GROUP_DIM]
        xr_parts.append(_dot(blk, cc_ref[...]))
        xi_parts.append(-_dot(blk, sc_ref[...]))
    xr = jnp.concatenate(xr_parts, axis=1).astype(BF16)
    xi = jnp.concatenate(xi_parts, axis=1).astype(BF16)
    m1r = m1r_ref[...]
    m1i = m1i_ref[...]
    tr = _dot(m1r, xr) - _dot(m1i, xi)
    ti = _dot(m1r, xi) + _dot(m1i, xr)
    reps = ct // V7X_LANES
    twr = jnp.tile(twr_ref[...].reshape(rows, V7X_LANES), (1, reps))
    twi = jnp.tile(twi_ref[...].reshape(rows, V7X_LANES), (1, reps))
    tr_ref[...] = (tr * twr - ti * twi).reshape(FFT_A, FFT_SUB, ct)
    ti_ref[...] = (tr * twi + ti * twr).reshape(FFT_A, FFT_SUB, ct)


def _fft2_kernel(tr_ref, ti_ref, m2r_ref, m2i_ref, wf_ref, o_ref):
    j = pl.program_id(1)
    tr = tr_ref[...].astype(BF16)
    ti = ti_ref[...].astype(BF16)
    z = _dot(m2r_ref[...], tr) - _dot(m2i_ref[...], ti)
    contrib = _dot(z.astype(BF16), wf_ref[...]).reshape(FFT_B, FFT_SUB, D_WIDTH)

    @pl.when(j == 0)
    def _():
        o_ref[...] = contrib

    @pl.when(j > 0)
    def _():
        o_ref[...] += contrib


def _fourier_mix(f, tabs, w_fourier_bf):
    n = f.shape[0]
    a_len, b_len, sub = FFT_A, FFT_B, FFT_SUB
    f3 = f.reshape(a_len, b_len, D_WIDTH)
    rows1 = a_len * sub
    const2 = lambda i, j: (0, 0)
    tr, ti = pl.pallas_call(
        _fft1_kernel,
        grid=(b_len // sub, D_WIDTH // FFT1_CT),
        in_specs=[
            pl.BlockSpec((a_len, sub, FFT1_CT), lambda i, j: (0, i, j)),
            pl.BlockSpec((D_GROUP_DIM, D_GROUP_DIM), const2),
            pl.BlockSpec((D_GROUP_DIM, D_GROUP_DIM), const2),
            pl.BlockSpec((rows1, rows1), const2),
            pl.BlockSpec((rows1, rows1), const2),
            pl.BlockSpec((a_len, sub, V7X_LANES), lambda i, j: (0, i, 0)),
            pl.BlockSpec((a_len, sub, V7X_LANES), lambda i, j: (0, i, 0)),
        ],
        out_specs=[pl.BlockSpec((a_len, sub, FFT1_CT), lambda i, j: (0, i, j))] * 2,
        out_shape=[jax.ShapeDtypeStruct((a_len, b_len, D_WIDTH), F32)] * 2,
        compiler_params=_params(("arbitrary", "arbitrary")),
        name="fourier_stage1",
    )(f3, tabs['cos_c'].astype(BF16), tabs['sin_c'].astype(BF16),
      tabs['m1r'].astype(BF16), tabs['m1i'].astype(BF16), tabs['twr'], tabs['twi'])

    rows2 = sub * b_len
    tr2 = tr.reshape(n, D_WIDTH)
    ti2 = ti.reshape(n, D_WIDTH)
    out = pl.pallas_call(
        _fft2_kernel,
        grid=(a_len // sub, D_WIDTH // FFT2_CT),
        in_specs=[
            pl.BlockSpec((rows2, FFT2_CT), lambda i, j: (i, j)),
            pl.BlockSpec((rows2, FFT2_CT), lambda i, j: (i, j)),
            pl.BlockSpec((rows2, rows2), const2),
            pl.BlockSpec((rows2, rows2), const2),
            pl.BlockSpec((FFT2_CT, D_WIDTH), lambda i, j: (j, 0)),
        ],
        out_specs=pl.BlockSpec((b_len, sub, D_WIDTH), lambda i, j: (0, i, 0)),
        out_shape=jax.ShapeDtypeStruct((b_len, a_len, D_WIDTH), F32),
        compiler_params=_params(("arbitrary", "arbitrary")),
        name="fourier_stage2",
    )(tr2, ti2, tabs['m2r'].astype(BF16), tabs['m2i'].astype(BF16), w_fourier_bf)
    return out.reshape(n, D_WIDTH)


def _rope_tables(n):
    rows = n // GRID_W
    row = np.repeat(np.arange(rows, dtype=np.float64), GRID_W)
    col = np.tile(np.arange(GRID_W, dtype=np.float64), rows)
    inv = ROPE_THETA ** (-np.arange(0, AXIS_DIM, 2, dtype=np.float64) / AXIS_DIM)
    ang_r = row[:, None] * inv[None, :]
    ang_c = col[:, None] * inv[None, :]
    cos = np.concatenate([np.cos(ang_r)] * 2 + [np.cos(ang_c)] * 2, axis=-1)
    sin = np.concatenate([-np.sin(ang_r), np.sin(ang_r), -np.sin(ang_c), np.sin(ang_c)], axis=-1)
    return jnp.asarray(cos, dtype=F32), jnp.asarray(sin, dtype=F32)


def kernel(x, c, ctx, c_ctx, w_mod, b_mod, norm1_g, norm2_g, ab_w_in, a_q_norm_g, a_k_norm_g, a_sink,
           b_w_pool, b_pool_scale, ab_w_out, cd_w_in, c_v_norm_g, c_w_spatial, c_b_spatial, d_w_fourier,
           cd_w_out, f_w_up, f_conv_w, f_conv_b, f_w_down):
    batch, n, _ = x.shape
    ctx_len = ctx.shape[1]
    assert batch == 1 and DEPTH == 2
    x2 = x.reshape(n, D_MODEL)
    ctx2 = ctx.reshape(ctx_len, D_MODEL)

    mod = _mod_vectors(c, c_ctx, w_mod, b_mod)

    def split6(v):
        return [v[:, k * D_MODEL:(k + 1) * D_MODEL] for k in range(6)]

    row1 = lambda v: v.reshape(1, -1)
    n_tiles = n // PROJ_TM
    attn_steps = n // ATTN_TQ
    up_steps = (n // FFN_UP_TM) * (FFN_UP_STEPS + 1)
    down_steps = (n // FFN_DOWN_TM) * FFN_DOWN_STEPS
    cast_down0 = _CastJob(f_w_down, 0, D_FF // n_tiles, n_tiles, col_block=FFN_TN)
    cast_up0 = _CastJob(f_w_up, 0, D_MODEL // attn_steps, attn_steps, col_block=FFN_TF)
    cast_up1 = _CastJob(f_w_up, 1, D_MODEL // attn_steps, up_steps, col_block=FFN_TF)
    cast_down1 = _CastJob(f_w_down, 1, D_FF // down_steps, down_steps, col_block=FFN_TN)

    ml = split6(mod[0, 0:1])
    mc = split6(mod[0, 1:2])
    g1 = row1(norm1_g[0])
    w_in = ab_w_in[0].astype(BF16)
    qn = row1(a_q_norm_g[0])
    kn = row1(a_k_norm_g[0])
    cos, sin = _rope_tables(n)
    qkv, z, w_down0 = _ab_in_proj(x2, g1, ml[0], ml[1], w_in, cos, sin, qn, kn, tm=PROJ_TM,
                                  cast=cast_down0)
    ones = jnp.ones((ctx_len, HEAD_DIM), F32)
    zeros = jnp.zeros((ctx_len, HEAD_DIM), F32)
    qkv_ctx, _ = _ab_in_proj(ctx2, g1, mc[0], mc[1], w_in, ones, zeros, qn, kn, tm=ctx_len)
    attn, w_up0 = _window_attention(qkv, qkv_ctx, a_sink[0], cast=cast_up0)
    pool_tiles = n // POOL_TM
    pooled, w_out0 = _pool_mix(z, b_w_pool[0].astype(BF16), row1(b_pool_scale[0]), tm=POOL_TM,
                               cast=_CastJob(ab_w_out, 0, D_MODEL // pool_tiles, pool_tiles))
    x2, h2, cd_w_in_bf = _out_proj(attn, pooled, w_out0, x2, ml[2], row1(norm2_g[0]), ml[3], ml[4],
                                   tm=PROJ_TM, cast=_CastJob(cd_w_in, 0, D_MODEL // n_tiles, n_tiles))
    x2, (w_up1,), (w_down1,) = _conv_ffn(x2, h2, w_up0, f_conv_w[0], f_conv_b[0], w_down0, ml[5],
                                         tm=FFN_UP_TM, down_tm=FFN_DOWN_TM,
                                         up_cast=cast_up1, down_cast=cast_down1)

    ml = split6(mod[1, 0:1])
    uv, f, w_out1 = _cd_in_proj(x2, row1(norm1_g[1]), ml[0], ml[1], cd_w_in_bf, row1(c_v_norm_g[0]),
                                tm=PROJ_TM, cast=_CastJob(cd_w_out, 0, D_MODEL // n_tiles, n_tiles))
    bias_full = jnp.repeat(c_b_spatial[0].T, C_GROUP_DIM, axis=1)
    c_out = _spatial_gate(uv, c_w_spatial[0].astype(BF16), bias_full)
    d_out = _fourier_mix(f, _fourier_tables(n), d_w_fourier[0].astype(BF16))
    x2, h2 = _out_proj(c_out, d_out, w_out1, x2, ml[2], row1(norm2_g[1]), ml[3], ml[4], tm=PROJ_TM)
    x2, _, _ = _conv_ffn(x2, h2, w_up1, f_conv_w[1], f_conv_b[1], w_down1, ml[5],
                         tm=FFN_UP_TM, down_tm=FFN_DOWN_TM)
    return x2.reshape(batch, n, D_MODEL)
```

```python
import functools

import numpy as np
import jax
import jax.numpy as jnp
from jax import lax
from jax.experimental import pallas as pl
from jax.experimental.pallas import tpu as pltpu

F32 = jnp.float32
BF16 = jnp.bfloat16

D_MODEL = 2048
DEPTH = 2
GRID_W = 64
HEAD_DIM = 128
A_Q_HEADS = 8
A_KV_HEADS = 2
A_GROUP = A_Q_HEADS // A_KV_HEADS
A_Q_DIM = A_Q_HEADS * HEAD_DIM
A_KV_DIM = A_KV_HEADS * HEAD_DIM
A_QKV_DIM = A_Q_DIM + 2 * A_KV_DIM
WINDOW = 128
BLOCK = 128
ROPE_THETA = 10000.0
AXIS_DIM = HEAD_DIM // 2
ATTN_SCALE = HEAD_DIM ** -0.5
NEG_INF = -1e30
B_GROUPS = 4
B_WIDTH = 1024
B_GROUP_DIM = B_WIDTH // B_GROUPS
POOL_WINDOWS = (2, 4, 8, 16)
AB_IN = A_QKV_DIM + B_WIDTH
C_WIDTH = 1024
C_GROUPS = 4
C_GROUP_DIM = C_WIDTH // C_GROUPS
CHUNK = 128
D_WIDTH = 1024
D_GROUPS = 8
D_GROUP_DIM = D_WIDTH // D_GROUPS
CD_IN = 2 * C_WIDTH + D_WIDTH
D_FF = 5632
EPS = 1e-6

V7X_SUBLANES_F32 = 8
V7X_SUBLANES_BF16 = 16
V7X_LANES = 128
V7X_VMEM_BYTES = 64 * 1024 * 1024
VMEM_LIMIT = 56 * 1024 * 1024

PROJ_TM = 512
POOL_TM = 512
ATTN_TQ = 128
FFN_UP_TM = 1024
FFN_DOWN_TM = 1024

FFT_A = 64
FFT_B = 128


def _params(sem):
    return pltpu.CompilerParams(dimension_semantics=sem, vmem_limit_bytes=VMEM_LIMIT)


def _dot(a, b):
    return jnp.dot(a, b, preferred_element_type=F32)


class _CastJob:
    def __init__(self, src, layer, rows, n_steps, col_block=None):
        _, total_rows, self.cols = src.shape
        assert total_rows % rows == 0 and total_rows // rows <= n_steps
        self.src, self.layer, self.rows, self.col_block = src, layer, rows, col_block
        self.last = total_rows // rows - 1
        if col_block is None:
            self.out_shape = jax.ShapeDtypeStruct((total_rows, self.cols), BF16)
        else:
            assert self.cols % col_block == 0
            self.out_shape = jax.ShapeDtypeStruct((self.cols // col_block, total_rows, col_block), BF16)

    def specs(self, step_of):
        blk = lambda *ids: jnp.minimum(step_of(*ids), self.last)
        src = pl.BlockSpec((None, self.rows, self.cols), lambda *ids: (self.layer, blk(*ids), 0))
        if self.col_block is None:
            return src, pl.BlockSpec((self.rows, self.cols), lambda *ids: (blk(*ids), 0))
        return src, pl.BlockSpec((self.cols // self.col_block, self.rows, self.col_block),
                                 lambda *ids: (0, blk(*ids), 0))

    def run(self, src_ref, dst_ref):
        if self.col_block is None:
            dst_ref[...] = src_ref[...].astype(BF16)
        else:
            for b in range(self.cols // self.col_block):
                dst_ref[b] = src_ref[:, b * self.col_block:(b + 1) * self.col_block].astype(BF16)


def _host_cast(kernel_fn, cast, n_in, n_out):
    def body(*refs):
        cast.run(refs[n_in], refs[n_in + 1 + n_out])
        kernel_fn(*refs[:n_in], *refs[n_in + 1:n_in + 1 + n_out], *refs[n_in + 2 + n_out:])
    return body


def _pallas(kernel_fn, *, grid, in_specs, out_specs, out_shape, args, name, scratch_shapes=(),
            cast=None, step_of=None):
    out_specs, out_shape = list(out_specs), list(out_shape)
    if cast is not None:
        src_spec, dst_spec = cast.specs(step_of)
        kernel_fn = _host_cast(kernel_fn, cast, len(in_specs), len(out_specs))
        in_specs = [*in_specs, src_spec]
        out_specs.append(dst_spec)
        out_shape.append(cast.out_shape)
        args = (*args, cast.src)
    return pl.pallas_call(
        kernel_fn, grid=grid, in_specs=list(in_specs), out_specs=out_specs, out_shape=out_shape,
        scratch_shapes=list(scratch_shapes), compiler_params=_params(("arbitrary",) * len(grid)),
        name=name)(*args)


def _norm_mod(x, g, shift, scale):
    ms = jnp.mean(x * x, axis=-1, keepdims=True)
    return x * lax.rsqrt(ms + EPS) * (g * (1.0 + scale)) + shift


MOD_TK = 256
MOD_STREAMS = 2


def _mod_kernel(cv_ref, w_ref, b_ref, o_ref, acc_scr):
    k = pl.program_id(1)
    sub = V7X_SUBLANES_F32
    groups = MOD_TK // sub
    n_out = 6 * D_MODEL
    lane_tiles = n_out // V7X_LANES

    @pl.when(k == 0)
    def _():
        acc_scr[...] = jnp.zeros_like(acc_scr)

    w = w_ref[0].reshape(groups, sub, n_out)
    for s in range(MOD_STREAMS):
        a = cv_ref[s]
        a = (a * jax.nn.sigmoid(a)).reshape(groups, sub, V7X_LANES)
        a = jnp.concatenate([a] * lane_tiles, axis=-1)
        acc_scr[s] += jnp.sum(w * a, axis=0)

    @pl.when(k == pl.num_programs(1) - 1)
    def _():
        rows = [jnp.sum(acc_scr[s], axis=0, keepdims=True) for s in range(MOD_STREAMS)]
        rows.append(jnp.zeros((sub - MOD_STREAMS, n_out), F32))
        o_ref[0] = jnp.concatenate(rows, axis=0) + b_ref[0]


def _mod_vectors(c, c_ctx, w_mod, b_mod):
    n_out = 6 * D_MODEL
    cv = jnp.stack([c.reshape(D_MODEL), c_ctx.reshape(D_MODEL)])
    cv = jnp.broadcast_to(cv[:, :, None], (MOD_STREAMS, D_MODEL, V7X_LANES))
    b3 = b_mod.reshape(DEPTH, 1, n_out)
    return pl.pallas_call(
        _mod_kernel,
        grid=(DEPTH, D_MODEL // MOD_TK),
        in_specs=[
            pl.BlockSpec((MOD_STREAMS, MOD_TK, V7X_LANES), lambda l, k: (0, k, 0)),
            pl.BlockSpec((1, MOD_TK, n_out), lambda l, k: (l, k, 0)),
            pl.BlockSpec((1, 1, n_out), lambda l, k: (l, 0, 0)),
        ],
        out_specs=pl.BlockSpec((1, V7X_SUBLANES_F32, n_out), lambda l, k: (l, 0, 0)),
        out_shape=jax.ShapeDtypeStruct((DEPTH, V7X_SUBLANES_F32, n_out), F32),
        scratch_shapes=[pltpu.VMEM((MOD_STREAMS, V7X_SUBLANES_F32, n_out), F32)],
        compiler_params=_params(("arbitrary", "arbitrary")),
        name="mod_vectors",
    )(cv, w_mod, b3)


AB_TN = 512
ROW_SPLIT = 1


def _rope(t, cos, sin_signed):
    lane = lax.broadcasted_iota(jnp.int32, t.shape, 1)
    first = (lane % AXIS_DIM) < (AXIS_DIM // 2)
    partner = jnp.where(first,
                        pltpu.roll(t, HEAD_DIM - AXIS_DIM // 2, 1),
                        pltpu.roll(t, AXIS_DIM // 2, 1))
    return t * cos + partner * sin_signed


def _head_norm_rope(t, g, cos, sin_signed):
    ms = jnp.mean(t * t, axis=-1, keepdims=True)
    return _rope(t * lax.rsqrt(ms + EPS) * g, cos, sin_signed)


def _ab_in_kernel(x_ref, g_ref, sh_ref, sc_ref, w_ref, cos_ref, sin_ref, qn_ref, kn_ref,
                  qkv_ref, z_ref, *, tm):
    rows_per = tm // ROW_SPLIT
    n_q_tiles = A_Q_DIM // AB_TN
    for part in range(ROW_SPLIT):
        rs = slice(part * rows_per, (part + 1) * rows_per)
        h = _norm_mod(x_ref[rs, :], g_ref[...], sh_ref[...], sc_ref[...]).astype(BF16)
        cos = cos_ref[rs, :]
        sin = sin_ref[rs, :]
        for t in range(AB_IN // AB_TN):
            p = _dot(h, w_ref[:, t * AB_TN:(t + 1) * AB_TN])
            if t < n_q_tiles:
                for hh in range(AB_TN // HEAD_DIM):
                    c0 = t * AB_TN + hh * HEAD_DIM
                    qkv_ref[rs, c0:c0 + HEAD_DIM] = _head_norm_rope(
                        p[:, hh * HEAD_DIM:(hh + 1) * HEAD_DIM], qn_ref[...] * ATTN_SCALE,
                        cos, sin).astype(BF16)
            elif t == n_q_tiles:
                for hh in range(A_KV_HEADS):
                    c0 = A_Q_DIM + hh * HEAD_DIM
                    qkv_ref[rs, c0:c0 + HEAD_DIM] = _head_norm_rope(
                        p[:, hh * HEAD_DIM:(hh + 1) * HEAD_DIM], kn_ref[...], cos, sin).astype(BF16)
                qkv_ref[rs, A_Q_DIM + A_KV_DIM:] = p[:, A_KV_DIM:].astype(BF16)
            else:
                c0 = (t - n_q_tiles - 1) * AB_TN
                z_ref[rs, c0:c0 + AB_TN] = p


def _ab_in_proj(x2, g, shift, scale, w_in_bf, cos, sin, qn, kn, tm, cast=None):
    n = x2.shape[0]
    assert A_Q_DIM % AB_TN == 0 and 2 * A_KV_DIM == AB_TN and B_WIDTH % AB_TN == 0
    row = lambda i: (0, 0)
    return _pallas(
        functools.partial(_ab_in_kernel, tm=tm),
        grid=(n // tm,),
        cast=cast, step_of=lambda i: i,
        in_specs=[
            pl.BlockSpec((tm, D_MODEL), lambda i: (i, 0)),
            pl.BlockSpec((1, D_MODEL), row),
            pl.BlockSpec((1, D_MODEL), row),
            pl.BlockSpec((1, D_MODEL), row),
            pl.BlockSpec((D_MODEL, AB_IN), row, pipeline_mode=pl.Buffered(1)),
            pl.BlockSpec((tm, HEAD_DIM), lambda i: (i, 0)),
            pl.BlockSpec((tm, HEAD_DIM), lambda i: (i, 0)),
            pl.BlockSpec((1, HEAD_DIM), row),
            pl.BlockSpec((1, HEAD_DIM), row),
        ],
        out_specs=[
            pl.BlockSpec((tm, A_QKV_DIM), lambda i: (i, 0)),
            pl.BlockSpec((tm, B_WIDTH), lambda i: (i, 0)),
        ],
        out_shape=[jax.ShapeDtypeStruct((n, A_QKV_DIM), BF16),
                   jax.ShapeDtypeStruct((n, B_WIDTH), F32)],
        name="ab_in_proj",
        args=(x2, g, shift, scale, w_in_bf, cos, sin, qn, kn))


def _attn_kernel(sink_ref, bias_ref, q_ref, kp_ref, kc_ref, kn_ref, vp_ref, vc_ref, vn_ref,
                 kx_ref, vx_ref, o_ref):
    rows = A_GROUP * ATTN_TQ
    band_cols = ATTN_TQ + 2 * BLOCK
    n_keys = band_cols + kx_ref.shape[0]
    bias = jnp.concatenate([bias_ref[...]] * A_GROUP, axis=0)
    r1 = lax.broadcasted_iota(jnp.int32, (rows, 1), 0) // ATTN_TQ
    ones_col = (lax.broadcasted_iota(jnp.int32, (n_keys, HEAD_DIM), 1) == 0).astype(BF16)
    for hk in range(A_KV_HEADS):
        hs = slice(hk * HEAD_DIM, (hk + 1) * HEAD_DIM)
        kcat = jnp.concatenate([kp_ref[:, hs], kc_ref[:, hs], kn_ref[:, hs], kx_ref[:, hs]], axis=0)
        vcat = jnp.concatenate([vp_ref[:, hs], vc_ref[:, hs], vn_ref[:, hs], vx_ref[:, hs]], axis=0)
        q0 = hk * A_GROUP * HEAD_DIM
        q4 = jnp.concatenate(
            [q_ref[:, q0 + g * HEAD_DIM:q0 + (g + 1) * HEAD_DIM] for g in range(A_GROUP)], axis=0)
        s = lax.dot_general(q4, kcat, (((1,), (1,)), ((), ())), preferred_element_type=F32)
        s = jnp.concatenate([s[:, :band_cols] + bias, s[:, band_cols:]], axis=1)
        sink = jnp.zeros((rows, 1), F32)
        for g in range(A_GROUP):
            sink = jnp.where(r1 == g, sink_ref[hk * A_GROUP + g], sink)
        m = jnp.maximum(jnp.max(s, axis=-1, keepdims=True), sink)
        e = jnp.exp(s - m).astype(BF16)
        pv = _dot(e, jnp.concatenate([vcat, ones_col], axis=1))
        den = pv[:, HEAD_DIM:HEAD_DIM + 1] + jnp.exp(sink - m)
        o4 = pv[:, :HEAD_DIM] / den
        for g in range(A_GROUP):
            o_ref[:, q0 + g * HEAD_DIM:q0 + (g + 1) * HEAD_DIM] = (
                o4[g * ATTN_TQ:(g + 1) * ATTN_TQ].astype(BF16))


def _attn_bias(ctx_len):
    del ctx_len
    band_cols = ATTN_TQ + 2 * BLOCK
    r = np.arange(ATTN_TQ)[:, None]
    c = np.arange(band_cols)[None, :]
    band = np.abs(BLOCK + r - c) <= WINDOW
    first = band & (c >= BLOCK)
    last = band & (c < BLOCK + ATTN_TQ)
    masks = np.stack([first, band, last])
    return jnp.asarray(np.where(masks, 0.0, NEG_INF), dtype=F32)


def _window_attention(qkv, qkv_ctx, sink, cast=None):
    n = qkv.shape[0]
    ctx_len = qkv_ctx.shape[0]
    n_tiles = n // ATTN_TQ
    n_blocks = n // BLOCK
    per_tile = ATTN_TQ // BLOCK
    assert n_tiles >= 2 and WINDOW <= BLOCK
    k_col = A_Q_DIM // A_KV_DIM
    v_col = k_col + 1
    prev = lambda i: jnp.maximum(i * per_tile - 1, 0)
    nxt = lambda i: jnp.minimum((i + 1) * per_tile, n_blocks - 1)
    which = lambda i: jnp.where(i == 0, 0, jnp.where(i == n_tiles - 1, 2, 1))
    blk = (BLOCK, A_KV_DIM)
    cur = (ATTN_TQ, A_KV_DIM)
    cols = ATTN_TQ + 2 * BLOCK
    return _pallas(
        _attn_kernel,
        grid=(n_tiles,),
        cast=cast, step_of=lambda i: i,
        in_specs=[
            pl.BlockSpec(memory_space=pltpu.SMEM),
            pl.BlockSpec((None, ATTN_TQ, cols), lambda i: (which(i), 0, 0)),
            pl.BlockSpec((ATTN_TQ, A_Q_DIM), lambda i: (i, 0)),
            pl.BlockSpec(blk, lambda i: (prev(i), k_col)),
            pl.BlockSpec(cur, lambda i: (i, k_col)),
            pl.BlockSpec(blk, lambda i: (nxt(i), k_col)),
            pl.BlockSpec(blk, lambda i: (prev(i), v_col)),
            pl.BlockSpec(cur, lambda i: (i, v_col)),
            pl.BlockSpec(blk, lambda i: (nxt(i), v_col)),
            pl.BlockSpec((ctx_len, A_KV_DIM), lambda i: (0, k_col)),
            pl.BlockSpec((ctx_len, A_KV_DIM), lambda i: (0, v_col)),
        ],
        out_specs=[pl.BlockSpec((ATTN_TQ, A_Q_DIM), lambda i: (i, 0))],
        out_shape=[jax.ShapeDtypeStruct((n, A_Q_DIM), BF16)],
        name="window_attention",
        args=(sink, _attn_bias(ctx_len), qkv, qkv, qkv, qkv, qkv, qkv, qkv, qkv_ctx, qkv_ctx))


POOL_HALO = 8
assert max(POOL_WINDOWS) // 2 <= POOL_HALO


def _pool_kernel(zm_ref, zp_ref, zn_ref, w_ref, ps_ref, o_ref, z_scr, *, n_rows, tm):
    i = pl.program_id(0)
    last = pl.num_programs(0) - 1
    pad = POOL_HALO
    span_rows = tm + 2 * pad
    z_scr[0:pad, :] = jnp.zeros((pad, B_WIDTH), F32)
    z_scr[pad:2 * pad, :] = jnp.where(i > 0, zp_ref[...], 0.0)
    z_scr[2 * pad:2 * pad + tm, :] = zm_ref[...]
    z_scr[2 * pad + tm:3 * pad + tm, :] = jnp.where(i < last, zn_ref[...], 0.0)
    z_scr[3 * pad + tm:, :] = jnp.zeros((pad, B_WIDTH), F32)
    t = i * tm + lax.broadcasted_iota(jnp.int32, (tm, B_GROUP_DIM), 0)
    for g in range(B_GROUPS):
        window = POOL_WINDOWS[g]
        half = window // 2
        cs = slice(g * B_GROUP_DIM, (g + 1) * B_GROUP_DIM)
        s = z_scr[pl.ds(2 * pad - half, span_rows), cs]
        width = 1
        while width < window:
            s = s + pltpu.roll(s, span_rows - width, 0)
            width *= 2
        acc = s[0:tm]
        cnt = (jnp.minimum(t + half, n_rows) - jnp.maximum(t - half, 0)).astype(F32)
        d = (acc / cnt - zm_ref[:, cs]).astype(BF16)
        y = _dot(d, w_ref[g]) * ps_ref[:, cs]
        o_ref[:, cs] = y.astype(BF16)


def _pool_mix(z, w_pool_bf, pool_scale, tm, cast=None):
    n = z.shape[0]
    hb = tm // POOL_HALO
    n_halo_blocks = n // POOL_HALO
    return _pallas(
        functools.partial(_pool_kernel, n_rows=n, tm=tm),
        grid=(n // tm,),
        cast=cast, step_of=lambda i: i,
        in_specs=[
            pl.BlockSpec((tm, B_WIDTH), lambda i: (i, 0)),
            pl.BlockSpec((POOL_HALO, B_WIDTH), lambda i: (jnp.maximum(i * hb - 1, 0), 0)),
            pl.BlockSpec((POOL_HALO, B_WIDTH), lambda i: (jnp.minimum((i + 1) * hb, n_halo_blocks - 1), 0)),
            pl.BlockSpec((B_GROUPS, B_GROUP_DIM, B_GROUP_DIM), lambda i: (0, 0, 0)),
            pl.BlockSpec((1, B_WIDTH), lambda i: (0, 0)),
        ],
        out_specs=[pl.BlockSpec((tm, B_WIDTH), lambda i: (i, 0))],
        out_shape=[jax.ShapeDtypeStruct((n, B_WIDTH), BF16)],
        scratch_shapes=[pltpu.VMEM((tm + 4 * POOL_HALO, B_WIDTH), F32)],
        name="pool_mix",
        args=(z, z, z, w_pool_bf, pool_scale))


OUT_TN = 512


def _out_proj_kernel(a1_ref, a2_ref, w_ref, x_ref, gate_ref, g_ref, sh_ref, sc_ref, o_ref, h_ref, *, tm):
    rows_per = tm // ROW_SPLIT
    for part in range(ROW_SPLIT):
        rs = slice(part * rows_per, (part + 1) * rows_per)
        a = jnp.concatenate([a1_ref[rs, :].astype(BF16), a2_ref[rs, :].astype(BF16)], axis=1)
        for t in range(D_MODEL // OUT_TN):
            cs = slice(t * OUT_TN, (t + 1) * OUT_TN)
            o_ref[rs, cs] = x_ref[rs, cs] + gate_ref[:, cs] * _dot(a, w_ref[:, cs])
        h_ref[rs, :] = _norm_mod(o_ref[rs, :], g_ref[...], sh_ref[...], sc_ref[...])


def _out_proj(a1, a2, w_out_bf, x2, gate, g2, shift2, scale2, tm, cast=None):
    n = x2.shape[0]
    k1 = a1.shape[1]
    k2 = a2.shape[1]
    assert k1 + k2 == w_out_bf.shape[0] and k1 % V7X_LANES == 0
    row = lambda i: (0, 0)
    return _pallas(
        functools.partial(_out_proj_kernel, tm=tm),
        grid=(n // tm,),
        cast=cast, step_of=lambda i: i,
        in_specs=[
            pl.BlockSpec((tm, k1), lambda i: (i, 0)),
            pl.BlockSpec((tm, k2), lambda i: (i, 0)),
            pl.BlockSpec((k1 + k2, D_MODEL), row, pipeline_mode=pl.Buffered(1)),
            pl.BlockSpec((tm, D_MODEL), lambda i: (i, 0)),
            pl.BlockSpec((1, D_MODEL), row),
            pl.BlockSpec((1, D_MODEL), row),
            pl.BlockSpec((1, D_MODEL), row),
            pl.BlockSpec((1, D_MODEL), row),
        ],
        out_specs=[pl.BlockSpec((tm, D_MODEL), lambda i: (i, 0)),
                   pl.BlockSpec((tm, D_MODEL), lambda i: (i, 0))],
        out_shape=[jax.ShapeDtypeStruct((n, D_MODEL), F32),
                   jax.ShapeDtypeStruct((n, D_MODEL), F32)],
        name="out_proj",
        args=(a1, a2, w_out_bf, x2, gate, g2, shift2, scale2))


FFN_SLABS = V7X_SUBLANES_F32
FFN_EDGE_ROWS = V7X_SUBLANES_BF16
FFN_PIECE_ROWS = 16
FFN_ELEMENTWISE_DTYPE = BF16
FFN_CW = 256
FFN_TF = 2 * FFN_CW
FFN_TN = 512
FFN_CHUNKS = D_FF // FFN_CW
FFN_UP_STEPS = D_FF // FFN_TF
FFN_DOWN_STEPS = D_MODEL // FFN_TN


def _ffn_up_kernel(hm_ref, hp_ref, hn_ref, wg_ref, wv_ref, cw_ref, cb_ref, o_ref,
                   h_scr, carry_a, carry_b, *, tm):
    i = pl.program_id(0)
    j = pl.program_id(1)
    last_i = pl.num_programs(0) - 1
    nc = FFN_CHUNKS
    ns = FFN_SLABS
    sr = tm // ns

    def finish(carry, k, r, step):
        c = jnp.maximum(2 * step + k, 0)

        def conv(idx, kk, q0):
            slab = lambda s: carry[idx, pl.ds(s * sr + q0, FFN_PIECE_ROWS), :]
            prev = slab(r - 1) if r > 0 else slab(ns)
            nxt = slab(r + 1) if r < ns - 1 else slab(ns + 1)
            return prev * cw_ref[kk, 0] + slab(r) * cw_ref[kk, 1] + nxt * cw_ref[kk, 2] + cb_ref[kk]

        for q0 in range(0, sr, FFN_PIECE_ROWS):
            gg = conv(k, c, q0)
            vv = conv(2 + k, c + nc, q0)
            o_ref[pl.ds(r * sr + q0, FFN_PIECE_ROWS), k * FFN_CW:(k + 1) * FFN_CW] = (
                gg * jax.nn.sigmoid(gg) * vv).astype(BF16)

    @pl.when(j == 0)
    def _():
        hm = hm_ref[...].reshape(sr, ns, D_MODEL)
        h_scr[0:tm, :] = jnp.swapaxes(hm, 0, 1).reshape(tm, D_MODEL).astype(BF16)
        before = jnp.where(i > 0, hp_ref[ns - 1:ns, :], 0.0)
        after = jnp.where(i < last_i, hn_ref[0:1, :], 0.0)
        pad = jnp.zeros((FFN_EDGE_ROWS - 2, D_MODEL), F32)
        h_scr[tm:, :] = jnp.concatenate([before, after, pad], axis=0).astype(BF16)
        carry_b[...] = jnp.zeros_like(carry_b)

    def keep(carry, idx, u):
        dt = FFN_ELEMENTWISE_DTYPE
        carry[idx, 0:tm, :] = u[0:tm].astype(dt)
        carry[idx, tm:tm + sr, :] = jnp.concatenate(
            [u[tm:tm + 1], u[(ns - 1) * sr:ns * sr - 1]], axis=0).astype(dt)
        carry[idx, tm + sr:, :] = jnp.concatenate([u[1:sr], u[tm + 1:tm + 2]], axis=0).astype(dt)

    def up_step(write, read):
        for k in range(2):
            for r in range(ns):
                finish(read, k, r, j - 1)
        for k in range(2):
            cs = slice(k * FFN_CW, (k + 1) * FFN_CW)
            keep(write, k, _dot(h_scr[...], wg_ref[:, cs]))
            keep(write, 2 + k, _dot(h_scr[...], wv_ref[:, cs]))

    @pl.when((j < FFN_UP_STEPS) & (j % 2 == 0))
    def _():
        up_step(carry_a, carry_b)

    @pl.when((j < FFN_UP_STEPS) & (j % 2 == 1))
    def _():
        up_step(carry_b, carry_a)

    @pl.when(j == FFN_UP_STEPS)
    def _():
        last = carry_a if (FFN_UP_STEPS - 1) % 2 == 0 else carry_b
        for k in range(2):
            for r in range(ns):
                finish(last, k, r, FFN_UP_STEPS - 1)


def _ffn_down_kernel(a_ref, wd_ref, x_ref, gate_ref, o_ref, *, tm, up_tm):
    ns = FFN_SLABS
    sr = up_tm // ns
    y = _dot(a_ref[:, FFN_TF:], wd_ref[...])
    y = jnp.swapaxes(y.reshape(tm // up_tm, ns, sr, FFN_TN), 1, 2).reshape(tm, FFN_TN)
    o_ref[...] = x_ref[...] + gate_ref[...] * y


def _conv_ffn(x2, h2, w_up_bf, conv_w, conv_b, w_down_bf, gate, tm, down_tm, up_cast=None,
              down_cast=None):
    n = x2.shape[0]
    nj = FFN_UP_STEPS
    hb = tm // FFN_SLABS
    n_halo_blocks = n // FFN_SLABS
    act_cols = D_FF + FFN_TF
    cw3 = conv_w.reshape(3, 2 * FFN_CHUNKS, FFN_CW).transpose(1, 0, 2)
    cw3 = jnp.broadcast_to(cw3[:, :, None, :], (2 * FFN_CHUNKS, 3, FFN_PIECE_ROWS, FFN_CW))
    cb3 = jnp.broadcast_to(conv_b.reshape(2 * FFN_CHUNKS, 1, FFN_CW),
                           (2 * FFN_CHUNKS, FFN_PIECE_ROWS, FFN_CW))
    cw3 = cw3.astype(FFN_ELEMENTWISE_DTYPE)
    cb3 = cb3.astype(FFN_ELEMENTWISE_DTYPE)
    once = pl.Buffered(1)
    act, *up_cast = _pallas(
        functools.partial(_ffn_up_kernel, tm=tm),
        grid=(n // tm, nj + 1),
        cast=up_cast, step_of=lambda i, j: i * (nj + 1) + j,
        in_specs=[
            pl.BlockSpec((tm, D_MODEL), lambda i, j: (i, 0)),
            pl.BlockSpec((FFN_SLABS, D_MODEL), lambda i, j: (jnp.maximum(i * hb - 1, 0), 0)),
            pl.BlockSpec((FFN_SLABS, D_MODEL), lambda i, j: (jnp.minimum((i + 1) * hb, n_halo_blocks - 1), 0)),
            pl.BlockSpec((None, D_MODEL, FFN_TF), lambda i, j: (jnp.minimum(j, nj - 1), 0, 0)),
            pl.BlockSpec((None, D_MODEL, FFN_TF), lambda i, j: (jnp.minimum(j, nj - 1) + nj, 0, 0)),
            pl.BlockSpec((2 * FFN_CHUNKS, 3, FFN_PIECE_ROWS, FFN_CW), lambda i, j: (0, 0, 0, 0),
                         pipeline_mode=once),
            pl.BlockSpec((2 * FFN_CHUNKS, FFN_PIECE_ROWS, FFN_CW), lambda i, j: (0, 0, 0),
                         pipeline_mode=once),
        ],
        out_specs=[pl.BlockSpec((tm, FFN_TF), lambda i, j: (i, j))],
        out_shape=[jax.ShapeDtypeStruct((n, act_cols), BF16)],
        scratch_shapes=[pltpu.VMEM((tm + FFN_EDGE_ROWS, D_MODEL), BF16),
                        pltpu.VMEM((4, tm + 2 * (tm // FFN_SLABS), FFN_CW), FFN_ELEMENTWISE_DTYPE),
                        pltpu.VMEM((4, tm + 2 * (tm // FFN_SLABS), FFN_CW), FFN_ELEMENTWISE_DTYPE)],
        name="ffn_up",
        args=(h2, h2, h2, w_up_bf, w_up_bf, cw3, cb3))
    out, *down_cast = _pallas(
        functools.partial(_ffn_down_kernel, tm=down_tm, up_tm=tm),
        grid=(n // down_tm, FFN_DOWN_STEPS),
        cast=down_cast, step_of=lambda i, j: i * FFN_DOWN_STEPS + j,
        in_specs=[
            pl.BlockSpec((down_tm, act_cols), lambda i, j: (i, 0)),
            pl.BlockSpec((None, D_FF, FFN_TN), lambda i, j: (j, 0, 0)),
            pl.BlockSpec((down_tm, FFN_TN), lambda i, j: (i, j)),
            pl.BlockSpec((1, FFN_TN), lambda i, j: (0, j)),
        ],
        out_specs=[pl.BlockSpec((down_tm, FFN_TN), lambda i, j: (i, j))],
        out_shape=[jax.ShapeDtypeStruct((n, D_MODEL), F32)],
        name="ffn_down",
        args=(act, w_down_bf, x2, gate))
    return out, up_cast, down_cast


def _gelu(x):
    return 0.5 * x * (1.0 + lax.erf(x * (2.0 ** -0.5)))


def _cd_in_kernel(x_ref, g_ref, sh_ref, sc_ref, w_ref, vg_ref, uv_ref, f_ref, *, tm):
    rows_per = tm // ROW_SPLIT
    for part in range(ROW_SPLIT):
        rs = slice(part * rows_per, (part + 1) * rows_per)
        h = _norm_mod(x_ref[rs, :], g_ref[...], sh_ref[...], sc_ref[...]).astype(BF16)
        uv_ref[rs, 0:C_WIDTH] = _gelu(_dot(h, w_ref[:, 0:C_WIDTH])).astype(BF16)
        v = _gelu(_dot(h, w_ref[:, C_WIDTH:2 * C_WIDTH]))
        ms = jnp.mean(v * v, axis=-1, keepdims=True)
        uv_ref[rs, C_WIDTH:] = (v * lax.rsqrt(ms + EPS) * vg_ref[...]).astype(BF16)
        f_ref[rs, :] = _dot(h, w_ref[:, 2 * C_WIDTH:])


def _cd_in_proj(x2, g, shift, scale, w_in_bf, v_norm_g, tm, cast=None):
    n = x2.shape[0]
    row = lambda i: (0, 0)
    return _pallas(
        functools.partial(_cd_in_kernel, tm=tm),
        grid=(n // tm,),
        cast=cast, step_of=lambda i: i,
        in_specs=[
            pl.BlockSpec((tm, D_MODEL), lambda i: (i, 0)),
            pl.BlockSpec((1, D_MODEL), row),
            pl.BlockSpec((1, D_MODEL), row),
            pl.BlockSpec((1, D_MODEL), row),
            pl.BlockSpec((D_MODEL, CD_IN), row, pipeline_mode=pl.Buffered(1)),
            pl.BlockSpec((1, C_WIDTH), row),
        ],
        out_specs=[
            pl.BlockSpec((tm, 2 * C_WIDTH), lambda i: (i, 0)),
            pl.BlockSpec((tm, D_WIDTH), lambda i: (i, 0)),
        ],
        out_shape=[jax.ShapeDtypeStruct((n, 2 * C_WIDTH), BF16),
                   jax.ShapeDtypeStruct((n, D_WIDTH), F32)],
        name="cd_in_proj",
        args=(x2, g, shift, scale, w_in_bf, v_norm_g))


def _spatial_kernel(u_ref, v_ref, ws_ref, bias_ref, o_ref, *, chunks):
    for k in range(chunks):
        rs = slice(k * CHUNK, (k + 1) * CHUNK)
        for g in range(C_GROUPS):
            cs = slice(g * C_GROUP_DIM, (g + 1) * C_GROUP_DIM)
            s = _dot(ws_ref[g], v_ref[rs, cs]) + bias_ref[:, cs]
            o_ref[rs, cs] = (u_ref[rs, cs].astype(F32) * s).astype(BF16)


def _spatial_gate(uv, w_spatial_bf, bias_full, chunks=8):
    n = uv.shape[0]
    rows = chunks * CHUNK
    return pl.pallas_call(
        functools.partial(_spatial_kernel, chunks=chunks),
        grid=(n // rows,),
        in_specs=[
            pl.BlockSpec((rows, C_WIDTH), lambda i: (i, 0)),
            pl.BlockSpec((rows, C_WIDTH), lambda i: (i, 1)),
            pl.BlockSpec((C_GROUPS, CHUNK, CHUNK), lambda i: (0, 0, 0)),
            pl.BlockSpec((CHUNK, C_WIDTH), lambda i: (0, 0)),
        ],
        out_specs=pl.BlockSpec((rows, C_WIDTH), lambda i: (i, 0)),
        out_shape=jax.ShapeDtypeStruct((n, C_WIDTH), BF16),
        compiler_params=_params(("arbitrary",)),
        name="spatial_gate",
    )(uv, uv, w_spatial_bf, bias_full)


FFT_SUB = V7X_SUBLANES_F32
FFT1_CT = 1024
FFT2_CT = 1024


def _fourier_tables(n):
    a_len, b_len, sub = FFT_A, FFT_B, FFT_SUB
    assert a_len * b_len == n
    ch = np.arange(D_GROUP_DIM)
    ang_c = 2.0 * np.pi * ((ch[:, None] * ch[None, :]) % D_GROUP_DIM) / D_GROUP_DIM
    a = np.arange(a_len)
    f_a = np.exp(-2j * np.pi * ((a[:, None] * a[None, :]) % a_len) / a_len)
    m1 = np.kron(f_a, np.eye(sub))
    b = np.arange(b_len)
    tw = np.exp(-2j * np.pi * ((a[:, None] * b[None, :]) % n) / n)
    f_b = np.exp(-2j * np.pi * ((b[:, None] * b[None, :]) % b_len) / b_len)
    m2 = np.einsum('db,pq->dpqb', f_b, np.eye(sub)).reshape(b_len * sub, sub * b_len)
    norm = 1.0 / np.sqrt(float(n) * D_GROUP_DIM)
    m2 = m2 * norm
    tw3 = np.broadcast_to(tw[:, :, None], (a_len, b_len, V7X_LANES))
    f32 = lambda v: jnp.asarray(np.ascontiguousarray(v), dtype=F32)
    return dict(cos_c=f32(np.cos(ang_c)), sin_c=f32(np.sin(ang_c)),
                m1r=f32(m1.real), m1i=f32(m1.imag), m2r=f32(m2.real), m2i=f32(m2.imag),
                twr=f32(tw3.real), twi=f32(tw3.imag))


def _fft1_kernel(f_ref, cc_ref, sc_ref, m1r_ref, m1i_ref, twr_ref, twi_ref, tr_ref, ti_ref):
    rows = FFT_A * FFT_SUB
    ct = FFT1_CT
    fb = f_ref[...].reshape(rows, ct).astype(BF16)
    xr_parts = []
    xi_parts = []
    for q in range(ct // D_GROUP_DIM):
        blk = fb[:, q * D_GROUP_DIM:(q + 1) * D_GROUP_DIM]
        xr_parts.append(_dot(blk, cc_ref[...]))
        xi_parts.append(-_dot(blk, sc_ref[...]))
    xr = jnp.concatenate(xr_parts, axis=1).astype(BF16)
    xi = jnp.concatenate(xi_parts, axis=1).astype(BF16)
    m1r = m1r_ref[...]
    m1i = m1i_ref[...]
    tr = _dot(m1r, xr) - _dot(m1i, xi)
    ti = _dot(m1r, xi) + _dot(m1i, xr)
    reps = ct // V7X_LANES
    twr = jnp.tile(twr_ref[...].reshape(rows, V7X_LANES), (1, reps))
    twi = jnp.tile(twi_ref[...].reshape(rows, V7X_LANES), (1, reps))
    tr_ref[...] = (tr * twr - ti * twi).reshape(FFT_A, FFT_SUB, ct)
    ti_ref[...] = (tr * twi + ti * twr).reshape(FFT_A, FFT_SUB, ct)


def _fft2_kernel(tr_ref, ti_ref, m2r_ref, m2i_ref, wf_ref, o_ref):
    j = pl.program_id(1)
    tr = tr_ref[...].astype(BF16)
    ti = ti_ref[...].astype(BF16)
    z = _dot(m2r_ref[...], tr) - _dot(m2i_ref[...], ti)
    contrib = _dot(z.astype(BF16), wf_ref[...]).reshape(FFT_B, FFT_SUB, D_WIDTH)

    @pl.when(j == 0)
    def _():
        o_ref[...] = contrib

    @pl.when(j > 0)
    def _():
        o_ref[...] += contrib


def _fourier_mix(f, tabs, w_fourier_bf):
    n = f.shape[0]
    a_len, b_len, sub = FFT_A, FFT_B, FFT_SUB
    f3 = f.reshape(a_len, b_len, D_WIDTH)
    rows1 = a_len * sub
    const2 = lambda i, j: (0, 0)
    tr, ti = pl.pallas_call(
        _fft1_kernel,
        grid=(b_len // sub, D_WIDTH // FFT1_CT),
        in_specs=[
            pl.BlockSpec((a_len, sub, FFT1_CT), lambda i, j: (0, i, j)),
            pl.BlockSpec((D_GROUP_DIM, D_GROUP_DIM), const2),
            pl.BlockSpec((D_GROUP_DIM, D_GROUP_DIM), const2),
            pl.BlockSpec((rows1, rows1), const2),
            pl.BlockSpec((rows1, rows1), const2),
            pl.BlockSpec((a_len, sub, V7X_LANES), lambda i, j: (0, i, 0)),
            pl.BlockSpec((a_len, sub, V7X_LANES), lambda i, j: (0, i, 0)),
        ],
        out_specs=[pl.BlockSpec((a_len, sub, FFT1_CT), lambda i, j: (0, i, j))] * 2,
        out_shape=[jax.ShapeDtypeStruct((a_len, b_len, D_WIDTH), F32)] * 2,
        compiler_params=_params(("arbitrary", "arbitrary")),
        name="fourier_stage1",
    )(f3, tabs['cos_c'].astype(BF16), tabs['sin_c'].astype(BF16),
      tabs['m1r'].astype(BF16), tabs['m1i'].astype(BF16), tabs['twr'], tabs['twi'])

    rows2 = sub * b_len
    tr2 = tr.reshape(n, D_WIDTH)
    ti2 = ti.reshape(n, D_WIDTH)
    out = pl.pallas_call(
        _fft2_kernel,
        grid=(a_len // sub, D_WIDTH // FFT2_CT),
        in_specs=[
            pl.BlockSpec((rows2, FFT2_CT), lambda i, j: (i, j)),
            pl.BlockSpec((rows2, FFT2_CT), lambda i, j: (i, j)),
            pl.BlockSpec((rows2, rows2), const2),
            pl.BlockSpec((rows2, rows2), const2),
            pl.BlockSpec((FFT2_CT, D_WIDTH), lambda i, j: (j, 0)),
        ],
        out_specs=pl.BlockSpec((b_len, sub, D_WIDTH), lambda i, j: (0, i, 0)),
        out_shape=jax.ShapeDtypeStruct((b_len, a_len, D_WIDTH), F32),
        compiler_params=_params(("arbitrary", "arbitrary")),
        name="fourier_stage2",
    )(tr2, ti2, tabs['m2r'].astype(BF16), tabs['m2i'].astype(BF16), w_fourier_bf)
    return out.reshape(n, D_WIDTH)


def _rope_tables(n):
    rows = n // GRID_W
    row = np.repeat(np.arange(rows, dtype=np.float64), GRID_W)
    col = np.tile(np.arange(GRID_W, dtype=np.float64), rows)
    inv = ROPE_THETA ** (-np.arange(0, AXIS_DIM, 2, dtype=np.float64) / AXIS_DIM)
    ang_r = row[:, None] * inv[None, :]
    ang_c = col[:, None] * inv[None, :]
    cos = np.concatenate([np.cos(ang_r)] * 2 + [np.cos(ang_c)] * 2, axis=-1)
    sin = np.concatenate([-np.sin(ang_r), np.sin(ang_r), -np.sin(ang_c), np.sin(ang_c)], axis=-1)
    return jnp.asarray(cos, dtype=F32), jnp.asarray(sin, dtype=F32)


def kernel(x, c, ctx, c_ctx, w_mod, b_mod, norm1_g, norm2_g, ab_w_in, a_q_norm_g, a_k_norm_g, a_sink,
           b_w_pool, b_pool_scale, ab_w_out, cd_w_in, c_v_norm_g, c_w_spatial, c_b_spatial, d_w_fourier,
           cd_w_out, f_w_up, f_conv_w, f_conv_b, f_w_down):
    batch, n, _ = x.shape
    ctx_len = ctx.shape[1]
    assert batch == 1 and DEPTH == 2
    x2 = x.reshape(n, D_MODEL)
    ctx2 = ctx.reshape(ctx_len, D_MODEL)

    mod = _mod_vectors(c, c_ctx, w_mod, b_mod)

    def split6(v):
        return [v[:, k * D_MODEL:(k + 1) * D_MODEL] for k in range(6)]

    row1 = lambda v: v.reshape(1, -1)
    n_tiles = n // PROJ_TM
    attn_steps = n // ATTN_TQ
    up_steps = (n // FFN_UP_TM) * (FFN_UP_STEPS + 1)
    down_steps = (n // FFN_DOWN_TM) * FFN_DOWN_STEPS
    cast_down0 = _CastJob(f_w_down, 0, D_FF // n_tiles, n_tiles, col_block=FFN_TN)
    cast_up0 = _CastJob(f_w_up, 0, D_MODEL // attn_steps, attn_steps, col_block=FFN_TF)
    cast_up1 = _CastJob(f_w_up, 1, D_MODEL // attn_steps, up_steps, col_block=FFN_TF)
    cast_down1 = _CastJob(f_w_down, 1, D_FF // down_steps, down_steps, col_block=FFN_TN)

    ml = split6(mod[0, 0:1])
    mc = split6(mod[0, 1:2])
    g1 = row1(norm1_g[0])
    w_in = ab_w_in[0].astype(BF16)
    qn = row1(a_q_norm_g[0])
    kn = row1(a_k_norm_g[0])
    cos, sin = _rope_tables(n)
    qkv, z, w_down0 = _ab_in_proj(x2, g1, ml[0], ml[1], w_in, cos, sin, qn, kn, tm=PROJ_TM,
                                  cast=cast_down0)
    ones = jnp.ones((ctx_len, HEAD_DIM), F32)
    zeros = jnp.zeros((ctx_len, HEAD_DIM), F32)
    qkv_ctx, _ = _ab_in_proj(ctx2, g1, mc[0], mc[1], w_in, ones, zeros, qn, kn, tm=ctx_len)
    attn, w_up0 = _window_attention(qkv, qkv_ctx, a_sink[0], cast=cast_up0)
    pool_tiles = n // POOL_TM
    pooled, w_out0 = _pool_mix(z, b_w_pool[0].astype(BF16), row1(b_pool_scale[0]), tm=POOL_TM,
                               cast=_CastJob(ab_w_out, 0, D_MODEL // pool_tiles, pool_tiles))
    x2, h2, cd_w_in_bf = _out_proj(attn, pooled, w_out0, x2, ml[2], row1(norm2_g[0]), ml[3], ml[4],
                                   tm=PROJ_TM, cast=_CastJob(cd_w_in, 0, D_MODEL // n_tiles, n_tiles))
    x2, (w_up1,), (w_down1,) = _conv_ffn(x2, h2, w_up0, f_conv_w[0], f_conv_b[0], w_down0, ml[5],
                                         tm=FFN_UP_TM, down_tm=FFN_DOWN_TM,
                                         up_cast=cast_up1, down_cast=cast_down1)

    ml = split6(mod[1, 0:1])
    uv, f, w_out1 = _cd_in_proj(x2, row1(norm1_g[1]), ml[0], ml[1], cd_w_in_bf, row1(c_v_norm_g[0]),
                                tm=PROJ_TM, cast=_CastJob(cd_w_out, 0, D_MODEL // n_tiles, n_tiles))
    bias_full = jnp.repeat(c_b_spatial[0].T, C_GROUP_DIM, axis=1)
    c_out = _spatial_gate(uv, c_w_spatial[0].astype(BF16), bias_full)
    d_out = _fourier_mix(f, _fourier_tables(n), d_w_fourier[0].astype(BF16))
    x2, h2 = _out_proj(c_out, d_out, w_out1, x2, ml[2], row1(norm2_g[1]), ml[3], ml[4], tm=PROJ_TM)
    x2, _, _ = _conv_ffn(x2, h2, w_up1, f_conv_w[1], f_conv_b[1], w_down1, ml[5],
                         tm=FFN_UP_TM, down_tm=FFN_DOWN_TM)
    return x2.reshape(batch, n, D_MODEL)
```

```python
import functools

import numpy as np
import jax
import jax.numpy as jnp
from jax import lax
from jax.experimental import pallas as pl
from jax.experimental.pallas import tpu as pltpu

F32 = jnp.float32
BF16 = jnp.bfloat16

D_MODEL = 2048
DEPTH = 2
GRID_W = 64
HEAD_DIM = 128
A_Q_HEADS = 8
A_KV_HEADS = 2
A_GROUP = A_Q_HEADS // A_KV_HEADS
A_Q_DIM = A_Q_HEADS * HEAD_DIM
A_KV_DIM = A_KV_HEADS * HEAD_DIM
A_QKV_DIM = A_Q_DIM + 2 * A_KV_DIM
WINDOW = 128
BLOCK = 128
ROPE_THETA = 10000.0
AXIS_DIM = HEAD_DIM // 2
ATTN_SCALE = HEAD_DIM ** -0.5
NEG_INF = -1e30
B_GROUPS = 4
B_WIDTH = 1024
B_GROUP_DIM = B_WIDTH // B_GROUPS
POOL_WINDOWS = (2, 4, 8, 16)
AB_IN = A_QKV_DIM + B_WIDTH
C_WIDTH = 1024
C_GROUPS = 4
C_GROUP_DIM = C_WIDTH // C_GROUPS
CHUNK = 128
D_WIDTH = 1024
D_GROUPS = 8
D_GROUP_DIM = D_WIDTH // D_GROUPS
CD_IN = 2 * C_WIDTH + D_WIDTH
D_FF = 5632
EPS = 1e-6

V7X_SUBLANES_F32 = 8
V7X_SUBLANES_BF16 = 16
V7X_LANES = 128
V7X_VMEM_BYTES = 64 * 1024 * 1024
VMEM_LIMIT = V7X_VMEM_BYTES * 7 // 8

PROJ_TM = 512
POOL_TM = 512
ATTN_TQ = 128
FFN_UP_TM = 1024
FFN_DOWN_TM = 1024

FFT_A = 64
FFT_B = 128


def _params(sem):
    return pltpu.CompilerParams(dimension_semantics=sem, vmem_limit_bytes=VMEM_LIMIT)


def _dot(a, b):
    return jnp.dot(a, b, preferred_element_type=F32)


class _CastJob:
    def __init__(self, src, layer, rows, n_steps, col_block=None):
        _, total_rows, self.cols = src.shape
        assert total_rows % rows == 0 and total_rows // rows <= n_steps
        self.src, self.layer, self.rows, self.col_block = src, layer, rows, col_block
        self.last = total_rows // rows - 1
        if col_block is None:
            self.out_shape = jax.ShapeDtypeStruct((total_rows, self.cols), BF16)
        else:
            assert self.cols % col_block == 0
            self.out_shape = jax.ShapeDtypeStruct((self.cols // col_block, total_rows, col_block), BF16)

    def specs(self, step_of):
        blk = lambda *ids: jnp.minimum(step_of(*ids), self.last)
        src = pl.BlockSpec((None, self.rows, self.cols), lambda *ids: (self.layer, blk(*ids), 0))
        if self.col_block is None:
            return src, pl.BlockSpec((self.rows, self.cols), lambda *ids: (blk(*ids), 0))
        return src, pl.BlockSpec((self.cols // self.col_block, self.rows, self.col_block),
                                 lambda *ids: (0, blk(*ids), 0))

    def run(self, src_ref, dst_ref):
        if self.col_block is None:
            dst_ref[...] = src_ref[...].astype(BF16)
        else:
            for b in range(self.cols // self.col_block):
                dst_ref[b] = src_ref[:, b * self.col_block:(b + 1) * self.col_block].astype(BF16)


def _host_cast(kernel_fn, cast, n_in, n_out):
    def body(*refs):
        cast.run(refs[n_in], refs[n_in + 1 + n_out])
        kernel_fn(*refs[:n_in], *refs[n_in + 1:n_in + 1 + n_out], *refs[n_in + 2 + n_out:])
    return body


def _pallas(kernel_fn, *, grid, in_specs, out_specs, out_shape, args, name, scratch_shapes=(),
            cast=None, step_of=None):
    out_specs, out_shape = list(out_specs), list(out_shape)
    if cast is not None:
        src_spec, dst_spec = cast.specs(step_of)
        kernel_fn = _host_cast(kernel_fn, cast, len(in_specs), len(out_specs))
        in_specs = [*in_specs, src_spec]
        out_specs.append(dst_spec)
        out_shape.append(cast.out_shape)
        args = (*args, cast.src)
    return pl.pallas_call(
        kernel_fn, grid=grid, in_specs=list(in_specs), out_specs=out_specs, out_shape=out_shape,
        scratch_shapes=list(scratch_shapes), compiler_params=_params(("arbitrary",) * len(grid)),
        name=name)(*args)


def _norm_mod(x, g, shift, scale):
    ms = jnp.mean(x * x, axis=-1, keepdims=True)
    return x * lax.rsqrt(ms + EPS) * (g * (1.0 + scale)) + shift


MOD_TK = 256
MOD_STREAMS = 2


def _mod_kernel(cv_ref, w_ref, b_ref, o_ref, acc_scr):
    k = pl.program_id(1)
    sub = V7X_SUBLANES_F32
    groups = MOD_TK // sub
    n_out = 6 * D_MODEL
    lane_tiles = n_out // V7X_LANES

    @pl.when(k == 0)
    def _():
        acc_scr[...] = jnp.zeros_like(acc_scr)

    w = w_ref[0].reshape(groups, sub, n_out)
    for s in range(MOD_STREAMS):
        a = cv_ref[s]
        a = (a * jax.nn.sigmoid(a)).reshape(groups, sub, V7X_LANES)
        a = jnp.concatenate([a] * lane_tiles, axis=-1)
        acc_scr[s] += jnp.sum(w * a, axis=0)

    @pl.when(k == pl.num_programs(1) - 1)
    def _():
        rows = [jnp.sum(acc_scr[s], axis=0, keepdims=True) for s in range(MOD_STREAMS)]
        rows.append(jnp.zeros((sub - MOD_STREAMS, n_out), F32))
        o_ref[0] = jnp.concatenate(rows, axis=0) + b_ref[0]


def _mod_vectors(c, c_ctx, w_mod, b_mod):
    n_out = 6 * D_MODEL
    cv = jnp.stack([c.reshape(D_MODEL), c_ctx.reshape(D_MODEL)])
    cv = jnp.broadcast_to(cv[:, :, None], (MOD_STREAMS, D_MODEL, V7X_LANES))
    b3 = b_mod.reshape(DEPTH, 1, n_out)
    return pl.pallas_call(
        _mod_kernel,
        grid=(DEPTH, D_MODEL // MOD_TK),
        in_specs=[
            pl.BlockSpec((MOD_STREAMS, MOD_TK, V7X_LANES), lambda l, k: (0, k, 0)),
            pl.BlockSpec((1, MOD_TK, n_out), lambda l, k: (l, k, 0)),
            pl.BlockSpec((1, 1, n_out), lambda l, k: (l, 0, 0)),
        ],
        out_specs=pl.BlockSpec((1, V7X_SUBLANES_F32, n_out), lambda l, k: (l, 0, 0)),
        out_shape=jax.ShapeDtypeStruct((DEPTH, V7X_SUBLANES_F32, n_out), F32),
        scratch_shapes=[pltpu.VMEM((MOD_STREAMS, V7X_SUBLANES_F32, n_out), F32)],
        compiler_params=_params(("arbitrary", "arbitrary")),
        name="mod_vectors",
    )(cv, w_mod, b3)


AB_TN = 512
ROW_SPLIT = 1


def _rope(t, cos, sin_signed):
    lane = lax.broadcasted_iota(jnp.int32, t.shape, 1)
    first = (lane % AXIS_DIM) < (AXIS_DIM // 2)
    partner = jnp.where(first,
                        pltpu.roll(t, HEAD_DIM - AXIS_DIM // 2, 1),
                        pltpu.roll(t, AXIS_DIM // 2, 1))
    return t * cos + partner * sin_signed


def _head_norm_rope(t, g, cos, sin_signed):
    ms = jnp.mean(t * t, axis=-1, keepdims=True)
    return _rope(t * lax.rsqrt(ms + EPS) * g, cos, sin_signed)


def _ctx_kv_kernel(x_ref, g_ref, sh_ref, sc_ref, w_ref, kn_ref, kv_ref):
    h = _norm_mod(x_ref[...], g_ref[...], sh_ref[...], sc_ref[...]).astype(BF16)
    p = _dot(h, w_ref[...])
    for hh in range(A_KV_HEADS):
        t = p[:, hh * HEAD_DIM:(hh + 1) * HEAD_DIM]
        ms = jnp.mean(t * t, axis=-1, keepdims=True)
        kv_ref[:, hh * HEAD_DIM:(hh + 1) * HEAD_DIM] = (t * lax.rsqrt(ms + EPS) * kn_ref[...]).astype(BF16)
    kv_ref[:, A_KV_DIM:] = p[:, A_KV_DIM:].astype(BF16)


def _ctx_kv_proj(ctx2, g, shift, scale, w_in_bf, kn):
    ctx_len = ctx2.shape[0]
    kv_cols = 2 * A_KV_DIM
    assert A_Q_DIM % kv_cols == 0
    row = lambda i: (0, 0)
    return _pallas(
        _ctx_kv_kernel,
        grid=(1,),
        in_specs=[
            pl.BlockSpec((ctx_len, D_MODEL), row),
            pl.BlockSpec((1, D_MODEL), row),
            pl.BlockSpec((1, D_MODEL), row),
            pl.BlockSpec((1, D_MODEL), row),
            pl.BlockSpec((D_MODEL, kv_cols), lambda i: (0, A_Q_DIM // kv_cols)),
            pl.BlockSpec((1, HEAD_DIM), row),
        ],
        out_specs=[pl.BlockSpec((ctx_len, kv_cols), row)],
        out_shape=[jax.ShapeDtypeStruct((ctx_len, kv_cols), BF16)],
        name="ctx_kv_proj",
        args=(ctx2, g, shift, scale, w_in_bf, kn))[0]


def _ab_in_kernel(x_ref, g_ref, sh_ref, sc_ref, w_ref, cos_ref, sin_ref, qn_ref, kn_ref,
                  qkv_ref, z_ref, *, tm):
    rows_per = tm // ROW_SPLIT
    n_q_tiles = A_Q_DIM // AB_TN
    for part in range(ROW_SPLIT):
        rs = slice(part * rows_per, (part + 1) * rows_per)
        h = _norm_mod(x_ref[rs, :], g_ref[...], sh_ref[...], sc_ref[...]).astype(BF16)
        cos = cos_ref[rs, :]
        sin = sin_ref[rs, :]
        for t in range(AB_IN // AB_TN):
            p = _dot(h, w_ref[:, t * AB_TN:(t + 1) * AB_TN])
            if t < n_q_tiles:
                for hh in range(AB_TN // HEAD_DIM):
                    c0 = t * AB_TN + hh * HEAD_DIM
                    qkv_ref[rs, c0:c0 + HEAD_DIM] = _head_norm_rope(
                        p[:, hh * HEAD_DIM:(hh + 1) * HEAD_DIM], qn_ref[...] * ATTN_SCALE,
                        cos, sin).astype(BF16)
            elif t == n_q_tiles:
                for hh in range(A_KV_HEADS):
                    c0 = A_Q_DIM + hh * HEAD_DIM
                    qkv_ref[rs, c0:c0 + HEAD_DIM] = _head_norm_rope(
                        p[:, hh * HEAD_DIM:(hh + 1) * HEAD_DIM], kn_ref[...], cos, sin).astype(BF16)
                qkv_ref[rs, A_Q_DIM + A_KV_DIM:] = p[:, A_KV_DIM:].astype(BF16)
            else:
                c0 = (t - n_q_tiles - 1) * AB_TN
                z_ref[rs, c0:c0 + AB_TN] = p


def _ab_in_proj(x2, g, shift, scale, w_in_bf, cos, sin, qn, kn, tm, cast=None):
    n = x2.shape[0]
    assert A_Q_DIM % AB_TN == 0 and 2 * A_KV_DIM == AB_TN and B_WIDTH % AB_TN == 0
    row = lambda i: (0, 0)
    return _pallas(
        functools.partial(_ab_in_kernel, tm=tm),
        grid=(n // tm,),
        cast=cast, step_of=lambda i: i,
        in_specs=[
            pl.BlockSpec((tm, D_MODEL), lambda i: (i, 0)),
            pl.BlockSpec((1, D_MODEL), row),
            pl.BlockSpec((1, D_MODEL), row),
            pl.BlockSpec((1, D_MODEL), row),
            pl.BlockSpec((D_MODEL, AB_IN), row, pipeline_mode=pl.Buffered(1)),
            pl.BlockSpec((tm, HEAD_DIM), lambda i: (i, 0)),
            pl.BlockSpec((tm, HEAD_DIM), lambda i: (i, 0)),
            pl.BlockSpec((1, HEAD_DIM), row),
            pl.BlockSpec((1, HEAD_DIM), row),
        ],
        out_specs=[
            pl.BlockSpec((tm, A_QKV_DIM), lambda i: (i, 0)),
            pl.BlockSpec((tm, B_WIDTH), lambda i: (i, 0)),
        ],
        out_shape=[jax.ShapeDtypeStruct((n, A_QKV_DIM), BF16),
                   jax.ShapeDtypeStruct((n, B_WIDTH), F32)],
        name="ab_in_proj",
        args=(x2, g, shift, scale, w_in_bf, cos, sin, qn, kn))


def _attn_kernel(sink_ref, bias_ref, q_ref, kp_ref, kc_ref, kn_ref, vp_ref, vc_ref, vn_ref,
                 kx_ref, vx_ref, o_ref):
    rows = A_GROUP * ATTN_TQ
    band_cols = ATTN_TQ + 2 * BLOCK
    n_keys = band_cols + kx_ref.shape[0]
    bias = jnp.concatenate([bias_ref[...]] * A_GROUP, axis=0)
    r1 = lax.broadcasted_iota(jnp.int32, (rows, 1), 0) // ATTN_TQ
    ones_col = (lax.broadcasted_iota(jnp.int32, (n_keys, HEAD_DIM), 1) == 0).astype(BF16)
    for hk in range(A_KV_HEADS):
        hs = slice(hk * HEAD_DIM, (hk + 1) * HEAD_DIM)
        kcat = jnp.concatenate([kp_ref[:, hs], kc_ref[:, hs], kn_ref[:, hs], kx_ref[:, hs]], axis=0)
        vcat = jnp.concatenate([vp_ref[:, hs], vc_ref[:, hs], vn_ref[:, hs], vx_ref[:, hs]], axis=0)
        q0 = hk * A_GROUP * HEAD_DIM
        q4 = jnp.concatenate(
            [q_ref[:, q0 + g * HEAD_DIM:q0 + (g + 1) * HEAD_DIM] for g in range(A_GROUP)], axis=0)
        s = lax.dot_general(q4, kcat, (((1,), (1,)), ((), ())), preferred_element_type=F32)
        s = jnp.concatenate([s[:, :band_cols] + bias, s[:, band_cols:]], axis=1)
        sink = jnp.zeros((rows, 1), F32)
        for g in range(A_GROUP):
            sink = jnp.where(r1 == g, sink_ref[hk * A_GROUP + g], sink)
        m = jnp.maximum(jnp.max(s, axis=-1, keepdims=True), sink)
        e = jnp.exp(s - m).astype(BF16)
        pv = _dot(e, jnp.concatenate([vcat, ones_col], axis=1))
        den = pv[:, HEAD_DIM:HEAD_DIM + 1] + jnp.exp(sink - m)
        o4 = pv[:, :HEAD_DIM] / den
        for g in range(A_GROUP):
            o_ref[:, q0 + g * HEAD_DIM:q0 + (g + 1) * HEAD_DIM] = (
                o4[g * ATTN_TQ:(g + 1) * ATTN_TQ].astype(BF16))


def _attn_bias(ctx_len):
    del ctx_len
    band_cols = ATTN_TQ + 2 * BLOCK
    r = np.arange(ATTN_TQ)[:, None]
    c = np.arange(band_cols)[None, :]
    band = np.abs(BLOCK + r - c) <= WINDOW
    first = band & (c >= BLOCK)
    last = band & (c < BLOCK + ATTN_TQ)
    masks = np.stack([first, band, last])
    return jnp.asarray(np.where(masks, 0.0, NEG_INF), dtype=F32)


def _window_attention(qkv, kv_ctx, sink, cast=None):
    n = qkv.shape[0]
    ctx_len = kv_ctx.shape[0]
    n_tiles = n // ATTN_TQ
    n_blocks = n // BLOCK
    per_tile = ATTN_TQ // BLOCK
    assert n_tiles >= 2 and WINDOW <= BLOCK
    k_col = A_Q_DIM // A_KV_DIM
    v_col = k_col + 1
    prev = lambda i: jnp.maximum(i * per_tile - 1, 0)
    nxt = lambda i: jnp.minimum((i + 1) * per_tile, n_blocks - 1)
    which = lambda i: jnp.where(i == 0, 0, jnp.where(i == n_tiles - 1, 2, 1))
    blk = (BLOCK, A_KV_DIM)
    cur = (ATTN_TQ, A_KV_DIM)
    cols = ATTN_TQ + 2 * BLOCK
    return _pallas(
        _attn_kernel,
        grid=(n_tiles,),
        cast=cast, step_of=lambda i: i,
        in_specs=[
            pl.BlockSpec(memory_space=pltpu.SMEM),
            pl.BlockSpec((None, ATTN_TQ, cols), lambda i: (which(i), 0, 0)),
            pl.BlockSpec((ATTN_TQ, A_Q_DIM), lambda i: (i, 0)),
            pl.BlockSpec(blk, lambda i: (prev(i), k_col)),
            pl.BlockSpec(cur, lambda i: (i, k_col)),
            pl.BlockSpec(blk, lambda i: (nxt(i), k_col)),
            pl.BlockSpec(blk, lambda i: (prev(i), v_col)),
            pl.BlockSpec(cur, lambda i: (i, v_col)),
            pl.BlockSpec(blk, lambda i: (nxt(i), v_col)),
            pl.BlockSpec((ctx_len, A_KV_DIM), lambda i: (0, 0)),
            pl.BlockSpec((ctx_len, A_KV_DIM), lambda i: (0, 1)),
        ],
        out_specs=[pl.BlockSpec((ATTN_TQ, A_Q_DIM), lambda i: (i, 0))],
        out_shape=[jax.ShapeDtypeStruct((n, A_Q_DIM), BF16)],
        name="window_attention",
        args=(sink, _attn_bias(ctx_len), qkv, qkv, qkv, qkv, qkv, qkv, qkv, kv_ctx, kv_ctx))


POOL_HALO = 8
assert max(POOL_WINDOWS) // 2 <= POOL_HALO


def _pool_kernel(zm_ref, zp_ref, zn_ref, w_ref, ps_ref, o_ref, z_scr, *, n_rows, tm):
    i = pl.program_id(0)
    last = pl.num_programs(0) - 1
    pad = POOL_HALO
    span_rows = tm + 2 * pad
    z_scr[0:pad, :] = jnp.zeros((pad, B_WIDTH), F32)
    z_scr[pad:2 * pad, :] = jnp.where(i > 0, zp_ref[...], 0.0)
    z_scr[2 * pad:2 * pad + tm, :] = zm_ref[...]
    z_scr[2 * pad + tm:3 * pad + tm, :] = jnp.where(i < last, zn_ref[...], 0.0)
    z_scr[3 * pad + tm:, :] = jnp.zeros((pad, B_WIDTH), F32)
    t = i * tm + lax.broadcasted_iota(jnp.int32, (tm, B_GROUP_DIM), 0)
    for g in range(B_GROUPS):
        window = POOL_WINDOWS[g]
        half = window // 2
        cs = slice(g * B_GROUP_DIM, (g + 1) * B_GROUP_DIM)
        s = z_scr[pl.ds(2 * pad - half, span_rows), cs]
        width = 1
        while width < window:
            s = s + pltpu.roll(s, span_rows - width, 0)
            width *= 2
        acc = s[0:tm]
        cnt = (jnp.minimum(t + half, n_rows) - jnp.maximum(t - half, 0)).astype(F32)
        d = (acc / cnt - zm_ref[:, cs]).astype(BF16)
        y = _dot(d, w_ref[g]) * ps_ref[:, cs]
        o_ref[:, cs] = y.astype(BF16)


def _pool_mix(z, w_pool_bf, pool_scale, tm, cast=None):
    n = z.shape[0]
    hb = tm // POOL_HALO
    n_halo_blocks = n // POOL_HALO
    return _pallas(
        functools.partial(_pool_kernel, n_rows=n, tm=tm),
        grid=(n // tm,),
        cast=cast, step_of=lambda i: i,
        in_specs=[
            pl.BlockSpec((tm, B_WIDTH), lambda i: (i, 0)),
            pl.BlockSpec((POOL_HALO, B_WIDTH), lambda i: (jnp.maximum(i * hb - 1, 0), 0)),
            pl.BlockSpec((POOL_HALO, B_WIDTH), lambda i: (jnp.minimum((i + 1) * hb, n_halo_blocks - 1), 0)),
            pl.BlockSpec((B_GROUPS, B_GROUP_DIM, B_GROUP_DIM), lambda i: (0, 0, 0)),
            pl.BlockSpec((1, B_WIDTH), lambda i: (0, 0)),
        ],
        out_specs=[pl.BlockSpec((tm, B_WIDTH), lambda i: (i, 0))],
        out_shape=[jax.ShapeDtypeStruct((n, B_WIDTH), BF16)],
        scratch_shapes=[pltpu.VMEM((tm + 4 * POOL_HALO, B_WIDTH), F32)],
        name="pool_mix",
        args=(z, z, z, w_pool_bf, pool_scale))


OUT_TN = 512


def _out_proj_kernel(a1_ref, a2_ref, w_ref, x_ref, gate_ref, g_ref, sh_ref, sc_ref, o_ref, h_ref, *, tm):
    rows_per = tm // ROW_SPLIT
    for part in range(ROW_SPLIT):
        rs = slice(part * rows_per, (part + 1) * rows_per)
        a = jnp.concatenate([a1_ref[rs, :].astype(BF16), a2_ref[rs, :].astype(BF16)], axis=1)
        for t in range(D_MODEL // OUT_TN):
            cs = slice(t * OUT_TN, (t + 1) * OUT_TN)
            o_ref[rs, cs] = x_ref[rs, cs] + gate_ref[:, cs] * _dot(a, w_ref[:, cs])
        h_ref[rs, :] = _norm_mod(o_ref[rs, :], g_ref[...], sh_ref[...], sc_ref[...])


def _out_proj(a1, a2, w_out_bf, x2, gate, g2, shift2, scale2, tm, cast=None):
    n = x2.shape[0]
    k1 = a1.shape[1]
    k2 = a2.shape[1]
    assert k1 + k2 == w_out_bf.shape[0] and k1 % V7X_LANES == 0
    row = lambda i: (0, 0)
    return _pallas(
        functools.partial(_out_proj_kernel, tm=tm),
        grid=(n // tm,),
        cast=cast, step_of=lambda i: i,
        in_specs=[
            pl.BlockSpec((tm, k1), lambda i: (i, 0)),
            pl.BlockSpec((tm, k2), lambda i: (i, 0)),
            pl.BlockSpec((k1 + k2, D_MODEL), row, pipeline_mode=pl.Buffered(1)),
            pl.BlockSpec((tm, D_MODEL), lambda i: (i, 0)),
            pl.BlockSpec((1, D_MODEL), row),
            pl.BlockSpec((1, D_MODEL), row),
            pl.BlockSpec((1, D_MODEL), row),
            pl.BlockSpec((1, D_MODEL), row),
        ],
        out_specs=[pl.BlockSpec((tm, D_MODEL), lambda i: (i, 0)),
                   pl.BlockSpec((tm, D_MODEL), lambda i: (i, 0))],
        out_shape=[jax.ShapeDtypeStruct((n, D_MODEL), F32),
                   jax.ShapeDtypeStruct((n, D_MODEL), F32)],
        name="out_proj",
        args=(a1, a2, w_out_bf, x2, gate, g2, shift2, scale2))


FFN_SLABS = V7X_SUBLANES_F32
FFN_EDGE_ROWS = V7X_SUBLANES_BF16
FFN_PIECE_ROWS = 16
FFN_ELEMENTWISE_DTYPE = BF16
FFN_CW = 256
FFN_TF = 2 * FFN_CW
FFN_TN = 512
FFN_CHUNKS = D_FF // FFN_CW
FFN_UP_STEPS = D_FF // FFN_TF
FFN_DOWN_STEPS = D_MODEL // FFN_TN


def _ffn_up_kernel(hm_ref, hp_ref, hn_ref, wg_ref, wv_ref, cw_ref, cb_ref, o_ref,
                   h_scr, carry_a, carry_b, *, tm):
    i = pl.program_id(0)
    j = pl.program_id(1)
    last_i = pl.num_programs(0) - 1
    nc = FFN_CHUNKS
    ns = FFN_SLABS
    sr = tm // ns

    def finish(carry, k, r, step):
        c = jnp.maximum(2 * step + k, 0)

        def conv(idx, kk, q0):
            slab = lambda s: carry[idx, pl.ds(s * sr + q0, FFN_PIECE_ROWS), :]
            prev = slab(r - 1) if r > 0 else slab(ns)
            nxt = slab(r + 1) if r < ns - 1 else slab(ns + 1)
            return prev * cw_ref[kk, 0] + slab(r) * cw_ref[kk, 1] + nxt * cw_ref[kk, 2] + cb_ref[kk]

        for q0 in range(0, sr, FFN_PIECE_ROWS):
            gg = conv(k, c, q0)
            vv = conv(2 + k, c + nc, q0)
            o_ref[pl.ds(r * sr + q0, FFN_PIECE_ROWS), k * FFN_CW:(k + 1) * FFN_CW] = (
                gg * jax.nn.sigmoid(gg) * vv).astype(BF16)

    @pl.when(j == 0)
    def _():
        hm = hm_ref[...].reshape(sr, ns, D_MODEL)
        h_scr[0:tm, :] = jnp.swapaxes(hm, 0, 1).reshape(tm, D_MODEL).astype(BF16)
        before = jnp.where(i > 0, hp_ref[ns - 1:ns, :], 0.0)
        after = jnp.where(i < last_i, hn_ref[0:1, :], 0.0)
        pad = jnp.zeros((FFN_EDGE_ROWS - 2, D_MODEL), F32)
        h_scr[tm:, :] = jnp.concatenate([before, after, pad], axis=0).astype(BF16)
        carry_b[...] = jnp.zeros_like(carry_b)

    def keep(carry, idx, u):
        dt = FFN_ELEMENTWISE_DTYPE
        carry[idx, 0:tm, :] = u[0:tm].astype(dt)
        carry[idx, tm:tm + sr, :] = jnp.concatenate(
            [u[tm:tm + 1], u[(ns - 1) * sr:ns * sr - 1]], axis=0).astype(dt)
        carry[idx, tm + sr:, :] = jnp.concatenate([u[1:sr], u[tm + 1:tm + 2]], axis=0).astype(dt)

    def up_step(write, read):
        for k in range(2):
            for r in range(ns):
                finish(read, k, r, j - 1)
        for k in range(2):
            cs = slice(k * FFN_CW, (k + 1) * FFN_CW)
            keep(write, k, _dot(h_scr[...], wg_ref[:, cs]))
            keep(write, 2 + k, _dot(h_scr[...], wv_ref[:, cs]))

    @pl.when((j < FFN_UP_STEPS) & (j % 2 == 0))
    def _():
        up_step(carry_a, carry_b)

    @pl.when((j < FFN_UP_STEPS) & (j % 2 == 1))
    def _():
        up_step(carry_b, carry_a)

    @pl.when(j == FFN_UP_STEPS)
    def _():
        last = carry_a if (FFN_UP_STEPS - 1) % 2 == 0 else carry_b
        for k in range(2):
            for r in range(ns):
                finish(last, k, r, FFN_UP_STEPS - 1)


def _ffn_down_kernel(a_ref, wd_ref, x_ref, gate_ref, o_ref, *, tm, up_tm):
    ns = FFN_SLABS
    sr = up_tm // ns
    y = _dot(a_ref[:, FFN_TF:], wd_ref[...])
    y = jnp.swapaxes(y.reshape(tm // up_tm, ns, sr, FFN_TN), 1, 2).reshape(tm, FFN_TN)
    o_ref[...] = x_ref[...] + gate_ref[...] * y


def _conv_ffn(x2, h2, w_up_bf, conv_w, conv_b, w_down_bf, gate, tm, down_tm, up_cast=None,
              down_cast=None):
    n = x2.shape[0]
    nj = FFN_UP_STEPS
    hb = tm // FFN_SLABS
    n_halo_blocks = n // FFN_SLABS
    act_cols = D_FF + FFN_TF
    cw3 = conv_w.reshape(3, 2 * FFN_CHUNKS, FFN_CW).transpose(1, 0, 2)
    cw3 = jnp.broadcast_to(cw3[:, :, None, :], (2 * FFN_CHUNKS, 3, FFN_PIECE_ROWS, FFN_CW))
    cb3 = jnp.broadcast_to(conv_b.reshape(2 * FFN_CHUNKS, 1, FFN_CW),
                           (2 * FFN_CHUNKS, FFN_PIECE_ROWS, FFN_CW))
    cw3 = cw3.astype(FFN_ELEMENTWISE_DTYPE)
    cb3 = cb3.astype(FFN_ELEMENTWISE_DTYPE)
    once = pl.Buffered(1)
    act, *up_cast = _pallas(
        functools.partial(_ffn_up_kernel, tm=tm),
        grid=(n // tm, nj + 1),
        cast=up_cast, step_of=lambda i, j: i * (nj + 1) + j,
        in_specs=[
            pl.BlockSpec((tm, D_MODEL), lambda i, j: (i, 0)),
            pl.BlockSpec((FFN_SLABS, D_MODEL), lambda i, j: (jnp.maximum(i * hb - 1, 0), 0)),
            pl.BlockSpec((FFN_SLABS, D_MODEL), lambda i, j: (jnp.minimum((i + 1) * hb, n_halo_blocks - 1), 0)),
            pl.BlockSpec((None, D_MODEL, FFN_TF), lambda i, j: (jnp.minimum(j, nj - 1), 0, 0)),
            pl.BlockSpec((None, D_MODEL, FFN_TF), lambda i, j: (jnp.minimum(j, nj - 1) + nj, 0, 0)),
            pl.BlockSpec((2 * FFN_CHUNKS, 3, FFN_PIECE_ROWS, FFN_CW), lambda i, j: (0, 0, 0, 0),
                         pipeline_mode=once),
            pl.BlockSpec((2 * FFN_CHUNKS, FFN_PIECE_ROWS, FFN_CW), lambda i, j: (0, 0, 0),
                         pipeline_mode=once),
        ],
        out_specs=[pl.BlockSpec((tm, FFN_TF), lambda i, j: (i, j))],
        out_shape=[jax.ShapeDtypeStruct((n, act_cols), BF16)],
        scratch_shapes=[pltpu.VMEM((tm + FFN_EDGE_ROWS, D_MODEL), BF16),
                        pltpu.VMEM((4, tm + 2 * (tm // FFN_SLABS), FFN_CW), FFN_ELEMENTWISE_DTYPE),
                        pltpu.VMEM((4, tm + 2 * (tm // FFN_SLABS), FFN_CW), FFN_ELEMENTWISE_DTYPE)],
        name="ffn_up",
        args=(h2, h2, h2, w_up_bf, w_up_bf, cw3, cb3))
    out, *down_cast = _pallas(
        functools.partial(_ffn_down_kernel, tm=down_tm, up_tm=tm),
        grid=(n // down_tm, FFN_DOWN_STEPS),
        cast=down_cast, step_of=lambda i, j: i * FFN_DOWN_STEPS + j,
        in_specs=[
            pl.BlockSpec((down_tm, act_cols), lambda i, j: (i, 0)),
            pl.BlockSpec((None, D_FF, FFN_TN), lambda i, j: (j, 0, 0)),
            pl.BlockSpec((down_tm, FFN_TN), lambda i, j: (i, j)),
            pl.BlockSpec((1, FFN_TN), lambda i, j: (0, j)),
        ],
        out_specs=[pl.BlockSpec((down_tm, FFN_TN), lambda i, j: (i, j))],
        out_shape=[jax.ShapeDtypeStruct((n, D_MODEL), F32)],
        name="ffn_down",
        args=(act, w_down_bf, x2, gate))
    return out, up_cast, down_cast


def _gelu(x):
    return 0.5 * x * (1.0 + lax.erf(x * (2.0 ** -0.5)))


def _cd_in_kernel(x_ref, g_ref, sh_ref, sc_ref, w_ref, vg_ref, uv_ref, f_ref, *, tm):
    rows_per = tm // ROW_SPLIT
    for part in range(ROW_SPLIT):
        rs = slice(part * rows_per, (part + 1) * rows_per)
        h = _norm_mod(x_ref[rs, :], g_ref[...], sh_ref[...], sc_ref[...]).astype(BF16)
        uv_ref[rs, 0:C_WIDTH] = _gelu(_dot(h, w_ref[:, 0:C_WIDTH])).astype(BF16)
        v = _gelu(_dot(h, w_ref[:, C_WIDTH:2 * C_WIDTH]))
        ms = jnp.mean(v * v, axis=-1, keepdims=True)
        uv_ref[rs, C_WIDTH:] = (v * lax.rsqrt(ms + EPS) * vg_ref[...]).astype(BF16)
        f_ref[rs, :] = _dot(h, w_ref[:, 2 * C_WIDTH:])


def _cd_in_proj(x2, g, shift, scale, w_in_bf, v_norm_g, tm, cast=None):
    n = x2.shape[0]
    row = lambda i: (0, 0)
    return _pallas(
        functools.partial(_cd_in_kernel, tm=tm),
        grid=(n // tm,),
        cast=cast, step_of=lambda i: i,
        in_specs=[
            pl.BlockSpec((tm, D_MODEL), lambda i: (i, 0)),
            pl.BlockSpec((1, D_MODEL), row),
            pl.BlockSpec((1, D_MODEL), row),
            pl.BlockSpec((1, D_MODEL), row),
            pl.BlockSpec((D_MODEL, CD_IN), row, pipeline_mode=pl.Buffered(1)),
            pl.BlockSpec((1, C_WIDTH), row),
        ],
        out_specs=[
            pl.BlockSpec((tm, 2 * C_WIDTH), lambda i: (i, 0)),
            pl.BlockSpec((tm, D_WIDTH), lambda i: (i, 0)),
        ],
        out_shape=[jax.ShapeDtypeStruct((n, 2 * C_WIDTH), BF16),
                   jax.ShapeDtypeStruct((n, D_WIDTH), F32)],
        name="cd_in_proj",
        args=(x2, g, shift, scale, w_in_bf, v_norm_g))


def _spatial_kernel(u_ref, v_ref, ws_ref, bias_ref, o_ref, *, chunks):
    for k in range(chunks):
        rs = slice(k * CHUNK, (k + 1) * CHUNK)
        for g in range(C_GROUPS):
            cs = slice(g * C_GROUP_DIM, (g + 1) * C_GROUP_DIM)
            s = _dot(ws_ref[g], v_ref[rs, cs]) + bias_ref[:, cs]
            o_ref[rs, cs] = (u_ref[rs, cs].astype(F32) * s).astype(BF16)


def _spatial_gate(uv, w_spatial_bf, bias_full, chunks=8):
    n = uv.shape[0]
    rows = chunks * CHUNK
    return pl.pallas_call(
        functools.partial(_spatial_kernel, chunks=chunks),
        grid=(n // rows,),
        in_specs=[
            pl.BlockSpec((rows, C_WIDTH), lambda i: (i, 0)),
            pl.BlockSpec((rows, C_WIDTH), lambda i: (i, 1)),
            pl.BlockSpec((C_GROUPS, CHUNK, CHUNK), lambda i: (0, 0, 0)),
            pl.BlockSpec((CHUNK, C_WIDTH), lambda i: (0, 0)),
        ],
        out_specs=pl.BlockSpec((rows, C_WIDTH), lambda i: (i, 0)),
        out_shape=jax.ShapeDtypeStruct((n, C_WIDTH), BF16),
        compiler_params=_params(("arbitrary",)),
        name="spatial_gate",
    )(uv, uv, w_spatial_bf, bias_full)


FFT_SUB = V7X_SUBLANES_F32
FFT1_CT = 1024
FFT2_CT = 1024


def _fourier_tables(n):
    a_len, b_len, sub = FFT_A, FFT_B, FFT_SUB
    assert a_len * b_len == n
    ch = np.arange(D_GROUP_DIM)
    ang_c = 2.0 * np.pi * ((ch[:, None] * ch[None, :]) % D_GROUP_DIM) / D_GROUP_DIM
    a = np.arange(a_len)
    f_a = np.exp(-2j * np.pi * ((a[:, None] * a[None, :]) % a_len) / a_len)
    m1 = np.kron(f_a, np.eye(sub))
    b = np.arange(b_len)
    tw = np.exp(-2j * np.pi * ((a[:, None] * b[None, :]) % n) / n)
    f_b = np.exp(-2j * np.pi * ((b[:, None] * b[None, :]) % b_len) / b_len)
    m2 = np.einsum('db,pq->dpqb', f_b, np.eye(sub)).reshape(b_len * sub, sub * b_len)
    norm = 1.0 / np.sqrt(float(n) * D_GROUP_DIM)
    m2 = m2 * norm
    tw3 = np.broadcast_to(tw[:, :, None], (a_len, b_len, V7X_LANES))
    f32 = lambda v: jnp.asarray(np.ascontiguousarray(v), dtype=F32)
    return dict(cos_c=f32(np.cos(ang_c)), sin_c=f32(np.sin(ang_c)),
                m1r=f32(m1.real), m1i=f32(m1.imag), m2r=f32(m2.real), m2i=f32(m2.imag),
                twr=f32(tw3.real), twi=f32(tw3.imag))


def _fft1_kernel(f_ref, cc_ref, sc_ref, m1r_ref, m1i_ref, twr_ref, twi_ref, tr_ref, ti_ref):
    rows = FFT_A * FFT_SUB
    ct = FFT1_CT
    fb = f_ref[...].reshape(rows, ct).astype(BF16)
    xr_parts = []
    xi_parts = []
    for q in range(ct // D_GROUP_DIM):
        blk = fb[:, q * D_GROUP_DIM:(q + 1) * D_GROUP_DIM]
        xr_parts.append(_dot(blk, cc_ref[...]))
        xi_parts.append(-_dot(blk, sc_ref[...]))
    xr = jnp.concatenate(xr_parts, axis=1).astype(BF16)
    xi = jnp.concatenate(xi_parts, axis=1).astype(BF16)
    m1r = m1r_ref[...]
    m1i = m1i_ref[...]
    tr = _dot(m1r, xr) - _dot(m1i, xi)
    ti = _dot(m1r, xi) + _dot(m1i, xr)
    reps = ct // V7X_LANES
    twr = jnp.tile(twr_ref[...].reshape(rows, V7X_LANES), (1, reps))
    twi = jnp.tile(twi_ref[...].reshape(rows, V7X_LANES), (1, reps))
    tr_ref[...] = (tr * twr - ti * twi).reshape(FFT_A, FFT_SUB, ct)
    ti_ref[...] = (tr * twi + ti * twr).reshape(FFT_A, FFT_SUB, ct)


def _fft2_kernel(tr_ref, ti_ref, m2r_ref, m2i_ref, wf_ref, o_ref):
    j = pl.program_id(1)
    tr = tr_ref[...].astype(BF16)
    ti = ti_ref[...].astype(BF16)
    z = _dot(m2r_ref[...], tr) - _dot(m2i_ref[...], ti)
    contrib = _dot(z.astype(BF16), wf_ref[...]).reshape(FFT_B, FFT_SUB, D_WIDTH)

    @pl.when(j == 0)
    def _():
        o_ref[...] = contrib

    @pl.when(j > 0)
    def _():
        o_ref[...] += contrib


def _fourier_mix(f, tabs, w_fourier_bf):
    n = f.shape[0]
    a_len, b_len, sub = FFT_A, FFT_B, FFT_SUB
    f3 = f.reshape(a_len, b_len, D_WIDTH)
    rows1 = a_len * sub
    const2 = lambda i, j: (0, 0)
    tr, ti = pl.pallas_call(
        _fft1_kernel,
        grid=(b_len // sub, D_WIDTH // FFT1_CT),
        in_specs=[
            pl.BlockSpec((a_len, sub, FFT1_CT), lambda i, j: (0, i, j)),
            pl.BlockSpec((D_GROUP_DIM, D_GROUP_DIM), const2),
            pl.BlockSpec((D_GROUP_DIM, D_GROUP_DIM), const2),
            pl.BlockSpec((rows1, rows1), const2),
            pl.BlockSpec((rows1, rows1), const2),
            pl.BlockSpec((a_len, sub, V7X_LANES), lambda i, j: (0, i, 0)),
            pl.BlockSpec((a_len, sub, V7X_LANES), lambda i, j: (0, i, 0)),
        ],
        out_specs=[pl.BlockSpec((a_len, sub, FFT1_CT), lambda i, j: (0, i, j))] * 2,
        out_shape=[jax.ShapeDtypeStruct((a_len, b_len, D_WIDTH), F32)] * 2,
        compiler_params=_params(("arbitrary", "arbitrary")),
        name="fourier_stage1",
    )(f3, tabs['cos_c'].astype(BF16), tabs['sin_c'].astype(BF16),
      tabs['m1r'].astype(BF16), tabs['m1i'].astype(BF16), tabs['twr'], tabs['twi'])

    rows2 = sub * b_len
    tr2 = tr.reshape(n, D_WIDTH)
    ti2 = ti.reshape(n, D_WIDTH)
    out = pl.pallas_call(
        _fft2_kernel,
        grid=(a_len // sub, D_WIDTH // FFT2_CT),
        in_specs=[
            pl.BlockSpec((rows2, FFT2_CT), lambda i, j: (i, j)),
            pl.BlockSpec((rows2, FFT2_CT), lambda i, j: (i, j)),
            pl.BlockSpec((rows2, rows2), const2),
            pl.BlockSpec((rows2, rows2), const2),
            pl.BlockSpec((FFT2_CT, D_WIDTH), lambda i, j: (j, 0)),
        ],
        out_specs=pl.BlockSpec((b_len, sub, D_WIDTH), lambda i, j: (0, i, 0)),
        out_shape=jax.ShapeDtypeStruct((b_len, a_len, D_WIDTH), F32),
        compiler_params=_params(("arbitrary", "arbitrary")),
        name="fourier_stage2",
    )(tr2, ti2, tabs['m2r'].astype(BF16), tabs['m2i'].astype(BF16), w_fourier_bf)
    return out.reshape(n, D_WIDTH)


def _rope_tables(n):
    rows = n // GRID_W
    row = np.repeat(np.arange(rows, dtype=np.float64), GRID_W)
    col = np.tile(np.arange(GRID_W, dtype=np.float64), rows)
    inv = ROPE_THETA ** (-np.arange(0, AXIS_DIM, 2, dtype=np.float64) / AXIS_DIM)
    ang_r = row[:, None] * inv[None, :]
    ang_c = col[:, None] * inv[None, :]
    cos = np.concatenate([np.cos(ang_r)] * 2 + [np.cos(ang_c)] * 2, axis=-1)
    sin = np.concatenate([-np.sin(ang_r), np.sin(ang_r), -np.sin(ang_c), np.sin(ang_c)], axis=-1)
    return jnp.asarray(cos, dtype=F32), jnp.asarray(sin, dtype=F32)


def kernel(x, c, ctx, c_ctx, w_mod, b_mod, norm1_g, norm2_g, ab_w_in, a_q_norm_g, a_k_norm_g, a_sink,
           b_w_pool, b_pool_scale, ab_w_out, cd_w_in, c_v_norm_g, c_w_spatial, c_b_spatial, d_w_fourier,
           cd_w_out, f_w_up, f_conv_w, f_conv_b, f_w_down):
    batch, n, _ = x.shape
    ctx_len = ctx.shape[1]
    assert batch == 1 and DEPTH == 2
    x2 = x.reshape(n, D_MODEL)
    ctx2 = ctx.reshape(ctx_len, D_MODEL)

    mod = _mod_vectors(c, c_ctx, w_mod, b_mod)

    def split6(v):
        return [v[:, k * D_MODEL:(k + 1) * D_MODEL] for k in range(6)]

    row1 = lambda v: v.reshape(1, -1)
    n_tiles = n // PROJ_TM
    attn_steps = n // ATTN_TQ
    up_steps = (n // FFN_UP_TM) * (FFN_UP_STEPS + 1)
    down_steps = (n // FFN_DOWN_TM) * FFN_DOWN_STEPS
    cast_down0 = _CastJob(f_w_down, 0, D_FF // n_tiles, n_tiles, col_block=FFN_TN)
    cast_up0 = _CastJob(f_w_up, 0, D_MODEL // attn_steps, attn_steps, col_block=FFN_TF)
    cast_up1 = _CastJob(f_w_up, 1, D_MODEL // attn_steps, up_steps, col_block=FFN_TF)
    cast_down1 = _CastJob(f_w_down, 1, D_FF // down_steps, down_steps, col_block=FFN_TN)

    ml = split6(mod[0, 0:1])
    mc = split6(mod[0, 1:2])
    g1 = row1(norm1_g[0])
    w_in = ab_w_in[0].astype(BF16)
    qn = row1(a_q_norm_g[0])
    kn = row1(a_k_norm_g[0])
    cos, sin = _rope_tables(n)
    qkv, z, w_down0 = _ab_in_proj(x2, g1, ml[0], ml[1], w_in, cos, sin, qn, kn, tm=PROJ_TM,
                                  cast=cast_down0)
    kv_ctx = _ctx_kv_proj(ctx2, g1, mc[0], mc[1], w_in, kn)
    attn, w_up0 = _window_attention(qkv, kv_ctx, a_sink[0], cast=cast_up0)
    pool_tiles = n // POOL_TM
    pooled, w_out0 = _pool_mix(z, b_w_pool[0].astype(BF16), row1(b_pool_scale[0]), tm=POOL_TM,
                               cast=_CastJob(ab_w_out, 0, D_MODEL // pool_tiles, pool_tiles))
    x2, h2, cd_w_in_bf = _out_proj(attn, pooled, w_out0, x2, ml[2], row1(norm2_g[0]), ml[3], ml[4],
                                   tm=PROJ_TM, cast=_CastJob(cd_w_in, 0, D_MODEL // n_tiles, n_tiles))
    x2, (w_up1,), (w_down1,) = _conv_ffn(x2, h2, w_up0, f_conv_w[0], f_conv_b[0], w_down0, ml[5],
                                         tm=FFN_UP_TM, down_tm=FFN_DOWN_TM,
                                         up_cast=cast_up1, down_cast=cast_down1)

    ml = split6(mod[1, 0:1])
    uv, f, w_out1 = _cd_in_proj(x2, row1(norm1_g[1]), ml[0], ml[1], cd_w_in_bf, row1(c_v_norm_g[0]),
                                tm=PROJ_TM, cast=_CastJob(cd_w_out, 0, D_MODEL // n_tiles, n_tiles))
    bias_full = jnp.repeat(c_b_spatial[0].T, C_GROUP_DIM, axis=1)
    c_out = _spatial_gate(uv, c_w_spatial[0].astype(BF16), bias_full)
    d_out = _fourier_mix(f, _fourier_tables(n), d_w_fourier[0].astype(BF16))
    x2, h2 = _out_proj(c_out, d_out, w_out1, x2, ml[2], row1(norm2_g[1]), ml[3], ml[4], tm=PROJ_TM)
    x2, _, _ = _conv_ffn(x2, h2, w_up1, f_conv_w[1], f_conv_b[1], w_down1, ml[5],
                         tm=FFN_UP_TM, down_tm=FFN_DOWN_TM)
    return x2.reshape(batch, n, D_MODEL)
```

```python
import functools

import numpy as np
import jax
import jax.numpy as jnp
from jax import lax
from jax.experimental import pallas as pl
from jax.experimental.pallas import tpu as pltpu

F32 = jnp.float32
BF16 = jnp.bfloat16

D_MODEL = 2048
DEPTH = 2
GRID_W = 64
HEAD_DIM = 128
A_Q_HEADS = 8
A_KV_HEADS = 2
A_GROUP = A_Q_HEADS // A_KV_HEADS
A_Q_DIM = A_Q_HEADS * HEAD_DIM
A_KV_DIM = A_KV_HEADS * HEAD_DIM
A_QKV_DIM = A_Q_DIM + 2 * A_KV_DIM
WINDOW = 128
BLOCK = 128
ROPE_THETA = 10000.0
AXIS_DIM = HEAD_DIM // 2
ATTN_SCALE = HEAD_DIM ** -0.5
NEG_INF = -1e30
B_GROUPS = 4
B_WIDTH = 1024
B_GROUP_DIM = B_WIDTH // B_GROUPS
POOL_WINDOWS = (2, 4, 8, 16)
AB_IN = A_QKV_DIM + B_WIDTH
C_WIDTH = 1024
C_GROUPS = 4
C_GROUP_DIM = C_WIDTH // C_GROUPS
CHUNK = 128
D_WIDTH = 1024
D_GROUPS = 8
D_GROUP_DIM = D_WIDTH // D_GROUPS
CD_IN = 2 * C_WIDTH + D_WIDTH
D_FF = 5632
EPS = 1e-6

V7X_SUBLANES_F32 = 8
V7X_SUBLANES_BF16 = 16
V7X_LANES = 128
V7X_VMEM_BYTES = 64 * 1024 * 1024
VMEM_LIMIT = V7X_VMEM_BYTES * 7 // 8

PROJ_TM = 512
POOL_TM = 512
ATTN_SUBS = 2
FFN_UP_TM = 1024
FFN_DOWN_TM = 1024

FFT_A = 64
FFT_B = 128


def _params(sem):
    return pltpu.CompilerParams(dimension_semantics=sem, vmem_limit_bytes=VMEM_LIMIT)


def _dot(a, b):
    return jnp.dot(a, b, preferred_element_type=F32)


class _CastJob:
    def __init__(self, src, layer, rows, n_steps, col_block=None):
        _, total_rows, self.cols = src.shape
        assert total_rows % rows == 0 and total_rows // rows <= n_steps
        self.src, self.layer, self.rows, self.col_block = src, layer, rows, col_block
        self.last = total_rows // rows - 1
        if col_block is None:
            self.out_shape = jax.ShapeDtypeStruct((total_rows, self.cols), BF16)
        else:
            assert self.cols % col_block == 0
            self.out_shape = jax.ShapeDtypeStruct((self.cols // col_block, total_rows, col_block), BF16)

    def specs(self, step_of):
        blk = lambda *ids: jnp.minimum(step_of(*ids), self.last)
        src = pl.BlockSpec((None, self.rows, self.cols), lambda *ids: (self.layer, blk(*ids), 0))
        if self.col_block is None:
            return src, pl.BlockSpec((self.rows, self.cols), lambda *ids: (blk(*ids), 0))
        return src, pl.BlockSpec((self.cols // self.col_block, self.rows, self.col_block),
                                 lambda *ids: (0, blk(*ids), 0))

    def run(self, src_ref, dst_ref):
        if self.col_block is None:
            dst_ref[...] = src_ref[...].astype(BF16)
        else:
            for b in range(self.cols // self.col_block):
                dst_ref[b] = src_ref[:, b * self.col_block:(b + 1) * self.col_block].astype(BF16)


def _host_cast(kernel_fn, cast, n_in, n_out):
    def body(*refs):
        cast.run(refs[n_in], refs[n_in + 1 + n_out])
        kernel_fn(*refs[:n_in], *refs[n_in + 1:n_in + 1 + n_out], *refs[n_in + 2 + n_out:])
    return body


def _pallas(kernel_fn, *, grid, in_specs, out_specs, out_shape, args, name, scratch_shapes=(),
            cast=None, step_of=None):
    out_specs, out_shape = list(out_specs), list(out_shape)
    if cast is not None:
        src_spec, dst_spec = cast.specs(step_of)
        kernel_fn = _host_cast(kernel_fn, cast, len(in_specs), len(out_specs))
        in_specs = [*in_specs, src_spec]
        out_specs.append(dst_spec)
        out_shape.append(cast.out_shape)
        args = (*args, cast.src)
    return pl.pallas_call(
        kernel_fn, grid=grid, in_specs=list(in_specs), out_specs=out_specs, out_shape=out_shape,
        scratch_shapes=list(scratch_shapes), compiler_params=_params(("arbitrary",) * len(grid)),
        name=name)(*args)


def _norm_mod(x, g, shift, scale):
    ms = jnp.mean(x * x, axis=-1, keepdims=True)
    return x * lax.rsqrt(ms + EPS) * (g * (1.0 + scale)) + shift


MOD_TK = 256
MOD_STREAMS = 2


def _mod_kernel(cv_ref, w_ref, b_ref, o_ref, acc_scr):
    k = pl.program_id(1)
    sub = V7X_SUBLANES_F32
    groups = MOD_TK // sub
    n_out = 6 * D_MODEL
    lane_tiles = n_out // V7X_LANES

    @pl.when(k == 0)
    def _():
        acc_scr[...] = jnp.zeros_like(acc_scr)

    w = w_ref[0].reshape(groups, sub, n_out)
    for s in range(MOD_STREAMS):
        a = cv_ref[s]
        a = (a * jax.nn.sigmoid(a)).reshape(groups, sub, V7X_LANES)
        a = jnp.concatenate([a] * lane_tiles, axis=-1)
        acc_scr[s] += jnp.sum(w * a, axis=0)

    @pl.when(k == pl.num_programs(1) - 1)
    def _():
        rows = [jnp.sum(acc_scr[s], axis=0, keepdims=True) for s in range(MOD_STREAMS)]
        rows.append(jnp.zeros((sub - MOD_STREAMS, n_out), F32))
        o_ref[0] = jnp.concatenate(rows, axis=0) + b_ref[0]


def _mod_vectors(c, c_ctx, w_mod, b_mod):
    n_out = 6 * D_MODEL
    cv = jnp.stack([c.reshape(D_MODEL), c_ctx.reshape(D_MODEL)])
    cv = jnp.broadcast_to(cv[:, :, None], (MOD_STREAMS, D_MODEL, V7X_LANES))
    b3 = b_mod.reshape(DEPTH, 1, n_out)
    return pl.pallas_call(
        _mod_kernel,
        grid=(DEPTH, D_MODEL // MOD_TK),
        in_specs=[
            pl.BlockSpec((MOD_STREAMS, MOD_TK, V7X_LANES), lambda l, k: (0, k, 0)),
            pl.BlockSpec((1, MOD_TK, n_out), lambda l, k: (l, k, 0)),
            pl.BlockSpec((1, 1, n_out), lambda l, k: (l, 0, 0)),
        ],
        out_specs=pl.BlockSpec((1, V7X_SUBLANES_F32, n_out), lambda l, k: (l, 0, 0)),
        out_shape=jax.ShapeDtypeStruct((DEPTH, V7X_SUBLANES_F32, n_out), F32),
        scratch_shapes=[pltpu.VMEM((MOD_STREAMS, V7X_SUBLANES_F32, n_out), F32)],
        compiler_params=_params(("arbitrary", "arbitrary")),
        name="mod_vectors",
    )(cv, w_mod, b3)


AB_TN = 512
ROW_SPLIT = 1


def _rope(t, cos, sin_signed):
    lane = lax.broadcasted_iota(jnp.int32, t.shape, 1)
    first = (lane % AXIS_DIM) < (AXIS_DIM // 2)
    partner = jnp.where(first,
                        pltpu.roll(t, HEAD_DIM - AXIS_DIM // 2, 1),
                        pltpu.roll(t, AXIS_DIM // 2, 1))
    return t * cos + partner * sin_signed


def _head_norm_rope(t, g, cos, sin_signed):
    ms = jnp.mean(t * t, axis=-1, keepdims=True)
    return _rope(t * lax.rsqrt(ms + EPS) * g, cos, sin_signed)


def _ctx_kv_kernel(x_ref, g_ref, sh_ref, sc_ref, w_ref, kn_ref, kv_ref):
    h = _norm_mod(x_ref[...], g_ref[...], sh_ref[...], sc_ref[...]).astype(BF16)
    p = _dot(h, w_ref[...])
    for hh in range(A_KV_HEADS):
        t = p[:, hh * HEAD_DIM:(hh + 1) * HEAD_DIM]
        ms = jnp.mean(t * t, axis=-1, keepdims=True)
        kv_ref[:, hh * HEAD_DIM:(hh + 1) * HEAD_DIM] = (t * lax.rsqrt(ms + EPS) * kn_ref[...]).astype(BF16)
    kv_ref[:, A_KV_DIM:] = p[:, A_KV_DIM:].astype(BF16)


def _ctx_kv_proj(ctx2, g, shift, scale, w_in_bf, kn):
    ctx_len = ctx2.shape[0]
    kv_cols = 2 * A_KV_DIM
    assert A_Q_DIM % kv_cols == 0
    row = lambda i: (0, 0)
    return _pallas(
        _ctx_kv_kernel,
        grid=(1,),
        in_specs=[
            pl.BlockSpec((ctx_len, D_MODEL), row),
            pl.BlockSpec((1, D_MODEL), row),
            pl.BlockSpec((1, D_MODEL), row),
            pl.BlockSpec((1, D_MODEL), row),
            pl.BlockSpec((D_MODEL, kv_cols), lambda i: (0, A_Q_DIM // kv_cols)),
            pl.BlockSpec((1, HEAD_DIM), row),
        ],
        out_specs=[pl.BlockSpec((ctx_len, kv_cols), row)],
        out_shape=[jax.ShapeDtypeStruct((ctx_len, kv_cols), BF16)],
        name="ctx_kv_proj",
        args=(ctx2, g, shift, scale, w_in_bf, kn))[0]


def _ab_in_kernel(x_ref, g_ref, sh_ref, sc_ref, w_ref, cos_ref, sin_ref, qn_ref, kn_ref,
                  qkv_ref, z_ref, *, tm):
    rows_per = tm // ROW_SPLIT
    n_q_tiles = A_Q_DIM // AB_TN
    for part in range(ROW_SPLIT):
        rs = slice(part * rows_per, (part + 1) * rows_per)
        h = _norm_mod(x_ref[rs, :], g_ref[...], sh_ref[...], sc_ref[...]).astype(BF16)
        cos = cos_ref[rs, :]
        sin = sin_ref[rs, :]
        for t in range(AB_IN // AB_TN):
            p = _dot(h, w_ref[:, t * AB_TN:(t + 1) * AB_TN])
            if t < n_q_tiles:
                for hh in range(AB_TN // HEAD_DIM):
                    c0 = t * AB_TN + hh * HEAD_DIM
                    qkv_ref[rs, c0:c0 + HEAD_DIM] = _head_norm_rope(
                        p[:, hh * HEAD_DIM:(hh + 1) * HEAD_DIM], qn_ref[...] * ATTN_SCALE,
                        cos, sin).astype(BF16)
            elif t == n_q_tiles:
                for hh in range(A_KV_HEADS):
                    c0 = A_Q_DIM + hh * HEAD_DIM
                    qkv_ref[rs, c0:c0 + HEAD_DIM] = _head_norm_rope(
                        p[:, hh * HEAD_DIM:(hh + 1) * HEAD_DIM], kn_ref[...], cos, sin).astype(BF16)
                qkv_ref[rs, A_Q_DIM + A_KV_DIM:] = p[:, A_KV_DIM:].astype(BF16)
            else:
                c0 = (t - n_q_tiles - 1) * AB_TN
                z_ref[rs, c0:c0 + AB_TN] = p


def _ab_in_proj(x2, g, shift, scale, w_in_bf, cos, sin, qn, kn, tm, cast=None):
    n = x2.shape[0]
    assert A_Q_DIM % AB_TN == 0 and 2 * A_KV_DIM == AB_TN and B_WIDTH % AB_TN == 0
    row = lambda i: (0, 0)
    return _pallas(
        functools.partial(_ab_in_kernel, tm=tm),
        grid=(n // tm,),
        cast=cast, step_of=lambda i: i,
        in_specs=[
            pl.BlockSpec((tm, D_MODEL), lambda i: (i, 0)),
            pl.BlockSpec((1, D_MODEL), row),
            pl.BlockSpec((1, D_MODEL), row),
            pl.BlockSpec((1, D_MODEL), row),
            pl.BlockSpec((D_MODEL, AB_IN), row, pipeline_mode=pl.Buffered(1)),
            pl.BlockSpec((tm, HEAD_DIM), lambda i: (i, 0)),
            pl.BlockSpec((tm, HEAD_DIM), lambda i: (i, 0)),
            pl.BlockSpec((1, HEAD_DIM), row),
            pl.BlockSpec((1, HEAD_DIM), row),
        ],
        out_specs=[
            pl.BlockSpec((tm, A_QKV_DIM), lambda i: (i, 0)),
            pl.BlockSpec((tm, B_WIDTH), lambda i: (i, 0)),
        ],
        out_shape=[jax.ShapeDtypeStruct((n, A_QKV_DIM), BF16),
                   jax.ShapeDtypeStruct((n, B_WIDTH), F32)],
        name="ab_in_proj",
        args=(x2, g, shift, scale, w_in_bf, cos, sin, qn, kn))


def _attn_kernel(sink_ref, bias_ref, q_ref, kp_ref, kc_ref, kn_ref, vp_ref, vc_ref, vn_ref,
                 kx_ref, vx_ref, o_ref):
    i = pl.program_id(0)
    last = pl.num_programs(0) - 1
    rows = A_GROUP * BLOCK
    band_cols = 3 * BLOCK
    n_keys = band_cols + kx_ref.shape[0]
    r1 = lax.broadcasted_iota(jnp.int32, (rows, 1), 0) // BLOCK
    ones_col = (lax.broadcasted_iota(jnp.int32, (n_keys, HEAD_DIM), 1) == 0).astype(BF16)

    def key_block(prev_ref, cur_ref, next_ref, j, hs):
        if j == 0:
            return prev_ref[:, hs]
        if j == ATTN_SUBS + 1:
            return next_ref[:, hs]
        return cur_ref[(j - 1) * BLOCK:j * BLOCK, hs]

    for sb, hk in [(sb, hk) for sb in range(ATTN_SUBS) for hk in range(A_KV_HEADS)]:
        rs = slice(sb * BLOCK, (sb + 1) * BLOCK)
        if sb == 0:
            bias = jnp.where(i == 0, bias_ref[0], bias_ref[1])
        elif sb == ATTN_SUBS - 1:
            bias = jnp.where(i == last, bias_ref[2], bias_ref[1])
        else:
            bias = bias_ref[1]
        bias = jnp.concatenate([bias] * A_GROUP, axis=0)
        hs = slice(hk * HEAD_DIM, (hk + 1) * HEAD_DIM)
        kcat = jnp.concatenate([key_block(kp_ref, kc_ref, kn_ref, sb + d, hs) for d in range(3)]
                               + [kx_ref[:, hs]], axis=0)
        vcat = jnp.concatenate([key_block(vp_ref, vc_ref, vn_ref, sb + d, hs) for d in range(3)]
                               + [vx_ref[:, hs]], axis=0)
        q0 = hk * A_GROUP * HEAD_DIM
        q4 = jnp.concatenate(
            [q_ref[rs, q0 + g * HEAD_DIM:q0 + (g + 1) * HEAD_DIM] for g in range(A_GROUP)], axis=0)
        s = lax.dot_general(q4, kcat, (((1,), (1,)), ((), ())), preferred_element_type=F32)
        s = jnp.concatenate([s[:, :band_cols] + bias, s[:, band_cols:]], axis=1)
        sink = jnp.zeros((rows, 1), F32)
        for g in range(A_GROUP):
            sink = jnp.where(r1 == g, sink_ref[hk * A_GROUP + g], sink)
        m = jnp.maximum(jnp.max(s, axis=-1, keepdims=True), sink)
        e = jnp.exp(s - m).astype(BF16)
        pv = _dot(e, jnp.concatenate([vcat, ones_col], axis=1))
        den = pv[:, HEAD_DIM:HEAD_DIM + 1] + jnp.exp(sink - m)
        o4 = pv[:, :HEAD_DIM] / den
        for g in range(A_GROUP):
            o_ref[rs, q0 + g * HEAD_DIM:q0 + (g + 1) * HEAD_DIM] = (
                o4[g * BLOCK:(g + 1) * BLOCK].astype(BF16))


def _attn_bias():
    r = np.arange(BLOCK)[:, None]
    c = np.arange(3 * BLOCK)[None, :]
    band = np.abs(BLOCK + r - c) <= WINDOW
    first = band & (c >= BLOCK)
    last = band & (c < 2 * BLOCK)
    masks = np.stack([first, band, last])
    return jnp.asarray(np.where(masks, 0.0, NEG_INF), dtype=F32)


def _window_attention(qkv, kv_ctx, sink, cast=None):
    n = qkv.shape[0]
    ctx_len = kv_ctx.shape[0]
    step_rows = ATTN_SUBS * BLOCK
    n_tiles = n // step_rows
    n_blocks = n // BLOCK
    assert ATTN_SUBS >= 2 and n % step_rows == 0 and WINDOW <= BLOCK
    k_col = A_Q_DIM // A_KV_DIM
    v_col = k_col + 1
    prev = lambda i: jnp.maximum(i * ATTN_SUBS - 1, 0)
    nxt = lambda i: jnp.minimum((i + 1) * ATTN_SUBS, n_blocks - 1)
    blk = (BLOCK, A_KV_DIM)
    cur = (step_rows, A_KV_DIM)
    return _pallas(
        _attn_kernel,
        grid=(n_tiles,),
        cast=cast, step_of=lambda i: i,
        in_specs=[
            pl.BlockSpec(memory_space=pltpu.SMEM),
            pl.BlockSpec((3, BLOCK, 3 * BLOCK), lambda i: (0, 0, 0)),
            pl.BlockSpec((step_rows, A_Q_DIM), lambda i: (i, 0)),
            pl.BlockSpec(blk, lambda i: (prev(i), k_col)),
            pl.BlockSpec(cur, lambda i: (i, k_col)),
            pl.BlockSpec(blk, lambda i: (nxt(i), k_col)),
            pl.BlockSpec(blk, lambda i: (prev(i), v_col)),
            pl.BlockSpec(cur, lambda i: (i, v_col)),
            pl.BlockSpec(blk, lambda i: (nxt(i), v_col)),
            pl.BlockSpec((ctx_len, A_KV_DIM), lambda i: (0, 0)),
            pl.BlockSpec((ctx_len, A_KV_DIM), lambda i: (0, 1)),
        ],
        out_specs=[pl.BlockSpec((step_rows, A_Q_DIM), lambda i: (i, 0))],
        out_shape=[jax.ShapeDtypeStruct((n, A_Q_DIM), BF16)],
        name="window_attention",
        args=(sink, _attn_bias(), qkv, qkv, qkv, qkv, qkv, qkv, qkv, kv_ctx, kv_ctx))


POOL_HALO = 8
assert max(POOL_WINDOWS) // 2 <= POOL_HALO


def _pool_kernel(zm_ref, zp_ref, zn_ref, w_ref, ps_ref, o_ref, z_scr, *, n_rows, tm):
    i = pl.program_id(0)
    last = pl.num_programs(0) - 1
    pad = POOL_HALO
    span_rows = tm + 2 * pad
    z_scr[0:pad, :] = jnp.zeros((pad, B_WIDTH), F32)
    z_scr[pad:2 * pad, :] = jnp.where(i > 0, zp_ref[...], 0.0)
    z_scr[2 * pad:2 * pad + tm, :] = zm_ref[...]
    z_scr[2 * pad + tm:3 * pad + tm, :] = jnp.where(i < last, zn_ref[...], 0.0)
    z_scr[3 * pad + tm:, :] = jnp.zeros((pad, B_WIDTH), F32)
    t = i * tm + lax.broadcasted_iota(jnp.int32, (tm, B_GROUP_DIM), 0)
    for g in range(B_GROUPS):
        window = POOL_WINDOWS[g]
        half = window // 2
        cs = slice(g * B_GROUP_DIM, (g + 1) * B_GROUP_DIM)
        s = z_scr[pl.ds(2 * pad - half, span_rows), cs]
        width = 1
        while width < window:
            s = s + pltpu.roll(s, span_rows - width, 0)
            width *= 2
        acc = s[0:tm]
        cnt = (jnp.minimum(t + half, n_rows) - jnp.maximum(t - half, 0)).astype(F32)
        d = (acc / cnt - zm_ref[:, cs]).astype(BF16)
        y = _dot(d, w_ref[g]) * ps_ref[:, cs]
        o_ref[:, cs] = y.astype(BF16)


def _pool_mix(z, w_pool_bf, pool_scale, tm, cast=None):
    n = z.shape[0]
    hb = tm // POOL_HALO
    n_halo_blocks = n // POOL_HALO
    return _pallas(
        functools.partial(_pool_kernel, n_rows=n, tm=tm),
        grid=(n // tm,),
        cast=cast, step_of=lambda i: i,
        in_specs=[
            pl.BlockSpec((tm, B_WIDTH), lambda i: (i, 0)),
            pl.BlockSpec((POOL_HALO, B_WIDTH), lambda i: (jnp.maximum(i * hb - 1, 0), 0)),
            pl.BlockSpec((POOL_HALO, B_WIDTH), lambda i: (jnp.minimum((i + 1) * hb, n_halo_blocks - 1), 0)),
            pl.BlockSpec((B_GROUPS, B_GROUP_DIM, B_GROUP_DIM), lambda i: (0, 0, 0)),
            pl.BlockSpec((1, B_WIDTH), lambda i: (0, 0)),
        ],
        out_specs=[pl.BlockSpec((tm, B_WIDTH), lambda i: (i, 0))],
        out_shape=[jax.ShapeDtypeStruct((n, B_WIDTH), BF16)],
        scratch_shapes=[pltpu.VMEM((tm + 4 * POOL_HALO, B_WIDTH), F32)],
        name="pool_mix",
        args=(z, z, z, w_pool_bf, pool_scale))


OUT_TN = 512


def _out_proj_kernel(a1_ref, a2_ref, w_ref, x_ref, gate_ref, g_ref, sh_ref, sc_ref, o_ref, h_ref, *, tm):
    rows_per = tm // ROW_SPLIT
    for part in range(ROW_SPLIT):
        rs = slice(part * rows_per, (part + 1) * rows_per)
        a = jnp.concatenate([a1_ref[rs, :].astype(BF16), a2_ref[rs, :].astype(BF16)], axis=1)
        for t in range(D_MODEL // OUT_TN):
            cs = slice(t * OUT_TN, (t + 1) * OUT_TN)
            o_ref[rs, cs] = x_ref[rs, cs] + gate_ref[:, cs] * _dot(a, w_ref[:, cs])
        h_ref[rs, :] = _norm_mod(o_ref[rs, :], g_ref[...], sh_ref[...], sc_ref[...])


def _out_proj(a1, a2, w_out_bf, x2, gate, g2, shift2, scale2, tm, cast=None):
    n = x2.shape[0]
    k1 = a1.shape[1]
    k2 = a2.shape[1]
    assert k1 + k2 == w_out_bf.shape[0] and k1 % V7X_LANES == 0
    row = lambda i: (0, 0)
    return _pallas(
        functools.partial(_out_proj_kernel, tm=tm),
        grid=(n // tm,),
        cast=cast, step_of=lambda i: i,
        in_specs=[
            pl.BlockSpec((tm, k1), lambda i: (i, 0)),
            pl.BlockSpec((tm, k2), lambda i: (i, 0)),
            pl.BlockSpec((k1 + k2, D_MODEL), row, pipeline_mode=pl.Buffered(1)),
            pl.BlockSpec((tm, D_MODEL), lambda i: (i, 0)),
            pl.BlockSpec((1, D_MODEL), row),
            pl.BlockSpec((1, D_MODEL), row),
            pl.BlockSpec((1, D_MODEL), row),
            pl.BlockSpec((1, D_MODEL), row),
        ],
        out_specs=[pl.BlockSpec((tm, D_MODEL), lambda i: (i, 0)),
                   pl.BlockSpec((tm, D_MODEL), lambda i: (i, 0))],
        out_shape=[jax.ShapeDtypeStruct((n, D_MODEL), F32),
                   jax.ShapeDtypeStruct((n, D_MODEL), F32)],
        name="out_proj",
        args=(a1, a2, w_out_bf, x2, gate, g2, shift2, scale2))


FFN_SLABS = V7X_SUBLANES_F32
FFN_EDGE_ROWS = V7X_SUBLANES_BF16
FFN_PIECE_ROWS = 16
FFN_ELEMENTWISE_DTYPE = BF16
FFN_CW = 256
FFN_TF = 2 * FFN_CW
FFN_TN = 512
FFN_CHUNKS = D_FF // FFN_CW
FFN_UP_STEPS = D_FF // FFN_TF
FFN_DOWN_STEPS = D_MODEL // FFN_TN


def _ffn_up_kernel(hm_ref, hp_ref, hn_ref, wg_ref, wv_ref, cw_ref, cb_ref, o_ref,
                   h_scr, carry_a, carry_b, *, tm):
    i = pl.program_id(0)
    j = pl.program_id(1)
    last_i = pl.num_programs(0) - 1
    nc = FFN_CHUNKS
    ns = FFN_SLABS
    sr = tm // ns

    def finish(carry, k, r, step):
        c = jnp.maximum(2 * step + k, 0)

        def conv(idx, kk, q0):
            slab = lambda s: carry[idx, pl.ds(s * sr + q0, FFN_PIECE_ROWS), :]
            prev = slab(r - 1) if r > 0 else slab(ns)
            nxt = slab(r + 1) if r < ns - 1 else slab(ns + 1)
            return prev * cw_ref[kk, 0] + slab(r) * cw_ref[kk, 1] + nxt * cw_ref[kk, 2] + cb_ref[kk]

        for q0 in range(0, sr, FFN_PIECE_ROWS):
            gg = conv(k, c, q0)
            vv = conv(2 + k, c + nc, q0)
            o_ref[pl.ds(r * sr + q0, FFN_PIECE_ROWS), k * FFN_CW:(k + 1) * FFN_CW] = (
                gg * jax.nn.sigmoid(gg) * vv).astype(BF16)

    @pl.when(j == 0)
    def _():
        hm = hm_ref[...].reshape(sr, ns, D_MODEL)
        h_scr[0:tm, :] = jnp.swapaxes(hm, 0, 1).reshape(tm, D_MODEL).astype(BF16)
        before = jnp.where(i > 0, hp_ref[ns - 1:ns, :], 0.0)
        after = jnp.where(i < last_i, hn_ref[0:1, :], 0.0)
        pad = jnp.zeros((FFN_EDGE_ROWS - 2, D_MODEL), F32)
        h_scr[tm:, :] = jnp.concatenate([before, after, pad], axis=0).astype(BF16)
        carry_b[...] = jnp.zeros_like(carry_b)

    def keep(carry, idx, u):
        dt = FFN_ELEMENTWISE_DTYPE
        carry[idx, 0:tm, :] = u[0:tm].astype(dt)
        carry[idx, tm:tm + sr, :] = jnp.concatenate(
            [u[tm:tm + 1], u[(ns - 1) * sr:ns * sr - 1]], axis=0).astype(dt)
        carry[idx, tm + sr:, :] = jnp.concatenate([u[1:sr], u[tm + 1:tm + 2]], axis=0).astype(dt)

    def up_step(write, read):
        for k in range(2):
            for r in range(ns):
                finish(read, k, r, j - 1)
        for k in range(2):
            cs = slice(k * FFN_CW, (k + 1) * FFN_CW)
            keep(write, k, _dot(h_scr[...], wg_ref[:, cs]))
            keep(write, 2 + k, _dot(h_scr[...], wv_ref[:, cs]))

    @pl.when((j < FFN_UP_STEPS) & (j % 2 == 0))
    def _():
        up_step(carry_a, carry_b)

    @pl.when((j < FFN_UP_STEPS) & (j % 2 == 1))
    def _():
        up_step(carry_b, carry_a)

    @pl.when(j == FFN_UP_STEPS)
    def _():
        last = carry_a if (FFN_UP_STEPS - 1) % 2 == 0 else carry_b
        for k in range(2):
            for r in range(ns):
                finish(last, k, r, FFN_UP_STEPS - 1)


def _ffn_down_kernel(a_ref, wd_ref, x_ref, gate_ref, o_ref, *, tm, up_tm):
    ns = FFN_SLABS
    sr = up_tm // ns
    y = _dot(a_ref[:, FFN_TF:], wd_ref[...])
    y = jnp.swapaxes(y.reshape(tm // up_tm, ns, sr, FFN_TN), 1, 2).reshape(tm, FFN_TN)
    o_ref[...] = x_ref[...] + gate_ref[...] * y


def _conv_ffn(x2, h2, w_up_bf, conv_w, conv_b, w_down_bf, gate, tm, down_tm, up_cast=None,
              down_cast=None):
    n = x2.shape[0]
    nj = FFN_UP_STEPS
    hb = tm // FFN_SLABS
    n_halo_blocks = n // FFN_SLABS
    act_cols = D_FF + FFN_TF
    cw3 = conv_w.reshape(3, 2 * FFN_CHUNKS, FFN_CW).transpose(1, 0, 2)
    cw3 = jnp.broadcast_to(cw3[:, :, None, :], (2 * FFN_CHUNKS, 3, FFN_PIECE_ROWS, FFN_CW))
    cb3 = jnp.broadcast_to(conv_b.reshape(2 * FFN_CHUNKS, 1, FFN_CW),
                           (2 * FFN_CHUNKS, FFN_PIECE_ROWS, FFN_CW))
    cw3 = cw3.astype(FFN_ELEMENTWISE_DTYPE)
    cb3 = cb3.astype(FFN_ELEMENTWISE_DTYPE)
    once = pl.Buffered(1)
    act, *up_cast = _pallas(
        functools.partial(_ffn_up_kernel, tm=tm),
        grid=(n // tm, nj + 1),
        cast=up_cast, step_of=lambda i, j: i * (nj + 1) + j,
        in_specs=[
            pl.BlockSpec((tm, D_MODEL), lambda i, j: (i, 0)),
            pl.BlockSpec((FFN_SLABS, D_MODEL), lambda i, j: (jnp.maximum(i * hb - 1, 0), 0)),
            pl.BlockSpec((FFN_SLABS, D_MODEL), lambda i, j: (jnp.minimum((i + 1) * hb, n_halo_blocks - 1), 0)),
            pl.BlockSpec((None, D_MODEL, FFN_TF), lambda i, j: (jnp.minimum(j, nj - 1), 0, 0)),
            pl.BlockSpec((None, D_MODEL, FFN_TF), lambda i, j: (jnp.minimum(j, nj - 1) + nj, 0, 0)),
            pl.BlockSpec((2 * FFN_CHUNKS, 3, FFN_PIECE_ROWS, FFN_CW), lambda i, j: (0, 0, 0, 0),
                         pipeline_mode=once),
            pl.BlockSpec((2 * FFN_CHUNKS, FFN_PIECE_ROWS, FFN_CW), lambda i, j: (0, 0, 0),
                         pipeline_mode=once),
        ],
        out_specs=[pl.BlockSpec((tm, FFN_TF), lambda i, j: (i, j))],
        out_shape=[jax.ShapeDtypeStruct((n, act_cols), BF16)],
        scratch_shapes=[pltpu.VMEM((tm + FFN_EDGE_ROWS, D_MODEL), BF16),
                        pltpu.VMEM((4, tm + 2 * (tm // FFN_SLABS), FFN_CW), FFN_ELEMENTWISE_DTYPE),
                        pltpu.VMEM((4, tm + 2 * (tm // FFN_SLABS), FFN_CW), FFN_ELEMENTWISE_DTYPE)],
        name="ffn_up",
        args=(h2, h2, h2, w_up_bf, w_up_bf, cw3, cb3))
    out, *down_cast = _pallas(
        functools.partial(_ffn_down_kernel, tm=down_tm, up_tm=tm),
        grid=(n // down_tm, FFN_DOWN_STEPS),
        cast=down_cast, step_of=lambda i, j: i * FFN_DOWN_STEPS + j,
        in_specs=[
            pl.BlockSpec((down_tm, act_cols), lambda i, j: (i, 0)),
            pl.BlockSpec((None, D_FF, FFN_TN), lambda i, j: (j, 0, 0)),
            pl.BlockSpec((down_tm, FFN_TN), lambda i, j: (i, j)),
            pl.BlockSpec((1, FFN_TN), lambda i, j: (0, j)),
        ],
        out_specs=[pl.BlockSpec((down_tm, FFN_TN), lambda i, j: (i, j))],
        out_shape=[jax.ShapeDtypeStruct((n, D_MODEL), F32)],
        name="ffn_down",
        args=(act, w_down_bf, x2, gate))
    return out, up_cast, down_cast


def _gelu(x):
    return 0.5 * x * (1.0 + lax.erf(x * (2.0 ** -0.5)))


def _cd_in_kernel(x_ref, g_ref, sh_ref, sc_ref, w_ref, vg_ref, uv_ref, f_ref, *, tm):
    rows_per = tm // ROW_SPLIT
    for part in range(ROW_SPLIT):
        rs = slice(part * rows_per, (part + 1) * rows_per)
        h = _norm_mod(x_ref[rs, :], g_ref[...], sh_ref[...], sc_ref[...]).astype(BF16)
        uv_ref[rs, 0:C_WIDTH] = _gelu(_dot(h, w_ref[:, 0:C_WIDTH])).astype(BF16)
        v = _gelu(_dot(h, w_ref[:, C_WIDTH:2 * C_WIDTH]))
        ms = jnp.mean(v * v, axis=-1, keepdims=True)
        uv_ref[rs, C_WIDTH:] = (v * lax.rsqrt(ms + EPS) * vg_ref[...]).astype(BF16)
        f_ref[rs, :] = _dot(h, w_ref[:, 2 * C_WIDTH:])


def _cd_in_proj(x2, g, shift, scale, w_in_bf, v_norm_g, tm, cast=None):
    n = x2.shape[0]
    row = lambda i: (0, 0)
    return _pallas(
        functools.partial(_cd_in_kernel, tm=tm),
        grid=(n // tm,),
        cast=cast, step_of=lambda i: i,
        in_specs=[
            pl.BlockSpec((tm, D_MODEL), lambda i: (i, 0)),
            pl.BlockSpec((1, D_MODEL), row),
            pl.BlockSpec((1, D_MODEL), row),
            pl.BlockSpec((1, D_MODEL), row),
            pl.BlockSpec((D_MODEL, CD_IN), row, pipeline_mode=pl.Buffered(1)),
            pl.BlockSpec((1, C_WIDTH), row),
        ],
        out_specs=[
            pl.BlockSpec((tm, 2 * C_WIDTH), lambda i: (i, 0)),
            pl.BlockSpec((tm, D_WIDTH), lambda i: (i, 0)),
        ],
        out_shape=[jax.ShapeDtypeStruct((n, 2 * C_WIDTH), BF16),
                   jax.ShapeDtypeStruct((n, D_WIDTH), F32)],
        name="cd_in_proj",
        args=(x2, g, shift, scale, w_in_bf, v_norm_g))


def _spatial_kernel(u_ref, v_ref, ws_ref, bias_ref, o_ref, *, chunks):
    for k in range(chunks):
        rs = slice(k * CHUNK, (k + 1) * CHUNK)
        for g in range(C_GROUPS):
            cs = slice(g * C_GROUP_DIM, (g + 1) * C_GROUP_DIM)
            s = _dot(ws_ref[g], v_ref[rs, cs]) + bias_ref[:, cs]
            o_ref[rs, cs] = (u_ref[rs, cs].astype(F32) * s).astype(BF16)


def _spatial_gate(uv, w_spatial_bf, bias_full, chunks=8):
    n = uv.shape[0]
    rows = chunks * CHUNK
    return pl.pallas_call(
        functools.partial(_spatial_kernel, chunks=chunks),
        grid=(n // rows,),
        in_specs=[
            pl.BlockSpec((rows, C_WIDTH), lambda i: (i, 0)),
            pl.BlockSpec((rows, C_WIDTH), lambda i: (i, 1)),
            pl.BlockSpec((C_GROUPS, CHUNK, CHUNK), lambda i: (0, 0, 0)),
            pl.BlockSpec((CHUNK, C_WIDTH), lambda i: (0, 0)),
        ],
        out_specs=pl.BlockSpec((rows, C_WIDTH), lambda i: (i, 0)),
        out_shape=jax.ShapeDtypeStruct((n, C_WIDTH), BF16),
        compiler_params=_params(("arbitrary",)),
        name="spatial_gate",
    )(uv, uv, w_spatial_bf, bias_full)


FFT_SUB = V7X_SUBLANES_F32
FFT1_CT = 1024
FFT2_CT = 1024


def _fourier_tables(n):
    a_len, b_len, sub = FFT_A, FFT_B, FFT_SUB
    assert a_len * b_len == n
    ch = np.arange(D_GROUP_DIM)
    ang_c = 2.0 * np.pi * ((ch[:, None] * ch[None, :]) % D_GROUP_DIM) / D_GROUP_DIM
    a = np.arange(a_len)
    f_a = np.exp(-2j * np.pi * ((a[:, None] * a[None, :]) % a_len) / a_len)
    m1 = np.kron(f_a, np.eye(sub))
    b = np.arange(b_len)
    tw = np.exp(-2j * np.pi * ((a[:, None] * b[None, :]) % n) / n)
    f_b = np.exp(-2j * np.pi * ((b[:, None] * b[None, :]) % b_len) / b_len)
    m2 = np.einsum('db,pq->dpqb', f_b, np.eye(sub)).reshape(b_len * sub, sub * b_len)
    norm = 1.0 / np.sqrt(float(n) * D_GROUP_DIM)
    m2 = m2 * norm
    tw3 = np.broadcast_to(tw[:, :, None], (a_len, b_len, V7X_LANES))
    f32 = lambda v: jnp.asarray(np.ascontiguousarray(v), dtype=F32)
    return dict(cos_c=f32(np.cos(ang_c)), sin_c=f32(np.sin(ang_c)),
                m1r=f32(m1.real), m1i=f32(m1.imag), m2r=f32(m2.real), m2i=f32(m2.imag),
                twr=f32(tw3.real), twi=f32(tw3.imag))


def _fft1_kernel(f_ref, cc_ref, sc_ref, m1r_ref, m1i_ref, twr_ref, twi_ref, tr_ref, ti_ref):
    rows = FFT_A * FFT_SUB
    ct = FFT1_CT
    fb = f_ref[...].reshape(rows, ct).astype(BF16)
    xr_parts = []
    xi_parts = []
    for q in range(ct // D_GROUP_DIM):
        blk = fb[:, q * D_GROUP_DIM:(q + 1) * D_GROUP_DIM]
        xr_parts.append(_dot(blk, cc_ref[...]))
        xi_parts.append(-_dot(blk, sc_ref[...]))
    xr = jnp.concatenate(xr_parts, axis=1).astype(BF16)
    xi = jnp.concatenate(xi_parts, axis=1).astype(BF16)
    m1r = m1r_ref[...]
    m1i = m1i_ref[...]
    tr = _dot(m1r, xr) - _dot(m1i, xi)
    ti = _dot(m1r, xi) + _dot(m1i, xr)
    reps = ct // V7X_LANES
    twr = jnp.tile(twr_ref[...].reshape(rows, V7X_LANES), (1, reps))
    twi = jnp.tile(twi_ref[...].reshape(rows, V7X_LANES), (1, reps))
    tr_ref[...] = (tr * twr - ti * twi).reshape(FFT_A, FFT_SUB, ct)
    ti_ref[...] = (tr * twi + ti * twr).reshape(FFT_A, FFT_SUB, ct)


def _fft2_kernel(tr_ref, ti_ref, m2r_ref, m2i_ref, wf_ref, o_ref):
    j = pl.program_id(1)
    tr = tr_ref[...].astype(BF16)
    ti = ti_ref[...].astype(BF16)
    z = _dot(m2r_ref[...], tr) - _dot(m2i_ref[...], ti)
    contrib = _dot(z.astype(BF16), wf_ref[...]).reshape(FFT_B, FFT_SUB, D_WIDTH)

    @pl.when(j == 0)
    def _():
        o_ref[...] = contrib

    @pl.when(j > 0)
    def _():
        o_ref[...] += contrib


def _fourier_mix(f, tabs, w_fourier_bf):
    n = f.shape[0]
    a_len, b_len, sub = FFT_A, FFT_B, FFT_SUB
    f3 = f.reshape(a_len, b_len, D_WIDTH)
    rows1 = a_len * sub
    const2 = lambda i, j: (0, 0)
    tr, ti = pl.pallas_call(
        _fft1_kernel,
        grid=(b_len // sub, D_WIDTH // FFT1_CT),
        in_specs=[
            pl.BlockSpec((a_len, sub, FFT1_CT), lambda i, j: (0, i, j)),
            pl.BlockSpec((D_GROUP_DIM, D_GROUP_DIM), const2),
            pl.BlockSpec((D_GROUP_DIM, D_GROUP_DIM), const2),
            pl.BlockSpec((rows1, rows1), const2),
            pl.BlockSpec((rows1, rows1), const2),
            pl.BlockSpec((a_len, sub, V7X_LANES), lambda i, j: (0, i, 0)),
            pl.BlockSpec((a_len, sub, V7X_LANES), lambda i, j: (0, i, 0)),
        ],
        out_specs=[pl.BlockSpec((a_len, sub, FFT1_CT), lambda i, j: (0, i, j))] * 2,
        out_shape=[jax.ShapeDtypeStruct((a_len, b_len, D_WIDTH), F32)] * 2,
        compiler_params=_params(("arbitrary", "arbitrary")),
        name="fourier_stage1",
    )(f3, tabs['cos_c'].astype(BF16), tabs['sin_c'].astype(BF16),
      tabs['m1r'].astype(BF16), tabs['m1i'].astype(BF16), tabs['twr'], tabs['twi'])

    rows2 = sub * b_len
    tr2 = tr.reshape(n, D_WIDTH)
    ti2 = ti.reshape(n, D_WIDTH)
    out = pl.pallas_call(
        _fft2_kernel,
        grid=(a_len // sub, D_WIDTH // FFT2_CT),
        in_specs=[
            pl.BlockSpec((rows2, FFT2_CT), lambda i, j: (i, j)),
            pl.BlockSpec((rows2, FFT2_CT), lambda i, j: (i, j)),
            pl.BlockSpec((rows2, rows2), const2),
            pl.BlockSpec((rows2, rows2), const2),
            pl.BlockSpec((FFT2_CT, D_WIDTH), lambda i, j: (j, 0)),
        ],
        out_specs=pl.BlockSpec((b_len, sub, D_WIDTH), lambda i, j: (0, i, 0)),
        out_shape=jax.ShapeDtypeStruct((b_len, a_len, D_WIDTH), F32),
        compiler_params=_params(("arbitrary", "arbitrary")),
        name="fourier_stage2",
    )(tr2, ti2, tabs['m2r'].astype(BF16), tabs['m2i'].astype(BF16), w_fourier_bf)
    return out.reshape(n, D_WIDTH)


def _rope_tables(n):
    rows = n // GRID_W
    row = np.repeat(np.arange(rows, dtype=np.float64), GRID_W)
    col = np.tile(np.arange(GRID_W, dtype=np.float64), rows)
    inv = ROPE_THETA ** (-np.arange(0, AXIS_DIM, 2, dtype=np.float64) / AXIS_DIM)
    ang_r = row[:, None] * inv[None, :]
    ang_c = col[:, None] * inv[None, :]
    cos = np.concatenate([np.cos(ang_r)] * 2 + [np.cos(ang_c)] * 2, axis=-1)
    sin = np.concatenate([-np.sin(ang_r), np.sin(ang_r), -np.sin(ang_c), np.sin(ang_c)], axis=-1)
    return jnp.asarray(cos, dtype=F32), jnp.asarray(sin, dtype=F32)


def kernel(x, c, ctx, c_ctx, w_mod, b_mod, norm1_g, norm2_g, ab_w_in, a_q_norm_g, a_k_norm_g, a_sink,
           b_w_pool, b_pool_scale, ab_w_out, cd_w_in, c_v_norm_g, c_w_spatial, c_b_spatial, d_w_fourier,
           cd_w_out, f_w_up, f_conv_w, f_conv_b, f_w_down):
    batch, n, _ = x.shape
    ctx_len = ctx.shape[1]
    assert batch == 1 and DEPTH == 2
    x2 = x.reshape(n, D_MODEL)
    ctx2 = ctx.reshape(ctx_len, D_MODEL)

    mod = _mod_vectors(c, c_ctx, w_mod, b_mod)

    def split6(v):
        return [v[:, k * D_MODEL:(k + 1) * D_MODEL] for k in range(6)]

    row1 = lambda v: v.reshape(1, -1)
    n_tiles = n // PROJ_TM
    attn_steps = n // (ATTN_SUBS * BLOCK)
    up_steps = (n // FFN_UP_TM) * (FFN_UP_STEPS + 1)
    down_steps = (n // FFN_DOWN_TM) * FFN_DOWN_STEPS
    cast_down0 = _CastJob(f_w_down, 0, D_FF // n_tiles, n_tiles, col_block=FFN_TN)
    cast_up0 = _CastJob(f_w_up, 0, D_MODEL // attn_steps, attn_steps, col_block=FFN_TF)
    cast_up1 = _CastJob(f_w_up, 1, D_MODEL // attn_steps, up_steps, col_block=FFN_TF)
    cast_down1 = _CastJob(f_w_down, 1, D_FF // down_steps, down_steps, col_block=FFN_TN)

    ml = split6(mod[0, 0:1])
    mc = split6(mod[0, 1:2])
    g1 = row1(norm1_g[0])
    w_in = ab_w_in[0].astype(BF16)
    qn = row1(a_q_norm_g[0])
    kn = row1(a_k_norm_g[0])
    cos, sin = _rope_tables(n)
    qkv, z, w_down0 = _ab_in_proj(x2, g1, ml[0], ml[1], w_in, cos, sin, qn, kn, tm=PROJ_TM,
                                  cast=cast_down0)
    kv_ctx = _ctx_kv_proj(ctx2, g1, mc[0], mc[1], w_in, kn)
    attn, w_up0 = _window_attention(qkv, kv_ctx, a_sink[0], cast=cast_up0)
    pool_tiles = n // POOL_TM
    pooled, w_out0 = _pool_mix(z, b_w_pool[0].astype(BF16), row1(b_pool_scale[0]), tm=POOL_TM,
                               cast=_CastJob(ab_w_out, 0, D_MODEL // pool_tiles, pool_tiles))
    x2, h2, cd_w_in_bf = _out_proj(attn, pooled, w_out0, x2, ml[2], row1(norm2_g[0]), ml[3], ml[4],
                                   tm=PROJ_TM, cast=_CastJob(cd_w_in, 0, D_MODEL // n_tiles, n_tiles))
    x2, (w_up1,), (w_down1,) = _conv_ffn(x2, h2, w_up0, f_conv_w[0], f_conv_b[0], w_down0, ml[5],
                                         tm=FFN_UP_TM, down_tm=FFN_DOWN_TM,
                                         up_cast=cast_up1, down_cast=cast_down1)

    ml = split6(mod[1, 0:1])
    uv, f, w_out1 = _cd_in_proj(x2, row1(norm1_g[1]), ml[0], ml[1], cd_w_in_bf, row1(c_v_norm_g[0]),
                                tm=PROJ_TM, cast=_CastJob(cd_w_out, 0, D_MODEL // n_tiles, n_tiles))
    bias_full = jnp.repeat(c_b_spatial[0].T, C_GROUP_DIM, axis=1)
    c_out = _spatial_gate(uv, c_w_spatial[0].astype(BF16), bias_full)
    d_out = _fourier_mix(f, _fourier_tables(n), d_w_fourier[0].astype(BF16))
    x2, h2 = _out_proj(c_out, d_out, w_out1, x2, ml[2], row1(norm2_g[1]), ml[3], ml[4], tm=PROJ_TM)
    x2, _, _ = _conv_ffn(x2, h2, w_up1, f_conv_w[1], f_conv_b[1], w_down1, ml[5],
                         tm=FFN_UP_TM, down_tm=FFN_DOWN_TM)
    return x2.reshape(batch, n, D_MODEL)
```

```python
import functools

import numpy as np
import jax
import jax.numpy as jnp
from jax import lax
from jax.experimental import pallas as pl
from jax.experimental.pallas import tpu as pltpu

F32 = jnp.float32
BF16 = jnp.bfloat16

D_MODEL = 2048
DEPTH = 2
GRID_W = 64
HEAD_DIM = 128
A_Q_HEADS = 8
A_KV_HEADS = 2
A_GROUP = A_Q_HEADS // A_KV_HEADS
A_Q_DIM = A_Q_HEADS * HEAD_DIM
A_KV_DIM = A_KV_HEADS * HEAD_DIM
A_QKV_DIM = A_Q_DIM + 2 * A_KV_DIM
WINDOW = 128
BLOCK = 128
ROPE_THETA = 10000.0
AXIS_DIM = HEAD_DIM // 2
ATTN_SCALE = HEAD_DIM ** -0.5
NEG_INF = -1e30
B_GROUPS = 4
B_WIDTH = 1024
B_GROUP_DIM = B_WIDTH // B_GROUPS
POOL_WINDOWS = (2, 4, 8, 16)
AB_IN = A_QKV_DIM + B_WIDTH
C_WIDTH = 1024
C_GROUPS = 4
C_GROUP_DIM = C_WIDTH // C_GROUPS
CHUNK = 128
D_WIDTH = 1024
D_GROUPS = 8
D_GROUP_DIM = D_WIDTH // D_GROUPS
CD_IN = 2 * C_WIDTH + D_WIDTH
D_FF = 5632
EPS = 1e-6

V7X_SUBLANES_F32 = 8
V7X_SUBLANES_BF16 = 16
V7X_LANES = 128
V7X_VMEM_BYTES = 64 * 1024 * 1024
VMEM_LIMIT = V7X_VMEM_BYTES * 7 // 8

PROJ_TM = 512
POOL_TM = 512
ATTN_SUBS = 4
FFN_UP_TM = 1024
FFN_DOWN_TM = 1024

FFT_A = 64
FFT_B = 128


def _params(sem):
    return pltpu.CompilerParams(dimension_semantics=sem, vmem_limit_bytes=VMEM_LIMIT)


def _dot(a, b):
    return jnp.dot(a, b, preferred_element_type=F32)


class _CastJob:
    def __init__(self, src, layer, rows, n_steps, col_block=None):
        _, total_rows, self.cols = src.shape
        assert total_rows % rows == 0 and total_rows // rows <= n_steps
        self.src, self.layer, self.rows, self.col_block = src, layer, rows, col_block
        self.last = total_rows // rows - 1
        if col_block is None:
            self.out_shape = jax.ShapeDtypeStruct((total_rows, self.cols), BF16)
        else:
            assert self.cols % col_block == 0
            self.out_shape = jax.ShapeDtypeStruct((self.cols // col_block, total_rows, col_block), BF16)

    def specs(self, step_of):
        blk = lambda *ids: jnp.minimum(step_of(*ids), self.last)
        src = pl.BlockSpec((None, self.rows, self.cols), lambda *ids: (self.layer, blk(*ids), 0))
        if self.col_block is None:
            return src, pl.BlockSpec((self.rows, self.cols), lambda *ids: (blk(*ids), 0))
        return src, pl.BlockSpec((self.cols // self.col_block, self.rows, self.col_block),
                                 lambda *ids: (0, blk(*ids), 0))

    def run(self, src_ref, dst_ref):
        if self.col_block is None:
            dst_ref[...] = src_ref[...].astype(BF16)
        else:
            for b in range(self.cols // self.col_block):
                dst_ref[b] = src_ref[:, b * self.col_block:(b + 1) * self.col_block].astype(BF16)


def _host_cast(kernel_fn, cast, n_in, n_out):
    def body(*refs):
        cast.run(refs[n_in], refs[n_in + 1 + n_out])
        kernel_fn(*refs[:n_in], *refs[n_in + 1:n_in + 1 + n_out], *refs[n_in + 2 + n_out:])
    return body


def _pallas(kernel_fn, *, grid, in_specs, out_specs, out_shape, args, name, scratch_shapes=(),
            cast=None, step_of=None):
    out_specs, out_shape = list(out_specs), list(out_shape)
    if cast is not None:
        src_spec, dst_spec = cast.specs(step_of)
        kernel_fn = _host_cast(kernel_fn, cast, len(in_specs), len(out_specs))
        in_specs = [*in_specs, src_spec]
        out_specs.append(dst_spec)
        out_shape.append(cast.out_shape)
        args = (*args, cast.src)
    return pl.pallas_call(
        kernel_fn, grid=grid, in_specs=list(in_specs), out_specs=out_specs, out_shape=out_shape,
        scratch_shapes=list(scratch_shapes), compiler_params=_params(("arbitrary",) * len(grid)),
        name=name)(*args)


def _norm_mod(x, g, shift, scale):
    ms = jnp.mean(x * x, axis=-1, keepdims=True)
    return x * lax.rsqrt(ms + EPS) * (g * (1.0 + scale)) + shift


MOD_TK = 256
MOD_STREAMS = 2


def _mod_kernel(cv_ref, w_ref, b_ref, o_ref, acc_scr):
    k = pl.program_id(1)
    sub = V7X_SUBLANES_F32
    groups = MOD_TK // sub
    n_out = 6 * D_MODEL
    lane_tiles = n_out // V7X_LANES

    @pl.when(k == 0)
    def _():
        acc_scr[...] = jnp.zeros_like(acc_scr)

    w = w_ref[0].reshape(groups, sub, n_out)
    for s in range(MOD_STREAMS):
        a = cv_ref[s]
        a = (a * jax.nn.sigmoid(a)).reshape(groups, sub, V7X_LANES)
        a = jnp.concatenate([a] * lane_tiles, axis=-1)
        acc_scr[s] += jnp.sum(w * a, axis=0)

    @pl.when(k == pl.num_programs(1) - 1)
    def _():
        rows = [jnp.sum(acc_scr[s], axis=0, keepdims=True) for s in range(MOD_STREAMS)]
        rows.append(jnp.zeros((sub - MOD_STREAMS, n_out), F32))
        o_ref[0] = jnp.concatenate(rows, axis=0) + b_ref[0]


def _mod_vectors(c, c_ctx, w_mod, b_mod):
    n_out = 6 * D_MODEL
    cv = jnp.stack([c.reshape(D_MODEL), c_ctx.reshape(D_MODEL)])
    cv = jnp.broadcast_to(cv[:, :, None], (MOD_STREAMS, D_MODEL, V7X_LANES))
    b3 = b_mod.reshape(DEPTH, 1, n_out)
    return pl.pallas_call(
        _mod_kernel,
        grid=(DEPTH, D_MODEL // MOD_TK),
        in_specs=[
            pl.BlockSpec((MOD_STREAMS, MOD_TK, V7X_LANES), lambda l, k: (0, k, 0)),
            pl.BlockSpec((1, MOD_TK, n_out), lambda l, k: (l, k, 0)),
            pl.BlockSpec((1, 1, n_out), lambda l, k: (l, 0, 0)),
        ],
        out_specs=pl.BlockSpec((1, V7X_SUBLANES_F32, n_out), lambda l, k: (l, 0, 0)),
        out_shape=jax.ShapeDtypeStruct((DEPTH, V7X_SUBLANES_F32, n_out), F32),
        scratch_shapes=[pltpu.VMEM((MOD_STREAMS, V7X_SUBLANES_F32, n_out), F32)],
        compiler_params=_params(("arbitrary", "arbitrary")),
        name="mod_vectors",
    )(cv, w_mod, b3)


AB_TN = 512
ROW_SPLIT = 1


def _rope(t, cos, sin_signed):
    lane = lax.broadcasted_iota(jnp.int32, t.shape, 1)
    first = (lane % AXIS_DIM) < (AXIS_DIM // 2)
    partner = jnp.where(first,
                        pltpu.roll(t, HEAD_DIM - AXIS_DIM // 2, 1),
                        pltpu.roll(t, AXIS_DIM // 2, 1))
    return t * cos + partner * sin_signed


def _head_norm_rope(t, g, cos, sin_signed):
    ms = jnp.mean(t * t, axis=-1, keepdims=True)
    return _rope(t * lax.rsqrt(ms + EPS) * g, cos, sin_signed)


def _ctx_kv_kernel(x_ref, g_ref, sh_ref, sc_ref, w_ref, kn_ref, kv_ref):
    h = _norm_mod(x_ref[...], g_ref[...], sh_ref[...], sc_ref[...]).astype(BF16)
    p = _dot(h, w_ref[...])
    for hh in range(A_KV_HEADS):
        t = p[:, hh * HEAD_DIM:(hh + 1) * HEAD_DIM]
        ms = jnp.mean(t * t, axis=-1, keepdims=True)
        kv_ref[:, hh * HEAD_DIM:(hh + 1) * HEAD_DIM] = (t * lax.rsqrt(ms + EPS) * kn_ref[...]).astype(BF16)
    kv_ref[:, A_KV_DIM:] = p[:, A_KV_DIM:].astype(BF16)


def _ctx_kv_proj(ctx2, g, shift, scale, w_in_bf, kn):
    ctx_len = ctx2.shape[0]
    kv_cols = 2 * A_KV_DIM
    assert A_Q_DIM % kv_cols == 0
    row = lambda i: (0, 0)
    return _pallas(
        _ctx_kv_kernel,
        grid=(1,),
        in_specs=[
            pl.BlockSpec((ctx_len, D_MODEL), row),
            pl.BlockSpec((1, D_MODEL), row),
            pl.BlockSpec((1, D_MODEL), row),
            pl.BlockSpec((1, D_MODEL), row),
            pl.BlockSpec((D_MODEL, kv_cols), lambda i: (0, A_Q_DIM // kv_cols)),
            pl.BlockSpec((1, HEAD_DIM), row),
        ],
        out_specs=[pl.BlockSpec((ctx_len, kv_cols), row)],
        out_shape=[jax.ShapeDtypeStruct((ctx_len, kv_cols), BF16)],
        name="ctx_kv_proj",
        args=(ctx2, g, shift, scale, w_in_bf, kn))[0]


def _ab_in_kernel(x_ref, g_ref, sh_ref, sc_ref, w_ref, cos_ref, sin_ref, qn_ref, kn_ref,
                  qkv_ref, z_ref, *, tm):
    rows_per = tm // ROW_SPLIT
    n_q_tiles = A_Q_DIM // AB_TN
    for part in range(ROW_SPLIT):
        rs = slice(part * rows_per, (part + 1) * rows_per)
        h = _norm_mod(x_ref[rs, :], g_ref[...], sh_ref[...], sc_ref[...]).astype(BF16)
        cos = cos_ref[rs, :]
        sin = sin_ref[rs, :]
        for t in range(AB_IN // AB_TN):
            p = _dot(h, w_ref[:, t * AB_TN:(t + 1) * AB_TN])
            if t < n_q_tiles:
                for hh in range(AB_TN // HEAD_DIM):
                    c0 = t * AB_TN + hh * HEAD_DIM
                    qkv_ref[rs, c0:c0 + HEAD_DIM] = _head_norm_rope(
                        p[:, hh * HEAD_DIM:(hh + 1) * HEAD_DIM], qn_ref[...] * ATTN_SCALE,
                        cos, sin).astype(BF16)
            elif t == n_q_tiles:
                for hh in range(A_KV_HEADS):
                    c0 = A_Q_DIM + hh * HEAD_DIM
                    qkv_ref[rs, c0:c0 + HEAD_DIM] = _head_norm_rope(
                        p[:, hh * HEAD_DIM:(hh + 1) * HEAD_DIM], kn_ref[...], cos, sin).astype(BF16)
                qkv_ref[rs, A_Q_DIM + A_KV_DIM:] = p[:, A_KV_DIM:].astype(BF16)
            else:
                c0 = (t - n_q_tiles - 1) * AB_TN
                z_ref[rs, c0:c0 + AB_TN] = p


def _ab_in_proj(x2, g, shift, scale, w_in_bf, cos, sin, qn, kn, tm, cast=None):
    n = x2.shape[0]
    assert A_Q_DIM % AB_TN == 0 and 2 * A_KV_DIM == AB_TN and B_WIDTH % AB_TN == 0
    row = lambda i: (0, 0)
    return _pallas(
        functools.partial(_ab_in_kernel, tm=tm),
        grid=(n // tm,),
        cast=cast, step_of=lambda i: i,
        in_specs=[
            pl.BlockSpec((tm, D_MODEL), lambda i: (i, 0)),
            pl.BlockSpec((1, D_MODEL), row),
            pl.BlockSpec((1, D_MODEL), row),
            pl.BlockSpec((1, D_MODEL), row),
            pl.BlockSpec((D_MODEL, AB_IN), row, pipeline_mode=pl.Buffered(1)),
            pl.BlockSpec((tm, HEAD_DIM), lambda i: (i, 0)),
            pl.BlockSpec((tm, HEAD_DIM), lambda i: (i, 0)),
            pl.BlockSpec((1, HEAD_DIM), row),
            pl.BlockSpec((1, HEAD_DIM), row),
        ],
        out_specs=[
            pl.BlockSpec((tm, A_QKV_DIM), lambda i: (i, 0)),
            pl.BlockSpec((tm, B_WIDTH), lambda i: (i, 0)),
        ],
        out_shape=[jax.ShapeDtypeStruct((n, A_QKV_DIM), BF16),
                   jax.ShapeDtypeStruct((n, B_WIDTH), F32)],
        name="ab_in_proj",
        args=(x2, g, shift, scale, w_in_bf, cos, sin, qn, kn))


def _attn_kernel(sink_ref, bias_ref, q_ref, kp_ref, kc_ref, kn_ref, vp_ref, vc_ref, vn_ref,
                 kx_ref, vx_ref, o_ref):
    i = pl.program_id(0)
    last = pl.num_programs(0) - 1
    rows = A_GROUP * BLOCK
    band_cols = 3 * BLOCK
    n_keys = band_cols + kx_ref.shape[0]
    r1 = lax.broadcasted_iota(jnp.int32, (rows, 1), 0) // BLOCK
    ones_col = (lax.broadcasted_iota(jnp.int32, (n_keys, HEAD_DIM), 1) == 0).astype(BF16)

    def key_block(prev_ref, cur_ref, next_ref, j, hs):
        if j == 0:
            return prev_ref[:, hs]
        if j == ATTN_SUBS + 1:
            return next_ref[:, hs]
        return cur_ref[(j - 1) * BLOCK:j * BLOCK, hs]

    for sb, hk in [(sb, hk) for sb in range(ATTN_SUBS) for hk in range(A_KV_HEADS)]:
        rs = slice(sb * BLOCK, (sb + 1) * BLOCK)
        if sb == 0:
            bias = jnp.where(i == 0, bias_ref[0], bias_ref[1])
        elif sb == ATTN_SUBS - 1:
            bias = jnp.where(i == last, bias_ref[2], bias_ref[1])
        else:
            bias = bias_ref[1]
        bias = jnp.concatenate([bias] * A_GROUP, axis=0)
        hs = slice(hk * HEAD_DIM, (hk + 1) * HEAD_DIM)
        kcat = jnp.concatenate([key_block(kp_ref, kc_ref, kn_ref, sb + d, hs) for d in range(3)]
                               + [kx_ref[:, hs]], axis=0)
        vcat = jnp.concatenate([key_block(vp_ref, vc_ref, vn_ref, sb + d, hs) for d in range(3)]
                               + [vx_ref[:, hs]], axis=0)
        q0 = hk * A_GROUP * HEAD_DIM
        q4 = jnp.concatenate(
            [q_ref[rs, q0 + g * HEAD_DIM:q0 + (g + 1) * HEAD_DIM] for g in range(A_GROUP)], axis=0)
        s = lax.dot_general(q4, kcat, (((1,), (1,)), ((), ())), preferred_element_type=F32)
        s = jnp.concatenate([s[:, :band_cols] + bias, s[:, band_cols:]], axis=1)
        sink = jnp.zeros((rows, 1), F32)
        for g in range(A_GROUP):
            sink = jnp.where(r1 == g, sink_ref[hk * A_GROUP + g], sink)
        m = jnp.maximum(jnp.max(s, axis=-1, keepdims=True), sink)
        e = jnp.exp(s - m).astype(BF16)
        pv = _dot(e, jnp.concatenate([vcat, ones_col], axis=1))
        den = pv[:, HEAD_DIM:HEAD_DIM + 1] + jnp.exp(sink - m)
        o4 = pv[:, :HEAD_DIM] / den
        for g in range(A_GROUP):
            o_ref[rs, q0 + g * HEAD_DIM:q0 + (g + 1) * HEAD_DIM] = (
                o4[g * BLOCK:(g + 1) * BLOCK].astype(BF16))


def _attn_bias():
    r = np.arange(BLOCK)[:, None]
    c = np.arange(3 * BLOCK)[None, :]
    band = np.abs(BLOCK + r - c) <= WINDOW
    first = band & (c >= BLOCK)
    last = band & (c < 2 * BLOCK)
    masks = np.stack([first, band, last])
    return jnp.asarray(np.where(masks, 0.0, NEG_INF), dtype=F32)


def _window_attention(qkv, kv_ctx, sink, cast=None):
    n = qkv.shape[0]
    ctx_len = kv_ctx.shape[0]
    step_rows = ATTN_SUBS * BLOCK
    n_tiles = n // step_rows
    n_blocks = n // BLOCK
    assert ATTN_SUBS >= 2 and n % step_rows == 0 and WINDOW <= BLOCK
    k_col = A_Q_DIM // A_KV_DIM
    v_col = k_col + 1
    prev = lambda i: jnp.maximum(i * ATTN_SUBS - 1, 0)
    nxt = lambda i: jnp.minimum((i + 1) * ATTN_SUBS, n_blocks - 1)
    blk = (BLOCK, A_KV_DIM)
    cur = (step_rows, A_KV_DIM)
    return _pallas(
        _attn_kernel,
        grid=(n_tiles,),
        cast=cast, step_of=lambda i: i,
        in_specs=[
            pl.BlockSpec(memory_space=pltpu.SMEM),
            pl.BlockSpec((3, BLOCK, 3 * BLOCK), lambda i: (0, 0, 0)),
            pl.BlockSpec((step_rows, A_Q_DIM), lambda i: (i, 0)),
            pl.BlockSpec(blk, lambda i: (prev(i), k_col)),
            pl.BlockSpec(cur, lambda i: (i, k_col)),
            pl.BlockSpec(blk, lambda i: (nxt(i), k_col)),
            pl.BlockSpec(blk, lambda i: (prev(i), v_col)),
            pl.BlockSpec(cur, lambda i: (i, v_col)),
            pl.BlockSpec(blk, lambda i: (nxt(i), v_col)),
            pl.BlockSpec((ctx_len, A_KV_DIM), lambda i: (0, 0)),
            pl.BlockSpec((ctx_len, A_KV_DIM), lambda i: (0, 1)),
        ],
        out_specs=[pl.BlockSpec((step_rows, A_Q_DIM), lambda i: (i, 0))],
        out_shape=[jax.ShapeDtypeStruct((n, A_Q_DIM), BF16)],
        name="window_attention",
        args=(sink, _attn_bias(), qkv, qkv, qkv, qkv, qkv, qkv, qkv, kv_ctx, kv_ctx))


POOL_HALO = 8
assert max(POOL_WINDOWS) // 2 <= POOL_HALO


def _pool_kernel(zm_ref, zp_ref, zn_ref, w_ref, ps_ref, o_ref, z_scr, *, n_rows, tm):
    i = pl.program_id(0)
    last = pl.num_programs(0) - 1
    pad = POOL_HALO
    span_rows = tm + 2 * pad
    z_scr[0:pad, :] = jnp.zeros((pad, B_WIDTH), F32)
    z_scr[pad:2 * pad, :] = jnp.where(i > 0, zp_ref[...], 0.0)
    z_scr[2 * pad:2 * pad + tm, :] = zm_ref[...]
    z_scr[2 * pad + tm:3 * pad + tm, :] = jnp.where(i < last, zn_ref[...], 0.0)
    z_scr[3 * pad + tm:, :] = jnp.zeros((pad, B_WIDTH), F32)
    t = i * tm + lax.broadcasted_iota(jnp.int32, (tm, B_GROUP_DIM), 0)
    for g in range(B_GROUPS):
        window = POOL_WINDOWS[g]
        half = window // 2
        cs = slice(g * B_GROUP_DIM, (g + 1) * B_GROUP_DIM)
        s = z_scr[pl.ds(2 * pad - half, span_rows), cs]
        width = 1
        while width < window:
            s = s + pltpu.roll(s, span_rows - width, 0)
            width *= 2
        acc = s[0:tm]
        cnt = (jnp.minimum(t + half, n_rows) - jnp.maximum(t - half, 0)).astype(F32)
        d = (acc / cnt - zm_ref[:, cs]).astype(BF16)
        y = _dot(d, w_ref[g]) * ps_ref[:, cs]
        o_ref[:, cs] = y.astype(BF16)


def _pool_mix(z, w_pool_bf, pool_scale, tm, cast=None):
    n = z.shape[0]
    hb = tm // POOL_HALO
    n_halo_blocks = n // POOL_HALO
    return _pallas(
        functools.partial(_pool_kernel, n_rows=n, tm=tm),
        grid=(n // tm,),
        cast=cast, step_of=lambda i: i,
        in_specs=[
            pl.BlockSpec((tm, B_WIDTH), lambda i: (i, 0)),
            pl.BlockSpec((POOL_HALO, B_WIDTH), lambda i: (jnp.maximum(i * hb - 1, 0), 0)),
            pl.BlockSpec((POOL_HALO, B_WIDTH), lambda i: (jnp.minimum((i + 1) * hb, n_halo_blocks - 1), 0)),
            pl.BlockSpec((B_GROUPS, B_GROUP_DIM, B_GROUP_DIM), lambda i: (0, 0, 0)),
            pl.BlockSpec((1, B_WIDTH), lambda i: (0, 0)),
        ],
        out_specs=[pl.BlockSpec((tm, B_WIDTH), lambda i: (i, 0))],
        out_shape=[jax.ShapeDtypeStruct((n, B_WIDTH), BF16)],
        scratch_shapes=[pltpu.VMEM((tm + 4 * POOL_HALO, B_WIDTH), F32)],
        name="pool_mix",
        args=(z, z, z, w_pool_bf, pool_scale))


OUT_TN = 512


def _out_proj_kernel(a1_ref, a2_ref, w_ref, x_ref, gate_ref, g_ref, sh_ref, sc_ref, o_ref, h_ref, *, tm):
    rows_per = tm // ROW_SPLIT
    for part in range(ROW_SPLIT):
        rs = slice(part * rows_per, (part + 1) * rows_per)
        a = jnp.concatenate([a1_ref[rs, :].astype(BF16), a2_ref[rs, :].astype(BF16)], axis=1)
        for t in range(D_MODEL // OUT_TN):
            cs = slice(t * OUT_TN, (t + 1) * OUT_TN)
            o_ref[rs, cs] = x_ref[rs, cs] + gate_ref[:, cs] * _dot(a, w_ref[:, cs])
        h_ref[rs, :] = _norm_mod(o_ref[rs, :], g_ref[...], sh_ref[...], sc_ref[...])


def _out_proj(a1, a2, w_out_bf, x2, gate, g2, shift2, scale2, tm, cast=None):
    n = x2.shape[0]
    k1 = a1.shape[1]
    k2 = a2.shape[1]
    assert k1 + k2 == w_out_bf.shape[0] and k1 % V7X_LANES == 0
    row = lambda i: (0, 0)
    return _pallas(
        functools.partial(_out_proj_kernel, tm=tm),
        grid=(n // tm,),
        cast=cast, step_of=lambda i: i,
        in_specs=[
            pl.BlockSpec((tm, k1), lambda i: (i, 0)),
            pl.BlockSpec((tm, k2), lambda i: (i, 0)),
            pl.BlockSpec((k1 + k2, D_MODEL), row, pipeline_mode=pl.Buffered(1)),
            pl.BlockSpec((tm, D_MODEL), lambda i: (i, 0)),
            pl.BlockSpec((1, D_MODEL), row),
            pl.BlockSpec((1, D_MODEL), row),
            pl.BlockSpec((1, D_MODEL), row),
            pl.BlockSpec((1, D_MODEL), row),
        ],
        out_specs=[pl.BlockSpec((tm, D_MODEL), lambda i: (i, 0)),
                   pl.BlockSpec((tm, D_MODEL), lambda i: (i, 0))],
        out_shape=[jax.ShapeDtypeStruct((n, D_MODEL), F32),
                   jax.ShapeDtypeStruct((n, D_MODEL), F32)],
        name="out_proj",
        args=(a1, a2, w_out_bf, x2, gate, g2, shift2, scale2))


FFN_SLABS = V7X_SUBLANES_F32
FFN_EDGE_ROWS = V7X_SUBLANES_BF16
FFN_PIECE_ROWS = 16
FFN_ELEMENTWISE_DTYPE = BF16
FFN_CW = 256
FFN_TF = 2 * FFN_CW
FFN_TN = 512
FFN_CHUNKS = D_FF // FFN_CW
FFN_UP_STEPS = D_FF // FFN_TF
FFN_DOWN_STEPS = D_MODEL // FFN_TN


def _ffn_up_kernel(hm_ref, hp_ref, hn_ref, wg_ref, wv_ref, cw_ref, cb_ref, o_ref,
                   h_scr, carry_a, carry_b, *, tm):
    i = pl.program_id(0)
    j = pl.program_id(1)
    last_i = pl.num_programs(0) - 1
    nc = FFN_CHUNKS
    ns = FFN_SLABS
    sr = tm // ns

    def finish(carry, k, r, step):
        c = jnp.maximum(2 * step + k, 0)

        def conv(idx, kk, q0):
            slab = lambda s: carry[idx, pl.ds(s * sr + q0, FFN_PIECE_ROWS), :]
            prev = slab(r - 1) if r > 0 else slab(ns)
            nxt = slab(r + 1) if r < ns - 1 else slab(ns + 1)
            return prev * cw_ref[kk, 0] + slab(r) * cw_ref[kk, 1] + nxt * cw_ref[kk, 2] + cb_ref[kk]

        for q0 in range(0, sr, FFN_PIECE_ROWS):
            gg = conv(k, c, q0)
            vv = conv(2 + k, c + nc, q0)
            o_ref[pl.ds(r * sr + q0, FFN_PIECE_ROWS), k * FFN_CW:(k + 1) * FFN_CW] = (
                gg * jax.nn.sigmoid(gg) * vv).astype(BF16)

    @pl.when(j == 0)
    def _():
        hm = hm_ref[...].reshape(sr, ns, D_MODEL)
        h_scr[0:tm, :] = jnp.swapaxes(hm, 0, 1).reshape(tm, D_MODEL).astype(BF16)
        before = jnp.where(i > 0, hp_ref[ns - 1:ns, :], 0.0)
        after = jnp.where(i < last_i, hn_ref[0:1, :], 0.0)
        pad = jnp.zeros((FFN_EDGE_ROWS - 2, D_MODEL), F32)
        h_scr[tm:, :] = jnp.concatenate([before, after, pad], axis=0).astype(BF16)
        carry_b[...] = jnp.zeros_like(carry_b)

    def keep(carry, idx, u):
        dt = FFN_ELEMENTWISE_DTYPE
        carry[idx, 0:tm, :] = u[0:tm].astype(dt)
        carry[idx, tm:tm + sr, :] = jnp.concatenate(
            [u[tm:tm + 1], u[(ns - 1) * sr:ns * sr - 1]], axis=0).astype(dt)
        carry[idx, tm + sr:, :] = jnp.concatenate([u[1:sr], u[tm + 1:tm + 2]], axis=0).astype(dt)

    def up_step(write, read):
        for k in range(2):
            for r in range(ns):
                finish(read, k, r, j - 1)
        for k in range(2):
            cs = slice(k * FFN_CW, (k + 1) * FFN_CW)
            keep(write, k, _dot(h_scr[...], wg_ref[:, cs]))
            keep(write, 2 + k, _dot(h_scr[...], wv_ref[:, cs]))

    @pl.when((j < FFN_UP_STEPS) & (j % 2 == 0))
    def _():
        up_step(carry_a, carry_b)

    @pl.when((j < FFN_UP_STEPS) & (j % 2 == 1))
    def _():
        up_step(carry_b, carry_a)

    @pl.when(j == FFN_UP_STEPS)
    def _():
        last = carry_a if (FFN_UP_STEPS - 1) % 2 == 0 else carry_b
        for k in range(2):
            for r in range(ns):
                finish(last, k, r, FFN_UP_STEPS - 1)


def _ffn_down_kernel(a_ref, wd_ref, x_ref, gate_ref, o_ref, *, tm, up_tm):
    ns = FFN_SLABS
    sr = up_tm // ns
    y = _dot(a_ref[:, FFN_TF:], wd_ref[...])
    y = jnp.swapaxes(y.reshape(tm // up_tm, ns, sr, FFN_TN), 1, 2).reshape(tm, FFN_TN)
    o_ref[...] = x_ref[...] + gate_ref[...] * y


def _conv_ffn(x2, h2, w_up_bf, conv_w, conv_b, w_down_bf, gate, tm, down_tm, up_cast=None,
              down_cast=None):
    n = x2.shape[0]
    nj = FFN_UP_STEPS
    hb = tm // FFN_SLABS
    n_halo_blocks = n // FFN_SLABS
    act_cols = D_FF + FFN_TF
    cw3 = conv_w.reshape(3, 2 * FFN_CHUNKS, FFN_CW).transpose(1, 0, 2)
    cw3 = jnp.broadcast_to(cw3[:, :, None, :], (2 * FFN_CHUNKS, 3, FFN_PIECE_ROWS, FFN_CW))
    cb3 = jnp.broadcast_to(conv_b.reshape(2 * FFN_CHUNKS, 1, FFN_CW),
                           (2 * FFN_CHUNKS, FFN_PIECE_ROWS, FFN_CW))
    cw3 = cw3.astype(FFN_ELEMENTWISE_DTYPE)
    cb3 = cb3.astype(FFN_ELEMENTWISE_DTYPE)
    once = pl.Buffered(1)
    act, *up_cast = _pallas(
        functools.partial(_ffn_up_kernel, tm=tm),
        grid=(n // tm, nj + 1),
        cast=up_cast, step_of=lambda i, j: i * (nj + 1) + j,
        in_specs=[
            pl.BlockSpec((tm, D_MODEL), lambda i, j: (i, 0)),
            pl.BlockSpec((FFN_SLABS, D_MODEL), lambda i, j: (jnp.maximum(i * hb - 1, 0), 0)),
            pl.BlockSpec((FFN_SLABS, D_MODEL), lambda i, j: (jnp.minimum((i + 1) * hb, n_halo_blocks - 1), 0)),
            pl.BlockSpec((None, D_MODEL, FFN_TF), lambda i, j: (jnp.minimum(j, nj - 1), 0, 0)),
            pl.BlockSpec((None, D_MODEL, FFN_TF), lambda i, j: (jnp.minimum(j, nj - 1) + nj, 0, 0)),
            pl.BlockSpec((2 * FFN_CHUNKS, 3, FFN_PIECE_ROWS, FFN_CW), lambda i, j: (0, 0, 0, 0),
                         pipeline_mode=once),
            pl.BlockSpec((2 * FFN_CHUNKS, FFN_PIECE_ROWS, FFN_CW), lambda i, j: (0, 0, 0),
                         pipeline_mode=once),
        ],
        out_specs=[pl.BlockSpec((tm, FFN_TF), lambda i, j: (i, j))],
        out_shape=[jax.ShapeDtypeStruct((n, act_cols), BF16)],
        scratch_shapes=[pltpu.VMEM((tm + FFN_EDGE_ROWS, D_MODEL), BF16),
                        pltpu.VMEM((4, tm + 2 * (tm // FFN_SLABS), FFN_CW), FFN_ELEMENTWISE_DTYPE),
                        pltpu.VMEM((4, tm + 2 * (tm // FFN_SLABS), FFN_CW), FFN_ELEMENTWISE_DTYPE)],
        name="ffn_up",
        args=(h2, h2, h2, w_up_bf, w_up_bf, cw3, cb3))
    out, *down_cast = _pallas(
        functools.partial(_ffn_down_kernel, tm=down_tm, up_tm=tm),
        grid=(n // down_tm, FFN_DOWN_STEPS),
        cast=down_cast, step_of=lambda i, j: i * FFN_DOWN_STEPS + j,
        in_specs=[
            pl.BlockSpec((down_tm, act_cols), lambda i, j: (i, 0)),
            pl.BlockSpec((None, D_FF, FFN_TN), lambda i, j: (j, 0, 0)),
            pl.BlockSpec((down_tm, FFN_TN), lambda i, j: (i, j)),
            pl.BlockSpec((1, FFN_TN), lambda i, j: (0, j)),
        ],
        out_specs=[pl.BlockSpec((down_tm, FFN_TN), lambda i, j: (i, j))],
        out_shape=[jax.ShapeDtypeStruct((n, D_MODEL), F32)],
        name="ffn_down",
        args=(act, w_down_bf, x2, gate))
    return out, up_cast, down_cast


def _gelu(x):
    return 0.5 * x * (1.0 + lax.erf(x * (2.0 ** -0.5)))


def _cd_in_kernel(x_ref, g_ref, sh_ref, sc_ref, w_ref, vg_ref, uv_ref, f_ref, *, tm):
    rows_per = tm // ROW_SPLIT
    for part in range(ROW_SPLIT):
        rs = slice(part * rows_per, (part + 1) * rows_per)
        h = _norm_mod(x_ref[rs, :], g_ref[...], sh_ref[...], sc_ref[...]).astype(BF16)
        uv_ref[rs, 0:C_WIDTH] = _gelu(_dot(h, w_ref[:, 0:C_WIDTH])).astype(BF16)
        v = _gelu(_dot(h, w_ref[:, C_WIDTH:2 * C_WIDTH]))
        ms = jnp.mean(v * v, axis=-1, keepdims=True)
        uv_ref[rs, C_WIDTH:] = (v * lax.rsqrt(ms + EPS) * vg_ref[...]).astype(BF16)
        f_ref[rs, :] = _dot(h, w_ref[:, 2 * C_WIDTH:])


def _cd_in_proj(x2, g, shift, scale, w_in_bf, v_norm_g, tm, cast=None):
    n = x2.shape[0]
    row = lambda i: (0, 0)
    return _pallas(
        functools.partial(_cd_in_kernel, tm=tm),
        grid=(n // tm,),
        cast=cast, step_of=lambda i: i,
        in_specs=[
            pl.BlockSpec((tm, D_MODEL), lambda i: (i, 0)),
            pl.BlockSpec((1, D_MODEL), row),
            pl.BlockSpec((1, D_MODEL), row),
            pl.BlockSpec((1, D_MODEL), row),
            pl.BlockSpec((D_MODEL, CD_IN), row, pipeline_mode=pl.Buffered(1)),
            pl.BlockSpec((1, C_WIDTH), row),
        ],
        out_specs=[
            pl.BlockSpec((tm, 2 * C_WIDTH), lambda i: (i, 0)),
            pl.BlockSpec((tm, D_WIDTH), lambda i: (i, 0)),
        ],
        out_shape=[jax.ShapeDtypeStruct((n, 2 * C_WIDTH), BF16),
                   jax.ShapeDtypeStruct((n, D_WIDTH), F32)],
        name="cd_in_proj",
        args=(x2, g, shift, scale, w_in_bf, v_norm_g))


def _spatial_kernel(u_ref, v_ref, ws_ref, bias_ref, o_ref, *, chunks):
    for k in range(chunks):
        rs = slice(k * CHUNK, (k + 1) * CHUNK)
        for g in range(C_GROUPS):
            cs = slice(g * C_GROUP_DIM, (g + 1) * C_GROUP_DIM)
            s = _dot(ws_ref[g], v_ref[rs, cs]) + bias_ref[:, cs]
            o_ref[rs, cs] = (u_ref[rs, cs].astype(F32) * s).astype(BF16)


def _spatial_gate(uv, w_spatial_bf, bias_full, chunks=8):
    n = uv.shape[0]
    rows = chunks * CHUNK
    return pl.pallas_call(
        functools.partial(_spatial_kernel, chunks=chunks),
        grid=(n // rows,),
        in_specs=[
            pl.BlockSpec((rows, C_WIDTH), lambda i: (i, 0)),
            pl.BlockSpec((rows, C_WIDTH), lambda i: (i, 1)),
            pl.BlockSpec((C_GROUPS, CHUNK, CHUNK), lambda i: (0, 0, 0)),
            pl.BlockSpec((CHUNK, C_WIDTH), lambda i: (0, 0)),
        ],
        out_specs=pl.BlockSpec((rows, C_WIDTH), lambda i: (i, 0)),
        out_shape=jax.ShapeDtypeStruct((n, C_WIDTH), BF16),
        compiler_params=_params(("arbitrary",)),
        name="spatial_gate",
    )(uv, uv, w_spatial_bf, bias_full)


FFT_SUB = V7X_SUBLANES_F32
FFT1_CT = 1024
FFT2_CT = 1024


def _fourier_tables(n):
    a_len, b_len, sub = FFT_A, FFT_B, FFT_SUB
    assert a_len * b_len == n
    ch = np.arange(D_GROUP_DIM)
    ang_c = 2.0 * np.pi * ((ch[:, None] * ch[None, :]) % D_GROUP_DIM) / D_GROUP_DIM
    a = np.arange(a_len)
    f_a = np.exp(-2j * np.pi * ((a[:, None] * a[None, :]) % a_len) / a_len)
    m1 = np.kron(f_a, np.eye(sub))
    b = np.arange(b_len)
    tw = np.exp(-2j * np.pi * ((a[:, None] * b[None, :]) % n) / n)
    f_b = np.exp(-2j * np.pi * ((b[:, None] * b[None, :]) % b_len) / b_len)
    m2 = np.einsum('db,pq->dpqb', f_b, np.eye(sub)).reshape(b_len * sub, sub * b_len)
    norm = 1.0 / np.sqrt(float(n) * D_GROUP_DIM)
    m2 = m2 * norm
    tw3 = np.broadcast_to(tw[:, :, None], (a_len, b_len, V7X_LANES))
    f32 = lambda v: jnp.asarray(np.ascontiguousarray(v), dtype=F32)
    return dict(cos_c=f32(np.cos(ang_c)), sin_c=f32(np.sin(ang_c)),
                m1r=f32(m1.real), m1i=f32(m1.imag), m2r=f32(m2.real), m2i=f32(m2.imag),
                twr=f32(tw3.real), twi=f32(tw3.imag))


def _fft1_kernel(f_ref, cc_ref, sc_ref, m1r_ref, m1i_ref, twr_ref, twi_ref, tr_ref, ti_ref):
    rows = FFT_A * FFT_SUB
    ct = FFT1_CT
    fb = f_ref[...].reshape(rows, ct).astype(BF16)
    xr_parts = []
    xi_parts = []
    for q in range(ct // D_GROUP_DIM):
        blk = fb[:, q * D_GROUP_DIM:(q + 1) * D_GROUP_DIM]
        xr_parts.append(_dot(blk, cc_ref[...]))
        xi_parts.append(-_dot(blk, sc_ref[...]))
    xr = jnp.concatenate(xr_parts, axis=1).astype(BF16)
    xi = jnp.concatenate(xi_parts, axis=1).astype(BF16)
    m1r = m1r_ref[...]
    m1i = m1i_ref[...]
    tr = _dot(m1r, xr) - _dot(m1i, xi)
    ti = _dot(m1r, xi) + _dot(m1i, xr)
    reps = ct // V7X_LANES
    twr = jnp.tile(twr_ref[...].reshape(rows, V7X_LANES), (1, reps))
    twi = jnp.tile(twi_ref[...].reshape(rows, V7X_LANES), (1, reps))
    tr_ref[...] = (tr * twr - ti * twi).reshape(FFT_A, FFT_SUB, ct)
    ti_ref[...] = (tr * twi + ti * twr).reshape(FFT_A, FFT_SUB, ct)


def _fft2_kernel(tr_ref, ti_ref, m2r_ref, m2i_ref, wf_ref, o_ref):
    j = pl.program_id(1)
    tr = tr_ref[...].astype(BF16)
    ti = ti_ref[...].astype(BF16)
    z = _dot(m2r_ref[...], tr) - _dot(m2i_ref[...], ti)
    contrib = _dot(z.astype(BF16), wf_ref[...]).reshape(FFT_B, FFT_SUB, D_WIDTH)

    @pl.when(j == 0)
    def _():
        o_ref[...] = contrib

    @pl.when(j > 0)
    def _():
        o_ref[...] += contrib


def _fourier_mix(f, tabs, w_fourier_bf):
    n = f.shape[0]
    a_len, b_len, sub = FFT_A, FFT_B, FFT_SUB
    f3 = f.reshape(a_len, b_len, D_WIDTH)
    rows1 = a_len * sub
    const2 = lambda i, j: (0, 0)
    tr, ti = pl.pallas_call(
        _fft1_kernel,
        grid=(b_len // sub, D_WIDTH // FFT1_CT),
        in_specs=[
            pl.BlockSpec((a_len, sub, FFT1_CT), lambda i, j: (0, i, j)),
            pl.BlockSpec((D_GROUP_DIM, D_GROUP_DIM), const2),
            pl.BlockSpec((D_GROUP_DIM, D_GROUP_DIM), const2),
            pl.BlockSpec((rows1, rows1), const2),
            pl.BlockSpec((rows1, rows1), const2),
            pl.BlockSpec((a_len, sub, V7X_LANES), lambda i, j: (0, i, 0)),
            pl.BlockSpec((a_len, sub, V7X_LANES), lambda i, j: (0, i, 0)),
        ],
        out_specs=[pl.BlockSpec((a_len, sub, FFT1_CT), lambda i, j: (0, i, j))] * 2,
        out_shape=[jax.ShapeDtypeStruct((a_len, b_len, D_WIDTH), F32)] * 2,
        compiler_params=_params(("arbitrary", "arbitrary")),
        name="fourier_stage1",
    )(f3, tabs['cos_c'].astype(BF16), tabs['sin_c'].astype(BF16),
      tabs['m1r'].astype(BF16), tabs['m1i'].astype(BF16), tabs['twr'], tabs['twi'])

    rows2 = sub * b_len
    tr2 = tr.reshape(n, D_WIDTH)
    ti2 = ti.reshape(n, D_WIDTH)
    out = pl.pallas_call(
        _fft2_kernel,
        grid=(a_len // sub, D_WIDTH // FFT2_CT),
        in_specs=[
            pl.BlockSpec((rows2, FFT2_CT), lambda i, j: (i, j)),
            pl.BlockSpec((rows2, FFT2_CT), lambda i, j: (i, j)),
            pl.BlockSpec((rows2, rows2), const2),
            pl.BlockSpec((rows2, rows2), const2),
            pl.BlockSpec((FFT2_CT, D_WIDTH), lambda i, j: (j, 0)),
        ],
        out_specs=pl.BlockSpec((b_len, sub, D_WIDTH), lambda i, j: (0, i, 0)),
        out_shape=jax.ShapeDtypeStruct((b_len, a_len, D_WIDTH), F32),
        compiler_params=_params(("arbitrary", "arbitrary")),
        name="fourier_stage2",
    )(tr2, ti2, tabs['m2r'].astype(BF16), tabs['m2i'].astype(BF16), w_fourier_bf)
    return out.reshape(n, D_WIDTH)


def _rope_tables(n):
    rows = n // GRID_W
    row = np.repeat(np.arange(rows, dtype=np.float64), GRID_W)
    col = np.tile(np.arange(GRID_W, dtype=np.float64), rows)
    inv = ROPE_THETA ** (-np.arange(0, AXIS_DIM, 2, dtype=np.float64) / AXIS_DIM)
    ang_r = row[:, None] * inv[None, :]
    ang_c = col[:, None] * inv[None, :]
    cos = np.concatenate([np.cos(ang_r)] * 2 + [np.cos(ang_c)] * 2, axis=-1)
    sin = np.concatenate([-np.sin(ang_r), np.sin(ang_r), -np.sin(ang_c), np.sin(ang_c)], axis=-1)
    return jnp.asarray(cos, dtype=F32), jnp.asarray(sin, dtype=F32)


def kernel(x, c, ctx, c_ctx, w_mod, b_mod, norm1_g, norm2_g, ab_w_in, a_q_norm_g, a_k_norm_g, a_sink,
           b_w_pool, b_pool_scale, ab_w_out, cd_w_in, c_v_norm_g, c_w_spatial, c_b_spatial, d_w_fourier,
           cd_w_out, f_w_up, f_conv_w, f_conv_b, f_w_down):
    batch, n, _ = x.shape
    ctx_len = ctx.shape[1]
    assert batch == 1 and DEPTH == 2
    x2 = x.reshape(n, D_MODEL)
    ctx2 = ctx.reshape(ctx_len, D_MODEL)

    mod = _mod_vectors(c, c_ctx, w_mod, b_mod)

    def split6(v):
        return [v[:, k * D_MODEL:(k + 1) * D_MODEL] for k in range(6)]

    row1 = lambda v: v.reshape(1, -1)
    n_tiles = n // PROJ_TM
    attn_steps = n // (ATTN_SUBS * BLOCK)
    up_steps = (n // FFN_UP_TM) * (FFN_UP_STEPS + 1)
    down_steps = (n // FFN_DOWN_TM) * FFN_DOWN_STEPS
    cast_up0 = _CastJob(f_w_up, 0, D_MODEL // n_tiles, n_tiles, col_block=FFN_TF)
    cast_down0 = _CastJob(f_w_down, 0, D_FF // attn_steps, attn_steps, col_block=FFN_TN)
    cast_up1 = _CastJob(f_w_up, 1, D_MODEL // (2 * n_tiles), up_steps, col_block=FFN_TF)
    cast_down1 = _CastJob(f_w_down, 1, D_FF // down_steps, down_steps, col_block=FFN_TN)

    ml = split6(mod[0, 0:1])
    mc = split6(mod[0, 1:2])
    g1 = row1(norm1_g[0])
    w_in = ab_w_in[0].astype(BF16)
    qn = row1(a_q_norm_g[0])
    kn = row1(a_k_norm_g[0])
    cos, sin = _rope_tables(n)
    qkv, z, w_up0 = _ab_in_proj(x2, g1, ml[0], ml[1], w_in, cos, sin, qn, kn, tm=PROJ_TM,
                                cast=cast_up0)
    kv_ctx = _ctx_kv_proj(ctx2, g1, mc[0], mc[1], w_in, kn)
    attn, w_down0 = _window_attention(qkv, kv_ctx, a_sink[0], cast=cast_down0)
    pool_tiles = n // POOL_TM
    pooled, w_out0 = _pool_mix(z, b_w_pool[0].astype(BF16), row1(b_pool_scale[0]), tm=POOL_TM,
                               cast=_CastJob(ab_w_out, 0, D_MODEL // pool_tiles, pool_tiles))
    x2, h2, cd_w_in_bf = _out_proj(attn, pooled, w_out0, x2, ml[2], row1(norm2_g[0]), ml[3], ml[4],
                                   tm=PROJ_TM, cast=_CastJob(cd_w_in, 0, D_MODEL // n_tiles, n_tiles))
    x2, (w_up1,), (w_down1,) = _conv_ffn(x2, h2, w_up0, f_conv_w[0], f_conv_b[0], w_down0, ml[5],
                                         tm=FFN_UP_TM, down_tm=FFN_DOWN_TM,
                                         up_cast=cast_up1, down_cast=cast_down1)

    ml = split6(mod[1, 0:1])
    uv, f, w_out1 = _cd_in_proj(x2, row1(norm1_g[1]), ml[0], ml[1], cd_w_in_bf, row1(c_v_norm_g[0]),
                                tm=PROJ_TM, cast=_CastJob(cd_w_out, 0, D_MODEL // n_tiles, n_tiles))
    bias_full = jnp.repeat(c_b_spatial[0].T, C_GROUP_DIM, axis=1)
    c_out = _spatial_gate(uv, c_w_spatial[0].astype(BF16), bias_full)
    d_out = _fourier_mix(f, _fourier_tables(n), d_w_fourier[0].astype(BF16))
    x2, h2 = _out_proj(c_out, d_out, w_out1, x2, ml[2], row1(norm2_g[1]), ml[3], ml[4], tm=PROJ_TM)
    x2, _, _ = _conv_ffn(x2, h2, w_up1, f_conv_w[1], f_conv_b[1], w_down1, ml[5],
                         tm=FFN_UP_TM, down_tm=FFN_DOWN_TM)
    return x2.reshape(batch, n, D_MODEL)
```

```python
import functools

import numpy as np
import jax
import jax.numpy as jnp
from jax import lax
from jax.experimental import pallas as pl
from jax.experimental.pallas import tpu as pltpu

F32 = jnp.float32
BF16 = jnp.bfloat16

D_MODEL = 2048
DEPTH = 2
GRID_W = 64
HEAD_DIM = 128
A_Q_HEADS = 8
A_KV_HEADS = 2
A_GROUP = A_Q_HEADS // A_KV_HEADS
A_Q_DIM = A_Q_HEADS * HEAD_DIM
A_KV_DIM = A_KV_HEADS * HEAD_DIM
A_QKV_DIM = A_Q_DIM + 2 * A_KV_DIM
WINDOW = 128
BLOCK = 128
ROPE_THETA = 10000.0
AXIS_DIM = HEAD_DIM // 2
ATTN_SCALE = HEAD_DIM ** -0.5
NEG_INF = -1e30
B_GROUPS = 4
B_WIDTH = 1024
B_GROUP_DIM = B_WIDTH // B_GROUPS
POOL_WINDOWS = (2, 4, 8, 16)
AB_IN = A_QKV_DIM + B_WIDTH
C_WIDTH = 1024
C_GROUPS = 4
C_GROUP_DIM = C_WIDTH // C_GROUPS
CHUNK = 128
D_WIDTH = 1024
D_GROUPS = 8
D_GROUP_DIM = D_WIDTH // D_GROUPS
CD_IN = 2 * C_WIDTH + D_WIDTH
D_FF = 5632
EPS = 1e-6

V7X_SUBLANES_F32 = 8
V7X_SUBLANES_BF16 = 16
V7X_LANES = 128
V7X_VMEM_BYTES = 64 * 1024 * 1024
VMEM_LIMIT = V7X_VMEM_BYTES * 7 // 8

PROJ_TM = 512
POOL_TM = 512
ATTN_SUBS = 4
FFN_UP_TM = 1024
FFN_DOWN_TM = 1024

FFT_A = 64
FFT_B = 128


def _params(sem):
    return pltpu.CompilerParams(dimension_semantics=sem, vmem_limit_bytes=VMEM_LIMIT)


def _dot(a, b):
    return jnp.dot(a, b, preferred_element_type=F32)


class _CastJob:
    def __init__(self, src, layer, rows, n_steps, col_block=None):
        _, total_rows, self.cols = src.shape
        assert total_rows % rows == 0 and total_rows // rows <= n_steps
        self.src, self.layer, self.rows, self.col_block = src, layer, rows, col_block
        self.last = total_rows // rows - 1
        if col_block is None:
            self.out_shape = jax.ShapeDtypeStruct((total_rows, self.cols), BF16)
        else:
            assert self.cols % col_block == 0
            self.out_shape = jax.ShapeDtypeStruct((self.cols // col_block, total_rows, col_block), BF16)

    def specs(self, step_of):
        blk = lambda *ids: jnp.minimum(step_of(*ids), self.last)
        src = pl.BlockSpec((None, self.rows, self.cols), lambda *ids: (self.layer, blk(*ids), 0))
        if self.col_block is None:
            return src, pl.BlockSpec((self.rows, self.cols), lambda *ids: (blk(*ids), 0))
        return src, pl.BlockSpec((self.cols // self.col_block, self.rows, self.col_block),
                                 lambda *ids: (0, blk(*ids), 0))

    def run(self, src_ref, dst_ref):
        if self.col_block is None:
            dst_ref[...] = src_ref[...].astype(BF16)
        else:
            for b in range(self.cols // self.col_block):
                dst_ref[b] = src_ref[:, b * self.col_block:(b + 1) * self.col_block].astype(BF16)


def _host_cast(kernel_fn, cast, n_in, n_out):
    def body(*refs):
        cast.run(refs[n_in], refs[n_in + 1 + n_out])
        kernel_fn(*refs[:n_in], *refs[n_in + 1:n_in + 1 + n_out], *refs[n_in + 2 + n_out:])
    return body


def _pallas(kernel_fn, *, grid, in_specs, out_specs, out_shape, args, name, scratch_shapes=(),
            cast=None, step_of=None):
    out_specs, out_shape = list(out_specs), list(out_shape)
    if cast is not None:
        src_spec, dst_spec = cast.specs(step_of)
        kernel_fn = _host_cast(kernel_fn, cast, len(in_specs), len(out_specs))
        in_specs = [*in_specs, src_spec]
        out_specs.append(dst_spec)
        out_shape.append(cast.out_shape)
        args = (*args, cast.src)
    return pl.pallas_call(
        kernel_fn, grid=grid, in_specs=list(in_specs), out_specs=out_specs, out_shape=out_shape,
        scratch_shapes=list(scratch_shapes), compiler_params=_params(("arbitrary",) * len(grid)),
        name=name)(*args)


def _norm_mod(x, g, shift, scale):
    ms = jnp.mean(x * x, axis=-1, keepdims=True)
    return x * lax.rsqrt(ms + EPS) * (g * (1.0 + scale)) + shift


MOD_TK = 256
MOD_STREAMS = 2


def _mod_kernel(cv_ref, w_ref, b_ref, o_ref, acc_scr):
    k = pl.program_id(1)
    sub = V7X_SUBLANES_F32
    groups = MOD_TK // sub
    n_out = 6 * D_MODEL
    lane_tiles = n_out // V7X_LANES

    @pl.when(k == 0)
    def _():
        acc_scr[...] = jnp.zeros_like(acc_scr)

    w = w_ref[0].reshape(groups, sub, n_out)
    for s in range(MOD_STREAMS):
        a = cv_ref[s]
        a = (a * jax.nn.sigmoid(a)).reshape(groups, sub, V7X_LANES)
        a = jnp.concatenate([a] * lane_tiles, axis=-1)
        acc_scr[s] += jnp.sum(w * a, axis=0)

    @pl.when(k == pl.num_programs(1) - 1)
    def _():
        rows = [jnp.sum(acc_scr[s], axis=0, keepdims=True) for s in range(MOD_STREAMS)]
        rows.append(jnp.zeros((sub - MOD_STREAMS, n_out), F32))
        o_ref[0] = jnp.concatenate(rows, axis=0) + b_ref[0]


def _mod_vectors(c, c_ctx, w_mod, b_mod):
    n_out = 6 * D_MODEL
    cv = jnp.stack([c.reshape(D_MODEL), c_ctx.reshape(D_MODEL)])
    cv = jnp.broadcast_to(cv[:, :, None], (MOD_STREAMS, D_MODEL, V7X_LANES))
    b3 = b_mod.reshape(DEPTH, 1, n_out)
    return pl.pallas_call(
        _mod_kernel,
        grid=(DEPTH, D_MODEL // MOD_TK),
        in_specs=[
            pl.BlockSpec((MOD_STREAMS, MOD_TK, V7X_LANES), lambda l, k: (0, k, 0)),
            pl.BlockSpec((1, MOD_TK, n_out), lambda l, k: (l, k, 0)),
            pl.BlockSpec((1, 1, n_out), lambda l, k: (l, 0, 0)),
        ],
        out_specs=pl.BlockSpec((1, V7X_SUBLANES_F32, n_out), lambda l, k: (l, 0, 0)),
        out_shape=jax.ShapeDtypeStruct((DEPTH, V7X_SUBLANES_F32, n_out), F32),
        scratch_shapes=[pltpu.VMEM((MOD_STREAMS, V7X_SUBLANES_F32, n_out), F32)],
        compiler_params=_params(("arbitrary", "arbitrary")),
        name="mod_vectors",
    )(cv, w_mod, b3)


AB_TN = 512
ROW_SPLIT = 1


def _rope(t, cos, sin_signed):
    lane = lax.broadcasted_iota(jnp.int32, t.shape, 1)
    first = (lane % AXIS_DIM) < (AXIS_DIM // 2)
    partner = jnp.where(first,
                        pltpu.roll(t, HEAD_DIM - AXIS_DIM // 2, 1),
                        pltpu.roll(t, AXIS_DIM // 2, 1))
    return t * cos + partner * sin_signed


def _head_norm_rope(t, g, cos, sin_signed):
    ms = jnp.mean(t * t, axis=-1, keepdims=True)
    return _rope(t * lax.rsqrt(ms + EPS) * g, cos, sin_signed)


def _ctx_kv_kernel(x_ref, g_ref, sh_ref, sc_ref, w_ref, kn_ref, kv_ref):
    h = _norm_mod(x_ref[...], g_ref[...], sh_ref[...], sc_ref[...]).astype(BF16)
    p = _dot(h, w_ref[...])
    for hh in range(A_KV_HEADS):
        t = p[:, hh * HEAD_DIM:(hh + 1) * HEAD_DIM]
        ms = jnp.mean(t * t, axis=-1, keepdims=True)
        kv_ref[:, hh * HEAD_DIM:(hh + 1) * HEAD_DIM] = (t * lax.rsqrt(ms + EPS) * kn_ref[...]).astype(BF16)
    kv_ref[:, A_KV_DIM:] = p[:, A_KV_DIM:].astype(BF16)


def _ctx_kv_proj(ctx2, g, shift, scale, w_in_bf, kn):
    ctx_len = ctx2.shape[0]
    kv_cols = 2 * A_KV_DIM
    assert A_Q_DIM % kv_cols == 0
    row = lambda i: (0, 0)
    return _pallas(
        _ctx_kv_kernel,
        grid=(1,),
        in_specs=[
            pl.BlockSpec((ctx_len, D_MODEL), row),
            pl.BlockSpec((1, D_MODEL), row),
            pl.BlockSpec((1, D_MODEL), row),
            pl.BlockSpec((1, D_MODEL), row),
            pl.BlockSpec((D_MODEL, kv_cols), lambda i: (0, A_Q_DIM // kv_cols)),
            pl.BlockSpec((1, HEAD_DIM), row),
        ],
        out_specs=[pl.BlockSpec((ctx_len, kv_cols), row)],
        out_shape=[jax.ShapeDtypeStruct((ctx_len, kv_cols), BF16)],
        name="ctx_kv_proj",
        args=(ctx2, g, shift, scale, w_in_bf, kn))[0]


def _ab_in_kernel(x_ref, g_ref, sh_ref, sc_ref, w_ref, cos_ref, sin_ref, qn_ref, kn_ref,
                  qkv_ref, z_ref, *, tm):
    rows_per = tm // ROW_SPLIT
    n_q_tiles = A_Q_DIM // AB_TN
    for part in range(ROW_SPLIT):
        rs = slice(part * rows_per, (part + 1) * rows_per)
        h = _norm_mod(x_ref[rs, :], g_ref[...], sh_ref[...], sc_ref[...]).astype(BF16)
        cos = cos_ref[rs, :]
        sin = sin_ref[rs, :]
        for t in range(AB_IN // AB_TN):
            p = _dot(h, w_ref[:, t * AB_TN:(t + 1) * AB_TN])
            if t < n_q_tiles:
                for hh in range(AB_TN // HEAD_DIM):
                    c0 = t * AB_TN + hh * HEAD_DIM
                    qkv_ref[rs, c0:c0 + HEAD_DIM] = _head_norm_rope(
                        p[:, hh * HEAD_DIM:(hh + 1) * HEAD_DIM], qn_ref[...] * ATTN_SCALE,
                        cos, sin).astype(BF16)
            elif t == n_q_tiles:
                for hh in range(A_KV_HEADS):
                    c0 = A_Q_DIM + hh * HEAD_DIM
                    qkv_ref[rs, c0:c0 + HEAD_DIM] = _head_norm_rope(
                        p[:, hh * HEAD_DIM:(hh + 1) * HEAD_DIM], kn_ref[...], cos, sin).astype(BF16)
                qkv_ref[rs, A_Q_DIM + A_KV_DIM:] = p[:, A_KV_DIM:].astype(BF16)
            else:
                c0 = (t - n_q_tiles - 1) * AB_TN
                z_ref[rs, c0:c0 + AB_TN] = p


def _ab_in_proj(x2, g, shift, scale, w_in_bf, cos, sin, qn, kn, tm, cast=None):
    n = x2.shape[0]
    assert A_Q_DIM % AB_TN == 0 and 2 * A_KV_DIM == AB_TN and B_WIDTH % AB_TN == 0
    row = lambda i: (0, 0)
    return _pallas(
        functools.partial(_ab_in_kernel, tm=tm),
        grid=(n // tm,),
        cast=cast, step_of=lambda i: i,
        in_specs=[
            pl.BlockSpec((tm, D_MODEL), lambda i: (i, 0)),
            pl.BlockSpec((1, D_MODEL), row),
            pl.BlockSpec((1, D_MODEL), row),
            pl.BlockSpec((1, D_MODEL), row),
            pl.BlockSpec((D_MODEL, AB_IN), row, pipeline_mode=pl.Buffered(1)),
            pl.BlockSpec((tm, HEAD_DIM), lambda i: (i, 0)),
            pl.BlockSpec((tm, HEAD_DIM), lambda i: (i, 0)),
            pl.BlockSpec((1, HEAD_DIM), row),
            pl.BlockSpec((1, HEAD_DIM), row),
        ],
        out_specs=[
            pl.BlockSpec((tm, A_QKV_DIM), lambda i: (i, 0)),
            pl.BlockSpec((tm, B_WIDTH), lambda i: (i, 0)),
        ],
        out_shape=[jax.ShapeDtypeStruct((n, A_QKV_DIM), BF16),
                   jax.ShapeDtypeStruct((n, B_WIDTH), F32)],
        name="ab_in_proj",
        args=(x2, g, shift, scale, w_in_bf, cos, sin, qn, kn))


def _attn_kernel(sink_ref, bias_ref, q_ref, kp_ref, kc_ref, kn_ref, vp_ref, vc_ref, vn_ref,
                 kx_ref, vx_ref, o_ref):
    i = pl.program_id(0)
    last = pl.num_programs(0) - 1
    rows = A_GROUP * BLOCK
    band_cols = 3 * BLOCK
    n_keys = band_cols + kx_ref.shape[0]
    r1 = lax.broadcasted_iota(jnp.int32, (rows, 1), 0) // BLOCK
    ones_col = (lax.broadcasted_iota(jnp.int32, (n_keys, HEAD_DIM), 1) == 0).astype(BF16)

    def key_block(prev_ref, cur_ref, next_ref, j, hs):
        if j == 0:
            return prev_ref[:, hs]
        if j == ATTN_SUBS + 1:
            return next_ref[:, hs]
        return cur_ref[(j - 1) * BLOCK:j * BLOCK, hs]

    for sb, hk in [(sb, hk) for sb in range(ATTN_SUBS) for hk in range(A_KV_HEADS)]:
        rs = slice(sb * BLOCK, (sb + 1) * BLOCK)
        if sb == 0:
            bias = jnp.where(i == 0, bias_ref[0], bias_ref[1])
        elif sb == ATTN_SUBS - 1:
            bias = jnp.where(i == last, bias_ref[2], bias_ref[1])
        else:
            bias = bias_ref[1]
        bias = jnp.concatenate([bias] * A_GROUP, axis=0)
        hs = slice(hk * HEAD_DIM, (hk + 1) * HEAD_DIM)
        kcat = jnp.concatenate([key_block(kp_ref, kc_ref, kn_ref, sb + d, hs) for d in range(3)]
                               + [kx_ref[:, hs]], axis=0)
        vcat = jnp.concatenate([key_block(vp_ref, vc_ref, vn_ref, sb + d, hs) for d in range(3)]
                               + [vx_ref[:, hs]], axis=0)
        q0 = hk * A_GROUP * HEAD_DIM
        q4 = jnp.concatenate(
            [q_ref[rs, q0 + g * HEAD_DIM:q0 + (g + 1) * HEAD_DIM] for g in range(A_GROUP)], axis=0)
        s = lax.dot_general(q4, kcat, (((1,), (1,)), ((), ())), preferred_element_type=F32)
        s = jnp.concatenate([s[:, :band_cols] + bias, s[:, band_cols:]], axis=1)
        sink = jnp.zeros((rows, 1), F32)
        for g in range(A_GROUP):
            sink = jnp.where(r1 == g, sink_ref[hk * A_GROUP + g], sink)
        m = jnp.maximum(jnp.max(s, axis=-1, keepdims=True), sink)
        e = jnp.exp(s - m).astype(BF16)
        pv = _dot(e, jnp.concatenate([vcat, ones_col], axis=1))
        den = pv[:, HEAD_DIM:HEAD_DIM + 1] + jnp.exp(sink - m)
        o4 = pv[:, :HEAD_DIM] / den
        for g in range(A_GROUP):
            o_ref[rs, q0 + g * HEAD_DIM:q0 + (g + 1) * HEAD_DIM] = (
                o4[g * BLOCK:(g + 1) * BLOCK].astype(BF16))


def _attn_bias():
    r = np.arange(BLOCK)[:, None]
    c = np.arange(3 * BLOCK)[None, :]
    band = np.abs(BLOCK + r - c) <= WINDOW
    first = band & (c >= BLOCK)
    last = band & (c < 2 * BLOCK)
    masks = np.stack([first, band, last])
    return jnp.asarray(np.where(masks, 0.0, NEG_INF), dtype=F32)


def _window_attention(qkv, kv_ctx, sink, cast=None):
    n = qkv.shape[0]
    ctx_len = kv_ctx.shape[0]
    step_rows = ATTN_SUBS * BLOCK
    n_tiles = n // step_rows
    n_blocks = n // BLOCK
    assert ATTN_SUBS >= 2 and n % step_rows == 0 and WINDOW <= BLOCK
    k_col = A_Q_DIM // A_KV_DIM
    v_col = k_col + 1
    prev = lambda i: jnp.maximum(i * ATTN_SUBS - 1, 0)
    nxt = lambda i: jnp.minimum((i + 1) * ATTN_SUBS, n_blocks - 1)
    blk = (BLOCK, A_KV_DIM)
    cur = (step_rows, A_KV_DIM)
    return _pallas(
        _attn_kernel,
        grid=(n_tiles,),
        cast=cast, step_of=lambda i: i,
        in_specs=[
            pl.BlockSpec(memory_space=pltpu.SMEM),
            pl.BlockSpec((3, BLOCK, 3 * BLOCK), lambda i: (0, 0, 0)),
            pl.BlockSpec((step_rows, A_Q_DIM), lambda i: (i, 0)),
            pl.BlockSpec(blk, lambda i: (prev(i), k_col)),
            pl.BlockSpec(cur, lambda i: (i, k_col)),
            pl.BlockSpec(blk, lambda i: (nxt(i), k_col)),
            pl.BlockSpec(blk, lambda i: (prev(i), v_col)),
            pl.BlockSpec(cur, lambda i: (i, v_col)),
            pl.BlockSpec(blk, lambda i: (nxt(i), v_col)),
            pl.BlockSpec((ctx_len, A_KV_DIM), lambda i: (0, 0)),
            pl.BlockSpec((ctx_len, A_KV_DIM), lambda i: (0, 1)),
        ],
        out_specs=[pl.BlockSpec((step_rows, A_Q_DIM), lambda i: (i, 0))],
        out_shape=[jax.ShapeDtypeStruct((n, A_Q_DIM), BF16)],
        name="window_attention",
        args=(sink, _attn_bias(), qkv, qkv, qkv, qkv, qkv, qkv, qkv, kv_ctx, kv_ctx))


POOL_HALO = 8
assert max(POOL_WINDOWS) // 2 <= POOL_HALO


def _pool_kernel(zm_ref, zp_ref, zn_ref, w_ref, ps_ref, o_ref, z_scr, *, n_rows, tm):
    i = pl.program_id(0)
    last = pl.num_programs(0) - 1
    pad = POOL_HALO
    span_rows = tm + 2 * pad
    z_scr[0:pad, :] = jnp.zeros((pad, B_WIDTH), F32)
    z_scr[pad:2 * pad, :] = jnp.where(i > 0, zp_ref[...], 0.0)
    z_scr[2 * pad:2 * pad + tm, :] = zm_ref[...]
    z_scr[2 * pad + tm:3 * pad + tm, :] = jnp.where(i < last, zn_ref[...], 0.0)
    z_scr[3 * pad + tm:, :] = jnp.zeros((pad, B_WIDTH), F32)
    t = i * tm + lax.broadcasted_iota(jnp.int32, (tm, B_GROUP_DIM), 0)
    for g in range(B_GROUPS):
        window = POOL_WINDOWS[g]
        half = window // 2
        cs = slice(g * B_GROUP_DIM, (g + 1) * B_GROUP_DIM)
        s = z_scr[pl.ds(2 * pad - half, span_rows), cs]
        width = 1
        while width < window:
            s = s + pltpu.roll(s, span_rows - width, 0)
            width *= 2
        acc = s[0:tm]
        cnt = (jnp.minimum(t + half, n_rows) - jnp.maximum(t - half, 0)).astype(F32)
        d = (acc / cnt - zm_ref[:, cs]).astype(BF16)
        y = _dot(d, w_ref[g]) * ps_ref[:, cs]
        o_ref[:, cs] = y.astype(BF16)


def _pool_mix(z, w_pool_bf, pool_scale, tm, cast=None):
    n = z.shape[0]
    hb = tm // POOL_HALO
    n_halo_blocks = n // POOL_HALO
    return _pallas(
        functools.partial(_pool_kernel, n_rows=n, tm=tm),
        grid=(n // tm,),
        cast=cast, step_of=lambda i: i,
        in_specs=[
            pl.BlockSpec((tm, B_WIDTH), lambda i: (i, 0)),
            pl.BlockSpec((POOL_HALO, B_WIDTH), lambda i: (jnp.maximum(i * hb - 1, 0), 0)),
            pl.BlockSpec((POOL_HALO, B_WIDTH), lambda i: (jnp.minimum((i + 1) * hb, n_halo_blocks - 1), 0)),
            pl.BlockSpec((B_GROUPS, B_GROUP_DIM, B_GROUP_DIM), lambda i: (0, 0, 0)),
            pl.BlockSpec((1, B_WIDTH), lambda i: (0, 0)),
        ],
        out_specs=[pl.BlockSpec((tm, B_WIDTH), lambda i: (i, 0))],
        out_shape=[jax.ShapeDtypeStruct((n, B_WIDTH), BF16)],
        scratch_shapes=[pltpu.VMEM((tm + 4 * POOL_HALO, B_WIDTH), F32)],
        name="pool_mix",
        args=(z, z, z, w_pool_bf, pool_scale))


OUT_TN = 512


def _out_proj_kernel(a1_ref, a2_ref, w_ref, x_ref, gate_ref, g_ref, sh_ref, sc_ref, o_ref, h_ref, *, tm):
    rows_per = tm // ROW_SPLIT
    for part in range(ROW_SPLIT):
        rs = slice(part * rows_per, (part + 1) * rows_per)
        a = jnp.concatenate([a1_ref[rs, :].astype(BF16), a2_ref[rs, :].astype(BF16)], axis=1)
        for t in range(D_MODEL // OUT_TN):
            cs = slice(t * OUT_TN, (t + 1) * OUT_TN)
            o_ref[rs, cs] = x_ref[rs, cs] + gate_ref[:, cs] * _dot(a, w_ref[:, cs])
        h_ref[rs, :] = _norm_mod(o_ref[rs, :], g_ref[...], sh_ref[...], sc_ref[...])


def _out_proj(a1, a2, w_out_bf, x2, gate, g2, shift2, scale2, tm, cast=None):
    n = x2.shape[0]
    k1 = a1.shape[1]
    k2 = a2.shape[1]
    assert k1 + k2 == w_out_bf.shape[0] and k1 % V7X_LANES == 0
    row = lambda i: (0, 0)
    return _pallas(
        functools.partial(_out_proj_kernel, tm=tm),
        grid=(n // tm,),
        cast=cast, step_of=lambda i: i,
        in_specs=[
            pl.BlockSpec((tm, k1), lambda i: (i, 0)),
            pl.BlockSpec((tm, k2), lambda i: (i, 0)),
            pl.BlockSpec((k1 + k2, D_MODEL), row, pipeline_mode=pl.Buffered(1)),
            pl.BlockSpec((tm, D_MODEL), lambda i: (i, 0)),
            pl.BlockSpec((1, D_MODEL), row),
            pl.BlockSpec((1, D_MODEL), row),
            pl.BlockSpec((1, D_MODEL), row),
            pl.BlockSpec((1, D_MODEL), row),
        ],
        out_specs=[pl.BlockSpec((tm, D_MODEL), lambda i: (i, 0)),
                   pl.BlockSpec((tm, D_MODEL), lambda i: (i, 0))],
        out_shape=[jax.ShapeDtypeStruct((n, D_MODEL), F32),
                   jax.ShapeDtypeStruct((n, D_MODEL), F32)],
        name="out_proj",
        args=(a1, a2, w_out_bf, x2, gate, g2, shift2, scale2))


FFN_SLABS = V7X_SUBLANES_F32
FFN_EDGE_ROWS = V7X_SUBLANES_BF16
FFN_PIECE_ROWS = 16
FFN_ELEMENTWISE_DTYPE = BF16
FFN_CW = 256
FFN_TF = 2 * FFN_CW
FFN_TN = 512
FFN_CHUNKS = D_FF // FFN_CW
FFN_UP_STEPS = D_FF // FFN_TF
FFN_DOWN_STEPS = D_MODEL // FFN_TN


def _ffn_up_kernel(hm_ref, hp_ref, hn_ref, wg_ref, wv_ref, cw_ref, cb_ref, o_ref,
                   h_scr, carry_a, carry_b, *, tm):
    i = pl.program_id(0)
    j = pl.program_id(1)
    last_i = pl.num_programs(0) - 1
    nc = FFN_CHUNKS
    ns = FFN_SLABS
    sr = tm // ns

    def finish(carry, k, r, step):
        c = jnp.maximum(2 * step + k, 0)

        def conv(idx, kk, q0):
            slab = lambda s: carry[idx, pl.ds(s * sr + q0, FFN_PIECE_ROWS), :]
            prev = slab(r - 1) if r > 0 else slab(ns)
            nxt = slab(r + 1) if r < ns - 1 else slab(ns + 1)
            return prev * cw_ref[kk, 0] + slab(r) * cw_ref[kk, 1] + nxt * cw_ref[kk, 2] + cb_ref[kk]

        for q0 in range(0, sr, FFN_PIECE_ROWS):
            gg = conv(k, c, q0)
            vv = conv(2 + k, c + nc, q0)
            o_ref[pl.ds(r * sr + q0, FFN_PIECE_ROWS), k * FFN_CW:(k + 1) * FFN_CW] = (
                gg * jax.nn.sigmoid(gg) * vv).astype(BF16)

    @pl.when(j == 0)
    def _():
        hm = hm_ref[...].reshape(sr, ns, D_MODEL)
        h_scr[0:tm, :] = jnp.swapaxes(hm, 0, 1).reshape(tm, D_MODEL).astype(BF16)
        before = jnp.where(i > 0, hp_ref[ns - 1:ns, :], 0.0)
        after = jnp.where(i < last_i, hn_ref[0:1, :], 0.0)
        pad = jnp.zeros((FFN_EDGE_ROWS - 2, D_MODEL), F32)
        h_scr[tm:, :] = jnp.concatenate([before, after, pad], axis=0).astype(BF16)
        carry_b[...] = jnp.zeros_like(carry_b)

    def keep(carry, idx, u):
        dt = FFN_ELEMENTWISE_DTYPE
        carry[idx, 0:tm, :] = u[0:tm].astype(dt)
        carry[idx, tm:tm + sr, :] = jnp.concatenate(
            [u[tm:tm + 1], u[(ns - 1) * sr:ns * sr - 1]], axis=0).astype(dt)
        carry[idx, tm + sr:, :] = jnp.concatenate([u[1:sr], u[tm + 1:tm + 2]], axis=0).astype(dt)

    def up_step(write, read):
        for k in range(2):
            for r in range(ns):
                finish(read, k, r, j - 1)
        for k in range(2):
            cs = slice(k * FFN_CW, (k + 1) * FFN_CW)
            keep(write, k, _dot(h_scr[...], wg_ref[:, cs]))
            keep(write, 2 + k, _dot(h_scr[...], wv_ref[:, cs]))

    @pl.when((j < FFN_UP_STEPS) & (j % 2 == 0))
    def _():
        up_step(carry_a, carry_b)

    @pl.when((j < FFN_UP_STEPS) & (j % 2 == 1))
    def _():
        up_step(carry_b, carry_a)

    @pl.when(j == FFN_UP_STEPS)
    def _():
        last = carry_a if (FFN_UP_STEPS - 1) % 2 == 0 else carry_b
        for k in range(2):
            for r in range(ns):
                finish(last, k, r, FFN_UP_STEPS - 1)


def _ffn_down_kernel(a_ref, wd_ref, x_ref, gate_ref, o_ref, *, tm, up_tm):
    ns = FFN_SLABS
    sr = up_tm // ns
    y = _dot(a_ref[:, FFN_TF:], wd_ref[...])
    y = jnp.swapaxes(y.reshape(tm // up_tm, ns, sr, FFN_TN), 1, 2).reshape(tm, FFN_TN)
    o_ref[...] = x_ref[...] + gate_ref[...] * y


def _conv_ffn(x2, h2, w_up_bf, conv_w, conv_b, w_down_bf, gate, tm, down_tm, up_cast=None,
              down_cast=None):
    n = x2.shape[0]
    nj = FFN_UP_STEPS
    hb = tm // FFN_SLABS
    n_halo_blocks = n // FFN_SLABS
    act_cols = D_FF + FFN_TF
    cw3 = conv_w.reshape(3, 2 * FFN_CHUNKS, FFN_CW).transpose(1, 0, 2)
    cw3 = jnp.broadcast_to(cw3[:, :, None, :], (2 * FFN_CHUNKS, 3, FFN_PIECE_ROWS, FFN_CW))
    cb3 = jnp.broadcast_to(conv_b.reshape(2 * FFN_CHUNKS, 1, FFN_CW),
                           (2 * FFN_CHUNKS, FFN_PIECE_ROWS, FFN_CW))
    cw3 = cw3.astype(FFN_ELEMENTWISE_DTYPE)
    cb3 = cb3.astype(FFN_ELEMENTWISE_DTYPE)
    once = pl.Buffered(1)
    act, *up_cast = _pallas(
        functools.partial(_ffn_up_kernel, tm=tm),
        grid=(n // tm, nj + 1),
        cast=up_cast, step_of=lambda i, j: i * (nj + 1) + j,
        in_specs=[
            pl.BlockSpec((tm, D_MODEL), lambda i, j: (i, 0)),
            pl.BlockSpec((FFN_SLABS, D_MODEL), lambda i, j: (jnp.maximum(i * hb - 1, 0), 0)),
            pl.BlockSpec((FFN_SLABS, D_MODEL), lambda i, j: (jnp.minimum((i + 1) * hb, n_halo_blocks - 1), 0)),
            pl.BlockSpec((None, D_MODEL, FFN_TF), lambda i, j: (jnp.minimum(j, nj - 1), 0, 0)),
            pl.BlockSpec((None, D_MODEL, FFN_TF), lambda i, j: (jnp.minimum(j, nj - 1) + nj, 0, 0)),
            pl.BlockSpec((2 * FFN_CHUNKS, 3, FFN_PIECE_ROWS, FFN_CW), lambda i, j: (0, 0, 0, 0),
                         pipeline_mode=once),
            pl.BlockSpec((2 * FFN_CHUNKS, FFN_PIECE_ROWS, FFN_CW), lambda i, j: (0, 0, 0),
                         pipeline_mode=once),
        ],
        out_specs=[pl.BlockSpec((tm, FFN_TF), lambda i, j: (i, j))],
        out_shape=[jax.ShapeDtypeStruct((n, act_cols), BF16)],
        scratch_shapes=[pltpu.VMEM((tm + FFN_EDGE_ROWS, D_MODEL), BF16),
                        pltpu.VMEM((4, tm + 2 * (tm // FFN_SLABS), FFN_CW), FFN_ELEMENTWISE_DTYPE),
                        pltpu.VMEM((4, tm + 2 * (tm // FFN_SLABS), FFN_CW), FFN_ELEMENTWISE_DTYPE)],
        name="ffn_up",
        args=(h2, h2, h2, w_up_bf, w_up_bf, cw3, cb3))
    out, *down_cast = _pallas(
        functools.partial(_ffn_down_kernel, tm=down_tm, up_tm=tm),
        grid=(n // down_tm, FFN_DOWN_STEPS),
        cast=down_cast, step_of=lambda i, j: i * FFN_DOWN_STEPS + j,
        in_specs=[
            pl.BlockSpec((down_tm, act_cols), lambda i, j: (i, 0)),
            pl.BlockSpec((None, D_FF, FFN_TN), lambda i, j: (j, 0, 0)),
            pl.BlockSpec((down_tm, FFN_TN), lambda i, j: (i, j)),
            pl.BlockSpec((1, FFN_TN), lambda i, j: (0, j)),
        ],
        out_specs=[pl.BlockSpec((down_tm, FFN_TN), lambda i, j: (i, j))],
        out_shape=[jax.ShapeDtypeStruct((n, D_MODEL), F32)],
        name="ffn_down",
        args=(act, w_down_bf, x2, gate))
    return out, up_cast, down_cast


def _gelu(x):
    return 0.5 * x * (1.0 + lax.erf(x * (2.0 ** -0.5)))


def _cd_in_kernel(x_ref, g_ref, sh_ref, sc_ref, w_ref, vg_ref, ws_ref, bias_ref, c_ref, f_ref, *, tm):
    h = _norm_mod(x_ref[...], g_ref[...], sh_ref[...], sc_ref[...]).astype(BF16)
    u = _gelu(_dot(h, w_ref[:, 0:C_WIDTH]))
    v = _gelu(_dot(h, w_ref[:, C_WIDTH:2 * C_WIDTH]))
    ms = jnp.mean(v * v, axis=-1, keepdims=True)
    v = (v * lax.rsqrt(ms + EPS) * vg_ref[...]).astype(BF16)
    for k in range(tm // CHUNK):
        rs = slice(k * CHUNK, (k + 1) * CHUNK)
        for grp in range(C_GROUPS):
            cs = slice(grp * C_GROUP_DIM, (grp + 1) * C_GROUP_DIM)
            s = _dot(ws_ref[grp], v[rs, cs]) + bias_ref[:, cs]
            c_ref[rs, cs] = (u[rs, cs] * s).astype(BF16)
    f_ref[...] = _dot(h, w_ref[:, 2 * C_WIDTH:])


def _cd_in_proj(x2, g, shift, scale, w_in_bf, v_norm_g, w_spatial_bf, bias_full, tm, cast=None):
    n = x2.shape[0]
    assert tm % CHUNK == 0
    row = lambda i: (0, 0)
    return _pallas(
        functools.partial(_cd_in_kernel, tm=tm),
        grid=(n // tm,),
        cast=cast, step_of=lambda i: i,
        in_specs=[
            pl.BlockSpec((tm, D_MODEL), lambda i: (i, 0)),
            pl.BlockSpec((1, D_MODEL), row),
            pl.BlockSpec((1, D_MODEL), row),
            pl.BlockSpec((1, D_MODEL), row),
            pl.BlockSpec((D_MODEL, CD_IN), row, pipeline_mode=pl.Buffered(1)),
            pl.BlockSpec((1, C_WIDTH), row),
            pl.BlockSpec((C_GROUPS, CHUNK, CHUNK), lambda i: (0, 0, 0)),
            pl.BlockSpec((CHUNK, C_WIDTH), row),
        ],
        out_specs=[
            pl.BlockSpec((tm, C_WIDTH), lambda i: (i, 0)),
            pl.BlockSpec((tm, D_WIDTH), lambda i: (i, 0)),
        ],
        out_shape=[jax.ShapeDtypeStruct((n, C_WIDTH), BF16),
                   jax.ShapeDtypeStruct((n, D_WIDTH), F32)],
        name="cd_in_proj",
        args=(x2, g, shift, scale, w_in_bf, v_norm_g, w_spatial_bf, bias_full))


FFT_SUB = V7X_SUBLANES_F32
FFT1_CT = 1024
FFT2_CT = 1024


def _fourier_tables(n):
    a_len, b_len, sub = FFT_A, FFT_B, FFT_SUB
    assert a_len * b_len == n
    ch = np.arange(D_GROUP_DIM)
    ang_c = 2.0 * np.pi * ((ch[:, None] * ch[None, :]) % D_GROUP_DIM) / D_GROUP_DIM
    a = np.arange(a_len)
    f_a = np.exp(-2j * np.pi * ((a[:, None] * a[None, :]) % a_len) / a_len)
    m1 = np.kron(f_a, np.eye(sub))
    b = np.arange(b_len)
    tw = np.exp(-2j * np.pi * ((a[:, None] * b[None, :]) % n) / n)
    f_b = np.exp(-2j * np.pi * ((b[:, None] * b[None, :]) % b_len) / b_len)
    m2 = np.einsum('db,pq->dpqb', f_b, np.eye(sub)).reshape(b_len * sub, sub * b_len)
    norm = 1.0 / np.sqrt(float(n) * D_GROUP_DIM)
    m2 = m2 * norm
    tw3 = np.broadcast_to(tw[:, :, None], (a_len, b_len, V7X_LANES))
    f32 = lambda v: jnp.asarray(np.ascontiguousarray(v), dtype=F32)
    return dict(cos_c=f32(np.cos(ang_c)), sin_c=f32(np.sin(ang_c)),
                m1r=f32(m1.real), m1i=f32(m1.imag), m2r=f32(m2.real), m2i=f32(m2.imag),
                twr=f32(tw3.real), twi=f32(tw3.imag))


def _fft1_kernel(f_ref, cc_ref, sc_ref, m1r_ref, m1i_ref, twr_ref, twi_ref, tr_ref, ti_ref):
    rows = FFT_A * FFT_SUB
    ct = FFT1_CT
    fb = f_ref[...].reshape(rows, ct).astype(BF16)
    xr_parts = []
    xi_parts = []
    for q in range(ct // D_GROUP_DIM):
        blk = fb[:, q * D_GROUP_DIM:(q + 1) * D_GROUP_DIM]
        xr_parts.append(_dot(blk, cc_ref[...]))
        xi_parts.append(-_dot(blk, sc_ref[...]))
    xr = jnp.concatenate(xr_parts, axis=1).astype(BF16)
    xi = jnp.concatenate(xi_parts, axis=1).astype(BF16)
    m1r = m1r_ref[...]
    m1i = m1i_ref[...]
    tr = _dot(m1r, xr) - _dot(m1i, xi)
    ti = _dot(m1r, xi) + _dot(m1i, xr)
    reps = ct // V7X_LANES
    twr = jnp.tile(twr_ref[...].reshape(rows, V7X_LANES), (1, reps))
    twi = jnp.tile(twi_ref[...].reshape(rows, V7X_LANES), (1, reps))
    tr_ref[...] = (tr * twr - ti * twi).reshape(FFT_A, FFT_SUB, ct)
    ti_ref[...] = (tr * twi + ti * twr).reshape(FFT_A, FFT_SUB, ct)


def _fft2_kernel(tr_ref, ti_ref, m2r_ref, m2i_ref, wf_ref, o_ref):
    j = pl.program_id(1)
    tr = tr_ref[...].astype(BF16)
    ti = ti_ref[...].astype(BF16)
    z = _dot(m2r_ref[...], tr) - _dot(m2i_ref[...], ti)
    contrib = _dot(z.astype(BF16), wf_ref[...]).reshape(FFT_B, FFT_SUB, D_WIDTH)

    @pl.when(j == 0)
    def _():
        o_ref[...] = contrib

    @pl.when(j > 0)
    def _():
        o_ref[...] += contrib


def _fourier_mix(f, tabs, w_fourier_bf):
    n = f.shape[0]
    a_len, b_len, sub = FFT_A, FFT_B, FFT_SUB
    f3 = f.reshape(a_len, b_len, D_WIDTH)
    rows1 = a_len * sub
    const2 = lambda i, j: (0, 0)
    tr, ti = pl.pallas_call(
        _fft1_kernel,
        grid=(b_len // sub, D_WIDTH // FFT1_CT),
        in_specs=[
            pl.BlockSpec((a_len, sub, FFT1_CT), lambda i, j: (0, i, j)),
            pl.BlockSpec((D_GROUP_DIM, D_GROUP_DIM), const2),
            pl.BlockSpec((D_GROUP_DIM, D_GROUP_DIM), const2),
            pl.BlockSpec((rows1, rows1), const2),
            pl.BlockSpec((rows1, rows1), const2),
            pl.BlockSpec((a_len, sub, V7X_LANES), lambda i, j: (0, i, 0)),
            pl.BlockSpec((a_len, sub, V7X_LANES), lambda i, j: (0, i, 0)),
        ],
        out_specs=[pl.BlockSpec((a_len, sub, FFT1_CT), lambda i, j: (0, i, j))] * 2,
        out_shape=[jax.ShapeDtypeStruct((a_len, b_len, D_WIDTH), F32)] * 2,
        compiler_params=_params(("arbitrary", "arbitrary")),
        name="fourier_stage1",
    )(f3, tabs['cos_c'].astype(BF16), tabs['sin_c'].astype(BF16),
      tabs['m1r'].astype(BF16), tabs['m1i'].astype(BF16), tabs['twr'], tabs['twi'])

    rows2 = sub * b_len
    tr2 = tr.reshape(n, D_WIDTH)
    ti2 = ti.reshape(n, D_WIDTH)
    out = pl.pallas_call(
        _fft2_kernel,
        grid=(a_len // sub, D_WIDTH // FFT2_CT),
        in_specs=[
            pl.BlockSpec((rows2, FFT2_CT), lambda i, j: (i, j)),
            pl.BlockSpec((rows2, FFT2_CT), lambda i, j: (i, j)),
            pl.BlockSpec((rows2, rows2), const2),
            pl.BlockSpec((rows2, rows2), const2),
            pl.BlockSpec((FFT2_CT, D_WIDTH), lambda i, j: (j, 0)),
        ],
        out_specs=pl.BlockSpec((b_len, sub, D_WIDTH), lambda i, j: (0, i, 0)),
        out_shape=jax.ShapeDtypeStruct((b_len, a_len, D_WIDTH), F32),
        compiler_params=_params(("arbitrary", "arbitrary")),
        name="fourier_stage2",
    )(tr2, ti2, tabs['m2r'].astype(BF16), tabs['m2i'].astype(BF16), w_fourier_bf)
    return out.reshape(n, D_WIDTH)


def _rope_tables(n):
    rows = n // GRID_W
    row = np.repeat(np.arange(rows, dtype=np.float64), GRID_W)
    col = np.tile(np.arange(GRID_W, dtype=np.float64), rows)
    inv = ROPE_THETA ** (-np.arange(0, AXIS_DIM, 2, dtype=np.float64) / AXIS_DIM)
    ang_r = row[:, None] * inv[None, :]
    ang_c = col[:, None] * inv[None, :]
    cos = np.concatenate([np.cos(ang_r)] * 2 + [np.cos(ang_c)] * 2, axis=-1)
    sin = np.concatenate([-np.sin(ang_r), np.sin(ang_r), -np.sin(ang_c), np.sin(ang_c)], axis=-1)
    return jnp.asarray(cos, dtype=F32), jnp.asarray(sin, dtype=F32)


def kernel(x, c, ctx, c_ctx, w_mod, b_mod, norm1_g, norm2_g, ab_w_in, a_q_norm_g, a_k_norm_g, a_sink,
           b_w_pool, b_pool_scale, ab_w_out, cd_w_in, c_v_norm_g, c_w_spatial, c_b_spatial, d_w_fourier,
           cd_w_out, f_w_up, f_conv_w, f_conv_b, f_w_down):
    batch, n, _ = x.shape
    ctx_len = ctx.shape[1]
    assert batch == 1 and DEPTH == 2
    x2 = x.reshape(n, D_MODEL)
    ctx2 = ctx.reshape(ctx_len, D_MODEL)

    mod = _mod_vectors(c, c_ctx, w_mod, b_mod)

    def split6(v):
        return [v[:, k * D_MODEL:(k + 1) * D_MODEL] for k in range(6)]

    row1 = lambda v: v.reshape(1, -1)
    n_tiles = n // PROJ_TM
    attn_steps = n // (ATTN_SUBS * BLOCK)
    up_steps = (n // FFN_UP_TM) * (FFN_UP_STEPS + 1)
    down_steps = (n // FFN_DOWN_TM) * FFN_DOWN_STEPS
    cast_up0 = _CastJob(f_w_up, 0, D_MODEL // n_tiles, n_tiles, col_block=FFN_TF)
    cast_down0 = _CastJob(f_w_down, 0, D_FF // attn_steps, attn_steps, col_block=FFN_TN)
    cast_up1 = _CastJob(f_w_up, 1, D_MODEL // (2 * n_tiles), up_steps, col_block=FFN_TF)
    cast_down1 = _CastJob(f_w_down, 1, D_FF // down_steps, down_steps, col_block=FFN_TN)

    ml = split6(mod[0, 0:1])
    mc = split6(mod[0, 1:2])
    g1 = row1(norm1_g[0])
    w_in = ab_w_in[0].astype(BF16)
    qn = row1(a_q_norm_g[0])
    kn = row1(a_k_norm_g[0])
    cos, sin = _rope_tables(n)
    qkv, z, w_up0 = _ab_in_proj(x2, g1, ml[0], ml[1], w_in, cos, sin, qn, kn, tm=PROJ_TM,
                                cast=cast_up0)
    kv_ctx = _ctx_kv_proj(ctx2, g1, mc[0], mc[1], w_in, kn)
    attn, w_down0 = _window_attention(qkv, kv_ctx, a_sink[0], cast=cast_down0)
    pool_tiles = n // POOL_TM
    pooled, w_out0 = _pool_mix(z, b_w_pool[0].astype(BF16), row1(b_pool_scale[0]), tm=POOL_TM,
                               cast=_CastJob(ab_w_out, 0, D_MODEL // pool_tiles, pool_tiles))
    x2, h2, cd_w_in_bf = _out_proj(attn, pooled, w_out0, x2, ml[2], row1(norm2_g[0]), ml[3], ml[4],
                                   tm=PROJ_TM, cast=_CastJob(cd_w_in, 0, D_MODEL // n_tiles, n_tiles))
    x2, (w_up1,), (w_down1,) = _conv_ffn(x2, h2, w_up0, f_conv_w[0], f_conv_b[0], w_down0, ml[5],
                                         tm=FFN_UP_TM, down_tm=FFN_DOWN_TM,
                                         up_cast=cast_up1, down_cast=cast_down1)

    ml = split6(mod[1, 0:1])
    bias_full = jnp.repeat(c_b_spatial[0].T, C_GROUP_DIM, axis=1)
    c_out, f, w_out1 = _cd_in_proj(x2, row1(norm1_g[1]), ml[0], ml[1], cd_w_in_bf, row1(c_v_norm_g[0]),
                                   c_w_spatial[0].astype(BF16), bias_full, tm=PROJ_TM,
                                   cast=_CastJob(cd_w_out, 0, D_MODEL // n_tiles, n_tiles))
    d_out = _fourier_mix(f, _fourier_tables(n), d_w_fourier[0].astype(BF16))
    x2, h2 = _out_proj(c_out, d_out, w_out1, x2, ml[2], row1(norm2_g[1]), ml[3], ml[4], tm=PROJ_TM)
    x2, _, _ = _conv_ffn(x2, h2, w_up1, f_conv_w[1], f_conv_b[1], w_down1, ml[5],
                         tm=FFN_UP_TM, down_tm=FFN_DOWN_TM)
    return x2.reshape(batch, n, D_MODEL)
```
